```python
import jax, jax.numpy as jnp
from jax import lax
import numpy as np

D_MODEL = 1024
BATCH = 8
SEQ = 2048
DEPTH = 1
DEC_BATCH = 128
DEC_SEQ = 8
PAST_LEN = 16384
PAGE_SIZE = 128

LRU_WIDTH = D_MODEL
LRU_HEADS = 16
LRU_BLOCK = LRU_WIDTH // LRU_HEADS
LRU_C = 8.0
CONV_A_WIDTH = 4
SC_WIDTH = D_MODEL
CONV_B_WIDTH = 3
IN_WIDTH = 2 * LRU_WIDTH + 3 * SC_WIDTH + 2 * D_MODEL
N_EXPERTS = 32
TOP_K = 4
D_FF = D_MODEL
SWIGLU_LIMIT = 7.0
SWIGLU_ALPHA = 1.702
MOE_BLOCK = 128
LN_EPS = 1e-5
DEEPNORM_ALPHA = (2.0 * DEPTH) ** 0.25
DEEPNORM_BETA = (8.0 * DEPTH) ** -0.25

kernel_name = "hawk_shortconv_moe_deepnorm_step"


def layernorm(x, g, b):
    xf = x.astype(jnp.float32)
    mu = jnp.mean(xf, axis=-1, keepdims=True)
    var = jnp.mean(jnp.square(xf - mu), axis=-1, keepdims=True)
    y = (xf - mu) * lax.rsqrt(var + LN_EPS) * g.astype(jnp.float32) + b.astype(jnp.float32)
    return y.astype(x.dtype)


def causal_dwconv(x, buf, w):
    width = w.shape[0]
    s = x.shape[1]
    xp = jnp.concatenate([buf.astype(x.dtype), x], axis=1)
    y = xp[:, 0:s] * w[0]
    for k in range(1, width):
        y = y + xp[:, k:k + s] * w[k]
    return y, xp[:, -(width - 1):]


def rglru(x, wa, ba, wx, bx, lam, h0):
    bsz, s, _ = x.shape
    xb = x.reshape(bsz, s, LRU_HEADS, LRU_BLOCK)
    r = jax.nn.sigmoid((jnp.einsum("bshi,hij->bshj", xb, wa).reshape(bsz, s, LRU_WIDTH) + ba).astype(jnp.float32))
    i = jax.nn.sigmoid((jnp.einsum("bshi,hij->bshj", xb, wx).reshape(bsz, s, LRU_WIDTH) + bx).astype(jnp.float32))
    log_a = -LRU_C * r * jax.nn.softplus(-lam.astype(jnp.float32))
    a = jnp.exp(log_a)
    b = jnp.sqrt(-jnp.expm1(2.0 * log_a)) * (i * x.astype(jnp.float32))
    b = b.at[:, 0].add(a[:, 0] * h0.astype(jnp.float32))

    def combine(left, right):
        a_l, b_l = left
        a_r, b_r = right
        return a_l * a_r, a_r * b_l + b_r

    _, h = lax.associative_scan(combine, (a, b), axis=1)
    return h.astype(x.dtype), h[:, -1].astype(x.dtype)


def moe_ffn(x, router_w, router_b, w_gu, b_gu, w_down, b_down):
    bsz, s, d = x.shape
    t = bsz * s
    x2 = x.reshape(t, d)
    logits = x2.astype(jnp.float32) @ router_w.astype(jnp.float32) + router_b.astype(jnp.float32)
    top_vals, top_idx = lax.top_k(logits, TOP_K)
    gates = jax.nn.softmax(top_vals, axis=-1).astype(x.dtype)
    n_assign = t * TOP_K
    flat_e = top_idx.reshape(n_assign).astype(jnp.int32)
    flat_tok = jnp.repeat(jnp.arange(t, dtype=jnp.int32), TOP_K)
    flat_w = gates.reshape(n_assign)
    order = jnp.argsort(flat_e)
    sorted_e = flat_e[order]
    counts = jnp.zeros((N_EXPERTS,), jnp.int32).at[flat_e].add(1)
    padded = (counts + MOE_BLOCK - 1) // MOE_BLOCK * MOE_BLOCK
    start = jnp.cumsum(counts) - counts
    end_pad = jnp.cumsum(padded)
    start_pad = end_pad - padded
    dest = start_pad[sorted_e] + jnp.arange(n_assign, dtype=jnp.int32) - start[sorted_e]
    n_blocks = -(-n_assign // MOE_BLOCK) + N_EXPERTS
    n_rows = n_blocks * MOE_BLOCK
    row_tok = jnp.full((n_rows,), t, jnp.int32).at[dest].set(flat_tok[order])
    row_w = jnp.zeros((n_rows,), x.dtype).at[dest].set(flat_w[order])
    block_e = jnp.minimum(
        jnp.searchsorted(end_pad, jnp.arange(n_blocks, dtype=jnp.int32) * MOE_BLOCK, side="right"),
        N_EXPERTS - 1)
    x_rows = jnp.concatenate([x2, jnp.zeros((1, d), x.dtype)], axis=0)[row_tok]
    x_rows = x_rows.reshape(n_blocks, MOE_BLOCK, d)

    def expert_block(args):
        xb, e = args
        gu = xb @ w_gu[e] + b_gu[e]
        gate = jnp.minimum(gu[:, :D_FF], SWIGLU_LIMIT)
        up = jnp.clip(gu[:, D_FF:], -SWIGLU_LIMIT, SWIGLU_LIMIT)
        h = (up + 1.0) * (gate * jax.nn.sigmoid(SWIGLU_ALPHA * gate))
        return h @ w_down[e] + b_down[e]

    y_rows = lax.map(expert_block, (x_rows, block_e)).reshape(n_rows, d)
    y = jax.ops.segment_sum(y_rows * row_w[:, None], row_tok, num_segments=t + 1)[:t]
    return y.reshape(bsz, s, d)


def hybrid_layer(x, h0, conv_a_buf, sc_buf, w_in, b_in, conv_a_w, conv_a_b, lru_wa, lru_ba, lru_wx, lru_bx,
                 lru_lambda, conv_b_w, w_out, ln1_g, ln1_b, router_w, router_b, w_gu, b_gu, w_down, b_down,
                 ln2_g, ln2_b):
    proj = x @ w_in + b_in
    splits = [LRU_WIDTH, 2 * LRU_WIDTH, 2 * LRU_WIDTH + SC_WIDTH, 2 * LRU_WIDTH + 2 * SC_WIDTH,
              2 * LRU_WIDTH + 3 * SC_WIDTH, 2 * LRU_WIDTH + 3 * SC_WIDTH + D_MODEL]
    xa, ga, b_gate, c_gate, v_sc, m_a, m_b = jnp.split(proj, splits, axis=-1)
    xa_c, conv_a_new = causal_dwconv(xa, conv_a_buf, conv_a_w)
    h, h_last = rglru(xa_c + conv_a_b, lru_wa, lru_ba, lru_wx, lru_bx, lru_lambda, h0)
    y_a = h * jax.nn.gelu(ga)
    u_c, sc_new = causal_dwconv(c_gate * v_sc, sc_buf, conv_b_w)
    y_b = b_gate * u_c
    mixed = (jax.nn.sigmoid(m_a) * y_a + jax.nn.sigmoid(m_b) * y_b) @ w_out
    x = layernorm(DEEPNORM_ALPHA * x + mixed, ln1_g, ln1_b)
    x = layernorm(DEEPNORM_ALPHA * x + moe_ffn(x, router_w, router_b, w_gu, b_gu, w_down, b_down), ln2_g, ln2_b)
    return x, h_last, conv_a_new, sc_new


def setup_inputs(seed: int = 0) -> dict:
    key = jax.random.key(seed)
    ks = jax.random.split(key, 26)
    nrm = lambda k, shape, scale: scale * jax.random.normal(k, shape, jnp.float32)
    u = jax.random.uniform(ks[9], (DEPTH, LRU_WIDTH), jnp.float32, 0.9, 0.999)
    a0 = u ** (1.0 / LRU_C)
    lru_lambda = jnp.log(a0) - jnp.log1p(-a0)
    return {
        "x_prompt": nrm(ks[0], (BATCH, SEQ, D_MODEL), 1.0),
        "x_sample": nrm(ks[1], (DEC_BATCH, DEC_SEQ, D_MODEL), 1.0),
        "state_rglru_h": nrm(ks[2], (DEPTH, DEC_BATCH, LRU_WIDTH), 0.5),
        "state_rglru_conv": nrm(ks[3], (DEPTH, DEC_BATCH, CONV_A_WIDTH - 1, LRU_WIDTH), 1.0),
        "state_shortconv": nrm(ks[4], (DEPTH, DEC_BATCH, CONV_B_WIDTH - 1, SC_WIDTH), 1.0),
        "w_in": nrm(ks[5], (DEPTH, D_MODEL, IN_WIDTH), D_MODEL ** -0.5),
        "b_in": nrm(ks[6], (DEPTH, IN_WIDTH), 0.02),
        "conv_a_w": nrm(ks[7], (DEPTH, CONV_A_WIDTH, LRU_WIDTH), CONV_A_WIDTH ** -0.5),
        "conv_a_b": nrm(ks[8], (DEPTH, LRU_WIDTH), 0.02),
        "lru_wa": nrm(ks[10], (DEPTH, LRU_HEADS, LRU_BLOCK, LRU_BLOCK), LRU_BLOCK ** -0.5),
        "lru_ba": nrm(ks[11], (DEPTH, LRU_WIDTH), 0.02),
        "lru_wx": nrm(ks[12], (DEPTH, LRU_HEADS, LRU_BLOCK, LRU_BLOCK), LRU_BLOCK ** -0.5),
        "lru_bx": nrm(ks[13], (DEPTH, LRU_WIDTH), 0.02),
        "lru_lambda": lru_lambda,
        "conv_b_w": nrm(ks[14], (DEPTH, CONV_B_WIDTH, SC_WIDTH), CONV_B_WIDTH ** -0.5),
        "w_out": nrm(ks[15], (DEPTH, D_MODEL, D_MODEL), DEEPNORM_BETA * D_MODEL ** -0.5),
        "ln1_g": 1.0 + nrm(ks[16], (DEPTH, D_MODEL), 0.02),
        "ln1_b": nrm(ks[17], (DEPTH, D_MODEL), 0.02),
        "router_w": nrm(ks[18], (DEPTH, D_MODEL, N_EXPERTS), D_MODEL ** -0.5),
        "router_b": nrm(ks[19], (DEPTH, N_EXPERTS), 0.01),
        "w_gu": nrm(ks[20], (DEPTH, N_EXPERTS, D_MODEL, 2 * D_FF), D_MODEL ** -0.5),
        "b_gu": nrm(ks[21], (DEPTH, N_EXPERTS, 2 * D_FF), 0.02),
        "w_down": nrm(ks[22], (DEPTH, N_EXPERTS, D_FF, D_MODEL), DEEPNORM_BETA * D_FF ** -0.5),
        "b_down": nrm(ks[23], (DEPTH, N_EXPERTS, D_MODEL), 0.02),
        "ln2_g": 1.0 + nrm(ks[24], (DEPTH, D_MODEL), 0.02),
        "ln2_b": nrm(ks[25], (DEPTH, D_MODEL), 0.02),
    }


def reference(x_prompt, x_sample, state_rglru_h, state_rglru_conv, state_shortconv, w_in, b_in, conv_a_w,
              conv_a_b, lru_wa, lru_ba, lru_wx, lru_bx, lru_lambda, conv_b_w, w_out, ln1_g, ln1_b, router_w,
              router_b, w_gu, b_gu, w_down, b_down, ln2_g, ln2_b):
    xp, xs = x_prompt, x_sample
    bp = x_prompt.shape[0]
    hp_l, cp_l, sp_l, hs_l, cs_l, ss_l = [], [], [], [], [], []
    for l in range(DEPTH):
        params = (w_in[l], b_in[l], conv_a_w[l], conv_a_b[l], lru_wa[l], lru_ba[l], lru_wx[l], lru_bx[l],
                  lru_lambda[l], conv_b_w[l], w_out[l], ln1_g[l], ln1_b[l], router_w[l], router_b[l],
                  w_gu[l], b_gu[l], w_down[l], b_down[l], ln2_g[l], ln2_b[l])
        h0 = jnp.zeros((bp, LRU_WIDTH), xp.dtype)
        ca0 = jnp.zeros((bp, CONV_A_WIDTH - 1, LRU_WIDTH), xp.dtype)
        cb0 = jnp.zeros((bp, CONV_B_WIDTH - 1, SC_WIDTH), xp.dtype)
        xp, hp, cp, sp = hybrid_layer(xp, h0, ca0, cb0, *params)
        xs, hs, cs, ss = hybrid_layer(xs, state_rglru_h[l], state_rglru_conv[l], state_shortconv[l], *params)
        hp_l.append(hp); cp_l.append(cp); sp_l.append(sp)
        hs_l.append(hs); cs_l.append(cs); ss_l.append(ss)
    return (xp, xs, jnp.stack(hp_l), jnp.stack(cp_l), jnp.stack(sp_l), jnp.stack(hs_l), jnp.stack(cs_l),
            jnp.stack(ss_l))
```

```python
import functools

import jax
import jax.numpy as jnp
from jax import lax
from jax.experimental import pallas as pl
from jax.experimental.pallas import tpu as pltpu

D_MODEL = 1024
LRU_HEADS = 16
LRU_BLOCK = D_MODEL // LRU_HEADS
LRU_C = 8.0
CONV_A_WIDTH = 4
CONV_B_WIDTH = 3
N_GROUPS = 7
N_EXPERTS = 32
TOP_K = 4
D_FF = D_MODEL
SWIGLU_LIMIT = 7.0
SWIGLU_ALPHA = 1.702
LN_EPS = 1e-5

GATE_TILE = 256
HEADS_PER_TILE = GATE_TILE // LRU_BLOCK
N_GATE_TILES = D_MODEL // GATE_TILE
LANES = 128
MIXER_ROWS = 512
MOE_ROWS = 256
COMBINE_ROWS = 512
VMEM_LIMIT = 58 * 1024 * 1024

_F32 = jnp.float32
_BF16 = jnp.bfloat16
_NT = (((1,), (1,)), ((), ()))


def _sigmoid(v):
    return 0.5 * jnp.tanh(0.5 * v) + 0.5


def _gelu_tanh(v):
    c = 0.7978845608028654
    return 0.5 * v * (1.0 + jnp.tanh(c * (v + 0.044715 * (v * v * v))))


def _layernorm(z, g, b):
    mu = jnp.mean(z, axis=-1, keepdims=True)
    zc = z - mu
    var = jnp.mean(zc * zc, axis=-1, keepdims=True)
    return zc * lax.rsqrt(var + LN_EPS) * g + b


def _mixer_kernel(x_ref, h0_ref, ca0_ref, cb0_ref, cnt0_ref,
                  w_in_ref, b_in_ref, wca_ref, bca_ref, wa_ref, ba_ref, wx_ref, bx_ref, lam_ref,
                  wcb_ref, w_out_ref, g1_ref, be1_ref, rwt_hi_ref, rwt_lo_ref, rb_ref, tri_ref,
                  x1_ref, idx_ref, rank_ref, grow_ref, hl_ref, ca_ref, cb_ref, cnt_ref,
                  xa_s, u_s, a_s, b_s, h_s, hst_s, cnt_s, *, ts, nb, alpha):
    i = pl.program_id(0)
    j = pl.program_id(1)
    rows = ts * nb
    ta = (CONV_A_WIDTH - 1) * nb
    tb = (CONV_B_WIDTH - 1) * nb

    @pl.when(j == 0)
    def _():
        hst_s[...] = h0_ref[...]
        xa_s[0:ta, :] = ca0_ref[...].reshape(ta, D_MODEL)
        u_s[0:tb, :] = cb0_ref[...].reshape(tb, D_MODEL)

    @pl.when((i == 0) & (j == 0))
    def _():
        cnt_s[...] = cnt0_ref[...]

    x = x_ref[...].reshape(rows, D_MODEL)
    xb = x.astype(_BF16)

    def proj(g):
        lo, hi = g * D_MODEL, (g + 1) * D_MODEL
        return jnp.dot(xb, w_in_ref[:, lo:hi], preferred_element_type=_F32) + b_in_ref[:, lo:hi]

    xa_s[ta:ta + rows, :] = proj(0)
    xc = bca_ref[...] + xa_s[0:rows, :] * wca_ref[0:1, :]
    for k in range(1, CONV_A_WIDTH):
        xc = xc + xa_s[k * nb:k * nb + rows, :] * wca_ref[k:k + 1, :]
    new_ta = xa_s[rows:rows + ta, :]
    xa_s[0:ta, :] = new_ta
    ca_ref[...] = new_ta.reshape(CONV_A_WIDTH - 1, nb, D_MODEL)

    xcb = xc.astype(_BF16)

    def block_diag(w_ref):
        return jnp.concatenate(
            [jnp.dot(xcb[:, q * GATE_TILE:(q + 1) * GATE_TILE], w_ref[q], preferred_element_type=_F32)
             for q in range(N_GATE_TILES)], axis=-1)

    r = _sigmoid(block_diag(wa_ref) + ba_ref[...])
    ig = _sigmoid(block_diag(wx_ref) + bx_ref[...])
    nlam = -lam_ref[...]
    softplus = jnp.maximum(nlam, 0.0) + jnp.log1p(jnp.exp(-jnp.abs(nlam)))
    log_a = (-LRU_C * softplus) * r
    a = jnp.exp(log_a)
    a_s[...] = a
    b_s[...] = jnp.sqrt(-jnp.tanh(log_a) * (a * a + 1.0)) * (ig * xc)

    h = hst_s[...]
    for t in range(ts):
        sl = slice(t * nb, (t + 1) * nb)
        h = a_s[sl, :] * h + b_s[sl, :]
        h_s[sl, :] = h
    hst_s[...] = h
    hl_ref[...] = h

    a_s[...] = h_s[...] * _gelu_tanh(proj(1))

    u_s[tb:tb + rows, :] = proj(3) * proj(4)
    uc = u_s[0:rows, :] * wcb_ref[0:1, :]
    for k in range(1, CONV_B_WIDTH):
        uc = uc + u_s[k * nb:k * nb + rows, :] * wcb_ref[k:k + 1, :]
    new_tb = u_s[rows:rows + tb, :]
    u_s[0:tb, :] = new_tb
    cb_ref[...] = new_tb.reshape(CONV_B_WIDTH - 1, nb, D_MODEL)
    y_b = proj(2) * uc

    merged = _sigmoid(proj(5)) * a_s[...] + _sigmoid(proj(6)) * y_b
    mixed = jnp.dot(merged.astype(_BF16), w_out_ref[...], preferred_element_type=_F32)
    x1 = _layernorm(alpha * x + mixed, g1_ref[...], be1_ref[...])
    x1_ref[...] = x1

    x1_hi = x1.astype(_BF16)
    x1_lo = (x1 - x1_hi.astype(_F32)).astype(_BF16)
    logits = (lax.dot_general(rwt_hi_ref[...], x1_hi, _NT, preferred_element_type=_F32)
              + lax.dot_general(rwt_hi_ref[...], x1_lo, _NT, preferred_element_type=_F32)
              + lax.dot_general(rwt_lo_ref[...], x1_hi, _NT, preferred_element_type=_F32)
              + rb_ref[:, 0:1])
    e_iota = lax.broadcasted_iota(jnp.int32, (N_EXPERTS, rows), 0)
    work = logits
    vals, sels, idxs = [], [], []
    for _ in range(TOP_K):
        m = jnp.max(work, axis=0, keepdims=True)
        ik = jnp.min(jnp.where(work == m, e_iota, N_EXPERTS), axis=0, keepdims=True)
        sel = e_iota == ik
        work = jnp.where(sel, -jnp.inf, work)
        vals.append(m)
        sels.append(sel)
        idxs.append(ik)
    exps = [jnp.exp(v - vals[0]) for v in vals]
    denom = exps[0] + exps[1] + exps[2] + exps[3]
    gates = [ex / denom for ex in exps]

    onehot = jnp.zeros((N_EXPERTS, rows), _F32)
    for sel in sels:
        onehot = onehot + sel.astype(_F32)
    prefix = jnp.dot(onehot.astype(_BF16), tri_ref[...], preferred_element_type=_F32)
    pos = prefix + cnt_s[:, 0:1]
    ranks = [jnp.sum(jnp.where(sel, pos, 0.0), axis=0, keepdims=True) for sel in sels]
    new_cnt = cnt_s[...] + jnp.sum(onehot, axis=1, keepdims=True)
    cnt_s[...] = new_cnt
    cnt_ref[...] = new_cnt

    row8 = lax.broadcasted_iota(jnp.int32, (8, rows), 0)
    idx8 = jnp.zeros((8, rows), jnp.int32)
    rank8 = jnp.zeros((8, rows), jnp.int32)
    for k in range(TOP_K):
        idx8 = jnp.where(row8 == k, idxs[k], idx8)
        rank8 = jnp.where(row8 == k, ranks[k].astype(jnp.int32), rank8)
    idx_ref[...] = idx8
    rank_ref[...] = rank8

    row_l = lax.broadcasted_iota(jnp.int32, (LANES, rows), 0)
    g_t = jnp.zeros((LANES, rows), _F32)
    for k in range(TOP_K):
        g_t = jnp.where(row_l == k, gates[k], g_t)
    grow_ref[...] = jnp.transpose(g_t)


def _const_spec(shape):
    nd = len(shape)
    return pl.BlockSpec(shape, lambda i, j: (0,) * nd, pipeline_mode=pl.Buffered(1))


def _mixer_call(x4, h0, ca0, cb0, cnt0, wts, *, ts, alpha):
    nblk, seq, nb, _ = x4.shape
    nt = seq // ts
    rows = ts * nb
    total = nblk * seq * nb
    f32 = lambda *s: jax.ShapeDtypeStruct(s, _F32)
    i32 = lambda *s: jax.ShapeDtypeStruct(s, jnp.int32)
    in_specs = [
        pl.BlockSpec((None, ts, nb, D_MODEL), lambda i, j: (i, j, 0, 0)),
        pl.BlockSpec((None, nb, D_MODEL), lambda i, j: (i, 0, 0)),
        pl.BlockSpec((None, CONV_A_WIDTH - 1, nb, D_MODEL), lambda i, j: (i, 0, 0, 0)),
        pl.BlockSpec((None, CONV_B_WIDTH - 1, nb, D_MODEL), lambda i, j: (i, 0, 0, 0)),
        _const_spec(cnt0.shape),
    ] + [_const_spec(w.shape) for w in wts]
    out_shape = (
        f32(total, D_MODEL),
        i32(8, total),
        i32(8, total),
        f32(total, LANES),
        f32(nblk, nb, D_MODEL),
        f32(nblk, CONV_A_WIDTH - 1, nb, D_MODEL),
        f32(nblk, CONV_B_WIDTH - 1, nb, D_MODEL),
        f32(N_EXPERTS, LANES),
    )
    out_specs = (
        pl.BlockSpec((rows, D_MODEL), lambda i, j: (i * nt + j, 0)),
        pl.BlockSpec((8, rows), lambda i, j: (0, i * nt + j)),
        pl.BlockSpec((8, rows), lambda i, j: (0, i * nt + j)),
        pl.BlockSpec((rows, LANES), lambda i, j: (i * nt + j, 0)),
        pl.BlockSpec((None, nb, D_MODEL), lambda i, j: (i, 0, 0)),
        pl.BlockSpec((None, CONV_A_WIDTH - 1, nb, D_MODEL), lambda i, j: (i, 0, 0, 0)),
        pl.BlockSpec((None, CONV_B_WIDTH - 1, nb, D_MODEL), lambda i, j: (i, 0, 0, 0)),
        pl.BlockSpec((N_EXPERTS, LANES), lambda i, j: (0, 0)),
    )
    scratch = [
        pltpu.VMEM((rows + (CONV_A_WIDTH - 1) * nb, D_MODEL), _F32),
        pltpu.VMEM((rows + (CONV_B_WIDTH - 1) * nb, D_MODEL), _F32),
        pltpu.VMEM((rows, D_MODEL), _F32),
        pltpu.VMEM((rows, D_MODEL), _F32),
        pltpu.VMEM((rows, D_MODEL), _F32),
        pltpu.VMEM((nb, D_MODEL), _F32),
        pltpu.VMEM((N_EXPERTS, LANES), _F32),
    ]
    return pl.pallas_call(
        functools.partial(_mixer_kernel, ts=ts, nb=nb, alpha=alpha),
        grid=(nblk, nt),
        in_specs=in_specs,
        out_specs=out_specs,
        out_shape=out_shape,
        scratch_shapes=scratch,
        compiler_params=pltpu.CompilerParams(
            dimension_semantics=("arbitrary", "arbitrary"), vmem_limit_bytes=VMEM_LIMIT),
        name="mixer",
    )(x4, h0, ca0, cb0, cnt0, *wts)


def _moe_kernel(blk_e_ref, n_used_ref, src_ref, dst_ref,
                x1_hbm, wgu_ref, bgu_ref, wd_ref, bd_ref,
                ytok_hbm,
                xbuf, ybuf, wgu_s, wd_s, gsem, ssem, zsem, *, n_slots):
    b = pl.program_id(0)
    nblk = pl.num_programs(0)
    n_used = n_used_ref[0]
    slot = b % 2

    def gather_copy(row, s, r):
        return pltpu.make_async_copy(
            x1_hbm.at[pl.ds(row, 1), :], xbuf.at[s, pl.ds(r, 1), :], gsem.at[s])

    def scatter_copy(row, s, r):
        return pltpu.make_async_copy(
            ybuf.at[s, pl.ds(r, 1), :], ytok_hbm.at[pl.ds(row, 1), :], ssem.at[s])

    def start_gather(blk, s):
        base = blk * MOE_ROWS
        for r in range(MOE_ROWS):
            gather_copy(src_ref[base + r], s, r).start()

    def wait_rows(copy_fn, s):
        for r in range(MOE_ROWS):
            copy_fn(0, s, r).wait()

    @pl.when(b == 0)
    def _():
        ybuf[0] = jnp.zeros((MOE_ROWS, D_MODEL), _F32)
        for s in range(2):
            cp = pltpu.make_async_copy(
                ybuf.at[0], ytok_hbm.at[pl.ds(n_slots + s * MOE_ROWS, MOE_ROWS), :], zsem)
            cp.start()
            cp.wait()
        start_gather(0, 0)

    first_of_expert = (b == 0) | (blk_e_ref[b] != blk_e_ref[jnp.maximum(b - 1, 0)])

    @pl.when(first_of_expert & (b < n_used))
    def _():
        wgu_s[...] = wgu_ref[...].astype(_BF16)
        wd_s[...] = wd_ref[...].astype(_BF16)

    @pl.when(b < n_used)
    def _():
        @pl.when(b + 1 < n_used)
        def _():
            start_gather(b + 1, 1 - slot)

        @pl.when(b >= 2)
        def _():
            wait_rows(scatter_copy, slot)

        wait_rows(gather_copy, slot)
        xb = xbuf[slot].astype(_BF16)
        gu = jnp.dot(xb, wgu_s[...], preferred_element_type=_F32) + bgu_ref[...]
        gate = jnp.minimum(gu[:, :D_FF], SWIGLU_LIMIT)
        up = jnp.clip(gu[:, D_FF:], -SWIGLU_LIMIT, SWIGLU_LIMIT)
        hmid = (up + 1.0) * (gate * _sigmoid(SWIGLU_ALPHA * gate))
        ybuf[slot] = jnp.dot(hmid.astype(_BF16), wd_s[...], preferred_element_type=_F32) + bd_ref[...]
        base = b * MOE_ROWS
        for r in range(MOE_ROWS):
            scatter_copy(dst_ref[base + r], slot, r).start()

    @pl.when(b == nblk - 1)
    def _():
        for back in range(2):
            last = n_used - 1 - back

            @pl.when(last >= 0)
            def _():
                wait_rows(scatter_copy, last % 2)


def _moe_call(blk_e, n_used, row_src, row_dst, x1, w_gu, b_gu, w_down, b_down, *, n_blocks, n_slots):
    grid_spec = pltpu.PrefetchScalarGridSpec(
        num_scalar_prefetch=4,
        grid=(n_blocks,),
        in_specs=[
            pl.BlockSpec(memory_space=pl.ANY),
            pl.BlockSpec((None, D_MODEL, 2 * D_FF), lambda b, be, nu, rs, rd: (be[b], 0, 0)),
            pl.BlockSpec((None, 1, 2 * D_FF), lambda b, be, nu, rs, rd: (be[b], 0, 0)),
            pl.BlockSpec((None, D_FF, D_MODEL), lambda b, be, nu, rs, rd: (be[b], 0, 0)),
            pl.BlockSpec((None, 1, D_MODEL), lambda b, be, nu, rs, rd: (be[b], 0, 0)),
        ],
        out_specs=pl.BlockSpec(memory_space=pl.ANY),
        scratch_shapes=[
            pltpu.VMEM((2, MOE_ROWS, D_MODEL), _F32),
            pltpu.VMEM((2, MOE_ROWS, D_MODEL), _F32),
            pltpu.VMEM((D_MODEL, 2 * D_FF), _BF16),
            pltpu.VMEM((D_FF, D_MODEL), _BF16),
            pltpu.SemaphoreType.DMA((2,)),
            pltpu.SemaphoreType.DMA((2,)),
            pltpu.SemaphoreType.DMA(()),
        ],
    )
    return pl.pallas_call(
        functools.partial(_moe_kernel, n_slots=n_slots),
        grid_spec=grid_spec,
        out_shape=jax.ShapeDtypeStruct((n_slots + 2 * MOE_ROWS, D_MODEL), _F32),
        compiler_params=pltpu.CompilerParams(
            dimension_semantics=("arbitrary",), vmem_limit_bytes=VMEM_LIMIT),
        name="moe",
    )(blk_e, n_used, row_src, row_dst, x1, w_gu, b_gu, w_down, b_down)


def _combine_kernel(x1_ref, g_ref, y0_ref, y1_ref, y2_ref, y3_ref, g2_ref, be2_ref, out_ref, *, alpha):
    g = g_ref[...]
    moe = (g[:, 0:1] * y0_ref[...] + g[:, 1:2] * y1_ref[...]
           + g[:, 2:3] * y2_ref[...] + g[:, 3:4] * y3_ref[...])
    out_ref[...] = _layernorm(alpha * x1_ref[...] + moe, g2_ref[...], be2_ref[...])


def _combine_call(x1, grow, ytok, g2, be2, *, alpha):
    total = x1.shape[0]
    nt = total // COMBINE_ROWS
    tile = lambda k: pl.BlockSpec((COMBINE_ROWS, D_MODEL), lambda i, k=k: (k * nt + i, 0))
    vec = pl.BlockSpec((1, D_MODEL), lambda i: (0, 0))
    return pl.pallas_call(
        functools.partial(_combine_kernel, alpha=alpha),
        grid=(nt,),
        in_specs=[tile(0), pl.BlockSpec((COMBINE_ROWS, LANES), lambda i: (i, 0)),
                  tile(0), tile(1), tile(2), tile(3), vec, vec],
        out_specs=tile(0),
        out_shape=jax.ShapeDtypeStruct((total, D_MODEL), _F32),
        compiler_params=pltpu.CompilerParams(
            dimension_semantics=("arbitrary",), vmem_limit_bytes=VMEM_LIMIT),
        name="combine",
    )(x1, grow, ytok, ytok, ytok, ytok, g2, be2)


def _pack_block_diag(w):
    w = w.reshape(N_GATE_TILES, HEADS_PER_TILE, LRU_BLOCK, LRU_BLOCK)
    eye = jnp.eye(HEADS_PER_TILE, dtype=w.dtype)
    t = jnp.einsum("qhij,hg->qhigj", w, eye)
    return t.reshape(N_GATE_TILES, GATE_TILE, GATE_TILE)


def _layer(xp, xs, h_s0, ca_s0, cb_s0, p, *, alpha):
    bp, sp, _ = xp.shape
    bs, ss, _ = xs.shape
    tp, tsm = bp * sp, bs * ss
    total = tp + tsm
    row2 = lambda v: v.reshape(1, -1)

    rwt = jnp.transpose(p["router_w"])
    rwt_hi = rwt.astype(_BF16)
    rwt_lo = (rwt - rwt_hi.astype(_F32)).astype(_BF16)
    ii = jnp.arange(MIXER_ROWS)
    tri = (ii[:, None] < ii[None, :]).astype(_BF16)
    wts = (
        p["w_in"].astype(_BF16), row2(p["b_in"]), p["conv_a_w"], row2(p["conv_a_b"]),
        _pack_block_diag(p["lru_wa"]).astype(_BF16), row2(p["lru_ba"]),
        _pack_block_diag(p["lru_wx"]).astype(_BF16), row2(p["lru_bx"]), row2(p["lru_lambda"]),
        p["conv_b_w"], p["w_out"].astype(_BF16), row2(p["ln1_g"]), row2(p["ln1_b"]),
        rwt_hi, rwt_lo, jnp.broadcast_to(p["router_b"][:, None], (N_EXPERTS, LANES)), tri,
    )

    ts_p = MIXER_ROWS // bp
    xp4 = jnp.transpose(xp, (1, 0, 2))[None]
    zeros = lambda *s: jnp.zeros(s, _F32)
    outs_p = _mixer_call(xp4, zeros(1, bp, D_MODEL), zeros(1, CONV_A_WIDTH - 1, bp, D_MODEL),
                         zeros(1, CONV_B_WIDTH - 1, bp, D_MODEL), zeros(N_EXPERTS, LANES), wts,
                         ts=ts_p, alpha=alpha)
    nb_s = MIXER_ROWS // ss
    nblk_s = bs // nb_s
    xs4 = jnp.transpose(xs.reshape(nblk_s, nb_s, ss, D_MODEL), (0, 2, 1, 3))
    h0 = h_s0.reshape(nblk_s, nb_s, D_MODEL)
    ca0 = jnp.transpose(ca_s0.reshape(nblk_s, nb_s, CONV_A_WIDTH - 1, D_MODEL), (0, 2, 1, 3))
    cb0 = jnp.transpose(cb_s0.reshape(nblk_s, nb_s, CONV_B_WIDTH - 1, D_MODEL), (0, 2, 1, 3))
    outs_s = _mixer_call(xs4, h0, ca0, cb0, outs_p[7], wts, ts=ss, alpha=alpha)

    x1 = jnp.concatenate([outs_p[0], outs_s[0]], axis=0)
    idx = jnp.concatenate([outs_p[1][:TOP_K], outs_s[1][:TOP_K]], axis=1)
    rank = jnp.concatenate([outs_p[2][:TOP_K], outs_s[2][:TOP_K]], axis=1)
    grow = jnp.concatenate([outs_p[3], outs_s[3]], axis=0)
    counts = outs_s[7][:, 0].astype(jnp.int32)

    n_assign = total * TOP_K
    n_blocks = -(-n_assign // MOE_ROWS) + N_EXPERTS
    n_rows = n_blocks * MOE_ROWS
    padded = (counts + MOE_ROWS - 1) // MOE_ROWS * MOE_ROWS
    end_pad = jnp.cumsum(padded)
    start_pad = end_pad - padded
    dest = (start_pad[idx] + rank).reshape(n_assign)
    slot_ids = jnp.arange(n_assign, dtype=jnp.int32)
    tok_ids = slot_ids % total
    row_ids = jnp.arange(n_rows, dtype=jnp.int32)
    row_src = jnp.zeros((n_rows,), jnp.int32).at[dest].set(tok_ids, unique_indices=True)
    spare = n_assign + (row_ids // MOE_ROWS % 2) * MOE_ROWS + row_ids % MOE_ROWS
    row_dst = spare.at[dest].set(slot_ids, unique_indices=True)
    blk_start = jnp.arange(n_blocks, dtype=jnp.int32) * MOE_ROWS
    blk_e = jnp.minimum(jnp.searchsorted(end_pad, blk_start, side="right"), N_EXPERTS - 1).astype(jnp.int32)
    n_used = (end_pad[-1] // MOE_ROWS).astype(jnp.int32).reshape(1)

    ytok = _moe_call(blk_e, n_used, row_src, row_dst, x1,
                     p["w_gu"], p["b_gu"][:, None, :], p["w_down"], p["b_down"][:, None, :],
                     n_blocks=n_blocks, n_slots=n_assign)
    y = _combine_call(x1, grow, ytok, row2(p["ln2_g"]), row2(p["ln2_b"]), alpha=alpha)

    yp = jnp.transpose(y[:tp].reshape(sp, bp, D_MODEL), (1, 0, 2))
    ys = jnp.transpose(y[tp:].reshape(nblk_s, ss, nb_s, D_MODEL), (0, 2, 1, 3)).reshape(bs, ss, D_MODEL)

    def batch_major(v, nblk):
        return jnp.transpose(v, (0, 2, 1, 3)).reshape(nblk * v.shape[2], v.shape[1], D_MODEL)

    states_p = (outs_p[4].reshape(bp, D_MODEL), batch_major(outs_p[5], 1), batch_major(outs_p[6], 1))
    states_s = (outs_s[4].reshape(bs, D_MODEL), batch_major(outs_s[5], nblk_s), batch_major(outs_s[6], nblk_s))
    return yp, ys, states_p, states_s


def kernel(x_prompt, x_sample, state_rglru_h, state_rglru_conv, state_shortconv, w_in, b_in, conv_a_w, conv_a_b, lru_wa, lru_ba, lru_wx, lru_bx, lru_lambda, conv_b_w, w_out, ln1_g, ln1_b, router_w, router_b, w_gu, b_gu, w_down, b_down, ln2_g, ln2_b):
    depth = w_in.shape[0]
    alpha = (2.0 * depth) ** 0.25
    names = ("w_in", "b_in", "conv_a_w", "conv_a_b", "lru_wa", "lru_ba", "lru_wx", "lru_bx", "lru_lambda",
             "conv_b_w", "w_out", "ln1_g", "ln1_b", "router_w", "router_b", "w_gu", "b_gu", "w_down",
             "b_down", "ln2_g", "ln2_b")
    stacked = (w_in, b_in, conv_a_w, conv_a_b, lru_wa, lru_ba, lru_wx, lru_bx, lru_lambda, conv_b_w, w_out,
               ln1_g, ln1_b, router_w, router_b, w_gu, b_gu, w_down, b_down, ln2_g, ln2_b)
    xp, xs = x_prompt, x_sample
    hp_l, cp_l, sp_l, hs_l, cs_l, ss_l = [], [], [], [], [], []
    for l in range(depth):
        p = {n: v[l] for n, v in zip(names, stacked)}
        xp, xs, (hp, cp, sp), (hs, cs, ss) = _layer(
            xp, xs, state_rglru_h[l], state_rglru_conv[l], state_shortconv[l], p, alpha=alpha)
        hp_l.append(hp); cp_l.append(cp); sp_l.append(sp)
        hs_l.append(hs); cs_l.append(cs); ss_l.append(ss)
    return (xp, xs, jnp.stack(hp_l), jnp.stack(cp_l), jnp.stack(sp_l), jnp.stack(hs_l), jnp.stack(cs_l),
            jnp.stack(ss_l))
```

```python
import functools

import jax
import jax.numpy as jnp
from jax import lax
from jax.experimental import pallas as pl
from jax.experimental.pallas import tpu as pltpu

D_MODEL = 1024
LRU_HEADS = 16
LRU_BLOCK = D_MODEL // LRU_HEADS
LRU_C = 8.0
CONV_A_WIDTH = 4
CONV_B_WIDTH = 3
N_GROUPS = 7
N_EXPERTS = 32
TOP_K = 4
D_FF = D_MODEL
SWIGLU_LIMIT = 7.0
SWIGLU_ALPHA = 1.702
LN_EPS = 1e-5

GATE_TILE = 256
HEADS_PER_TILE = GATE_TILE // LRU_BLOCK
N_GATE_TILES = D_MODEL // GATE_TILE
LANES = 128
MIXER_ROWS = 512
MOE_ROWS = 256
COMBINE_ROWS = 512
VMEM_LIMIT = 58 * 1024 * 1024

_F32 = jnp.float32
_BF16 = jnp.bfloat16
_NT = (((1,), (1,)), ((), ()))


def _sigmoid(v):
    return 0.5 * jnp.tanh(0.5 * v) + 0.5


def _gelu_tanh(v):
    c = 0.7978845608028654
    return 0.5 * v * (1.0 + jnp.tanh(c * (v + 0.044715 * (v * v * v))))


def _layernorm(z, g, b):
    mu = jnp.mean(z, axis=-1, keepdims=True)
    zc = z - mu
    var = jnp.mean(zc * zc, axis=-1, keepdims=True)
    return zc * lax.rsqrt(var + LN_EPS) * g + b


def _mixer_kernel(x_ref, h0_ref, ca0_ref, cb0_ref, cnt0_ref,
                  w_in_ref, b_in_ref, wca_ref, bca_ref, wa_ref, ba_ref, wx_ref, bx_ref, lam_ref,
                  wcb_ref, w_out_ref, g1_ref, be1_ref, rwt_hi_ref, rwt_lo_ref, rb_ref, tri_ref,
                  x1_ref, idx_ref, rank_ref, grow_ref, hl_ref, ca_ref, cb_ref, cnt_ref,
                  xa_s, u_s, a_s, b_s, h_s, hst_s, cnt_s, *, ts, nb, alpha):
    i = pl.program_id(0)
    j = pl.program_id(1)
    rows = ts * nb
    ta = (CONV_A_WIDTH - 1) * nb
    tb = (CONV_B_WIDTH - 1) * nb

    @pl.when(j == 0)
    def _():
        hst_s[...] = h0_ref[...]
        xa_s[0:ta, :] = ca0_ref[...].reshape(ta, D_MODEL)
        u_s[0:tb, :] = cb0_ref[...].reshape(tb, D_MODEL)

    @pl.when((i == 0) & (j == 0))
    def _():
        cnt_s[...] = cnt0_ref[...]

    x = x_ref[...].reshape(rows, D_MODEL)
    xb = x.astype(_BF16)

    def proj(g):
        lo, hi = g * D_MODEL, (g + 1) * D_MODEL
        return jnp.dot(xb, w_in_ref[:, lo:hi], preferred_element_type=_F32) + b_in_ref[:, lo:hi]

    xa_s[ta:ta + rows, :] = proj(0)
    xc = bca_ref[...] + xa_s[0:rows, :] * wca_ref[0:1, :]
    for k in range(1, CONV_A_WIDTH):
        xc = xc + xa_s[k * nb:k * nb + rows, :] * wca_ref[k:k + 1, :]
    new_ta = xa_s[rows:rows + ta, :]
    xa_s[0:ta, :] = new_ta
    ca_ref[...] = new_ta.reshape(CONV_A_WIDTH - 1, nb, D_MODEL)

    xcb = xc.astype(_BF16)

    def block_diag(w_ref):
        return jnp.concatenate(
            [jnp.dot(xcb[:, q * GATE_TILE:(q + 1) * GATE_TILE], w_ref[q], preferred_element_type=_F32)
             for q in range(N_GATE_TILES)], axis=-1)

    r = _sigmoid(block_diag(wa_ref) + ba_ref[...])
    ig = _sigmoid(block_diag(wx_ref) + bx_ref[...])
    nlam = -lam_ref[...]
    softplus = jnp.maximum(nlam, 0.0) + jnp.log1p(jnp.exp(-jnp.abs(nlam)))
    log_a = (-LRU_C * softplus) * r
    a = jnp.exp(log_a)
    a_s[...] = a
    b_s[...] = jnp.sqrt(-jnp.tanh(log_a) * (a * a + 1.0)) * (ig * xc)

    h = hst_s[...]
    for t in range(ts):
        sl = slice(t * nb, (t + 1) * nb)
        h = a_s[sl, :] * h + b_s[sl, :]
        h_s[sl, :] = h
    hst_s[...] = h
    hl_ref[...] = h

    a_s[...] = h_s[...] * _gelu_tanh(proj(1))

    u_s[tb:tb + rows, :] = proj(3) * proj(4)
    uc = u_s[0:rows, :] * wcb_ref[0:1, :]
    for k in range(1, CONV_B_WIDTH):
        uc = uc + u_s[k * nb:k * nb + rows, :] * wcb_ref[k:k + 1, :]
    new_tb = u_s[rows:rows + tb, :]
    u_s[0:tb, :] = new_tb
    cb_ref[...] = new_tb.reshape(CONV_B_WIDTH - 1, nb, D_MODEL)
    y_b = proj(2) * uc

    merged = _sigmoid(proj(5)) * a_s[...] + _sigmoid(proj(6)) * y_b
    mixed = jnp.dot(merged.astype(_BF16), w_out_ref[...], preferred_element_type=_F32)
    x1 = _layernorm(alpha * x + mixed, g1_ref[...], be1_ref[...])
    for s in range(8):
        x1_ref[pl.ds(s, rows, stride=8), :] = x1[:, s * LANES:(s + 1) * LANES]

    x1_hi = x1.astype(_BF16)
    x1_lo = (x1 - x1_hi.astype(_F32)).astype(_BF16)
    logits = (lax.dot_general(rwt_hi_ref[...], x1_hi, _NT, preferred_element_type=_F32)
              + lax.dot_general(rwt_hi_ref[...], x1_lo, _NT, preferred_element_type=_F32)
              + lax.dot_general(rwt_lo_ref[...], x1_hi, _NT, preferred_element_type=_F32)
              + rb_ref[:, 0:1])
    e_iota = lax.broadcasted_iota(jnp.int32, (N_EXPERTS, rows), 0)
    work = logits
    vals, sels, idxs = [], [], []
    for _ in range(TOP_K):
        m = jnp.max(work, axis=0, keepdims=True)
        ik = jnp.min(jnp.where(work == m, e_iota, N_EXPERTS), axis=0, keepdims=True)
        sel = e_iota == ik
        work = jnp.where(sel, -jnp.inf, work)
        vals.append(m)
        sels.append(sel)
        idxs.append(ik)
    exps = [jnp.exp(v - vals[0]) for v in vals]
    denom = exps[0] + exps[1] + exps[2] + exps[3]
    gates = [ex / denom for ex in exps]

    onehot = jnp.zeros((N_EXPERTS, rows), _F32)
    for sel in sels:
        onehot = onehot + sel.astype(_F32)
    prefix = jnp.dot(onehot.astype(_BF16), tri_ref[...], preferred_element_type=_F32)
    pos = prefix + cnt_s[:, 0:1]
    ranks = [jnp.sum(jnp.where(sel, pos, 0.0), axis=0, keepdims=True) for sel in sels]
    new_cnt = cnt_s[...] + jnp.sum(onehot, axis=1, keepdims=True)
    cnt_s[...] = new_cnt
    cnt_ref[...] = new_cnt

    row8 = lax.broadcasted_iota(jnp.int32, (8, rows), 0)
    idx8 = jnp.zeros((8, rows), jnp.int32)
    rank8 = jnp.zeros((8, rows), jnp.int32)
    for k in range(TOP_K):
        idx8 = jnp.where(row8 == k, idxs[k], idx8)
        rank8 = jnp.where(row8 == k, ranks[k].astype(jnp.int32), rank8)
    idx_ref[...] = idx8
    rank_ref[...] = rank8

    row_l = lax.broadcasted_iota(jnp.int32, (LANES, rows), 0)
    g_t = jnp.zeros((LANES, rows), _F32)
    for k in range(TOP_K):
        g_t = jnp.where(row_l == k, gates[k], g_t)
    grow_ref[...] = jnp.transpose(g_t)


def _const_spec(shape):
    nd = len(shape)
    return pl.BlockSpec(shape, lambda i, j: (0,) * nd, pipeline_mode=pl.Buffered(1))


def _mixer_call(x4, h0, ca0, cb0, cnt0, wts, *, ts, alpha):
    nblk, seq, nb, _ = x4.shape
    nt = seq // ts
    rows = ts * nb
    total = nblk * seq * nb
    f32 = lambda *s: jax.ShapeDtypeStruct(s, _F32)
    i32 = lambda *s: jax.ShapeDtypeStruct(s, jnp.int32)
    in_specs = [
        pl.BlockSpec((None, ts, nb, D_MODEL), lambda i, j: (i, j, 0, 0)),
        pl.BlockSpec((None, nb, D_MODEL), lambda i, j: (i, 0, 0)),
        pl.BlockSpec((None, CONV_A_WIDTH - 1, nb, D_MODEL), lambda i, j: (i, 0, 0, 0)),
        pl.BlockSpec((None, CONV_B_WIDTH - 1, nb, D_MODEL), lambda i, j: (i, 0, 0, 0)),
        _const_spec(cnt0.shape),
    ] + [_const_spec(w.shape) for w in wts]
    out_shape = (
        f32(total * 8, LANES),
        i32(8, total),
        i32(8, total),
        f32(total, LANES),
        f32(nblk, nb, D_MODEL),
        f32(nblk, CONV_A_WIDTH - 1, nb, D_MODEL),
        f32(nblk, CONV_B_WIDTH - 1, nb, D_MODEL),
        f32(N_EXPERTS, LANES),
    )
    out_specs = (
        pl.BlockSpec((rows * 8, LANES), lambda i, j: (i * nt + j, 0)),
        pl.BlockSpec((8, rows), lambda i, j: (0, i * nt + j)),
        pl.BlockSpec((8, rows), lambda i, j: (0, i * nt + j)),
        pl.BlockSpec((rows, LANES), lambda i, j: (i * nt + j, 0)),
        pl.BlockSpec((None, nb, D_MODEL), lambda i, j: (i, 0, 0)),
        pl.BlockSpec((None, CONV_A_WIDTH - 1, nb, D_MODEL), lambda i, j: (i, 0, 0, 0)),
        pl.BlockSpec((None, CONV_B_WIDTH - 1, nb, D_MODEL), lambda i, j: (i, 0, 0, 0)),
        pl.BlockSpec((N_EXPERTS, LANES), lambda i, j: (0, 0)),
    )
    scratch = [
        pltpu.VMEM((rows + (CONV_A_WIDTH - 1) * nb, D_MODEL), _F32),
        pltpu.VMEM((rows + (CONV_B_WIDTH - 1) * nb, D_MODEL), _F32),
        pltpu.VMEM((rows, D_MODEL), _F32),
        pltpu.VMEM((rows, D_MODEL), _F32),
        pltpu.VMEM((rows, D_MODEL), _F32),
        pltpu.VMEM((nb, D_MODEL), _F32),
        pltpu.VMEM((N_EXPERTS, LANES), _F32),
    ]
    return pl.pallas_call(
        functools.partial(_mixer_kernel, ts=ts, nb=nb, alpha=alpha),
        grid=(nblk, nt),
        in_specs=in_specs,
        out_specs=out_specs,
        out_shape=out_shape,
        scratch_shapes=scratch,
        compiler_params=pltpu.CompilerParams(
            dimension_semantics=("arbitrary", "arbitrary"), vmem_limit_bytes=VMEM_LIMIT),
        name="mixer",
    )(x4, h0, ca0, cb0, cnt0, *wts)


def _invmap_kernel(dest_ref, lo_ref, hi_ref, inv_ref, *, total, n_rows):
    n_slots = total * TOP_K

    def spare(row):
        return n_slots + (row // MOE_ROWS % 2) * MOE_ROWS + row % MOE_ROWS

    for r in range(MOE_ROWS):
        inv_ref[r] = n_slots + 2 * MOE_ROWS + r

    def fill(lo, hi):
        def body(row, c):
            inv_ref[MOE_ROWS + row] = spare(row)
            return c
        lax.fori_loop(lo, hi, body, 0)

    def per_expert(e, c):
        fill(lo_ref[e], hi_ref[e])
        return c
    lax.fori_loop(0, N_EXPERTS, per_expert, 0)
    fill(hi_ref[N_EXPERTS - 1], n_rows)

    unroll = 8
    for k in range(TOP_K):
        def body(it, c, k=k):
            for u in range(unroll):
                tok = it * unroll + u
                inv_ref[MOE_ROWS + dest_ref[k * total + tok]] = tok * TOP_K + k
            return c
        lax.fori_loop(0, total // unroll, body, 0)


def _invmap_call(dest, pad_lo, pad_hi, *, total, n_rows):
    smem = pl.BlockSpec(memory_space=pltpu.SMEM)
    return pl.pallas_call(
        functools.partial(_invmap_kernel, total=total, n_rows=n_rows),
        in_specs=[smem, smem, smem],
        out_specs=smem,
        out_shape=jax.ShapeDtypeStruct((n_rows + MOE_ROWS,), jnp.int32),
        name="invmap",
    )(dest, pad_lo, pad_hi)


def _moe_kernel(blk_e_ref, n_used_ref, inv_ref,
                x1_hbm, wgu_ref, bgu_ref, wd_ref, bd_ref,
                ytok_hbm,
                xbuf, ybuf, wgu_s, wd_s, gsem, ssem, *, n_slots):
    b = pl.program_id(0)
    nblk = pl.num_programs(0)
    n_used = n_used_ref[0]
    slot = b % 3
    nxt = (b + 2) % 3

    def gather_copy(v, s, r):
        return pltpu.make_async_copy(
            x1_hbm.at[v >> 2], xbuf.at[s, pl.ds(r * 8, 8), :], gsem.at[s])

    def scatter_copy(v, s, r):
        return pltpu.make_async_copy(
            ybuf.at[s, pl.ds(r * 8, 8), :], ytok_hbm.at[v], ssem.at[s])

    def start_gather(blk, s):
        base = (blk + 1) * MOE_ROWS
        for r in range(MOE_ROWS):
            gather_copy(inv_ref[base + r], s, r).start()

    def wait_rows(copy_fn, s):
        for r in range(MOE_ROWS):
            copy_fn(0, s, r).wait()

    @pl.when(b == 0)
    def _():
        ybuf[...] = jnp.zeros(ybuf.shape, _F32)
        for s in range(2):
            for r in range(MOE_ROWS):
                scatter_copy(n_slots + s * MOE_ROWS + r, s, r).start()
        start_gather(0, 0)
        start_gather(1, 1)

    first_of_expert = (b == 0) | (blk_e_ref[b] != blk_e_ref[jnp.maximum(b - 1, 0)])

    @pl.when(first_of_expert & (b < n_used))
    def _():
        wgu_s[...] = wgu_ref[...].astype(_BF16)
        wd_s[...] = wd_ref[...].astype(_BF16)

    @pl.when(b <= n_used)
    def _():
        wait_rows(gather_copy, slot)
        wait_rows(scatter_copy, slot)
        x = jnp.concatenate(
            [xbuf[slot, pl.ds(s, MOE_ROWS, stride=8), :] for s in range(8)], axis=-1)
        gu = jnp.dot(x.astype(_BF16), wgu_s[...], preferred_element_type=_F32) + bgu_ref[...]
        gate = jnp.minimum(gu[:, :D_FF], SWIGLU_LIMIT)
        up = jnp.clip(gu[:, D_FF:], -SWIGLU_LIMIT, SWIGLU_LIMIT)
        hmid = (up + 1.0) * (gate * _sigmoid(SWIGLU_ALPHA * gate))
        y = jnp.dot(hmid.astype(_BF16), wd_s[...], preferred_element_type=_F32) + bd_ref[...]
        for s in range(8):
            ybuf[slot, pl.ds(s, MOE_ROWS, stride=8), :] = y[:, s * LANES:(s + 1) * LANES]
        start_gather(jnp.minimum(b + 2, nblk - 1), nxt)
        base = b * MOE_ROWS
        for r in range(MOE_ROWS):
            scatter_copy(inv_ref[base + r], nxt, r).start()

    @pl.when(b == nblk - 1)
    def _():
        for d in (1, 2):
            wait_rows(gather_copy, (n_used + d) % 3)
            wait_rows(scatter_copy, (n_used + d) % 3)


def _moe_call(blk_e, n_used, inv, x1t, w_gu, b_gu, w_down, b_down, *, n_blocks, n_slots):
    grid_spec = pltpu.PrefetchScalarGridSpec(
        num_scalar_prefetch=3,
        grid=(n_blocks,),
        in_specs=[
            pl.BlockSpec(memory_space=pl.ANY),
            pl.BlockSpec((None, D_MODEL, 2 * D_FF), lambda b, be, nu, iv: (be[b], 0, 0)),
            pl.BlockSpec((None, 1, 2 * D_FF), lambda b, be, nu, iv: (be[b], 0, 0)),
            pl.BlockSpec((None, D_FF, D_MODEL), lambda b, be, nu, iv: (be[b], 0, 0)),
            pl.BlockSpec((None, 1, D_MODEL), lambda b, be, nu, iv: (be[b], 0, 0)),
        ],
        out_specs=pl.BlockSpec(memory_space=pl.ANY),
        scratch_shapes=[
            pltpu.VMEM((3, MOE_ROWS * 8, LANES), _F32),
            pltpu.VMEM((3, MOE_ROWS * 8, LANES), _F32),
            pltpu.VMEM((D_MODEL, 2 * D_FF), _BF16),
            pltpu.VMEM((D_FF, D_MODEL), _BF16),
            pltpu.SemaphoreType.DMA((3,)),
            pltpu.SemaphoreType.DMA((3,)),
        ],
    )
    return pl.pallas_call(
        functools.partial(_moe_kernel, n_slots=n_slots),
        grid_spec=grid_spec,
        out_shape=jax.ShapeDtypeStruct((n_slots + 3 * MOE_ROWS, 8, LANES), _F32),
        compiler_params=pltpu.CompilerParams(
            dimension_semantics=("arbitrary",), vmem_limit_bytes=VMEM_LIMIT),
        name="moe",
    )(blk_e, n_used, inv, x1t, w_gu, b_gu, w_down, b_down)


def _combine_kernel(x1_ref, g_ref, y0_ref, y1_ref, y2_ref, y3_ref, g2_ref, be2_ref, out_ref, *, alpha):
    g = g_ref[...]
    rows = g.shape[0]

    def token_major(ref):
        return jnp.concatenate([ref[pl.ds(s, rows, stride=8), :] for s in range(8)], axis=-1)

    ys = [token_major(r.reshape(rows * 8, LANES)) for r in (y0_ref, y1_ref, y2_ref, y3_ref)]
    moe = g[:, 0:1] * ys[0] + g[:, 1:2] * ys[1] + g[:, 2:3] * ys[2] + g[:, 3:4] * ys[3]
    out_ref[...] = _layernorm(alpha * token_major(x1_ref) + moe, g2_ref[...], be2_ref[...])


def _combine_call(x1t, grow, ytok, g2, be2, *, total, alpha):
    nt = total // COMBINE_ROWS
    plane = lambda k: pl.BlockSpec((COMBINE_ROWS, None, 8, LANES), lambda i, k=k: (i, k, 0, 0))
    vec = pl.BlockSpec((1, D_MODEL), lambda i: (0, 0))
    ytok4 = ytok.reshape(-1, TOP_K, 8, LANES)
    return pl.pallas_call(
        functools.partial(_combine_kernel, alpha=alpha),
        grid=(nt,),
        in_specs=[pl.BlockSpec((COMBINE_ROWS * 8, LANES), lambda i: (i, 0)),
                  pl.BlockSpec((COMBINE_ROWS, LANES), lambda i: (i, 0)),
                  plane(0), plane(1), plane(2), plane(3), vec, vec],
        out_specs=pl.BlockSpec((COMBINE_ROWS, D_MODEL), lambda i: (i, 0)),
        out_shape=jax.ShapeDtypeStruct((total, D_MODEL), _F32),
        compiler_params=pltpu.CompilerParams(
            dimension_semantics=("arbitrary",), vmem_limit_bytes=VMEM_LIMIT),
        name="combine",
    )(x1t, grow, ytok4, ytok4, ytok4, ytok4, g2, be2)


def _pack_block_diag(w):
    w = w.reshape(N_GATE_TILES, HEADS_PER_TILE, LRU_BLOCK, LRU_BLOCK)
    eye = jnp.eye(HEADS_PER_TILE, dtype=w.dtype)
    t = jnp.einsum("qhij,hg->qhigj", w, eye)
    return t.reshape(N_GATE_TILES, GATE_TILE, GATE_TILE)


def _layer(xp, xs, h_s0, ca_s0, cb_s0, p, *, alpha):
    bp, sp, _ = xp.shape
    bs, ss, _ = xs.shape
    tp, tsm = bp * sp, bs * ss
    total = tp + tsm
    row2 = lambda v: v.reshape(1, -1)

    rwt = jnp.transpose(p["router_w"])
    rwt_hi = rwt.astype(_BF16)
    rwt_lo = (rwt - rwt_hi.astype(_F32)).astype(_BF16)
    ii = jnp.arange(MIXER_ROWS)
    tri = (ii[:, None] < ii[None, :]).astype(_BF16)
    wts = (
        p["w_in"].astype(_BF16), row2(p["b_in"]), p["conv_a_w"], row2(p["conv_a_b"]),
        _pack_block_diag(p["lru_wa"]).astype(_BF16), row2(p["lru_ba"]),
        _pack_block_diag(p["lru_wx"]).astype(_BF16), row2(p["lru_bx"]), row2(p["lru_lambda"]),
        p["conv_b_w"], p["w_out"].astype(_BF16), row2(p["ln1_g"]), row2(p["ln1_b"]),
        rwt_hi, rwt_lo, jnp.broadcast_to(p["router_b"][:, None], (N_EXPERTS, LANES)), tri,
    )

    ts_p = MIXER_ROWS // bp
    xp4 = jnp.transpose(xp, (1, 0, 2))[None]
    zeros = lambda *s: jnp.zeros(s, _F32)
    outs_p = _mixer_call(xp4, zeros(1, bp, D_MODEL), zeros(1, CONV_A_WIDTH - 1, bp, D_MODEL),
                         zeros(1, CONV_B_WIDTH - 1, bp, D_MODEL), zeros(N_EXPERTS, LANES), wts,
                         ts=ts_p, alpha=alpha)
    nb_s = MIXER_ROWS // ss
    nblk_s = bs // nb_s
    xs4 = jnp.transpose(xs.reshape(nblk_s, nb_s, ss, D_MODEL), (0, 2, 1, 3))
    h0 = h_s0.reshape(nblk_s, nb_s, D_MODEL)
    ca0 = jnp.transpose(ca_s0.reshape(nblk_s, nb_s, CONV_A_WIDTH - 1, D_MODEL), (0, 2, 1, 3))
    cb0 = jnp.transpose(cb_s0.reshape(nblk_s, nb_s, CONV_B_WIDTH - 1, D_MODEL), (0, 2, 1, 3))
    outs_s = _mixer_call(xs4, h0, ca0, cb0, outs_p[7], wts, ts=ss, alpha=alpha)

    n_assign = total * TOP_K
    n_blocks = -(-n_assign // MOE_ROWS) + N_EXPERTS + 1
    n_rows = n_blocks * MOE_ROWS
    n_spare = 3 * MOE_ROWS
    x1t = jnp.concatenate([outs_p[0], outs_s[0], jnp.zeros((n_spare // TOP_K * 8, LANES), _F32)], axis=0)
    idx = jnp.concatenate([outs_p[1][:TOP_K], outs_s[1][:TOP_K]], axis=1)
    rank = jnp.concatenate([outs_p[2][:TOP_K], outs_s[2][:TOP_K]], axis=1)
    grow = jnp.concatenate([outs_p[3], outs_s[3]], axis=0)
    counts = outs_s[7][:, 0].astype(jnp.int32)

    padded = (counts + MOE_ROWS - 1) // MOE_ROWS * MOE_ROWS
    end_pad = jnp.cumsum(padded)
    start_pad = end_pad - padded
    experts = jnp.arange(N_EXPERTS, dtype=jnp.int32)
    start_of = jnp.sum(jnp.where(idx[:, :, None] == experts, start_pad, 0), axis=-1)
    dest = (start_of + rank).reshape(n_assign)
    blk_start = jnp.arange(n_blocks, dtype=jnp.int32) * MOE_ROWS
    blk_e = jnp.minimum(jnp.sum((blk_start[:, None] >= end_pad[None, :]).astype(jnp.int32), axis=1),
                        N_EXPERTS - 1)
    n_used = (end_pad[-1] // MOE_ROWS).astype(jnp.int32).reshape(1)

    inv = _invmap_call(dest, start_pad + counts, end_pad, total=total, n_rows=n_rows)
    ytok = _moe_call(blk_e, n_used, inv, x1t.reshape(-1, 8, LANES),
                     p["w_gu"], p["b_gu"][:, None, :], p["w_down"], p["b_down"][:, None, :],
                     n_blocks=n_blocks, n_slots=n_assign)
    y = _combine_call(x1t, grow, ytok, row2(p["ln2_g"]), row2(p["ln2_b"]), total=total, alpha=alpha)

    yp = jnp.transpose(y[:tp].reshape(sp, bp, D_MODEL), (1, 0, 2))
    ys = jnp.transpose(y[tp:].reshape(nblk_s, ss, nb_s, D_MODEL), (0, 2, 1, 3)).reshape(bs, ss, D_MODEL)

    def batch_major(v, nblk):
        return jnp.transpose(v, (0, 2, 1, 3)).reshape(nblk * v.shape[2], v.shape[1], D_MODEL)

    states_p = (outs_p[4].reshape(bp, D_MODEL), batch_major(outs_p[5], 1), batch_major(outs_p[6], 1))
    states_s = (outs_s[4].reshape(bs, D_MODEL), batch_major(outs_s[5], nblk_s), batch_major(outs_s[6], nblk_s))
    return yp, ys, states_p, states_s


def kernel(x_prompt, x_sample, state_rglru_h, state_rglru_conv, state_shortconv, w_in, b_in, conv_a_w, conv_a_b, lru_wa, lru_ba, lru_wx, lru_bx, lru_lambda, conv_b_w, w_out, ln1_g, ln1_b, router_w, router_b, w_gu, b_gu, w_down, b_down, ln2_g, ln2_b):
    depth = w_in.shape[0]
    alpha = (2.0 * depth) ** 0.25
    names = ("w_in", "b_in", "conv_a_w", "conv_a_b", "lru_wa", "lru_ba", "lru_wx", "lru_bx", "lru_lambda",
             "conv_b_w", "w_out", "ln1_g", "ln1_b", "router_w", "router_b", "w_gu", "b_gu", "w_down",
             "b_down", "ln2_g", "ln2_b")
    stacked = (w_in, b_in, conv_a_w, conv_a_b, lru_wa, lru_ba, lru_wx, lru_bx, lru_lambda, conv_b_w, w_out,
               ln1_g, ln1_b, router_w, router_b, w_gu, b_gu, w_down, b_down, ln2_g, ln2_b)
    xp, xs = x_prompt, x_sample
    hp_l, cp_l, sp_l, hs_l, cs_l, ss_l = [], [], [], [], [], []
    for l in range(depth):
        p = {n: v[l] for n, v in zip(names, stacked)}
        xp, xs, (hp, cp, sp), (hs, cs, ss) = _layer(
            xp, xs, state_rglru_h[l], state_rglru_conv[l], state_shortconv[l], p, alpha=alpha)
        hp_l.append(hp); cp_l.append(cp); sp_l.append(sp)
        hs_l.append(hs); cs_l.append(cs); ss_l.append(ss)
    return (xp, xs, jnp.stack(hp_l), jnp.stack(cp_l), jnp.stack(sp_l), jnp.stack(hs_l), jnp.stack(cs_l),
            jnp.stack(ss_l))
```

```python
import functools

import jax
import jax.numpy as jnp
from jax import lax
from jax.experimental import pallas as pl
from jax.experimental.pallas import tpu as pltpu

D_MODEL = 1024
LRU_HEADS = 16
LRU_BLOCK = D_MODEL // LRU_HEADS
LRU_C = 8.0
CONV_A_WIDTH = 4
CONV_B_WIDTH = 3
N_GROUPS = 7
N_EXPERTS = 32
TOP_K = 4
D_FF = D_MODEL
SWIGLU_LIMIT = 7.0
SWIGLU_ALPHA = 1.702
LN_EPS = 1e-5

GATE_TILE = 256
HEADS_PER_TILE = GATE_TILE // LRU_BLOCK
N_GATE_TILES = D_MODEL // GATE_TILE
LANES = 128
MIXER_ROWS = 512
MOE_ROWS = 256
COMBINE_ROWS = 512
VMEM_LIMIT = 58 * 1024 * 1024

_F32 = jnp.float32
_BF16 = jnp.bfloat16
_NT = (((1,), (1,)), ((), ()))


def _sigmoid(v):
    return 0.5 * jnp.tanh(0.5 * v) + 0.5


def _gelu_tanh(v):
    c = 0.7978845608028654
    return 0.5 * v * (1.0 + jnp.tanh(c * (v + 0.044715 * (v * v * v))))


def _layernorm(z, g, b):
    mu = jnp.mean(z, axis=-1, keepdims=True)
    zc = z - mu
    var = jnp.mean(zc * zc, axis=-1, keepdims=True)
    return zc * lax.rsqrt(var + LN_EPS) * g + b


def _mixer_kernel(x_ref, h0_ref, ca0_ref, cb0_ref, cnt0_ref,
                  w_in_ref, b_in_ref, wca_ref, bca_ref, wa_ref, ba_ref, wx_ref, bx_ref, lam_ref,
                  wcb_ref, w_out_ref, g1_ref, be1_ref, rwt_hi_ref, rwt_lo_ref, rb_ref, tri_ref,
                  x1_ref, idx_ref, rank_ref, grow_ref, hl_ref, ca_ref, cb_ref, cnt_ref,
                  xa_s, u_s, a_s, b_s, h_s, hst_s, cnt_s, *, ts, nb, alpha):
    i = pl.program_id(0)
    j = pl.program_id(1)
    rows = ts * nb
    ta = (CONV_A_WIDTH - 1) * nb
    tb = (CONV_B_WIDTH - 1) * nb

    @pl.when(j == 0)
    def _():
        hst_s[...] = h0_ref[...]
        xa_s[0:ta, :] = ca0_ref[...].reshape(ta, D_MODEL)
        u_s[0:tb, :] = cb0_ref[...].reshape(tb, D_MODEL)

    @pl.when((i == 0) & (j == 0))
    def _():
        cnt_s[...] = cnt0_ref[...]

    x = x_ref[...].reshape(rows, D_MODEL)
    xb = x.astype(_BF16)

    def proj(g):
        lo, hi = g * D_MODEL, (g + 1) * D_MODEL
        return jnp.dot(xb, w_in_ref[:, lo:hi], preferred_element_type=_F32) + b_in_ref[:, lo:hi]

    xa_s[ta:ta + rows, :] = proj(0)
    xc = bca_ref[...] + xa_s[0:rows, :] * wca_ref[0:1, :]
    for k in range(1, CONV_A_WIDTH):
        xc = xc + xa_s[k * nb:k * nb + rows, :] * wca_ref[k:k + 1, :]
    new_ta = xa_s[rows:rows + ta, :]
    xa_s[0:ta, :] = new_ta
    ca_ref[...] = new_ta.reshape(CONV_A_WIDTH - 1, nb, D_MODEL)

    xcb = xc.astype(_BF16)

    def block_diag(w_ref):
        return jnp.concatenate(
            [jnp.dot(xcb[:, q * GATE_TILE:(q + 1) * GATE_TILE], w_ref[q], preferred_element_type=_F32)
             for q in range(N_GATE_TILES)], axis=-1)

    r = _sigmoid(block_diag(wa_ref) + ba_ref[...])
    ig = _sigmoid(block_diag(wx_ref) + bx_ref[...])
    nlam = -lam_ref[...]
    softplus = jnp.maximum(nlam, 0.0) + jnp.log1p(jnp.exp(-jnp.abs(nlam)))
    log_a = (-LRU_C * softplus) * r
    a = jnp.exp(log_a)
    a_s[...] = a
    b_s[...] = jnp.sqrt(-jnp.tanh(log_a) * (a * a + 1.0)) * (ig * xc)

    h = hst_s[...]
    for t in range(ts):
        sl = slice(t * nb, (t + 1) * nb)
        h = a_s[sl, :] * h + b_s[sl, :]
        h_s[sl, :] = h
    hst_s[...] = h
    hl_ref[...] = h

    a_s[...] = h_s[...] * _gelu_tanh(proj(1))

    u_s[tb:tb + rows, :] = proj(3) * proj(4)
    uc = u_s[0:rows, :] * wcb_ref[0:1, :]
    for k in range(1, CONV_B_WIDTH):
        uc = uc + u_s[k * nb:k * nb + rows, :] * wcb_ref[k:k + 1, :]
    new_tb = u_s[rows:rows + tb, :]
    u_s[0:tb, :] = new_tb
    cb_ref[...] = new_tb.reshape(CONV_B_WIDTH - 1, nb, D_MODEL)
    y_b = proj(2) * uc

    merged = _sigmoid(proj(5)) * a_s[...] + _sigmoid(proj(6)) * y_b
    mixed = jnp.dot(merged.astype(_BF16), w_out_ref[...], preferred_element_type=_F32)
    x1 = _layernorm(alpha * x + mixed, g1_ref[...], be1_ref[...])
    for s in range(8):
        x1_ref[pl.ds(s, rows, stride=8), :] = x1[:, s * LANES:(s + 1) * LANES]

    x1_hi = x1.astype(_BF16)
    x1_lo = (x1 - x1_hi.astype(_F32)).astype(_BF16)
    logits = (lax.dot_general(rwt_hi_ref[...], x1_hi, _NT, preferred_element_type=_F32)
              + lax.dot_general(rwt_hi_ref[...], x1_lo, _NT, preferred_element_type=_F32)
              + lax.dot_general(rwt_lo_ref[...], x1_hi, _NT, preferred_element_type=_F32)
              + rb_ref[:, 0:1])
    e_iota = lax.broadcasted_iota(jnp.int32, (N_EXPERTS, rows), 0)
    work = logits
    vals, sels, idxs = [], [], []
    for _ in range(TOP_K):
        m = jnp.max(work, axis=0, keepdims=True)
        ik = jnp.min(jnp.where(work == m, e_iota, N_EXPERTS), axis=0, keepdims=True)
        sel = e_iota == ik
        work = jnp.where(sel, -jnp.inf, work)
        vals.append(m)
        sels.append(sel)
        idxs.append(ik)
    exps = [jnp.exp(v - vals[0]) for v in vals]
    denom = exps[0] + exps[1] + exps[2] + exps[3]
    gates = [ex / denom for ex in exps]

    onehot = jnp.zeros((N_EXPERTS, rows), _F32)
    for sel in sels:
        onehot = onehot + sel.astype(_F32)
    prefix = jnp.dot(onehot.astype(_BF16), tri_ref[...], preferred_element_type=_F32)
    pos = prefix + cnt_s[:, 0:1]
    ranks = [jnp.sum(jnp.where(sel, pos, 0.0), axis=0, keepdims=True) for sel in sels]
    new_cnt = cnt_s[...] + jnp.sum(onehot, axis=1, keepdims=True)
    cnt_s[...] = new_cnt
    cnt_ref[...] = new_cnt

    row8 = lax.broadcasted_iota(jnp.int32, (8, rows), 0)
    idx8 = jnp.zeros((8, rows), jnp.int32)
    rank8 = jnp.zeros((8, rows), jnp.int32)
    for k in range(TOP_K):
        idx8 = jnp.where(row8 == k, idxs[k], idx8)
        rank8 = jnp.where(row8 == k, ranks[k].astype(jnp.int32), rank8)
    idx_ref[...] = idx8
    rank_ref[...] = rank8

    row_l = lax.broadcasted_iota(jnp.int32, (LANES, rows), 0)
    g_t = jnp.zeros((LANES, rows), _F32)
    for k in range(TOP_K):
        g_t = jnp.where(row_l == k, gates[k], g_t)
    grow_ref[...] = jnp.transpose(g_t)


def _const_spec(shape):
    nd = len(shape)
    return pl.BlockSpec(shape, lambda i, j: (0,) * nd, pipeline_mode=pl.Buffered(1))


def _mixer_call(x4, h0, ca0, cb0, cnt0, wts, *, ts, alpha):
    nblk, seq, nb, _ = x4.shape
    nt = seq // ts
    rows = ts * nb
    total = nblk * seq * nb
    f32 = lambda *s: jax.ShapeDtypeStruct(s, _F32)
    i32 = lambda *s: jax.ShapeDtypeStruct(s, jnp.int32)
    in_specs = [
        pl.BlockSpec((None, ts, nb, D_MODEL), lambda i, j: (i, j, 0, 0)),
        pl.BlockSpec((None, nb, D_MODEL), lambda i, j: (i, 0, 0)),
        pl.BlockSpec((None, CONV_A_WIDTH - 1, nb, D_MODEL), lambda i, j: (i, 0, 0, 0)),
        pl.BlockSpec((None, CONV_B_WIDTH - 1, nb, D_MODEL), lambda i, j: (i, 0, 0, 0)),
        _const_spec(cnt0.shape),
    ] + [_const_spec(w.shape) for w in wts]
    out_shape = (
        f32(total * 8, LANES),
        i32(8, total),
        i32(8, total),
        f32(total, LANES),
        f32(nblk, nb, D_MODEL),
        f32(nblk, CONV_A_WIDTH - 1, nb, D_MODEL),
        f32(nblk, CONV_B_WIDTH - 1, nb, D_MODEL),
        f32(N_EXPERTS, LANES),
    )
    out_specs = (
        pl.BlockSpec((rows * 8, LANES), lambda i, j: (i * nt + j, 0)),
        pl.BlockSpec((8, rows), lambda i, j: (0, i * nt + j)),
        pl.BlockSpec((8, rows), lambda i, j: (0, i * nt + j)),
        pl.BlockSpec((rows, LANES), lambda i, j: (i * nt + j, 0)),
        pl.BlockSpec((None, nb, D_MODEL), lambda i, j: (i, 0, 0)),
        pl.BlockSpec((None, CONV_A_WIDTH - 1, nb, D_MODEL), lambda i, j: (i, 0, 0, 0)),
        pl.BlockSpec((None, CONV_B_WIDTH - 1, nb, D_MODEL), lambda i, j: (i, 0, 0, 0)),
        pl.BlockSpec((N_EXPERTS, LANES), lambda i, j: (0, 0)),
    )
    scratch = [
        pltpu.VMEM((rows + (CONV_A_WIDTH - 1) * nb, D_MODEL), _F32),
        pltpu.VMEM((rows + (CONV_B_WIDTH - 1) * nb, D_MODEL), _F32),
        pltpu.VMEM((rows, D_MODEL), _F32),
        pltpu.VMEM((rows, D_MODEL), _F32),
        pltpu.VMEM((rows, D_MODEL), _F32),
        pltpu.VMEM((nb, D_MODEL), _F32),
        pltpu.VMEM((N_EXPERTS, LANES), _F32),
    ]
    return pl.pallas_call(
        functools.partial(_mixer_kernel, ts=ts, nb=nb, alpha=alpha),
        grid=(nblk, nt),
        in_specs=in_specs,
        out_specs=out_specs,
        out_shape=out_shape,
        scratch_shapes=scratch,
        compiler_params=pltpu.CompilerParams(
            dimension_semantics=("arbitrary", "arbitrary"), vmem_limit_bytes=VMEM_LIMIT),
        name="mixer",
    )(x4, h0, ca0, cb0, cnt0, *wts)


def _invmap_kernel(dest_ref, lo_ref, hi_ref, inv_ref, *, total, n_rows):
    n_slots = total * TOP_K

    for r in range(MOE_ROWS):
        inv_ref[r] = n_slots + 2 * MOE_ROWS + r

    def fill(lo, hi):
        def body(row, c):
            inv_ref[MOE_ROWS + row] = n_slots + (row & (2 * MOE_ROWS - 1))
            return c
        lax.fori_loop(lo, hi, body, 0)

    def per_expert(e, c):
        fill(lo_ref[e], hi_ref[e])
        return c
    lax.fori_loop(0, N_EXPERTS, per_expert, 0)
    fill(hi_ref[N_EXPERTS - 1], n_rows)

    unroll = 16
    for k in range(TOP_K):
        def body(it, c, k=k):
            base = it * unroll
            rows = [dest_ref[k * total + base + u] for u in range(unroll)]
            val = base * TOP_K + k
            for u in range(unroll):
                inv_ref[MOE_ROWS + rows[u]] = val + u * TOP_K
            return c
        lax.fori_loop(0, total // unroll, body, 0)


def _invmap_call(dest, pad_lo, pad_hi, *, total, n_rows):
    smem = pl.BlockSpec(memory_space=pltpu.SMEM)
    return pl.pallas_call(
        functools.partial(_invmap_kernel, total=total, n_rows=n_rows),
        in_specs=[smem, smem, smem],
        out_specs=smem,
        out_shape=jax.ShapeDtypeStruct((n_rows + MOE_ROWS,), jnp.int32),
        name="invmap",
    )(dest, pad_lo, pad_hi)


def _moe_kernel(blk_e_ref, n_used_ref, inv_ref,
                x1_hbm, wgu_ref, bgu_ref, wd_ref, bd_ref,
                ytok_hbm,
                xb0, xb1, xb2, yb0, yb1, yb2, wgu_s, wd_s, gsem, ssem, *, n_slots):
    b = pl.program_id(0)
    nblk = pl.num_programs(0)
    n_used = n_used_ref[0]
    xbufs = (xb0, xb1, xb2)
    ybufs = (yb0, yb1, yb2)

    def gather_copy(v, s, r):
        return pltpu.make_async_copy(
            x1_hbm.at[v >> 2], xbufs[s].at[pl.ds(r * 8, 8), :], gsem.at[s])

    def scatter_copy(v, s, r):
        return pltpu.make_async_copy(
            ybufs[s].at[pl.ds(r * 8, 8), :], ytok_hbm.at[v], ssem.at[s])

    def start_gather(blk, s):
        base = (blk + 1) * MOE_ROWS
        for r in range(MOE_ROWS):
            gather_copy(inv_ref[base + r], s, r).start()

    def wait_rows(copy_fn, s):
        for r in range(MOE_ROWS):
            copy_fn(0, s, r).wait()

    @pl.when(b == 0)
    def _():
        for yb in ybufs:
            yb[...] = jnp.zeros(yb.shape, _F32)
        for s in range(2):
            for r in range(MOE_ROWS):
                scatter_copy(n_slots + s * MOE_ROWS + r, s, r).start()
        start_gather(0, 0)
        start_gather(1, 1)

    first_of_expert = (b == 0) | (blk_e_ref[b] != blk_e_ref[jnp.maximum(b - 1, 0)])

    @pl.when(first_of_expert & (b < n_used))
    def _():
        wgu_s[...] = wgu_ref[...].astype(_BF16)
        wd_s[...] = wd_ref[...].astype(_BF16)

    def run_block(s):
        nxt = (s + 2) % 3
        wait_rows(gather_copy, s)
        wait_rows(scatter_copy, s)
        x = jnp.concatenate(
            [xbufs[s][pl.ds(q, MOE_ROWS, stride=8), :] for q in range(8)], axis=-1)
        start_gather(jnp.minimum(b + 2, nblk - 1), nxt)
        base = b * MOE_ROWS
        for r in range(MOE_ROWS):
            scatter_copy(inv_ref[base + r], nxt, r).start()
        gu = jnp.dot(x.astype(_BF16), wgu_s[...], preferred_element_type=_F32) + bgu_ref[...]
        gate = jnp.minimum(gu[:, :D_FF], SWIGLU_LIMIT)
        up = jnp.clip(gu[:, D_FF:], -SWIGLU_LIMIT, SWIGLU_LIMIT)
        hmid = (up + 1.0) * (gate * _sigmoid(SWIGLU_ALPHA * gate))
        y = jnp.dot(hmid.astype(_BF16), wd_s[...], preferred_element_type=_F32) + bd_ref[...]
        for q in range(8):
            ybufs[s][pl.ds(q, MOE_ROWS, stride=8), :] = y[:, q * LANES:(q + 1) * LANES]

    for s in range(3):
        pl.when((b <= n_used) & (b % 3 == s))(functools.partial(run_block, s))

    for d in (1, 2):
        for s in range(3):
            @pl.when((b == nblk - 1) & ((n_used + d) % 3 == s))
            def _():
                wait_rows(gather_copy, s)
                wait_rows(scatter_copy, s)


def _moe_call(blk_e, n_used, inv, x1t, w_gu, b_gu, w_down, b_down, *, n_blocks, n_slots):
    grid_spec = pltpu.PrefetchScalarGridSpec(
        num_scalar_prefetch=3,
        grid=(n_blocks,),
        in_specs=[
            pl.BlockSpec(memory_space=pl.ANY),
            pl.BlockSpec((None, D_MODEL, 2 * D_FF), lambda b, be, nu, iv: (be[b], 0, 0)),
            pl.BlockSpec((None, 1, 2 * D_FF), lambda b, be, nu, iv: (be[b], 0, 0)),
            pl.BlockSpec((None, D_FF, D_MODEL), lambda b, be, nu, iv: (be[b], 0, 0)),
            pl.BlockSpec((None, 1, D_MODEL), lambda b, be, nu, iv: (be[b], 0, 0)),
        ],
        out_specs=pl.BlockSpec(memory_space=pl.ANY),
        scratch_shapes=[pltpu.VMEM((MOE_ROWS * 8, LANES), _F32)] * 6 + [
            pltpu.VMEM((D_MODEL, 2 * D_FF), _BF16),
            pltpu.VMEM((D_FF, D_MODEL), _BF16),
            pltpu.SemaphoreType.DMA((3,)),
            pltpu.SemaphoreType.DMA((3,)),
        ],
    )
    return pl.pallas_call(
        functools.partial(_moe_kernel, n_slots=n_slots),
        grid_spec=grid_spec,
        out_shape=jax.ShapeDtypeStruct((n_slots + 3 * MOE_ROWS, 8, LANES), _F32),
        compiler_params=pltpu.CompilerParams(
            dimension_semantics=("arbitrary",), vmem_limit_bytes=VMEM_LIMIT),
        name="moe",
    )(blk_e, n_used, inv, x1t, w_gu, b_gu, w_down, b_down)


def _combine_kernel(x1_ref, g_ref, y0_ref, y1_ref, y2_ref, y3_ref, g2_ref, be2_ref, out_ref, *, alpha):
    g = g_ref[...]
    rows = g.shape[0]

    def token_major(ref):
        return jnp.concatenate([ref[pl.ds(s, rows, stride=8), :] for s in range(8)], axis=-1)

    ys = [token_major(r.reshape(rows * 8, LANES)) for r in (y0_ref, y1_ref, y2_ref, y3_ref)]
    moe = g[:, 0:1] * ys[0] + g[:, 1:2] * ys[1] + g[:, 2:3] * ys[2] + g[:, 3:4] * ys[3]
    out_ref[...] = _layernorm(alpha * token_major(x1_ref) + moe, g2_ref[...], be2_ref[...])


def _combine_call(x1t, grow, ytok, g2, be2, *, total, alpha):
    nt = total // COMBINE_ROWS
    plane = lambda k: pl.BlockSpec((COMBINE_ROWS, None, 8, LANES), lambda i, k=k: (i, k, 0, 0))
    vec = pl.BlockSpec((1, D_MODEL), lambda i: (0, 0))
    ytok4 = ytok.reshape(-1, TOP_K, 8, LANES)
    return pl.pallas_call(
        functools.partial(_combine_kernel, alpha=alpha),
        grid=(nt,),
        in_specs=[pl.BlockSpec((COMBINE_ROWS * 8, LANES), lambda i: (i, 0)),
                  pl.BlockSpec((COMBINE_ROWS, LANES), lambda i: (i, 0)),
                  plane(0), plane(1), plane(2), plane(3), vec, vec],
        out_specs=pl.BlockSpec((COMBINE_ROWS, D_MODEL), lambda i: (i, 0)),
        out_shape=jax.ShapeDtypeStruct((total, D_MODEL), _F32),
        compiler_params=pltpu.CompilerParams(
            dimension_semantics=("arbitrary",), vmem_limit_bytes=VMEM_LIMIT),
        name="combine",
    )(x1t, grow, ytok4, ytok4, ytok4, ytok4, g2, be2)


def _pack_block_diag(w):
    w = w.reshape(N_GATE_TILES, HEADS_PER_TILE, LRU_BLOCK, LRU_BLOCK)
    eye = jnp.eye(HEADS_PER_TILE, dtype=w.dtype)
    t = jnp.einsum("qhij,hg->qhigj", w, eye)
    return t.reshape(N_GATE_TILES, GATE_TILE, GATE_TILE)


def _layer(xp, xs, h_s0, ca_s0, cb_s0, p, *, alpha):
    bp, sp, _ = xp.shape
    bs, ss, _ = xs.shape
    tp, tsm = bp * sp, bs * ss
    total = tp + tsm
    row2 = lambda v: v.reshape(1, -1)

    rwt = jnp.transpose(p["router_w"])
    rwt_hi = rwt.astype(_BF16)
    rwt_lo = (rwt - rwt_hi.astype(_F32)).astype(_BF16)
    ii = jnp.arange(MIXER_ROWS)
    tri = (ii[:, None] < ii[None, :]).astype(_BF16)
    wts = (
        p["w_in"].astype(_BF16), row2(p["b_in"]), p["conv_a_w"], row2(p["conv_a_b"]),
        _pack_block_diag(p["lru_wa"]).astype(_BF16), row2(p["lru_ba"]),
        _pack_block_diag(p["lru_wx"]).astype(_BF16), row2(p["lru_bx"]), row2(p["lru_lambda"]),
        p["conv_b_w"], p["w_out"].astype(_BF16), row2(p["ln1_g"]), row2(p["ln1_b"]),
        rwt_hi, rwt_lo, jnp.broadcast_to(p["router_b"][:, None], (N_EXPERTS, LANES)), tri,
    )

    ts_p = MIXER_ROWS // bp
    xp4 = jnp.transpose(xp, (1, 0, 2))[None]
    zeros = lambda *s: jnp.zeros(s, _F32)
    outs_p = _mixer_call(xp4, zeros(1, bp, D_MODEL), zeros(1, CONV_A_WIDTH - 1, bp, D_MODEL),
                         zeros(1, CONV_B_WIDTH - 1, bp, D_MODEL), zeros(N_EXPERTS, LANES), wts,
                         ts=ts_p, alpha=alpha)
    nb_s = MIXER_ROWS // ss
    nblk_s = bs // nb_s
    xs4 = jnp.transpose(xs.reshape(nblk_s, nb_s, ss, D_MODEL), (0, 2, 1, 3))
    h0 = h_s0.reshape(nblk_s, nb_s, D_MODEL)
    ca0 = jnp.transpose(ca_s0.reshape(nblk_s, nb_s, CONV_A_WIDTH - 1, D_MODEL), (0, 2, 1, 3))
    cb0 = jnp.transpose(cb_s0.reshape(nblk_s, nb_s, CONV_B_WIDTH - 1, D_MODEL), (0, 2, 1, 3))
    outs_s = _mixer_call(xs4, h0, ca0, cb0, outs_p[7], wts, ts=ss, alpha=alpha)

    n_assign = total * TOP_K
    n_blocks = -(-n_assign // MOE_ROWS) + N_EXPERTS + 1
    n_rows = n_blocks * MOE_ROWS
    n_spare = 3 * MOE_ROWS
    x1t = jnp.concatenate([outs_p[0], outs_s[0], jnp.zeros((n_spare // TOP_K * 8, LANES), _F32)], axis=0)
    idx = jnp.concatenate([outs_p[1][:TOP_K], outs_s[1][:TOP_K]], axis=1)
    rank = jnp.concatenate([outs_p[2][:TOP_K], outs_s[2][:TOP_K]], axis=1)
    grow = jnp.concatenate([outs_p[3], outs_s[3]], axis=0)
    counts = outs_s[7][:, 0].astype(jnp.int32)

    padded = (counts + MOE_ROWS - 1) // MOE_ROWS * MOE_ROWS
    end_pad = jnp.cumsum(padded)
    start_pad = end_pad - padded
    experts = jnp.arange(N_EXPERTS, dtype=jnp.int32)
    start_of = jnp.sum(jnp.where(idx[:, :, None] == experts, start_pad, 0), axis=-1)
    dest = (start_of + rank).reshape(n_assign)
    blk_start = jnp.arange(n_blocks, dtype=jnp.int32) * MOE_ROWS
    blk_e = jnp.minimum(jnp.sum((blk_start[:, None] >= end_pad[None, :]).astype(jnp.int32), axis=1),
                        N_EXPERTS - 1)
    n_used = (end_pad[-1] // MOE_ROWS).astype(jnp.int32).reshape(1)

    inv = _invmap_call(dest, start_pad + counts, end_pad, total=total, n_rows=n_rows)
    ytok = _moe_call(blk_e, n_used, inv, x1t.reshape(-1, 8, LANES),
                     p["w_gu"], p["b_gu"][:, None, :], p["w_down"], p["b_down"][:, None, :],
                     n_blocks=n_blocks, n_slots=n_assign)
    y = _combine_call(x1t, grow, ytok, row2(p["ln2_g"]), row2(p["ln2_b"]), total=total, alpha=alpha)

    yp = jnp.transpose(y[:tp].reshape(sp, bp, D_MODEL), (1, 0, 2))
    ys = jnp.transpose(y[tp:].reshape(nblk_s, ss, nb_s, D_MODEL), (0, 2, 1, 3)).reshape(bs, ss, D_MODEL)

    def batch_major(v, nblk):
        return jnp.transpose(v, (0, 2, 1, 3)).reshape(nblk * v.shape[2], v.shape[1], D_MODEL)

    states_p = (outs_p[4].reshape(bp, D_MODEL), batch_major(outs_p[5], 1), batch_major(outs_p[6], 1))
    states_s = (outs_s[4].reshape(bs, D_MODEL), batch_major(outs_s[5], nblk_s), batch_major(outs_s[6], nblk_s))
    return yp, ys, states_p, states_s


def kernel(x_prompt, x_sample, state_rglru_h, state_rglru_conv, state_shortconv, w_in, b_in, conv_a_w, conv_a_b, lru_wa, lru_ba, lru_wx, lru_bx, lru_lambda, conv_b_w, w_out, ln1_g, ln1_b, router_w, router_b, w_gu, b_gu, w_down, b_down, ln2_g, ln2_b):
    depth = w_in.shape[0]
    alpha = (2.0 * depth) ** 0.25
    names = ("w_in", "b_in", "conv_a_w", "conv_a_b", "lru_wa", "lru_ba", "lru_wx", "lru_bx", "lru_lambda",
             "conv_b_w", "w_out", "ln1_g", "ln1_b", "router_w", "router_b", "w_gu", "b_gu", "w_down",
             "b_down", "ln2_g", "ln2_b")
    stacked = (w_in, b_in, conv_a_w, conv_a_b, lru_wa, lru_ba, lru_wx, lru_bx, lru_lambda, conv_b_w, w_out,
               ln1_g, ln1_b, router_w, router_b, w_gu, b_gu, w_down, b_down, ln2_g, ln2_b)
    xp, xs = x_prompt, x_sample
    hp_l, cp_l, sp_l, hs_l, cs_l, ss_l = [], [], [], [], [], []
    for l in range(depth):
        p = {n: v[l] for n, v in zip(names, stacked)}
        xp, xs, (hp, cp, sp), (hs, cs, ss) = _layer(
            xp, xs, state_rglru_h[l], state_rglru_conv[l], state_shortconv[l], p, alpha=alpha)
        hp_l.append(hp); cp_l.append(cp); sp_l.append(sp)
        hs_l.append(hs); cs_l.append(cs); ss_l.append(ss)
    return (xp, xs, jnp.stack(hp_l), jnp.stack(cp_l), jnp.stack(sp_l), jnp.stack(hs_l), jnp.stack(cs_l),
            jnp.stack(ss_l))
```

```python
import functools

import jax
import jax.numpy as jnp
from jax import lax
from jax.experimental import pallas as pl
from jax.experimental.pallas import tpu as pltpu

D_MODEL = 1024
LRU_HEADS = 16
LRU_BLOCK = D_MODEL // LRU_HEADS
LRU_C = 8.0
CONV_A_WIDTH = 4
CONV_B_WIDTH = 3
N_GROUPS = 7
N_EXPERTS = 32
TOP_K = 4
D_FF = D_MODEL
SWIGLU_LIMIT = 7.0
SWIGLU_ALPHA = 1.702
LN_EPS = 1e-5

GATE_TILE = 256
HEADS_PER_TILE = GATE_TILE // LRU_BLOCK
N_GATE_TILES = D_MODEL // GATE_TILE
LANES = 128
MIXER_ROWS = 512
MOE_ROWS = 256
COMBINE_ROWS = 512
VMEM_LIMIT = 58 * 1024 * 1024

_F32 = jnp.float32
_BF16 = jnp.bfloat16
_NT = (((1,), (1,)), ((), ()))


def _sigmoid(v):
    return 0.5 * jnp.tanh(0.5 * v) + 0.5


def _gelu_tanh(v):
    c = 0.7978845608028654
    return 0.5 * v * (1.0 + jnp.tanh(c * (v + 0.044715 * (v * v * v))))


def _layernorm(z, g, b):
    mu = jnp.mean(z, axis=-1, keepdims=True)
    zc = z - mu
    var = jnp.mean(zc * zc, axis=-1, keepdims=True)
    return zc * lax.rsqrt(var + LN_EPS) * g + b


def _mixer_kernel(x_ref, h0_ref, ca0_ref, cb0_ref, cnt0_ref, x1_buf_ref,
                  w_in_ref, b_in_ref, wca_ref, bca_ref, wa_ref, ba_ref, wx_ref, bx_ref, lam_ref,
                  wcb_ref, w_out_ref, g1_ref, be1_ref, rwt_hi_ref, rwt_lo_ref, rb_ref, tri_ref,
                  x1_ref, idx_ref, rank_ref, grow_ref, hl_ref, ca_ref, cb_ref, cnt_ref,
                  xa_s, u_s, a_s, b_s, h_s, hst_s, cnt_s, *, ts, nb, alpha):
    i = pl.program_id(0)
    j = pl.program_id(1)
    rows = ts * nb
    ta = (CONV_A_WIDTH - 1) * nb
    tb = (CONV_B_WIDTH - 1) * nb

    @pl.when(j == 0)
    def _():
        hst_s[...] = h0_ref[...]
        xa_s[0:ta, :] = ca0_ref[...].reshape(ta, D_MODEL)
        u_s[0:tb, :] = cb0_ref[...].reshape(tb, D_MODEL)

    @pl.when((i == 0) & (j == 0))
    def _():
        cnt_s[...] = cnt0_ref[...]

    x = x_ref[...].reshape(rows, D_MODEL)
    xb = x.astype(_BF16)

    def proj(g):
        lo, hi = g * D_MODEL, (g + 1) * D_MODEL
        return jnp.dot(xb, w_in_ref[:, lo:hi], preferred_element_type=_F32) + b_in_ref[:, lo:hi]

    xa_s[ta:ta + rows, :] = proj(0)
    xc = bca_ref[...] + xa_s[0:rows, :] * wca_ref[0:1, :]
    for k in range(1, CONV_A_WIDTH):
        xc = xc + xa_s[k * nb:k * nb + rows, :] * wca_ref[k:k + 1, :]
    new_ta = xa_s[rows:rows + ta, :]
    xa_s[0:ta, :] = new_ta
    ca_ref[...] = new_ta.reshape(CONV_A_WIDTH - 1, nb, D_MODEL)

    xcb = xc.astype(_BF16)

    def block_diag(w_ref):
        return jnp.concatenate(
            [jnp.dot(xcb[:, q * GATE_TILE:(q + 1) * GATE_TILE], w_ref[q], preferred_element_type=_F32)
             for q in range(N_GATE_TILES)], axis=-1)

    r = _sigmoid(block_diag(wa_ref) + ba_ref[...])
    ig = _sigmoid(block_diag(wx_ref) + bx_ref[...])
    nlam = -lam_ref[...]
    softplus = jnp.maximum(nlam, 0.0) + jnp.log1p(jnp.exp(-jnp.abs(nlam)))
    log_a = (-LRU_C * softplus) * r
    a = jnp.exp(log_a)
    a_s[...] = a
    b_s[...] = jnp.sqrt(-jnp.tanh(log_a) * (a * a + 1.0)) * (ig * xc)

    h = hst_s[...]
    for t in range(ts):
        sl = slice(t * nb, (t + 1) * nb)
        h = a_s[sl, :] * h + b_s[sl, :]
        h_s[sl, :] = h
    hst_s[...] = h
    hl_ref[...] = h

    a_s[...] = h_s[...] * _gelu_tanh(proj(1))

    u_s[tb:tb + rows, :] = proj(3) * proj(4)
    uc = u_s[0:rows, :] * wcb_ref[0:1, :]
    for k in range(1, CONV_B_WIDTH):
        uc = uc + u_s[k * nb:k * nb + rows, :] * wcb_ref[k:k + 1, :]
    new_tb = u_s[rows:rows + tb, :]
    u_s[0:tb, :] = new_tb
    cb_ref[...] = new_tb.reshape(CONV_B_WIDTH - 1, nb, D_MODEL)
    y_b = proj(2) * uc

    merged = _sigmoid(proj(5)) * a_s[...] + _sigmoid(proj(6)) * y_b
    mixed = jnp.dot(merged.astype(_BF16), w_out_ref[...], preferred_element_type=_F32)
    x1 = _layernorm(alpha * x + mixed, g1_ref[...], be1_ref[...])
    for s in range(8):
        x1_ref[pl.ds(s, rows, stride=8), :] = x1[:, s * LANES:(s + 1) * LANES]

    x1_hi = x1.astype(_BF16)
    x1_lo = (x1 - x1_hi.astype(_F32)).astype(_BF16)
    logits = (lax.dot_general(rwt_hi_ref[...], x1_hi, _NT, preferred_element_type=_F32)
              + lax.dot_general(rwt_hi_ref[...], x1_lo, _NT, preferred_element_type=_F32)
              + lax.dot_general(rwt_lo_ref[...], x1_hi, _NT, preferred_element_type=_F32)
              + rb_ref[:, 0:1])
    e_iota = lax.broadcasted_iota(jnp.int32, (N_EXPERTS, rows), 0)
    work = logits
    vals, sels, idxs = [], [], []
    for _ in range(TOP_K):
        m = jnp.max(work, axis=0, keepdims=True)
        ik = jnp.min(jnp.where(work == m, e_iota, N_EXPERTS), axis=0, keepdims=True)
        sel = e_iota == ik
        work = jnp.where(sel, -jnp.inf, work)
        vals.append(m)
        sels.append(sel)
        idxs.append(ik)
    exps = [jnp.exp(v - vals[0]) for v in vals]
    denom = exps[0] + exps[1] + exps[2] + exps[3]
    gates = [ex / denom for ex in exps]

    onehot = jnp.zeros((N_EXPERTS, rows), _F32)
    for sel in sels:
        onehot = onehot + sel.astype(_F32)
    prefix = jnp.dot(onehot.astype(_BF16), tri_ref[...], preferred_element_type=_F32)
    pos = prefix + cnt_s[:, 0:1]
    ranks = [jnp.sum(jnp.where(sel, pos, 0.0), axis=0, keepdims=True) for sel in sels]
    new_cnt = cnt_s[...] + jnp.sum(onehot, axis=1, keepdims=True)
    cnt_s[...] = new_cnt
    cnt_ref[...] = new_cnt

    row8 = lax.broadcasted_iota(jnp.int32, (8, rows), 0)
    idx8 = jnp.zeros((8, rows), jnp.int32)
    rank8 = jnp.zeros((8, rows), jnp.int32)
    for k in range(TOP_K):
        idx8 = jnp.where(row8 == k, idxs[k], idx8)
        rank8 = jnp.where(row8 == k, ranks[k].astype(jnp.int32), rank8)
    idx_ref[...] = idx8
    rank_ref[...] = rank8

    row_l = lax.broadcasted_iota(jnp.int32, (LANES, rows), 0)
    g_t = jnp.zeros((LANES, rows), _F32)
    for k in range(TOP_K):
        g_t = jnp.where(row_l == k, gates[k], g_t)
    grow_ref[...] = jnp.transpose(g_t)


def _const_spec(shape):
    nd = len(shape)
    return pl.BlockSpec(shape, lambda i, j: (0,) * nd, pipeline_mode=pl.Buffered(1))


def _mixer_call(x4, h0, ca0, cb0, cnt0, x1_buf, wts, *, ts, blk_off, alpha):
    nblk, seq, nb, _ = x4.shape
    nt = seq // ts
    rows = ts * nb
    total = nblk * seq * nb
    f32 = lambda *s: jax.ShapeDtypeStruct(s, _F32)
    i32 = lambda *s: jax.ShapeDtypeStruct(s, jnp.int32)
    in_specs = [
        pl.BlockSpec((None, ts, nb, D_MODEL), lambda i, j: (i, j, 0, 0)),
        pl.BlockSpec((None, nb, D_MODEL), lambda i, j: (i, 0, 0)),
        pl.BlockSpec((None, CONV_A_WIDTH - 1, nb, D_MODEL), lambda i, j: (i, 0, 0, 0)),
        pl.BlockSpec((None, CONV_B_WIDTH - 1, nb, D_MODEL), lambda i, j: (i, 0, 0, 0)),
        _const_spec(cnt0.shape),
        pl.BlockSpec(memory_space=pl.ANY),
    ] + [_const_spec(w.shape) for w in wts]
    out_shape = (
        jax.ShapeDtypeStruct(x1_buf.shape, _F32),
        i32(8, total),
        i32(8, total),
        f32(total, LANES),
        f32(nblk, nb, D_MODEL),
        f32(nblk, CONV_A_WIDTH - 1, nb, D_MODEL),
        f32(nblk, CONV_B_WIDTH - 1, nb, D_MODEL),
        f32(N_EXPERTS, LANES),
    )
    out_specs = (
        pl.BlockSpec((rows * 8, LANES), lambda i, j: (blk_off + i * nt + j, 0)),
        pl.BlockSpec((8, rows), lambda i, j: (0, i * nt + j)),
        pl.BlockSpec((8, rows), lambda i, j: (0, i * nt + j)),
        pl.BlockSpec((rows, LANES), lambda i, j: (i * nt + j, 0)),
        pl.BlockSpec((None, nb, D_MODEL), lambda i, j: (i, 0, 0)),
        pl.BlockSpec((None, CONV_A_WIDTH - 1, nb, D_MODEL), lambda i, j: (i, 0, 0, 0)),
        pl.BlockSpec((None, CONV_B_WIDTH - 1, nb, D_MODEL), lambda i, j: (i, 0, 0, 0)),
        pl.BlockSpec((N_EXPERTS, LANES), lambda i, j: (0, 0)),
    )
    scratch = [
        pltpu.VMEM((rows + (CONV_A_WIDTH - 1) * nb, D_MODEL), _F32),
        pltpu.VMEM((rows + (CONV_B_WIDTH - 1) * nb, D_MODEL), _F32),
        pltpu.VMEM((rows, D_MODEL), _F32),
        pltpu.VMEM((rows, D_MODEL), _F32),
        pltpu.VMEM((rows, D_MODEL), _F32),
        pltpu.VMEM((nb, D_MODEL), _F32),
        pltpu.VMEM((N_EXPERTS, LANES), _F32),
    ]
    return pl.pallas_call(
        functools.partial(_mixer_kernel, ts=ts, nb=nb, alpha=alpha),
        grid=(nblk, nt),
        in_specs=in_specs,
        out_specs=out_specs,
        out_shape=out_shape,
        scratch_shapes=scratch,
        input_output_aliases={5: 0},
        compiler_params=pltpu.CompilerParams(
            dimension_semantics=("arbitrary", "arbitrary"), vmem_limit_bytes=VMEM_LIMIT),
        name="mixer",
    )(x4, h0, ca0, cb0, cnt0, x1_buf, *wts)


def _invmap_kernel(dest_ref, lo_ref, hi_ref, inv_ref, *, total, n_rows):
    n_slots = total * TOP_K

    for r in range(MOE_ROWS):
        inv_ref[r] = n_slots + 2 * MOE_ROWS + r

    def fill(lo, hi):
        def body(row, c):
            inv_ref[MOE_ROWS + row] = n_slots + (row & (2 * MOE_ROWS - 1))
            return c
        lax.fori_loop(lo, hi, body, 0)

    def per_expert(e, c):
        fill(lo_ref[e], hi_ref[e])
        return c
    lax.fori_loop(0, N_EXPERTS, per_expert, 0)
    fill(hi_ref[N_EXPERTS - 1], n_rows)

    unroll = 16
    for k in range(TOP_K):
        def body(it, c, k=k):
            base = it * unroll
            rows = [dest_ref[k * total + base + u] for u in range(unroll)]
            val = base * TOP_K + k
            for u in range(unroll):
                inv_ref[MOE_ROWS + rows[u]] = val + u * TOP_K
            return c
        lax.fori_loop(0, total // unroll, body, 0)


def _invmap_call(dest, pad_lo, pad_hi, *, total, n_rows):
    smem = pl.BlockSpec(memory_space=pltpu.SMEM)
    return pl.pallas_call(
        functools.partial(_invmap_kernel, total=total, n_rows=n_rows),
        in_specs=[smem, smem, smem],
        out_specs=smem,
        out_shape=jax.ShapeDtypeStruct((n_rows + MOE_ROWS,), jnp.int32),
        name="invmap",
    )(dest, pad_lo, pad_hi)


def _moe_kernel(blk_e_ref, n_used_ref, inv_ref,
                x1_hbm, wgu_ref, bgu_ref, wd_ref, bd_ref,
                ytok_hbm,
                xb0, xb1, xb2, yb0, yb1, yb2, wgu_s, wd_s, gsem, ssem, *, n_slots):
    b = pl.program_id(0)
    nblk = pl.num_programs(0)
    n_used = n_used_ref[0]
    xbufs = (xb0, xb1, xb2)
    ybufs = (yb0, yb1, yb2)

    def gather_copy(v, s, r):
        return pltpu.make_async_copy(
            x1_hbm.at[v >> 2], xbufs[s].at[pl.ds(r * 8, 8), :], gsem.at[s])

    def scatter_copy(v, s, r):
        return pltpu.make_async_copy(
            ybufs[s].at[pl.ds(r * 8, 8), :], ytok_hbm.at[v], ssem.at[s])

    def start_gather(blk, s):
        base = (blk + 1) * MOE_ROWS
        for r in range(MOE_ROWS):
            gather_copy(inv_ref[base + r], s, r).start()

    def wait_rows(copy_fn, s):
        for r in range(MOE_ROWS):
            copy_fn(0, s, r).wait()

    @pl.when(b == 0)
    def _():
        for yb in ybufs:
            yb[...] = jnp.zeros(yb.shape, _F32)
        for s in range(2):
            for r in range(MOE_ROWS):
                scatter_copy(n_slots + s * MOE_ROWS + r, s, r).start()
        start_gather(0, 0)
        start_gather(1, 1)

    first_of_expert = (b == 0) | (blk_e_ref[b] != blk_e_ref[jnp.maximum(b - 1, 0)])

    @pl.when(first_of_expert & (b < n_used))
    def _():
        wgu_s[...] = wgu_ref[...].astype(_BF16)
        wd_s[...] = wd_ref[...].astype(_BF16)

    def run_block(s):
        nxt = (s + 2) % 3
        wait_rows(gather_copy, s)
        wait_rows(scatter_copy, s)
        x = jnp.concatenate(
            [xbufs[s][pl.ds(q, MOE_ROWS, stride=8), :] for q in range(8)], axis=-1)
        start_gather(jnp.minimum(b + 2, nblk - 1), nxt)
        base = b * MOE_ROWS
        for r in range(MOE_ROWS):
            scatter_copy(inv_ref[base + r], nxt, r).start()
        gu = jnp.dot(x.astype(_BF16), wgu_s[...], preferred_element_type=_F32) + bgu_ref[...]
        gate = jnp.minimum(gu[:, :D_FF], SWIGLU_LIMIT)
        up = jnp.clip(gu[:, D_FF:], -SWIGLU_LIMIT, SWIGLU_LIMIT)
        hmid = (up + 1.0) * (gate * _sigmoid(SWIGLU_ALPHA * gate))
        y = jnp.dot(hmid.astype(_BF16), wd_s[...], preferred_element_type=_F32) + bd_ref[...]
        for q in range(8):
            ybufs[s][pl.ds(q, MOE_ROWS, stride=8), :] = y[:, q * LANES:(q + 1) * LANES]

    for s in range(3):
        pl.when((b <= n_used) & (b % 3 == s))(functools.partial(run_block, s))

    for d in (1, 2):
        for s in range(3):
            @pl.when((b == nblk - 1) & ((n_used + d) % 3 == s))
            def _():
                wait_rows(gather_copy, s)
                wait_rows(scatter_copy, s)


def _moe_call(blk_e, n_used, inv, x1t, w_gu, b_gu, w_down, b_down, *, n_blocks, n_slots):
    grid_spec = pltpu.PrefetchScalarGridSpec(
        num_scalar_prefetch=3,
        grid=(n_blocks,),
        in_specs=[
            pl.BlockSpec(memory_space=pl.ANY),
            pl.BlockSpec((None, D_MODEL, 2 * D_FF), lambda b, be, nu, iv: (be[b], 0, 0)),
            pl.BlockSpec((None, 1, 2 * D_FF), lambda b, be, nu, iv: (be[b], 0, 0)),
            pl.BlockSpec((None, D_FF, D_MODEL), lambda b, be, nu, iv: (be[b], 0, 0)),
            pl.BlockSpec((None, 1, D_MODEL), lambda b, be, nu, iv: (be[b], 0, 0)),
        ],
        out_specs=pl.BlockSpec(memory_space=pl.ANY),
        scratch_shapes=[pltpu.VMEM((MOE_ROWS * 8, LANES), _F32)] * 6 + [
            pltpu.VMEM((D_MODEL, 2 * D_FF), _BF16),
            pltpu.VMEM((D_FF, D_MODEL), _BF16),
            pltpu.SemaphoreType.DMA((3,)),
            pltpu.SemaphoreType.DMA((3,)),
        ],
    )
    return pl.pallas_call(
        functools.partial(_moe_kernel, n_slots=n_slots),
        grid_spec=grid_spec,
        out_shape=jax.ShapeDtypeStruct((n_slots + 3 * MOE_ROWS, 8, LANES), _F32),
        compiler_params=pltpu.CompilerParams(
            dimension_semantics=("arbitrary",), vmem_limit_bytes=VMEM_LIMIT),
        name="moe",
    )(blk_e, n_used, inv, x1t, w_gu, b_gu, w_down, b_down)


def _combine_kernel(x1_ref, g_ref, y0_ref, y1_ref, y2_ref, y3_ref, g2_ref, be2_ref,
                    yp_ref, ys_ref, y_s, *, alpha, n_prompt_steps):
    i = pl.program_id(0)
    g = g_ref[...]
    rows = g.shape[0]

    def token_major(ref):
        return jnp.concatenate([ref[pl.ds(s, rows, stride=8), :] for s in range(8)], axis=-1)

    ys = [token_major(r.reshape(rows * 8, LANES)) for r in (y0_ref, y1_ref, y2_ref, y3_ref)]
    moe = g[:, 0:1] * ys[0] + g[:, 1:2] * ys[1] + g[:, 2:3] * ys[2] + g[:, 3:4] * ys[3]
    y = _layernorm(alpha * token_major(x1_ref) + moe, g2_ref[...], be2_ref[...])
    for q in range(8):
        y_s[q] = y[:, q * LANES:(q + 1) * LANES]

    def batch_rows(bb, nb):
        return jnp.concatenate(
            [y_s[q, pl.ds(bb, rows // nb, stride=nb), :] for q in range(8)], axis=-1)

    @pl.when(i < n_prompt_steps)
    def _():
        for bb in range(yp_ref.shape[0]):
            yp_ref[bb] = batch_rows(bb, yp_ref.shape[0])

    @pl.when(i >= n_prompt_steps)
    def _():
        for bb in range(ys_ref.shape[0]):
            ys_ref[bb] = batch_rows(bb, ys_ref.shape[0])


def _combine_call(x1t, grow, ytok, g2, be2, *, prompt_shape, sample_shape, alpha):
    bp, sp, _ = prompt_shape
    bs, ss, _ = sample_shape
    ts_p = COMBINE_ROWS // bp
    nb_s = COMBINE_ROWS // ss
    n_p = sp // ts_p
    nt = n_p + bs // nb_s
    plane = lambda k: pl.BlockSpec((COMBINE_ROWS, None, 8, LANES), lambda i, k=k: (i, k, 0, 0))
    vec = pl.BlockSpec((1, D_MODEL), lambda i: (0, 0))
    ytok4 = ytok.reshape(-1, TOP_K, 8, LANES)
    return pl.pallas_call(
        functools.partial(_combine_kernel, alpha=alpha, n_prompt_steps=n_p),
        grid=(nt,),
        in_specs=[pl.BlockSpec((COMBINE_ROWS * 8, LANES), lambda i: (i, 0)),
                  pl.BlockSpec((COMBINE_ROWS, LANES), lambda i: (i, 0)),
                  plane(0), plane(1), plane(2), plane(3), vec, vec],
        out_specs=(pl.BlockSpec((bp, ts_p, D_MODEL), lambda i: (0, jnp.minimum(i, n_p - 1), 0)),
                   pl.BlockSpec((nb_s, ss, D_MODEL), lambda i: (jnp.maximum(i - n_p, 0), 0, 0))),
        out_shape=(jax.ShapeDtypeStruct(prompt_shape, _F32), jax.ShapeDtypeStruct(sample_shape, _F32)),
        scratch_shapes=[pltpu.VMEM((8, COMBINE_ROWS, LANES), _F32)],
        compiler_params=pltpu.CompilerParams(
            dimension_semantics=("arbitrary",), vmem_limit_bytes=VMEM_LIMIT),
        name="combine",
    )(x1t, grow, ytok4, ytok4, ytok4, ytok4, g2, be2)


def _pack_block_diag(w):
    w = w.reshape(N_GATE_TILES, HEADS_PER_TILE, LRU_BLOCK, LRU_BLOCK)
    eye = jnp.eye(HEADS_PER_TILE, dtype=w.dtype)
    t = jnp.einsum("qhij,hg->qhigj", w, eye)
    return t.reshape(N_GATE_TILES, GATE_TILE, GATE_TILE)


def _layer(xp, xs, h_s0, ca_s0, cb_s0, p, *, alpha):
    bp, sp, _ = xp.shape
    bs, ss, _ = xs.shape
    tp, tsm = bp * sp, bs * ss
    total = tp + tsm
    row2 = lambda v: v.reshape(1, -1)

    rwt = jnp.transpose(p["router_w"])
    rwt_hi = rwt.astype(_BF16)
    rwt_lo = (rwt - rwt_hi.astype(_F32)).astype(_BF16)
    ii = jnp.arange(MIXER_ROWS)
    tri = (ii[:, None] < ii[None, :]).astype(_BF16)
    wts = (
        p["w_in"].astype(_BF16), row2(p["b_in"]), p["conv_a_w"], row2(p["conv_a_b"]),
        _pack_block_diag(p["lru_wa"]).astype(_BF16), row2(p["lru_ba"]),
        _pack_block_diag(p["lru_wx"]).astype(_BF16), row2(p["lru_bx"]), row2(p["lru_lambda"]),
        p["conv_b_w"], p["w_out"].astype(_BF16), row2(p["ln1_g"]), row2(p["ln1_b"]),
        rwt_hi, rwt_lo, jnp.broadcast_to(p["router_b"][:, None], (N_EXPERTS, LANES)), tri,
    )

    ts_p = MIXER_ROWS // bp
    xp4 = jnp.transpose(xp, (1, 0, 2))[None]
    zeros = lambda *s: jnp.zeros(s, _F32)
    n_assign = total * TOP_K
    n_spare = 3 * MOE_ROWS
    x1_buf = zeros((total + n_spare // TOP_K) * 8, LANES)
    outs_p = _mixer_call(xp4, zeros(1, bp, D_MODEL), zeros(1, CONV_A_WIDTH - 1, bp, D_MODEL),
                         zeros(1, CONV_B_WIDTH - 1, bp, D_MODEL), zeros(N_EXPERTS, LANES), x1_buf, wts,
                         ts=ts_p, blk_off=0, alpha=alpha)
    nb_s = MIXER_ROWS // ss
    nblk_s = bs // nb_s
    xs4 = jnp.transpose(xs.reshape(nblk_s, nb_s, ss, D_MODEL), (0, 2, 1, 3))
    h0 = h_s0.reshape(nblk_s, nb_s, D_MODEL)
    ca0 = jnp.transpose(ca_s0.reshape(nblk_s, nb_s, CONV_A_WIDTH - 1, D_MODEL), (0, 2, 1, 3))
    cb0 = jnp.transpose(cb_s0.reshape(nblk_s, nb_s, CONV_B_WIDTH - 1, D_MODEL), (0, 2, 1, 3))
    outs_s = _mixer_call(xs4, h0, ca0, cb0, outs_p[7], outs_p[0], wts,
                         ts=ss, blk_off=tp // MIXER_ROWS, alpha=alpha)
    x1t = outs_s[0]

    n_blocks = -(-n_assign // MOE_ROWS) + N_EXPERTS + 1
    n_rows = n_blocks * MOE_ROWS
    idx = jnp.concatenate([outs_p[1][:TOP_K], outs_s[1][:TOP_K]], axis=1)
    rank = jnp.concatenate([outs_p[2][:TOP_K], outs_s[2][:TOP_K]], axis=1)
    grow = jnp.concatenate([outs_p[3], outs_s[3]], axis=0)
    counts = outs_s[7][:, 0].astype(jnp.int32)

    padded = (counts + MOE_ROWS - 1) // MOE_ROWS * MOE_ROWS
    end_pad = jnp.cumsum(padded)
    start_pad = end_pad - padded
    experts = jnp.arange(N_EXPERTS, dtype=jnp.int32)
    start_of = jnp.sum(jnp.where(idx[:, :, None] == experts, start_pad, 0), axis=-1)
    dest = (start_of + rank).reshape(n_assign)
    blk_start = jnp.arange(n_blocks, dtype=jnp.int32) * MOE_ROWS
    blk_e = jnp.minimum(jnp.sum((blk_start[:, None] >= end_pad[None, :]).astype(jnp.int32), axis=1),
                        N_EXPERTS - 1)
    n_used = (end_pad[-1] // MOE_ROWS).astype(jnp.int32).reshape(1)

    inv = _invmap_call(dest, start_pad + counts, end_pad, total=total, n_rows=n_rows)
    ytok = _moe_call(blk_e, n_used, inv, x1t.reshape(-1, 8, LANES),
                     p["w_gu"], p["b_gu"][:, None, :], p["w_down"], p["b_down"][:, None, :],
                     n_blocks=n_blocks, n_slots=n_assign)
    yp, ys = _combine_call(x1t, grow, ytok, row2(p["ln2_g"]), row2(p["ln2_b"]),
                           prompt_shape=xp.shape, sample_shape=xs.shape, alpha=alpha)

    def batch_major(v, nblk):
        return jnp.transpose(v, (0, 2, 1, 3)).reshape(nblk * v.shape[2], v.shape[1], D_MODEL)

    states_p = (outs_p[4].reshape(bp, D_MODEL), batch_major(outs_p[5], 1), batch_major(outs_p[6], 1))
    states_s = (outs_s[4].reshape(bs, D_MODEL), batch_major(outs_s[5], nblk_s), batch_major(outs_s[6], nblk_s))
    return yp, ys, states_p, states_s


def kernel(x_prompt, x_sample, state_rglru_h, state_rglru_conv, state_shortconv, w_in, b_in, conv_a_w, conv_a_b, lru_wa, lru_ba, lru_wx, lru_bx, lru_lambda, conv_b_w, w_out, ln1_g, ln1_b, router_w, router_b, w_gu, b_gu, w_down, b_down, ln2_g, ln2_b):
    depth = w_in.shape[0]
    alpha = (2.0 * depth) ** 0.25
    names = ("w_in", "b_in", "conv_a_w", "conv_a_b", "lru_wa", "lru_ba", "lru_wx", "lru_bx", "lru_lambda",
             "conv_b_w", "w_out", "ln1_g", "ln1_b", "router_w", "router_b", "w_gu", "b_gu", "w_down",
             "b_down", "ln2_g", "ln2_b")
    stacked = (w_in, b_in, conv_a_w, conv_a_b, lru_wa, lru_ba, lru_wx, lru_bx, lru_lambda, conv_b_w, w_out,
               ln1_g, ln1_b, router_w, router_b, w_gu, b_gu, w_down, b_down, ln2_g, ln2_b)
    xp, xs = x_prompt, x_sample
    hp_l, cp_l, sp_l, hs_l, cs_l, ss_l = [], [], [], [], [], []
    for l in range(depth):
        p = {n: v[l] for n, v in zip(names, stacked)}
        xp, xs, (hp, cp, sp), (hs, cs, ss) = _layer(
            xp, xs, state_rglru_h[l], state_rglru_conv[l], state_shortconv[l], p, alpha=alpha)
        hp_l.append(hp); cp_l.append(cp); sp_l.append(sp)
        hs_l.append(hs); cs_l.append(cs); ss_l.append(ss)
    return (xp, xs, jnp.stack(hp_l), jnp.stack(cp_l), jnp.stack(sp_l), jnp.stack(hs_l), jnp.stack(cs_l),
            jnp.stack(ss_l))
```

```python
import functools

import jax
import jax.numpy as jnp
from jax import lax
from jax.experimental import pallas as pl
from jax.experimental.pallas import tpu as pltpu

D_MODEL = 1024
LRU_HEADS = 16
LRU_BLOCK = D_MODEL // LRU_HEADS
LRU_C = 8.0
CONV_A_WIDTH = 4
CONV_B_WIDTH = 3
N_GROUPS = 7
N_EXPERTS = 32
TOP_K = 4
D_FF = D_MODEL
SWIGLU_LIMIT = 7.0
SWIGLU_ALPHA = 1.702
LN_EPS = 1e-5

GATE_TILE = 256
HEADS_PER_TILE = GATE_TILE // LRU_BLOCK
N_GATE_TILES = D_MODEL // GATE_TILE
LANES = 128
MIXER_ROWS = 512
MOE_ROWS = 512
COMBINE_ROWS = 512
COMBINE_GRAN = 32
COMBINE_MAX_COPIES = COMBINE_ROWS * TOP_K // COMBINE_GRAN + N_EXPERTS
VMEM_LIMIT = 58 * 1024 * 1024

_F32 = jnp.float32
_BF16 = jnp.bfloat16
_NT = (((1,), (1,)), ((), ()))


def _sigmoid(v):
    return 0.5 * jnp.tanh(0.5 * v) + 0.5


def _gelu_tanh(v):
    c = 0.7978845608028654
    return 0.5 * v * (1.0 + jnp.tanh(c * (v + 0.044715 * (v * v * v))))


def _layernorm(z, g, b):
    mu = jnp.mean(z, axis=-1, keepdims=True)
    zc = z - mu
    var = jnp.mean(zc * zc, axis=-1, keepdims=True)
    return zc * lax.rsqrt(var + LN_EPS) * g + b


def _mixer_kernel(x_ref, h0_ref, ca0_ref, cb0_ref, cnt0_ref, x1_buf_ref,
                  w_in_ref, b_in_ref, wca_ref, bca_ref, wa_ref, ba_ref, wx_ref, bx_ref, lam_ref,
                  wcb_ref, w_out_ref, g1_ref, be1_ref, rwt_hi_ref, rwt_lo_ref, rb_ref, tri_ref,
                  x1_ref, idx_ref, rank_ref, grow_ref, hl_ref, ca_ref, cb_ref, cnt_ref,
                  xa_s, u_s, a_s, b_s, h_s, hst_s, cnt_s, *, ts, nb, alpha):
    i = pl.program_id(0)
    j = pl.program_id(1)
    rows = ts * nb
    ta = (CONV_A_WIDTH - 1) * nb
    tb = (CONV_B_WIDTH - 1) * nb

    @pl.when(j == 0)
    def _():
        hst_s[...] = h0_ref[...]
        xa_s[0:ta, :] = ca0_ref[...].reshape(ta, D_MODEL)
        u_s[0:tb, :] = cb0_ref[...].reshape(tb, D_MODEL)

    @pl.when((i == 0) & (j == 0))
    def _():
        cnt_s[...] = cnt0_ref[...]

    x = x_ref[...].reshape(rows, D_MODEL)
    xb = x.astype(_BF16)

    def proj(g):
        lo, hi = g * D_MODEL, (g + 1) * D_MODEL
        return jnp.dot(xb, w_in_ref[:, lo:hi], preferred_element_type=_F32) + b_in_ref[:, lo:hi]

    xa_s[ta:ta + rows, :] = proj(0)
    xc = bca_ref[...] + xa_s[0:rows, :] * wca_ref[0:1, :]
    for k in range(1, CONV_A_WIDTH):
        xc = xc + xa_s[k * nb:k * nb + rows, :] * wca_ref[k:k + 1, :]
    new_ta = xa_s[rows:rows + ta, :]
    xa_s[0:ta, :] = new_ta
    ca_ref[...] = new_ta.reshape(CONV_A_WIDTH - 1, nb, D_MODEL)

    xcb = xc.astype(_BF16)

    def block_diag(w_ref):
        return jnp.concatenate(
            [jnp.dot(xcb[:, q * GATE_TILE:(q + 1) * GATE_TILE], w_ref[q], preferred_element_type=_F32)
             for q in range(N_GATE_TILES)], axis=-1)

    r = _sigmoid(block_diag(wa_ref) + ba_ref[...])
    ig = _sigmoid(block_diag(wx_ref) + bx_ref[...])
    nlam = -lam_ref[...]
    softplus = jnp.maximum(nlam, 0.0) + jnp.log1p(jnp.exp(-jnp.abs(nlam)))
    log_a = (-LRU_C * softplus) * r
    a = jnp.exp(log_a)
    a_s[...] = a
    b_s[...] = jnp.sqrt(-jnp.tanh(log_a) * (a * a + 1.0)) * (ig * xc)

    h = hst_s[...]
    for t in range(ts):
        sl = slice(t * nb, (t + 1) * nb)
        h = a_s[sl, :] * h + b_s[sl, :]
        h_s[sl, :] = h
    hst_s[...] = h
    hl_ref[...] = h

    a_s[...] = h_s[...] * _gelu_tanh(proj(1))

    u_s[tb:tb + rows, :] = proj(3) * proj(4)
    uc = u_s[0:rows, :] * wcb_ref[0:1, :]
    for k in range(1, CONV_B_WIDTH):
        uc = uc + u_s[k * nb:k * nb + rows, :] * wcb_ref[k:k + 1, :]
    new_tb = u_s[rows:rows + tb, :]
    u_s[0:tb, :] = new_tb
    cb_ref[...] = new_tb.reshape(CONV_B_WIDTH - 1, nb, D_MODEL)
    y_b = proj(2) * uc

    merged = _sigmoid(proj(5)) * a_s[...] + _sigmoid(proj(6)) * y_b
    mixed = jnp.dot(merged.astype(_BF16), w_out_ref[...], preferred_element_type=_F32)
    x1 = _layernorm(alpha * x + mixed, g1_ref[...], be1_ref[...])
    for s in range(8):
        x1_ref[pl.ds(s, rows, stride=8), :] = x1[:, s * LANES:(s + 1) * LANES]

    x1_hi = x1.astype(_BF16)
    x1_lo = (x1 - x1_hi.astype(_F32)).astype(_BF16)
    logits = (lax.dot_general(rwt_hi_ref[...], x1_hi, _NT, preferred_element_type=_F32)
              + lax.dot_general(rwt_hi_ref[...], x1_lo, _NT, preferred_element_type=_F32)
              + lax.dot_general(rwt_lo_ref[...], x1_hi, _NT, preferred_element_type=_F32)
              + rb_ref[:, 0:1])
    e_iota = lax.broadcasted_iota(jnp.int32, (N_EXPERTS, rows), 0)
    work = logits
    vals, sels, idxs = [], [], []
    for _ in range(TOP_K):
        m = jnp.max(work, axis=0, keepdims=True)
        ik = jnp.min(jnp.where(work == m, e_iota, N_EXPERTS), axis=0, keepdims=True)
        sel = e_iota == ik
        work = jnp.where(sel, -jnp.inf, work)
        vals.append(m)
        sels.append(sel)
        idxs.append(ik)
    exps = [jnp.exp(v - vals[0]) for v in vals]
    denom = exps[0] + exps[1] + exps[2] + exps[3]
    gates = [ex / denom for ex in exps]

    onehot = jnp.zeros((N_EXPERTS, rows), _F32)
    for sel in sels:
        onehot = onehot + sel.astype(_F32)
    prefix = jnp.dot(onehot.astype(_BF16), tri_ref[...], preferred_element_type=_F32)
    pos = prefix + cnt_s[:, 0:1]
    ranks = [jnp.sum(jnp.where(sel, pos, 0.0), axis=0, keepdims=True) for sel in sels]
    new_cnt = cnt_s[...] + jnp.sum(onehot, axis=1, keepdims=True)
    cnt_s[...] = new_cnt
    cnt_ref[...] = new_cnt

    row8 = lax.broadcasted_iota(jnp.int32, (8, rows), 0)
    idx8 = jnp.zeros((8, rows), jnp.int32)
    rank8 = jnp.zeros((8, rows), jnp.int32)
    for k in range(TOP_K):
        idx8 = jnp.where(row8 == k, idxs[k], idx8)
        rank8 = jnp.where(row8 == k, ranks[k].astype(jnp.int32), rank8)
    idx_ref[...] = idx8
    rank_ref[...] = rank8

    row_l = lax.broadcasted_iota(jnp.int32, (LANES, rows), 0)
    g_t = jnp.zeros((LANES, rows), _F32)
    for k in range(TOP_K):
        g_t = jnp.where(row_l == k, gates[k], g_t)
    grow_ref[...] = jnp.transpose(g_t)


def _const_spec(shape):
    nd = len(shape)
    return pl.BlockSpec(shape, lambda i, j: (0,) * nd, pipeline_mode=pl.Buffered(1))


def _mixer_call(x4, h0, ca0, cb0, cnt0, x1_buf, wts, *, ts, blk_off, alpha):
    nblk, seq, nb, _ = x4.shape
    nt = seq // ts
    rows = ts * nb
    total = nblk * seq * nb
    f32 = lambda *s: jax.ShapeDtypeStruct(s, _F32)
    i32 = lambda *s: jax.ShapeDtypeStruct(s, jnp.int32)
    in_specs = [
        pl.BlockSpec((None, ts, nb, D_MODEL), lambda i, j: (i, j, 0, 0)),
        pl.BlockSpec((None, nb, D_MODEL), lambda i, j: (i, 0, 0)),
        pl.BlockSpec((None, CONV_A_WIDTH - 1, nb, D_MODEL), lambda i, j: (i, 0, 0, 0)),
        pl.BlockSpec((None, CONV_B_WIDTH - 1, nb, D_MODEL), lambda i, j: (i, 0, 0, 0)),
        _const_spec(cnt0.shape),
        pl.BlockSpec(memory_space=pl.ANY),
    ] + [_const_spec(w.shape) for w in wts]
    out_shape = (
        jax.ShapeDtypeStruct(x1_buf.shape, _F32),
        i32(8, total),
        i32(8, total),
        f32(total, LANES),
        f32(nblk, nb, D_MODEL),
        f32(nblk, CONV_A_WIDTH - 1, nb, D_MODEL),
        f32(nblk, CONV_B_WIDTH - 1, nb, D_MODEL),
        f32(N_EXPERTS, LANES),
    )
    out_specs = (
        pl.BlockSpec((rows * 8, LANES), lambda i, j: (blk_off + i * nt + j, 0)),
        pl.BlockSpec((8, rows), lambda i, j: (0, i * nt + j)),
        pl.BlockSpec((8, rows), lambda i, j: (0, i * nt + j)),
        pl.BlockSpec((rows, LANES), lambda i, j: (i * nt + j, 0)),
        pl.BlockSpec((None, nb, D_MODEL), lambda i, j: (i, 0, 0)),
        pl.BlockSpec((None, CONV_A_WIDTH - 1, nb, D_MODEL), lambda i, j: (i, 0, 0, 0)),
        pl.BlockSpec((None, CONV_B_WIDTH - 1, nb, D_MODEL), lambda i, j: (i, 0, 0, 0)),
        pl.BlockSpec((N_EXPERTS, LANES), lambda i, j: (0, 0)),
    )
    scratch = [
        pltpu.VMEM((rows + (CONV_A_WIDTH - 1) * nb, D_MODEL), _F32),
        pltpu.VMEM((rows + (CONV_B_WIDTH - 1) * nb, D_MODEL), _F32),
        pltpu.VMEM((rows, D_MODEL), _F32),
        pltpu.VMEM((rows, D_MODEL), _F32),
        pltpu.VMEM((rows, D_MODEL), _F32),
        pltpu.VMEM((nb, D_MODEL), _F32),
        pltpu.VMEM((N_EXPERTS, LANES), _F32),
    ]
    return pl.pallas_call(
        functools.partial(_mixer_kernel, ts=ts, nb=nb, alpha=alpha),
        grid=(nblk, nt),
        in_specs=in_specs,
        out_specs=out_specs,
        out_shape=out_shape,
        scratch_shapes=scratch,
        input_output_aliases={5: 0},
        compiler_params=pltpu.CompilerParams(
            dimension_semantics=("arbitrary", "arbitrary"), vmem_limit_bytes=VMEM_LIMIT),
        name="mixer",
    )(x4, h0, ca0, cb0, cnt0, x1_buf, *wts)


def _invmap_kernel(dest_ref, lo_ref, hi_ref, inv_ref, *, total, n_rows):
    def fill(lo, hi):
        def body(row, c):
            inv_ref[row] = 0
            return c
        lax.fori_loop(lo, hi, body, 0)

    def per_expert(e, c):
        fill(lo_ref[e], hi_ref[e])
        return c
    lax.fori_loop(0, N_EXPERTS, per_expert, 0)
    fill(hi_ref[N_EXPERTS - 1], n_rows)

    unroll = 16
    for k in range(TOP_K):
        def body(it, c, k=k):
            base = it * unroll
            rows = [dest_ref[k * total + base + u] for u in range(unroll)]
            for u in range(unroll):
                inv_ref[rows[u]] = base + u
            return c
        lax.fori_loop(0, total // unroll, body, 0)


def _invmap_call(dest, pad_lo, pad_hi, *, total, n_rows):
    smem = pl.BlockSpec(memory_space=pltpu.SMEM)
    return pl.pallas_call(
        functools.partial(_invmap_kernel, total=total, n_rows=n_rows),
        in_specs=[smem, smem, smem],
        out_specs=smem,
        out_shape=jax.ShapeDtypeStruct((n_rows,), jnp.int32),
        name="invmap",
    )(dest, pad_lo, pad_hi)


def _moe_kernel(blk_e_ref, n_used_ref, inv_ref,
                x1_hbm, wgu_ref, bgu_ref, wd_ref, bd_ref,
                y_ref,
                xb0, xb1, xb2, wgu_s, wd_s, gsem):
    b = pl.program_id(0)
    nblk = pl.num_programs(0)
    n_used = n_used_ref[0]
    xbufs = (xb0, xb1, xb2)

    def gather_copy(tok, s, r):
        return pltpu.make_async_copy(x1_hbm.at[tok], xbufs[s].at[pl.ds(r * 8, 8), :], gsem.at[s])

    def wait_rows(s):
        for r in range(MOE_ROWS):
            gather_copy(0, s, r).wait()

    @pl.when(b == 0)
    def _():
        def prime(r, c):
            for s in range(2):
                pltpu.make_async_copy(
                    x1_hbm.at[inv_ref[s * MOE_ROWS + r]],
                    xbufs[s].at[pl.ds(pl.multiple_of(r * 8, 8), 8), :], gsem.at[s]).start()
            return c
        lax.fori_loop(0, MOE_ROWS, prime, 0)

    first_of_expert = (b == 0) | (blk_e_ref[b] != blk_e_ref[jnp.maximum(b - 1, 0)])

    @pl.when(first_of_expert & (b < n_used))
    def _():
        wgu_s[...] = wgu_ref[...].astype(_BF16)
        wd_s[...] = wd_ref[...].astype(_BF16)

    def run_block(s):
        wait_rows(s)
        x = jnp.concatenate(
            [xbufs[s][pl.ds(q, MOE_ROWS, stride=8), :] for q in range(8)], axis=-1)
        base = jnp.minimum(b + 2, nblk - 1) * MOE_ROWS
        for r in range(MOE_ROWS):
            gather_copy(inv_ref[base + r], (s + 2) % 3, r).start()
        gu = jnp.dot(x.astype(_BF16), wgu_s[...], preferred_element_type=_F32) + bgu_ref[...]
        gate = jnp.minimum(gu[:, :D_FF], SWIGLU_LIMIT)
        up = jnp.clip(gu[:, D_FF:], -SWIGLU_LIMIT, SWIGLU_LIMIT)
        hmid = (up + 1.0) * (gate * _sigmoid(SWIGLU_ALPHA * gate))
        y = jnp.dot(hmid.astype(_BF16), wd_s[...], preferred_element_type=_F32) + bd_ref[...]
        for q in range(8):
            y_ref[pl.ds(q, MOE_ROWS, stride=8), :] = y[:, q * LANES:(q + 1) * LANES]

    for s in range(3):
        pl.when((b < n_used) & (b % 3 == s))(functools.partial(run_block, s))

    @pl.when(b >= n_used)
    def _():
        y_ref[...] = jnp.zeros(y_ref.shape, _F32)

    for d in (0, 1):
        for s in range(3):
            pl.when((b == nblk - 1) & ((n_used + d) % 3 == s))(functools.partial(wait_rows, s))


def _moe_call(blk_e, n_used, inv, x1t, w_gu, b_gu, w_down, b_down, *, n_blocks):
    grid_spec = pltpu.PrefetchScalarGridSpec(
        num_scalar_prefetch=3,
        grid=(n_blocks,),
        in_specs=[
            pl.BlockSpec(memory_space=pl.ANY),
            pl.BlockSpec((None, D_MODEL, 2 * D_FF), lambda b, be, nu, iv: (be[b], 0, 0)),
            pl.BlockSpec((None, 1, 2 * D_FF), lambda b, be, nu, iv: (be[b], 0, 0)),
            pl.BlockSpec((None, D_FF, D_MODEL), lambda b, be, nu, iv: (be[b], 0, 0)),
            pl.BlockSpec((None, 1, D_MODEL), lambda b, be, nu, iv: (be[b], 0, 0)),
        ],
        out_specs=pl.BlockSpec((MOE_ROWS * 8, LANES), lambda b, be, nu, iv: (b, 0)),
        scratch_shapes=[pltpu.VMEM((MOE_ROWS * 8, LANES), _F32)] * 3 + [
            pltpu.VMEM((D_MODEL, 2 * D_FF), _BF16),
            pltpu.VMEM((D_FF, D_MODEL), _BF16),
            pltpu.SemaphoreType.DMA((3,)),
        ],
    )
    return pl.pallas_call(
        _moe_kernel,
        grid_spec=grid_spec,
        out_shape=jax.ShapeDtypeStruct((n_blocks * MOE_ROWS * 8, LANES), _F32),
        compiler_params=pltpu.CompilerParams(
            dimension_semantics=("arbitrary",), vmem_limit_bytes=VMEM_LIMIT),
        name="moe",
    )(blk_e, n_used, inv, x1t, w_gu, b_gu, w_down, b_down)


def _combine_kernel(gsrc_ref, ng_ref, pos_ref,
                    x1_ref, g_ref, yrows_hbm, g2_ref, be2_ref,
                    yp_ref, ys_ref,
                    gbuf, pl0, pl1, pl2, pl3, y_s, sem, *, alpha, n_prompt_steps, total):
    i = pl.program_id(0)
    nt = pl.num_programs(0)
    rows = g_ref.shape[0]
    slot = i % 2
    gran = COMBINE_GRAN * 8

    def gran_copy(src_row, sl, j):
        return pltpu.make_async_copy(
            yrows_hbm.at[pl.ds(pl.multiple_of(src_row * 8, 8), gran), :],
            gbuf.at[sl, pl.ds(pl.multiple_of(j * gran, 8), gran), :], sem.at[sl])

    def issue(tile, sl):
        def body(j, c):
            gran_copy(gsrc_ref[tile * COMBINE_MAX_COPIES + j], sl, j).start()
            return c
        lax.fori_loop(0, ng_ref[tile], body, 0)

    @pl.when(i == 0)
    def _():
        issue(0, 0)

    @pl.when(i + 1 < nt)
    def _():
        issue(i + 1, 1 - slot)

    def wait_one(j, c):
        gran_copy(0, slot, 0).wait()
        return c
    lax.fori_loop(0, ng_ref[i], wait_one, 0)

    planes = (pl0, pl1, pl2, pl3)
    for t in range(rows):
        for k in range(TOP_K):
            p = pos_ref[k * total + i * rows + t]
            planes[k][pl.ds(t * 8, 8), :] = gbuf[slot, pl.ds(pl.multiple_of(p * 8, 8), 8), :]

    g = g_ref[...]

    def token_major(ref):
        return jnp.concatenate([ref[pl.ds(s, rows, stride=8), :] for s in range(8)], axis=-1)

    ys = [token_major(r) for r in planes]
    moe = g[:, 0:1] * ys[0] + g[:, 1:2] * ys[1] + g[:, 2:3] * ys[2] + g[:, 3:4] * ys[3]
    y = _layernorm(alpha * token_major(x1_ref) + moe, g2_ref[...], be2_ref[...])
    for q in range(8):
        y_s[q] = y[:, q * LANES:(q + 1) * LANES]

    def batch_rows(bb, nb):
        return jnp.concatenate(
            [y_s[q, pl.ds(bb, rows // nb, stride=nb), :] for q in range(8)], axis=-1)

    @pl.when(i < n_prompt_steps)
    def _():
        for bb in range(yp_ref.shape[0]):
            yp_ref[bb] = batch_rows(bb, yp_ref.shape[0])

    @pl.when(i >= n_prompt_steps)
    def _():
        for bb in range(ys_ref.shape[0]):
            ys_ref[bb] = batch_rows(bb, ys_ref.shape[0])


def _combine_call(gsrc, ng, pos, x1t, grow, yrows, g2, be2, *, prompt_shape, sample_shape, alpha):
    bp, sp, _ = prompt_shape
    bs, ss, _ = sample_shape
    ts_p = COMBINE_ROWS // bp
    nb_s = COMBINE_ROWS // ss
    n_p = sp // ts_p
    nt = n_p + bs // nb_s
    total = bp * sp + bs * ss
    vec = pl.BlockSpec((1, D_MODEL), lambda i, *_: (0, 0))
    grid_spec = pltpu.PrefetchScalarGridSpec(
        num_scalar_prefetch=3,
        grid=(nt,),
        in_specs=[pl.BlockSpec((COMBINE_ROWS * 8, LANES), lambda i, *_: (i, 0)),
                  pl.BlockSpec((COMBINE_ROWS, LANES), lambda i, *_: (i, 0)),
                  pl.BlockSpec(memory_space=pl.ANY), vec, vec],
        out_specs=(pl.BlockSpec((bp, ts_p, D_MODEL), lambda i, *_: (0, jnp.minimum(i, n_p - 1), 0)),
                   pl.BlockSpec((nb_s, ss, D_MODEL), lambda i, *_: (jnp.maximum(i - n_p, 0), 0, 0))),
        scratch_shapes=[pltpu.VMEM((2, COMBINE_MAX_COPIES * COMBINE_GRAN * 8, LANES), _F32)]
        + [pltpu.VMEM((COMBINE_ROWS * 8, LANES), _F32)] * TOP_K
        + [pltpu.VMEM((8, COMBINE_ROWS, LANES), _F32), pltpu.SemaphoreType.DMA((2,))],
    )
    return pl.pallas_call(
        functools.partial(_combine_kernel, alpha=alpha, n_prompt_steps=n_p, total=total),
        grid_spec=grid_spec,
        out_shape=(jax.ShapeDtypeStruct(prompt_shape, _F32), jax.ShapeDtypeStruct(sample_shape, _F32)),
        compiler_params=pltpu.CompilerParams(
            dimension_semantics=("arbitrary",), vmem_limit_bytes=VMEM_LIMIT),
        name="combine",
    )(gsrc, ng, pos, x1t, grow, yrows, g2, be2)


def _pack_block_diag(w):
    w = w.reshape(N_GATE_TILES, HEADS_PER_TILE, LRU_BLOCK, LRU_BLOCK)
    eye = jnp.eye(HEADS_PER_TILE, dtype=w.dtype)
    t = jnp.einsum("qhij,hg->qhigj", w, eye)
    return t.reshape(N_GATE_TILES, GATE_TILE, GATE_TILE)


def _layer(xp, xs, h_s0, ca_s0, cb_s0, p, *, alpha):
    bp, sp, _ = xp.shape
    bs, ss, _ = xs.shape
    tp, tsm = bp * sp, bs * ss
    total = tp + tsm
    row2 = lambda v: v.reshape(1, -1)

    rwt = jnp.transpose(p["router_w"])
    rwt_hi = rwt.astype(_BF16)
    rwt_lo = (rwt - rwt_hi.astype(_F32)).astype(_BF16)
    ii = jnp.arange(MIXER_ROWS)
    tri = (ii[:, None] < ii[None, :]).astype(_BF16)
    wts = (
        p["w_in"].astype(_BF16), row2(p["b_in"]), p["conv_a_w"], row2(p["conv_a_b"]),
        _pack_block_diag(p["lru_wa"]).astype(_BF16), row2(p["lru_ba"]),
        _pack_block_diag(p["lru_wx"]).astype(_BF16), row2(p["lru_bx"]), row2(p["lru_lambda"]),
        p["conv_b_w"], p["w_out"].astype(_BF16), row2(p["ln1_g"]), row2(p["ln1_b"]),
        rwt_hi, rwt_lo, jnp.broadcast_to(p["router_b"][:, None], (N_EXPERTS, LANES)), tri,
    )

    ts_p = MIXER_ROWS // bp
    xp4 = jnp.transpose(xp, (1, 0, 2))[None]
    zeros = lambda *s: jnp.zeros(s, _F32)
    n_assign = total * TOP_K
    x1_buf = zeros(total * 8, LANES)
    outs_p = _mixer_call(xp4, zeros(1, bp, D_MODEL), zeros(1, CONV_A_WIDTH - 1, bp, D_MODEL),
                         zeros(1, CONV_B_WIDTH - 1, bp, D_MODEL), zeros(N_EXPERTS, LANES), x1_buf, wts,
                         ts=ts_p, blk_off=0, alpha=alpha)
    nb_s = MIXER_ROWS // ss
    nblk_s = bs // nb_s
    xs4 = jnp.transpose(xs.reshape(nblk_s, nb_s, ss, D_MODEL), (0, 2, 1, 3))
    h0 = h_s0.reshape(nblk_s, nb_s, D_MODEL)
    ca0 = jnp.transpose(ca_s0.reshape(nblk_s, nb_s, CONV_A_WIDTH - 1, D_MODEL), (0, 2, 1, 3))
    cb0 = jnp.transpose(cb_s0.reshape(nblk_s, nb_s, CONV_B_WIDTH - 1, D_MODEL), (0, 2, 1, 3))
    outs_s = _mixer_call(xs4, h0, ca0, cb0, outs_p[7], outs_p[0], wts,
                         ts=ss, blk_off=tp // MIXER_ROWS, alpha=alpha)
    x1t = outs_s[0]

    n_blocks = -(-n_assign // MOE_ROWS) + N_EXPERTS + 1
    n_rows = n_blocks * MOE_ROWS
    idx = jnp.concatenate([outs_p[1][:TOP_K], outs_s[1][:TOP_K]], axis=1)
    rank = jnp.concatenate([outs_p[2][:TOP_K], outs_s[2][:TOP_K]], axis=1)
    grow = jnp.concatenate([outs_p[3], outs_s[3]], axis=0)
    counts = outs_s[7][:, 0].astype(jnp.int32)

    padded = (counts + MOE_ROWS - 1) // MOE_ROWS * MOE_ROWS
    end_pad = jnp.cumsum(padded)
    start_pad = end_pad - padded
    experts = jnp.arange(N_EXPERTS, dtype=jnp.int32)
    onehot = idx[:, :, None] == experts
    start_of = jnp.sum(jnp.where(onehot, start_pad, 0), axis=-1)
    dest = (start_of + rank).reshape(n_assign)
    blk_start = jnp.arange(n_blocks, dtype=jnp.int32) * MOE_ROWS
    blk_e = jnp.minimum(jnp.sum((blk_start[:, None] >= end_pad[None, :]).astype(jnp.int32), axis=1),
                        N_EXPERTS - 1)
    n_used = (end_pad[-1] // MOE_ROWS).astype(jnp.int32).reshape(1)

    n_tiles = total // COMBINE_ROWS
    cnt_te = jnp.sum(onehot.reshape(TOP_K, n_tiles, COMBINE_ROWS, N_EXPERTS).astype(jnp.int32), axis=(0, 2))
    first_rank = jnp.cumsum(cnt_te, axis=0) - cnt_te
    n_copies = (cnt_te + COMBINE_GRAN - 1) // COMBINE_GRAN
    copy_end = jnp.cumsum(n_copies, axis=1)
    copy_off = copy_end - n_copies
    ng = copy_end[:, -1]
    j = jnp.arange(COMBINE_MAX_COPIES, dtype=jnp.int32)
    e_of_j = jnp.minimum(jnp.sum((copy_end[:, None, :] <= j[None, :, None]).astype(jnp.int32), axis=-1),
                         N_EXPERTS - 1)
    sel = e_of_j[:, :, None] == experts
    run_row = jnp.sum(jnp.where(sel, (start_pad + first_rank)[:, None, :], 0), axis=-1)
    run_off = jnp.sum(jnp.where(sel, copy_off[:, None, :], 0), axis=-1)
    gsrc = jnp.where(j[None, :] < ng[:, None], run_row + COMBINE_GRAN * (j[None, :] - run_off), 0)
    per_tok = lambda v: jnp.repeat(v, COMBINE_ROWS, axis=0)
    pos = jnp.sum(jnp.where(onehot, COMBINE_GRAN * per_tok(copy_off) - per_tok(first_rank), 0), axis=-1) + rank

    inv = _invmap_call(dest, start_pad + counts, end_pad, total=total, n_rows=n_rows)
    yrows = _moe_call(blk_e, n_used, inv, x1t.reshape(-1, 8, LANES),
                      p["w_gu"], p["b_gu"][:, None, :], p["w_down"], p["b_down"][:, None, :],
                      n_blocks=n_blocks)
    yp, ys = _combine_call(gsrc.reshape(-1), ng, pos.reshape(-1), x1t, grow, yrows,
                           row2(p["ln2_g"]), row2(p["ln2_b"]),
                           prompt_shape=xp.shape, sample_shape=xs.shape, alpha=alpha)

    def batch_major(v, nblk):
        return jnp.transpose(v, (0, 2, 1, 3)).reshape(nblk * v.shape[2], v.shape[1], D_MODEL)

    states_p = (outs_p[4].reshape(bp, D_MODEL), batch_major(outs_p[5], 1), batch_major(outs_p[6], 1))
    states_s = (outs_s[4].reshape(bs, D_MODEL), batch_major(outs_s[5], nblk_s), batch_major(outs_s[6], nblk_s))
    return yp, ys, states_p, states_s


def kernel(x_prompt, x_sample, state_rglru_h, state_rglru_conv, state_shortconv, w_in, b_in, conv_a_w, conv_a_b, lru_wa, lru_ba, lru_wx, lru_bx, lru_lambda, conv_b_w, w_out, ln1_g, ln1_b, router_w, router_b, w_gu, b_gu, w_down, b_down, ln2_g, ln2_b):
    depth = w_in.shape[0]
    alpha = (2.0 * depth) ** 0.25
    names = ("w_in", "b_in", "conv_a_w", "conv_a_b", "lru_wa", "lru_ba", "lru_wx", "lru_bx", "lru_lambda",
             "conv_b_w", "w_out", "ln1_g", "ln1_b", "router_w", "router_b", "w_gu", "b_gu", "w_down",
             "b_down", "ln2_g", "ln2_b")
    stacked = (w_in, b_in, conv_a_w, conv_a_b, lru_wa, lru_ba, lru_wx, lru_bx, lru_lambda, conv_b_w, w_out,
               ln1_g, ln1_b, router_w, router_b, w_gu, b_gu, w_down, b_down, ln2_g, ln2_b)
    xp, xs = x_prompt, x_sample
    hp_l, cp_l, sp_l, hs_l, cs_l, ss_l = [], [], [], [], [], []
    for l in range(depth):
        p = {n: v[l] for n, v in zip(names, stacked)}
        xp, xs, (hp, cp, sp), (hs, cs, ss) = _layer(
            xp, xs, state_rglru_h[l], state_rglru_conv[l], state_shortconv[l], p, alpha=alpha)
        hp_l.append(hp); cp_l.append(cp); sp_l.append(sp)
        hs_l.append(hs); cs_l.append(cs); ss_l.append(ss)
    return (xp, xs, jnp.stack(hp_l), jnp.stack(cp_l), jnp.stack(sp_l), jnp.stack(hs_l), jnp.stack(cs_l),
            jnp.stack(ss_l))
```

```python
import functools

import jax
import jax.numpy as jnp
from jax import lax
from jax.experimental import pallas as pl
from jax.experimental.pallas import tpu as pltpu

D_MODEL = 1024
LRU_HEADS = 16
LRU_BLOCK = D_MODEL // LRU_HEADS
LRU_C = 8.0
CONV_A_WIDTH = 4
CONV_B_WIDTH = 3
N_GROUPS = 7
N_EXPERTS = 32
TOP_K = 4
D_FF = D_MODEL
SWIGLU_LIMIT = 7.0
SWIGLU_ALPHA = 1.702
LN_EPS = 1e-5

GATE_TILE = 256
HEADS_PER_TILE = GATE_TILE // LRU_BLOCK
N_GATE_TILES = D_MODEL // GATE_TILE
LANES = 128
MIXER_ROWS = 512
MOE_ROWS = 256
COMBINE_ROWS = 512
VMEM_LIMIT = 58 * 1024 * 1024

_F32 = jnp.float32
_BF16 = jnp.bfloat16
_NT = (((1,), (1,)), ((), ()))


def _sigmoid(v):
    return 0.5 * jnp.tanh(0.5 * v) + 0.5


def _gelu_tanh(v):
    c = 0.7978845608028654
    return 0.5 * v * (1.0 + jnp.tanh(c * (v + 0.044715 * (v * v * v))))


def _layernorm(z, g, b):
    mu = jnp.mean(z, axis=-1, keepdims=True)
    zc = z - mu
    var = jnp.mean(zc * zc, axis=-1, keepdims=True)
    return zc * lax.rsqrt(var + LN_EPS) * g + b


def _mixer_kernel(x_ref, h0_ref, ca0_ref, cb0_ref, cnt0_ref, x1_buf_ref,
                  w_in_ref, b_in_ref, wca_ref, bca_ref, wa_ref, ba_ref, wx_ref, bx_ref, lam_ref,
                  wcb_ref, w_out_ref, g1_ref, be1_ref, rwt_hi_ref, rwt_lo_ref, rb_ref, tri_ref,
                  x1_ref, idx_ref, rank_ref, grow_ref, hl_ref, ca_ref, cb_ref, cnt_ref,
                  xa_s, u_s, a_s, b_s, h_s, hst_s, cnt_s, *, ts, nb, alpha):
    i = pl.program_id(0)
    j = pl.program_id(1)
    rows = ts * nb
    ta = (CONV_A_WIDTH - 1) * nb
    tb = (CONV_B_WIDTH - 1) * nb

    @pl.when(j == 0)
    def _():
        hst_s[...] = h0_ref[...]
        xa_s[0:ta, :] = ca0_ref[...].reshape(ta, D_MODEL)
        u_s[0:tb, :] = cb0_ref[...].reshape(tb, D_MODEL)

    @pl.when((i == 0) & (j == 0))
    def _():
        cnt_s[...] = cnt0_ref[...]

    x = x_ref[...].reshape(rows, D_MODEL)
    xb = x.astype(_BF16)

    def proj(g):
        lo, hi = g * D_MODEL, (g + 1) * D_MODEL
        return jnp.dot(xb, w_in_ref[:, lo:hi], preferred_element_type=_F32) + b_in_ref[:, lo:hi]

    xa_s[ta:ta + rows, :] = proj(0)
    xc = bca_ref[...] + xa_s[0:rows, :] * wca_ref[0:1, :]
    for k in range(1, CONV_A_WIDTH):
        xc = xc + xa_s[k * nb:k * nb + rows, :] * wca_ref[k:k + 1, :]
    new_ta = xa_s[rows:rows + ta, :]
    xa_s[0:ta, :] = new_ta
    ca_ref[...] = new_ta.reshape(CONV_A_WIDTH - 1, nb, D_MODEL)

    xcb = xc.astype(_BF16)

    def block_diag(w_ref):
        return jnp.concatenate(
            [jnp.dot(xcb[:, q * GATE_TILE:(q + 1) * GATE_TILE], w_ref[q], preferred_element_type=_F32)
             for q in range(N_GATE_TILES)], axis=-1)

    r = _sigmoid(block_diag(wa_ref) + ba_ref[...])
    ig = _sigmoid(block_diag(wx_ref) + bx_ref[...])
    nlam = -lam_ref[...]
    softplus = jnp.maximum(nlam, 0.0) + jnp.log1p(jnp.exp(-jnp.abs(nlam)))
    log_a = (-LRU_C * softplus) * r
    a = jnp.exp(log_a)
    a_s[...] = a
    b_s[...] = jnp.sqrt(-jnp.tanh(log_a) * (a * a + 1.0)) * (ig * xc)

    h = hst_s[...]
    for t in range(ts):
        sl = slice(t * nb, (t + 1) * nb)
        h = a_s[sl, :] * h + b_s[sl, :]
        h_s[sl, :] = h
    hst_s[...] = h
    hl_ref[...] = h

    a_s[...] = h_s[...] * _gelu_tanh(proj(1))

    u_s[tb:tb + rows, :] = proj(3) * proj(4)
    uc = u_s[0:rows, :] * wcb_ref[0:1, :]
    for k in range(1, CONV_B_WIDTH):
        uc = uc + u_s[k * nb:k * nb + rows, :] * wcb_ref[k:k + 1, :]
    new_tb = u_s[rows:rows + tb, :]
    u_s[0:tb, :] = new_tb
    cb_ref[...] = new_tb.reshape(CONV_B_WIDTH - 1, nb, D_MODEL)
    y_b = proj(2) * uc

    merged = _sigmoid(proj(5)) * a_s[...] + _sigmoid(proj(6)) * y_b
    mixed = jnp.dot(merged.astype(_BF16), w_out_ref[...], preferred_element_type=_F32)
    x1 = _layernorm(alpha * x + mixed, g1_ref[...], be1_ref[...])
    for s in range(8):
        x1_ref[pl.ds(s, rows, stride=8), :] = x1[:, s * LANES:(s + 1) * LANES]

    x1_hi = x1.astype(_BF16)
    x1_lo = (x1 - x1_hi.astype(_F32)).astype(_BF16)
    logits = (lax.dot_general(rwt_hi_ref[...], x1_hi, _NT, preferred_element_type=_F32)
              + lax.dot_general(rwt_hi_ref[...], x1_lo, _NT, preferred_element_type=_F32)
              + lax.dot_general(rwt_lo_ref[...], x1_hi, _NT, preferred_element_type=_F32)
              + rb_ref[:, 0:1])
    e_iota = lax.broadcasted_iota(jnp.int32, (N_EXPERTS, rows), 0)
    work = logits
    vals, sels, idxs = [], [], []
    for _ in range(TOP_K):
        m = jnp.max(work, axis=0, keepdims=True)
        ik = jnp.min(jnp.where(work == m, e_iota, N_EXPERTS), axis=0, keepdims=True)
        sel = e_iota == ik
        work = jnp.where(sel, -jnp.inf, work)
        vals.append(m)
        sels.append(sel)
        idxs.append(ik)
    exps = [jnp.exp(v - vals[0]) for v in vals]
    denom = exps[0] + exps[1] + exps[2] + exps[3]
    gates = [ex / denom for ex in exps]

    onehot = jnp.zeros((N_EXPERTS, rows), _F32)
    for sel in sels:
        onehot = onehot + sel.astype(_F32)
    prefix = jnp.dot(onehot.astype(_BF16), tri_ref[...], preferred_element_type=_F32)
    pos = prefix + cnt_s[:, 0:1]
    ranks = [jnp.sum(jnp.where(sel, pos, 0.0), axis=0, keepdims=True) for sel in sels]
    new_cnt = cnt_s[...] + jnp.sum(onehot, axis=1, keepdims=True)
    cnt_s[...] = new_cnt
    cnt_ref[...] = new_cnt

    row8 = lax.broadcasted_iota(jnp.int32, (8, rows), 0)
    idx8 = jnp.zeros((8, rows), jnp.int32)
    rank8 = jnp.zeros((8, rows), jnp.int32)
    for k in range(TOP_K):
        idx8 = jnp.where(row8 == k, idxs[k], idx8)
        rank8 = jnp.where(row8 == k, ranks[k].astype(jnp.int32), rank8)
    idx_ref[...] = idx8
    rank_ref[...] = rank8

    row_l = lax.broadcasted_iota(jnp.int32, (LANES, rows), 0)
    g_t = jnp.zeros((LANES, rows), _F32)
    for k in range(TOP_K):
        g_t = jnp.where(row_l == k, gates[k], g_t)
    grow_ref[...] = jnp.transpose(g_t)


def _const_spec(shape):
    nd = len(shape)
    return pl.BlockSpec(shape, lambda i, j: (0,) * nd, pipeline_mode=pl.Buffered(1))


def _mixer_call(x4, h0, ca0, cb0, cnt0, x1_buf, wts, *, ts, blk_off, alpha):
    nblk, seq, nb, _ = x4.shape
    nt = seq // ts
    rows = ts * nb
    total = nblk * seq * nb
    f32 = lambda *s: jax.ShapeDtypeStruct(s, _F32)
    i32 = lambda *s: jax.ShapeDtypeStruct(s, jnp.int32)
    in_specs = [
        pl.BlockSpec((None, ts, nb, D_MODEL), lambda i, j: (i, j, 0, 0)),
        pl.BlockSpec((None, nb, D_MODEL), lambda i, j: (i, 0, 0)),
        pl.BlockSpec((None, CONV_A_WIDTH - 1, nb, D_MODEL), lambda i, j: (i, 0, 0, 0)),
        pl.BlockSpec((None, CONV_B_WIDTH - 1, nb, D_MODEL), lambda i, j: (i, 0, 0, 0)),
        _const_spec(cnt0.shape),
        pl.BlockSpec(memory_space=pl.ANY),
    ] + [_const_spec(w.shape) for w in wts]
    out_shape = (
        jax.ShapeDtypeStruct(x1_buf.shape, _F32),
        i32(8, total),
        i32(8, total),
        f32(total, LANES),
        f32(nblk, nb, D_MODEL),
        f32(nblk, CONV_A_WIDTH - 1, nb, D_MODEL),
        f32(nblk, CONV_B_WIDTH - 1, nb, D_MODEL),
        f32(N_EXPERTS, LANES),
    )
    out_specs = (
        pl.BlockSpec((rows * 8, LANES), lambda i, j: (blk_off + i * nt + j, 0)),
        pl.BlockSpec((8, rows), lambda i, j: (0, i * nt + j)),
        pl.BlockSpec((8, rows), lambda i, j: (0, i * nt + j)),
        pl.BlockSpec((rows, LANES), lambda i, j: (i * nt + j, 0)),
        pl.BlockSpec((None, nb, D_MODEL), lambda i, j: (i, 0, 0)),
        pl.BlockSpec((None, CONV_A_WIDTH - 1, nb, D_MODEL), lambda i, j: (i, 0, 0, 0)),
        pl.BlockSpec((None, CONV_B_WIDTH - 1, nb, D_MODEL), lambda i, j: (i, 0, 0, 0)),
        pl.BlockSpec((N_EXPERTS, LANES), lambda i, j: (0, 0)),
    )
    scratch = [
        pltpu.VMEM((rows + (CONV_A_WIDTH - 1) * nb, D_MODEL), _F32),
        pltpu.VMEM((rows + (CONV_B_WIDTH - 1) * nb, D_MODEL), _F32),
        pltpu.VMEM((rows, D_MODEL), _F32),
        pltpu.VMEM((rows, D_MODEL), _F32),
        pltpu.VMEM((rows, D_MODEL), _F32),
        pltpu.VMEM((nb, D_MODEL), _F32),
        pltpu.VMEM((N_EXPERTS, LANES), _F32),
    ]
    return pl.pallas_call(
        functools.partial(_mixer_kernel, ts=ts, nb=nb, alpha=alpha),
        grid=(nblk, nt),
        in_specs=in_specs,
        out_specs=out_specs,
        out_shape=out_shape,
        scratch_shapes=scratch,
        input_output_aliases={5: 0},
        compiler_params=pltpu.CompilerParams(
            dimension_semantics=("arbitrary", "arbitrary"), vmem_limit_bytes=VMEM_LIMIT),
        name="mixer",
    )(x4, h0, ca0, cb0, cnt0, x1_buf, *wts)


def _invmap_kernel(dest_ref, lo_ref, hi_ref, inv_ref, *, total, n_rows):
    n_slots = total * TOP_K

    for r in range(MOE_ROWS):
        inv_ref[r] = n_slots + 2 * MOE_ROWS + r

    def fill(lo, hi):
        def body(row, c):
            inv_ref[MOE_ROWS + row] = n_slots + (row & (2 * MOE_ROWS - 1))
            return c
        lax.fori_loop(lo, hi, body, 0)

    def per_expert(e, c):
        fill(lo_ref[e], hi_ref[e])
        return c
    lax.fori_loop(0, N_EXPERTS, per_expert, 0)
    fill(hi_ref[N_EXPERTS - 1], n_rows)

    unroll = 16
    for k in range(TOP_K):
        def body(it, c, k=k):
            base = it * unroll
            rows = [dest_ref[k * total + base + u] for u in range(unroll)]
            val = base * TOP_K + k
            for u in range(unroll):
                inv_ref[MOE_ROWS + rows[u]] = val + u * TOP_K
            return c
        lax.fori_loop(0, total // unroll, body, 0)


def _invmap_call(dest, pad_lo, pad_hi, *, total, n_rows):
    smem = pl.BlockSpec(memory_space=pltpu.SMEM)
    return pl.pallas_call(
        functools.partial(_invmap_kernel, total=total, n_rows=n_rows),
        in_specs=[smem, smem, smem],
        out_specs=smem,
        out_shape=jax.ShapeDtypeStruct((n_rows + MOE_ROWS,), jnp.int32),
        name="invmap",
    )(dest, pad_lo, pad_hi)


def _moe_kernel(blk_e_ref, first_ref, wslot_ref, nxt_e_ref, n_used_ref, inv_ref,
                x1_hbm, wgu_hbm, bgu_ref, wd_hbm, bd_ref,
                ytok_hbm,
                xb0, xb1, xb2, yb0, yb1, yb2, wgu_f, wd_f, wgu_s, wd_s, bgu_s, bd_s,
                gsem, ssem, wsem, *, n_slots):
    n_used = n_used_ref[0]
    n_blk = blk_e_ref.shape[0]
    xbufs = (xb0, xb1, xb2)
    ybufs = (yb0, yb1, yb2)

    def gather_copy(v, s, r):
        return pltpu.make_async_copy(
            x1_hbm.at[v >> 2], xbufs[s].at[pl.ds(r * 8, 8), :], gsem.at[s])

    def scatter_copy(v, s, r):
        return pltpu.make_async_copy(
            ybufs[s].at[pl.ds(r * 8, 8), :], ytok_hbm.at[v], ssem.at[s])

    def weight_copies(e, ws):
        return (pltpu.make_async_copy(wgu_hbm.at[e], wgu_f.at[ws], wsem.at[ws]),
                pltpu.make_async_copy(wd_hbm.at[e], wd_f.at[ws], wsem.at[ws]))

    def wait_rows(copy_fn, s):
        for r in range(MOE_ROWS):
            copy_fn(0, s, r).wait()

    for cp in weight_copies(blk_e_ref[0], 0):
        cp.start()
    for yb in ybufs:
        yb[...] = jnp.zeros(yb.shape, _F32)

    def prime(r, c):
        for s in range(2):
            pltpu.make_async_copy(
                ybufs[s].at[pl.ds(pl.multiple_of(r * 8, 8), 8), :],
                ytok_hbm.at[n_slots + s * MOE_ROWS + r], ssem.at[s]).start()
            pltpu.make_async_copy(
                x1_hbm.at[inv_ref[(s + 1) * MOE_ROWS + r] >> 2],
                xbufs[s].at[pl.ds(pl.multiple_of(r * 8, 8), 8), :], gsem.at[s]).start()
        return c
    lax.fori_loop(0, MOE_ROWS, prime, 0)

    def run_block(b, s):
        nxt = (s + 2) % 3
        wait_rows(gather_copy, s)
        wait_rows(scatter_copy, s)
        x = jnp.concatenate(
            [xbufs[s][pl.ds(q, MOE_ROWS, stride=8), :] for q in range(8)], axis=-1)
        gbase = (jnp.minimum(b + 2, n_blk - 1) + 1) * MOE_ROWS
        sbase = b * MOE_ROWS
        for r in range(MOE_ROWS):
            gather_copy(inv_ref[gbase + r], nxt, r).start()
        for r in range(MOE_ROWS):
            scatter_copy(inv_ref[sbase + r], nxt, r).start()
        gu = jnp.dot(x.astype(_BF16), wgu_s[...], preferred_element_type=_F32) + bgu_s[...]
        gate = jnp.minimum(gu[:, :D_FF], SWIGLU_LIMIT)
        up = jnp.clip(gu[:, D_FF:], -SWIGLU_LIMIT, SWIGLU_LIMIT)
        hmid = (up + 1.0) * (gate * _sigmoid(SWIGLU_ALPHA * gate))
        y = jnp.dot(hmid.astype(_BF16), wd_s[...], preferred_element_type=_F32) + bd_s[...]
        for q in range(8):
            ybufs[s][pl.ds(q, MOE_ROWS, stride=8), :] = y[:, q * LANES:(q + 1) * LANES]

    def body(b, c):
        @pl.when(first_ref[b] == 1)
        def _():
            e = blk_e_ref[b]
            ws = wslot_ref[b]
            for cp in weight_copies(e, ws):
                cp.wait()
            nxt_e = nxt_e_ref[b]

            @pl.when(nxt_e >= 0)
            def _():
                for cp in weight_copies(nxt_e, 1 - ws):
                    cp.start()
            wgu_s[...] = wgu_f[ws].astype(_BF16)
            wd_s[...] = wd_f[ws].astype(_BF16)
            bgu_s[...] = bgu_ref[e]
            bd_s[...] = bd_ref[e]

        for s in range(3):
            pl.when(b % 3 == s)(functools.partial(run_block, b, s))
        return c

    lax.fori_loop(0, n_used + 1, body, 0)

    for d in (1, 2):
        for s in range(3):
            @pl.when((n_used + d) % 3 == s)
            def _():
                wait_rows(gather_copy, s)
                wait_rows(scatter_copy, s)


def _moe_call(blk_e, first, wslot, nxt_e, n_used, inv, x1t, w_gu, b_gu, w_down, b_down, *, n_slots):
    full = lambda shape: pl.BlockSpec(shape, lambda i, *_: (0,) * len(shape))
    grid_spec = pltpu.PrefetchScalarGridSpec(
        num_scalar_prefetch=6,
        grid=(1,),
        in_specs=[
            pl.BlockSpec(memory_space=pl.ANY),
            pl.BlockSpec(memory_space=pl.ANY),
            full(b_gu.shape),
            pl.BlockSpec(memory_space=pl.ANY),
            full(b_down.shape),
        ],
        out_specs=pl.BlockSpec(memory_space=pl.ANY),
        scratch_shapes=[pltpu.VMEM((MOE_ROWS * 8, LANES), _F32)] * 6 + [
            pltpu.VMEM((2, D_MODEL, 2 * D_FF), _F32),
            pltpu.VMEM((2, D_FF, D_MODEL), _F32),
            pltpu.VMEM((D_MODEL, 2 * D_FF), _BF16),
            pltpu.VMEM((D_FF, D_MODEL), _BF16),
            pltpu.VMEM((1, 2 * D_FF), _F32),
            pltpu.VMEM((1, D_MODEL), _F32),
            pltpu.SemaphoreType.DMA((3,)),
            pltpu.SemaphoreType.DMA((3,)),
            pltpu.SemaphoreType.DMA((2,)),
        ],
    )
    return pl.pallas_call(
        functools.partial(_moe_kernel, n_slots=n_slots),
        grid_spec=grid_spec,
        out_shape=jax.ShapeDtypeStruct((n_slots + 3 * MOE_ROWS, 8, LANES), _F32),
        compiler_params=pltpu.CompilerParams(
            dimension_semantics=("arbitrary",), vmem_limit_bytes=VMEM_LIMIT),
        name="moe",
    )(blk_e, first, wslot, nxt_e, n_used, inv, x1t, w_gu, b_gu, w_down, b_down)


def _combine_kernel(x1_ref, g_ref, y0_ref, y1_ref, y2_ref, y3_ref, g2_ref, be2_ref,
                    yp_ref, ys_ref, y_s, *, alpha, n_prompt_steps):
    i = pl.program_id(0)
    g = g_ref[...]
    rows = g.shape[0]

    def token_major(ref):
        return jnp.concatenate([ref[pl.ds(s, rows, stride=8), :] for s in range(8)], axis=-1)

    ys = [token_major(r.reshape(rows * 8, LANES)) for r in (y0_ref, y1_ref, y2_ref, y3_ref)]
    moe = g[:, 0:1] * ys[0] + g[:, 1:2] * ys[1] + g[:, 2:3] * ys[2] + g[:, 3:4] * ys[3]
    y = _layernorm(alpha * token_major(x1_ref) + moe, g2_ref[...], be2_ref[...])
    for q in range(8):
        y_s[q] = y[:, q * LANES:(q + 1) * LANES]

    def batch_rows(bb, nb):
        return jnp.concatenate(
            [y_s[q, pl.ds(bb, rows // nb, stride=nb), :] for q in range(8)], axis=-1)

    @pl.when(i < n_prompt_steps)
    def _():
        for bb in range(yp_ref.shape[0]):
            yp_ref[bb] = batch_rows(bb, yp_ref.shape[0])

    @pl.when(i >= n_prompt_steps)
    def _():
        for bb in range(ys_ref.shape[0]):
            ys_ref[bb] = batch_rows(bb, ys_ref.shape[0])


def _combine_call(x1t, grow, ytok, g2, be2, *, prompt_shape, sample_shape, alpha):
    bp, sp, _ = prompt_shape
    bs, ss, _ = sample_shape
    ts_p = COMBINE_ROWS // bp
    nb_s = COMBINE_ROWS // ss
    n_p = sp // ts_p
    nt = n_p + bs // nb_s
    plane = lambda k: pl.BlockSpec((COMBINE_ROWS, None, 8, LANES), lambda i, k=k: (i, k, 0, 0))
    vec = pl.BlockSpec((1, D_MODEL), lambda i: (0, 0))
    ytok4 = ytok.reshape(-1, TOP_K, 8, LANES)
    return pl.pallas_call(
        functools.partial(_combine_kernel, alpha=alpha, n_prompt_steps=n_p),
        grid=(nt,),
        in_specs=[pl.BlockSpec((COMBINE_ROWS * 8, LANES), lambda i: (i, 0)),
                  pl.BlockSpec((COMBINE_ROWS, LANES), lambda i: (i, 0)),
                  plane(0), plane(1), plane(2), plane(3), vec, vec],
        out_specs=(pl.BlockSpec((bp, ts_p, D_MODEL), lambda i: (0, jnp.minimum(i, n_p - 1), 0)),
                   pl.BlockSpec((nb_s, ss, D_MODEL), lambda i: (jnp.maximum(i - n_p, 0), 0, 0))),
        out_shape=(jax.ShapeDtypeStruct(prompt_shape, _F32), jax.ShapeDtypeStruct(sample_shape, _F32)),
        scratch_shapes=[pltpu.VMEM((8, COMBINE_ROWS, LANES), _F32)],
        compiler_params=pltpu.CompilerParams(
            dimension_semantics=("arbitrary",), vmem_limit_bytes=VMEM_LIMIT),
        name="combine",
    )(x1t, grow, ytok4, ytok4, ytok4, ytok4, g2, be2)


def _pack_block_diag(w):
    w = w.reshape(N_GATE_TILES, HEADS_PER_TILE, LRU_BLOCK, LRU_BLOCK)
    eye = jnp.eye(HEADS_PER_TILE, dtype=w.dtype)
    t = jnp.einsum("qhij,hg->qhigj", w, eye)
    return t.reshape(N_GATE_TILES, GATE_TILE, GATE_TILE)


def _layer(xp, xs, h_s0, ca_s0, cb_s0, p, *, alpha):
    bp, sp, _ = xp.shape
    bs, ss, _ = xs.shape
    tp, tsm = bp * sp, bs * ss
    total = tp + tsm
    row2 = lambda v: v.reshape(1, -1)

    rwt = jnp.transpose(p["router_w"])
    rwt_hi = rwt.astype(_BF16)
    rwt_lo = (rwt - rwt_hi.astype(_F32)).astype(_BF16)
    ii = jnp.arange(MIXER_ROWS)
    tri = (ii[:, None] < ii[None, :]).astype(_BF16)
    wts = (
        p["w_in"].astype(_BF16), row2(p["b_in"]), p["conv_a_w"], row2(p["conv_a_b"]),
        _pack_block_diag(p["lru_wa"]).astype(_BF16), row2(p["lru_ba"]),
        _pack_block_diag(p["lru_wx"]).astype(_BF16), row2(p["lru_bx"]), row2(p["lru_lambda"]),
        p["conv_b_w"], p["w_out"].astype(_BF16), row2(p["ln1_g"]), row2(p["ln1_b"]),
        rwt_hi, rwt_lo, jnp.broadcast_to(p["router_b"][:, None], (N_EXPERTS, LANES)), tri,
    )

    ts_p = MIXER_ROWS // bp
    xp4 = jnp.transpose(xp, (1, 0, 2))[None]
    zeros = lambda *s: jnp.zeros(s, _F32)
    n_assign = total * TOP_K
    n_spare = 3 * MOE_ROWS
    x1_buf = zeros((total + n_spare // TOP_K) * 8, LANES)
    outs_p = _mixer_call(xp4, zeros(1, bp, D_MODEL), zeros(1, CONV_A_WIDTH - 1, bp, D_MODEL),
                         zeros(1, CONV_B_WIDTH - 1, bp, D_MODEL), zeros(N_EXPERTS, LANES), x1_buf, wts,
                         ts=ts_p, blk_off=0, alpha=alpha)
    nb_s = MIXER_ROWS // ss
    nblk_s = bs // nb_s
    xs4 = jnp.transpose(xs.reshape(nblk_s, nb_s, ss, D_MODEL), (0, 2, 1, 3))
    h0 = h_s0.reshape(nblk_s, nb_s, D_MODEL)
    ca0 = jnp.transpose(ca_s0.reshape(nblk_s, nb_s, CONV_A_WIDTH - 1, D_MODEL), (0, 2, 1, 3))
    cb0 = jnp.transpose(cb_s0.reshape(nblk_s, nb_s, CONV_B_WIDTH - 1, D_MODEL), (0, 2, 1, 3))
    outs_s = _mixer_call(xs4, h0, ca0, cb0, outs_p[7], outs_p[0], wts,
                         ts=ss, blk_off=tp // MIXER_ROWS, alpha=alpha)
    x1t = outs_s[0]

    n_blocks = -(-n_assign // MOE_ROWS) + N_EXPERTS + 1
    n_rows = n_blocks * MOE_ROWS
    idx = jnp.concatenate([outs_p[1][:TOP_K], outs_s[1][:TOP_K]], axis=1)
    rank = jnp.concatenate([outs_p[2][:TOP_K], outs_s[2][:TOP_K]], axis=1)
    grow = jnp.concatenate([outs_p[3], outs_s[3]], axis=0)
    counts = outs_s[7][:, 0].astype(jnp.int32)

    padded = (counts + MOE_ROWS - 1) // MOE_ROWS * MOE_ROWS
    end_pad = jnp.cumsum(padded)
    start_pad = end_pad - padded
    experts = jnp.arange(N_EXPERTS, dtype=jnp.int32)
    start_of = jnp.sum(jnp.where(idx[:, :, None] == experts, start_pad, 0), axis=-1)
    dest = (start_of + rank).reshape(n_assign)
    blk_start = jnp.arange(n_blocks, dtype=jnp.int32) * MOE_ROWS
    blk_e = jnp.minimum(jnp.sum((blk_start[:, None] >= end_pad[None, :]).astype(jnp.int32), axis=1),
                        N_EXPERTS - 1)
    n_used = (end_pad[-1] // MOE_ROWS).astype(jnp.int32).reshape(1)

    blk_ids = jnp.arange(n_blocks, dtype=jnp.int32)
    changed = jnp.concatenate([jnp.ones((1,), bool), blk_e[1:] != blk_e[:-1]])
    first = changed & (blk_ids < n_used[0])
    wslot = (jnp.cumsum(first.astype(jnp.int32)) - 1) % 2
    later_first = first[None, :] & (blk_ids[None, :] > blk_ids[:, None])
    nxt_blk = jnp.min(jnp.where(later_first, blk_ids[None, :], n_blocks), axis=1)
    nxt_e = jnp.sum(jnp.where(nxt_blk[:, None] == blk_ids[None, :], blk_e[None, :], 0), axis=1)
    nxt_e = jnp.where(nxt_blk < n_blocks, nxt_e, -1)

    inv = _invmap_call(dest, start_pad + counts, end_pad, total=total, n_rows=n_rows)
    ytok = _moe_call(blk_e, first.astype(jnp.int32), wslot, nxt_e, n_used, inv, x1t.reshape(-1, 8, LANES),
                     p["w_gu"], p["b_gu"][:, None, :], p["w_down"], p["b_down"][:, None, :],
                     n_slots=n_assign)
    yp, ys = _combine_call(x1t, grow, ytok, row2(p["ln2_g"]), row2(p["ln2_b"]),
                           prompt_shape=xp.shape, sample_shape=xs.shape, alpha=alpha)

    def batch_major(v, nblk):
        return jnp.transpose(v, (0, 2, 1, 3)).reshape(nblk * v.shape[2], v.shape[1], D_MODEL)

    states_p = (outs_p[4].reshape(bp, D_MODEL), batch_major(outs_p[5], 1), batch_major(outs_p[6], 1))
    states_s = (outs_s[4].reshape(bs, D_MODEL), batch_major(outs_s[5], nblk_s), batch_major(outs_s[6], nblk_s))
    return yp, ys, states_p, states_s


def kernel(x_prompt, x_sample, state_rglru_h, state_rglru_conv, state_shortconv, w_in, b_in, conv_a_w, conv_a_b, lru_wa, lru_ba, lru_wx, lru_bx, lru_lambda, conv_b_w, w_out, ln1_g, ln1_b, router_w, router_b, w_gu, b_gu, w_down, b_down, ln2_g, ln2_b):
    depth = w_in.shape[0]
    alpha = (2.0 * depth) ** 0.25
    names = ("w_in", "b_in", "conv_a_w", "conv_a_b", "lru_wa", "lru_ba", "lru_wx", "lru_bx", "lru_lambda",
             "conv_b_w", "w_out", "ln1_g", "ln1_b", "router_w", "router_b", "w_gu", "b_gu", "w_down",
             "b_down", "ln2_g", "ln2_b")
    stacked = (w_in, b_in, conv_a_w, conv_a_b, lru_wa, lru_ba, lru_wx, lru_bx, lru_lambda, conv_b_w, w_out,
               ln1_g, ln1_b, router_w, router_b, w_gu, b_gu, w_down, b_down, ln2_g, ln2_b)
    xp, xs = x_prompt, x_sample
    hp_l, cp_l, sp_l, hs_l, cs_l, ss_l = [], [], [], [], [], []
    for l in range(depth):
        p = {n: v[l] for n, v in zip(names, stacked)}
        xp, xs, (hp, cp, sp), (hs, cs, ss) = _layer(
            xp, xs, state_rglru_h[l], state_rglru_conv[l], state_shortconv[l], p, alpha=alpha)
        hp_l.append(hp); cp_l.append(cp); sp_l.append(sp)
        hs_l.append(hs); cs_l.append(cs); ss_l.append(ss)
    return (xp, xs, jnp.stack(hp_l), jnp.stack(cp_l), jnp.stack(sp_l), jnp.stack(hs_l), jnp.stack(cs_l),
            jnp.stack(ss_l))
```

```python
import functools

import jax
import jax.numpy as jnp
from jax import lax
from jax.experimental import pallas as pl
from jax.experimental.pallas import tpu as pltpu
from jax.experimental.pallas import tpu_sc as plsc

D_MODEL = 1024
LRU_HEADS = 16
LRU_BLOCK = D_MODEL // LRU_HEADS
LRU_C = 8.0
CONV_A_WIDTH = 4
CONV_B_WIDTH = 3
N_GROUPS = 7
N_EXPERTS = 32
TOP_K = 4
D_FF = D_MODEL
SWIGLU_LIMIT = 7.0
SWIGLU_ALPHA = 1.702
LN_EPS = 1e-5

GATE_TILE = 256
HEADS_PER_TILE = GATE_TILE // LRU_BLOCK
N_GATE_TILES = D_MODEL // GATE_TILE
LANES = 128
MIXER_ROWS = 512
MOE_ROWS = 256
COMBINE_ROWS = 512
VMEM_LIMIT = 58 * 1024 * 1024

_F32 = jnp.float32
_BF16 = jnp.bfloat16
_NT = (((1,), (1,)), ((), ()))


def _sigmoid(v):
    return 0.5 * jnp.tanh(0.5 * v) + 0.5


def _gelu_tanh(v):
    c = 0.7978845608028654
    return 0.5 * v * (1.0 + jnp.tanh(c * (v + 0.044715 * (v * v * v))))


def _layernorm(z, g, b):
    mu = jnp.mean(z, axis=-1, keepdims=True)
    zc = z - mu
    var = jnp.mean(zc * zc, axis=-1, keepdims=True)
    return zc * lax.rsqrt(var + LN_EPS) * g + b


def _mixer_kernel(x_ref, h0_ref, ca0_ref, cb0_ref, cnt0_ref, x1_buf_ref,
                  w_in_ref, b_in_ref, wca_ref, bca_ref, wa_ref, ba_ref, wx_ref, bx_ref, lam_ref,
                  wcb_ref, w_out_ref, g1_ref, be1_ref, rwt_hi_ref, rwt_lo_ref, rb_ref, tri_ref,
                  x1_ref, idx_ref, rank_ref, grow_ref, hl_ref, ca_ref, cb_ref, cnt_ref,
                  xa_s, u_s, a_s, b_s, h_s, hst_s, cnt_s, *, ts, nb, alpha):
    i = pl.program_id(0)
    j = pl.program_id(1)
    rows = ts * nb
    ta = (CONV_A_WIDTH - 1) * nb
    tb = (CONV_B_WIDTH - 1) * nb

    @pl.when(j == 0)
    def _():
        hst_s[...] = h0_ref[...]
        xa_s[0:ta, :] = ca0_ref[...].reshape(ta, D_MODEL)
        u_s[0:tb, :] = cb0_ref[...].reshape(tb, D_MODEL)

    @pl.when((i == 0) & (j == 0))
    def _():
        cnt_s[...] = cnt0_ref[...]

    x = x_ref[...].reshape(rows, D_MODEL)
    xb = x.astype(_BF16)

    def proj(g):
        lo, hi = g * D_MODEL, (g + 1) * D_MODEL
        return jnp.dot(xb, w_in_ref[:, lo:hi], preferred_element_type=_F32) + b_in_ref[:, lo:hi]

    xa_s[ta:ta + rows, :] = proj(0)
    xc = bca_ref[...] + xa_s[0:rows, :] * wca_ref[0:1, :]
    for k in range(1, CONV_A_WIDTH):
        xc = xc + xa_s[k * nb:k * nb + rows, :] * wca_ref[k:k + 1, :]
    new_ta = xa_s[rows:rows + ta, :]
    xa_s[0:ta, :] = new_ta
    ca_ref[...] = new_ta.reshape(CONV_A_WIDTH - 1, nb, D_MODEL)

    xcb = xc.astype(_BF16)

    def block_diag(w_ref):
        return jnp.concatenate(
            [jnp.dot(xcb[:, q * GATE_TILE:(q + 1) * GATE_TILE], w_ref[q], preferred_element_type=_F32)
             for q in range(N_GATE_TILES)], axis=-1)

    r = _sigmoid(block_diag(wa_ref) + ba_ref[...])
    ig = _sigmoid(block_diag(wx_ref) + bx_ref[...])
    nlam = -lam_ref[...]
    softplus = jnp.maximum(nlam, 0.0) + jnp.log1p(jnp.exp(-jnp.abs(nlam)))
    log_a = (-LRU_C * softplus) * r
    a = jnp.exp(log_a)
    a_s[...] = a
    b_s[...] = jnp.sqrt(-jnp.tanh(log_a) * (a * a + 1.0)) * (ig * xc)

    h = hst_s[...]
    for t in range(ts):
        sl = slice(t * nb, (t + 1) * nb)
        h = a_s[sl, :] * h + b_s[sl, :]
        h_s[sl, :] = h
    hst_s[...] = h
    hl_ref[...] = h

    a_s[...] = h_s[...] * _gelu_tanh(proj(1))

    u_s[tb:tb + rows, :] = proj(3) * proj(4)
    uc = u_s[0:rows, :] * wcb_ref[0:1, :]
    for k in range(1, CONV_B_WIDTH):
        uc = uc + u_s[k * nb:k * nb + rows, :] * wcb_ref[k:k + 1, :]
    new_tb = u_s[rows:rows + tb, :]
    u_s[0:tb, :] = new_tb
    cb_ref[...] = new_tb.reshape(CONV_B_WIDTH - 1, nb, D_MODEL)
    y_b = proj(2) * uc

    merged = _sigmoid(proj(5)) * a_s[...] + _sigmoid(proj(6)) * y_b
    mixed = jnp.dot(merged.astype(_BF16), w_out_ref[...], preferred_element_type=_F32)
    x1 = _layernorm(alpha * x + mixed, g1_ref[...], be1_ref[...])
    for s in range(8):
        x1_ref[pl.ds(s, rows, stride=8), :] = x1[:, s * LANES:(s + 1) * LANES]

    x1_hi = x1.astype(_BF16)
    x1_lo = (x1 - x1_hi.astype(_F32)).astype(_BF16)
    logits = (lax.dot_general(rwt_hi_ref[...], x1_hi, _NT, preferred_element_type=_F32)
              + lax.dot_general(rwt_hi_ref[...], x1_lo, _NT, preferred_element_type=_F32)
              + lax.dot_general(rwt_lo_ref[...], x1_hi, _NT, preferred_element_type=_F32)
              + rb_ref[:, 0:1])
    e_iota = lax.broadcasted_iota(jnp.int32, (N_EXPERTS, rows), 0)
    work = logits
    vals, sels, idxs = [], [], []
    for _ in range(TOP_K):
        m = jnp.max(work, axis=0, keepdims=True)
        ik = jnp.min(jnp.where(work == m, e_iota, N_EXPERTS), axis=0, keepdims=True)
        sel = e_iota == ik
        work = jnp.where(sel, -jnp.inf, work)
        vals.append(m)
        sels.append(sel)
        idxs.append(ik)
    exps = [jnp.exp(v - vals[0]) for v in vals]
    denom = exps[0] + exps[1] + exps[2] + exps[3]
    gates = [ex / denom for ex in exps]

    onehot = jnp.zeros((N_EXPERTS, rows), _F32)
    for sel in sels:
        onehot = onehot + sel.astype(_F32)
    prefix = jnp.dot(onehot.astype(_BF16), tri_ref[...], preferred_element_type=_F32)
    pos = prefix + cnt_s[:, 0:1]
    ranks = [jnp.sum(jnp.where(sel, pos, 0.0), axis=0, keepdims=True) for sel in sels]
    new_cnt = cnt_s[...] + jnp.sum(onehot, axis=1, keepdims=True)
    cnt_s[...] = new_cnt
    cnt_ref[...] = new_cnt

    row8 = lax.broadcasted_iota(jnp.int32, (8, rows), 0)
    idx8 = jnp.zeros((8, rows), jnp.int32)
    rank8 = jnp.zeros((8, rows), jnp.int32)
    for k in range(TOP_K):
        idx8 = jnp.where(row8 == k, idxs[k], idx8)
        rank8 = jnp.where(row8 == k, ranks[k].astype(jnp.int32), rank8)
    idx_ref[...] = idx8
    rank_ref[...] = rank8

    row_l = lax.broadcasted_iota(jnp.int32, (LANES, rows), 0)
    g_t = jnp.zeros((LANES, rows), _F32)
    for k in range(TOP_K):
        g_t = jnp.where(row_l == k, gates[k], g_t)
    grow_ref[...] = jnp.transpose(g_t)


def _const_spec(shape):
    nd = len(shape)
    return pl.BlockSpec(shape, lambda i, j: (0,) * nd, pipeline_mode=pl.Buffered(1))


def _mixer_call(x4, h0, ca0, cb0, cnt0, x1_buf, wts, *, ts, blk_off, alpha):
    nblk, seq, nb, _ = x4.shape
    nt = seq // ts
    rows = ts * nb
    total = nblk * seq * nb
    f32 = lambda *s: jax.ShapeDtypeStruct(s, _F32)
    i32 = lambda *s: jax.ShapeDtypeStruct(s, jnp.int32)
    in_specs = [
        pl.BlockSpec((None, ts, nb, D_MODEL), lambda i, j: (i, j, 0, 0)),
        pl.BlockSpec((None, nb, D_MODEL), lambda i, j: (i, 0, 0)),
        pl.BlockSpec((None, CONV_A_WIDTH - 1, nb, D_MODEL), lambda i, j: (i, 0, 0, 0)),
        pl.BlockSpec((None, CONV_B_WIDTH - 1, nb, D_MODEL), lambda i, j: (i, 0, 0, 0)),
        _const_spec(cnt0.shape),
        pl.BlockSpec(memory_space=pl.ANY),
    ] + [_const_spec(w.shape) for w in wts]
    out_shape = (
        jax.ShapeDtypeStruct(x1_buf.shape, _F32),
        i32(8, total),
        i32(8, total),
        f32(total, LANES),
        f32(nblk, nb, D_MODEL),
        f32(nblk, CONV_A_WIDTH - 1, nb, D_MODEL),
        f32(nblk, CONV_B_WIDTH - 1, nb, D_MODEL),
        f32(N_EXPERTS, LANES),
    )
    out_specs = (
        pl.BlockSpec((rows * 8, LANES), lambda i, j: (blk_off + i * nt + j, 0)),
        pl.BlockSpec((8, rows), lambda i, j: (0, i * nt + j)),
        pl.BlockSpec((8, rows), lambda i, j: (0, i * nt + j)),
        pl.BlockSpec((rows, LANES), lambda i, j: (i * nt + j, 0)),
        pl.BlockSpec((None, nb, D_MODEL), lambda i, j: (i, 0, 0)),
        pl.BlockSpec((None, CONV_A_WIDTH - 1, nb, D_MODEL), lambda i, j: (i, 0, 0, 0)),
        pl.BlockSpec((None, CONV_B_WIDTH - 1, nb, D_MODEL), lambda i, j: (i, 0, 0, 0)),
        pl.BlockSpec((N_EXPERTS, LANES), lambda i, j: (0, 0)),
    )
    scratch = [
        pltpu.VMEM((rows + (CONV_A_WIDTH - 1) * nb, D_MODEL), _F32),
        pltpu.VMEM((rows + (CONV_B_WIDTH - 1) * nb, D_MODEL), _F32),
        pltpu.VMEM((rows, D_MODEL), _F32),
        pltpu.VMEM((rows, D_MODEL), _F32),
        pltpu.VMEM((rows, D_MODEL), _F32),
        pltpu.VMEM((nb, D_MODEL), _F32),
        pltpu.VMEM((N_EXPERTS, LANES), _F32),
    ]
    return pl.pallas_call(
        functools.partial(_mixer_kernel, ts=ts, nb=nb, alpha=alpha),
        grid=(nblk, nt),
        in_specs=in_specs,
        out_specs=out_specs,
        out_shape=out_shape,
        scratch_shapes=scratch,
        input_output_aliases={5: 0},
        compiler_params=pltpu.CompilerParams(
            dimension_semantics=("arbitrary", "arbitrary"), vmem_limit_bytes=VMEM_LIMIT),
        name="mixer",
    )(x4, h0, ca0, cb0, cnt0, x1_buf, *wts)


def _invmap_kernel(dest_ref, lo_ref, hi_ref, inv_ref, *, total, n_rows):
    n_slots = total * TOP_K

    for r in range(MOE_ROWS):
        inv_ref[r] = n_slots + 2 * MOE_ROWS + r

    def fill(lo, hi):
        def body(row, c):
            inv_ref[MOE_ROWS + row] = n_slots + (row & (2 * MOE_ROWS - 1))
            return c
        lax.fori_loop(lo, hi, body, 0)

    def per_expert(e, c):
        fill(lo_ref[e], hi_ref[e])
        return c
    lax.fori_loop(0, N_EXPERTS, per_expert, 0)
    fill(hi_ref[N_EXPERTS - 1], n_rows)

    unroll = 16
    for k in range(TOP_K):
        def body(it, c, k=k):
            base = it * unroll
            rows = [dest_ref[k * total + base + u] for u in range(unroll)]
            val = base * TOP_K + k
            for u in range(unroll):
                inv_ref[MOE_ROWS + rows[u]] = val + u * TOP_K
            return c
        lax.fori_loop(0, total // unroll, body, 0)


def _invmap_call(dest, pad_lo, pad_hi, *, total, n_rows):
    smem = pl.BlockSpec(memory_space=pltpu.SMEM)
    return pl.pallas_call(
        functools.partial(_invmap_kernel, total=total, n_rows=n_rows),
        in_specs=[smem, smem, smem],
        out_specs=smem,
        out_shape=jax.ShapeDtypeStruct((n_rows + MOE_ROWS,), jnp.int32),
        name="invmap",
    )(dest, pad_lo, pad_hi)


SC_WORKERS = 32
SC_WINDOW = 128
SC_ROW = 16


def _invmap_sc_call(dest, vals, *, n_out):
    n = dest.shape[0]
    per_worker = n // SC_WORKERS
    n_windows = per_worker // SC_WINDOW
    assert per_worker * SC_WORKERS == n and n_windows * SC_WINDOW == per_worker
    mesh = plsc.VectorSubcoreMesh(core_axis_name="c", subcore_axis_name="s")

    @functools.partial(
        pl.kernel, mesh=mesh,
        out_type=jax.ShapeDtypeStruct((n_out, SC_ROW), jnp.int32),
        scratch_types=[pltpu.VMEM((SC_WINDOW,), jnp.int32),
                       pltpu.VMEM((SC_WINDOW, SC_ROW), jnp.int32),
                       pltpu.SemaphoreType.DMA],
        compiler_params=pltpu.CompilerParams(use_tc_tiling_on_sc=False),
        name="invmap_sc")
    def scatter(dest_hbm, vals_hbm, out_hbm, idx_v, rows_v, sem):
        worker = lax.axis_index("s") * 2 + lax.axis_index("c")

        @pl.loop(0, n_windows)
        def _(w):
            base = worker * per_worker + w * SC_WINDOW
            pltpu.sync_copy(dest_hbm.at[pl.ds(base, SC_WINDOW)], idx_v)
            pltpu.sync_copy(vals_hbm.at[pl.ds(base, SC_WINDOW)], rows_v)
            pltpu.async_copy(rows_v, out_hbm.at[idx_v], sem).wait()

    return scatter(dest, vals)


def _moe_kernel(blk_e_ref, first_ref, wslot_ref, nxt_e_ref, n_used_ref, inv_ref,
                x1_hbm, wgu_hbm, bgu_ref, wd_hbm, bd_ref,
                ytok_hbm,
                xb0, xb1, xb2, yb0, yb1, yb2, wgu_f, wd_f, wgu_s, wd_s, bgu_s, bd_s,
                gsem, ssem, wsem, *, n_slots):
    n_used = n_used_ref[0]
    n_blk = blk_e_ref.shape[0]
    xbufs = (xb0, xb1, xb2)
    ybufs = (yb0, yb1, yb2)

    def gather_copy(v, s, r):
        return pltpu.make_async_copy(
            x1_hbm.at[v >> 2], xbufs[s].at[pl.ds(r * 8, 8), :], gsem.at[s])

    def scatter_copy(v, s, r):
        return pltpu.make_async_copy(
            ybufs[s].at[pl.ds(r * 8, 8), :], ytok_hbm.at[v], ssem.at[s])

    def weight_copies(e, ws):
        return (pltpu.make_async_copy(wgu_hbm.at[e], wgu_f.at[ws], wsem.at[ws]),
                pltpu.make_async_copy(wd_hbm.at[e], wd_f.at[ws], wsem.at[ws]))

    def wait_rows(copy_fn, s):
        for r in range(MOE_ROWS):
            copy_fn(0, s, r).wait()

    for cp in weight_copies(blk_e_ref[0], 0):
        cp.start()
    for yb in ybufs:
        yb[...] = jnp.zeros(yb.shape, _F32)

    def prime(r, c):
        for s in range(2):
            pltpu.make_async_copy(
                ybufs[s].at[pl.ds(pl.multiple_of(r * 8, 8), 8), :],
                ytok_hbm.at[n_slots + s * MOE_ROWS + r], ssem.at[s]).start()
            pltpu.make_async_copy(
                x1_hbm.at[inv_ref[(s + 1) * MOE_ROWS + r] >> 2],
                xbufs[s].at[pl.ds(pl.multiple_of(r * 8, 8), 8), :], gsem.at[s]).start()
        return c
    lax.fori_loop(0, MOE_ROWS, prime, 0)

    def run_block(b, s):
        nxt = (s + 2) % 3
        wait_rows(gather_copy, s)
        wait_rows(scatter_copy, s)
        x = jnp.concatenate(
            [xbufs[s][pl.ds(q, MOE_ROWS, stride=8), :] for q in range(8)], axis=-1)
        gbase = (jnp.minimum(b + 2, n_blk - 1) + 1) * MOE_ROWS
        sbase = b * MOE_ROWS
        for r in range(MOE_ROWS):
            gather_copy(inv_ref[gbase + r], nxt, r).start()
        for r in range(MOE_ROWS):
            scatter_copy(inv_ref[sbase + r], nxt, r).start()
        gu = jnp.dot(x.astype(_BF16), wgu_s[...], preferred_element_type=_F32) + bgu_s[...]
        gate = jnp.minimum(gu[:, :D_FF], SWIGLU_LIMIT)
        up = jnp.clip(gu[:, D_FF:], -SWIGLU_LIMIT, SWIGLU_LIMIT)
        hmid = (up + 1.0) * (gate * _sigmoid(SWIGLU_ALPHA * gate))
        y = jnp.dot(hmid.astype(_BF16), wd_s[...], preferred_element_type=_F32) + bd_s[...]
        for q in range(8):
            ybufs[s][pl.ds(q, MOE_ROWS, stride=8), :] = y[:, q * LANES:(q + 1) * LANES]

    def body(b, c):
        @pl.when(first_ref[b] == 1)
        def _():
            e = blk_e_ref[b]
            ws = wslot_ref[b]
            for cp in weight_copies(e, ws):
                cp.wait()
            nxt_e = nxt_e_ref[b]

            @pl.when(nxt_e >= 0)
            def _():
                for cp in weight_copies(nxt_e, 1 - ws):
                    cp.start()
            wgu_s[...] = wgu_f[ws].astype(_BF16)
            wd_s[...] = wd_f[ws].astype(_BF16)
            bgu_s[...] = bgu_ref[e]
            bd_s[...] = bd_ref[e]

        for s in range(3):
            pl.when(b % 3 == s)(functools.partial(run_block, b, s))
        return c

    lax.fori_loop(0, n_used + 1, body, 0)

    for d in (1, 2):
        for s in range(3):
            @pl.when((n_used + d) % 3 == s)
            def _():
                wait_rows(gather_copy, s)
                wait_rows(scatter_copy, s)


def _moe_call(blk_e, first, wslot, nxt_e, n_used, inv, x1t, w_gu, b_gu, w_down, b_down, *, n_slots):
    full = lambda shape: pl.BlockSpec(shape, lambda i, *_: (0,) * len(shape))
    grid_spec = pltpu.PrefetchScalarGridSpec(
        num_scalar_prefetch=6,
        grid=(1,),
        in_specs=[
            pl.BlockSpec(memory_space=pl.ANY),
            pl.BlockSpec(memory_space=pl.ANY),
            full(b_gu.shape),
            pl.BlockSpec(memory_space=pl.ANY),
            full(b_down.shape),
        ],
        out_specs=pl.BlockSpec(memory_space=pl.ANY),
        scratch_shapes=[pltpu.VMEM((MOE_ROWS * 8, LANES), _F32)] * 6 + [
            pltpu.VMEM((2, D_MODEL, 2 * D_FF), _F32),
            pltpu.VMEM((2, D_FF, D_MODEL), _F32),
            pltpu.VMEM((D_MODEL, 2 * D_FF), _BF16),
            pltpu.VMEM((D_FF, D_MODEL), _BF16),
            pltpu.VMEM((1, 2 * D_FF), _F32),
            pltpu.VMEM((1, D_MODEL), _F32),
            pltpu.SemaphoreType.DMA((3,)),
            pltpu.SemaphoreType.DMA((3,)),
            pltpu.SemaphoreType.DMA((2,)),
        ],
    )
    return pl.pallas_call(
        functools.partial(_moe_kernel, n_slots=n_slots),
        grid_spec=grid_spec,
        out_shape=jax.ShapeDtypeStruct((n_slots + 3 * MOE_ROWS, 8, LANES), _F32),
        compiler_params=pltpu.CompilerParams(
            dimension_semantics=("arbitrary",), vmem_limit_bytes=VMEM_LIMIT),
        name="moe",
    )(blk_e, first, wslot, nxt_e, n_used, inv, x1t, w_gu, b_gu, w_down, b_down)


def _combine_kernel(x1_ref, g_ref, y0_ref, y1_ref, y2_ref, y3_ref, g2_ref, be2_ref,
                    yp_ref, ys_ref, y_s, *, alpha, n_prompt_steps):
    i = pl.program_id(0)
    g = g_ref[...]
    rows = g.shape[0]

    def token_major(ref):
        return jnp.concatenate([ref[pl.ds(s, rows, stride=8), :] for s in range(8)], axis=-1)

    ys = [token_major(r.reshape(rows * 8, LANES)) for r in (y0_ref, y1_ref, y2_ref, y3_ref)]
    moe = g[:, 0:1] * ys[0] + g[:, 1:2] * ys[1] + g[:, 2:3] * ys[2] + g[:, 3:4] * ys[3]
    y = _layernorm(alpha * token_major(x1_ref) + moe, g2_ref[...], be2_ref[...])
    for q in range(8):
        y_s[q] = y[:, q * LANES:(q + 1) * LANES]

    def batch_rows(bb, nb):
        return jnp.concatenate(
            [y_s[q, pl.ds(bb, rows // nb, stride=nb), :] for q in range(8)], axis=-1)

    @pl.when(i < n_prompt_steps)
    def _():
        for bb in range(yp_ref.shape[0]):
            yp_ref[bb] = batch_rows(bb, yp_ref.shape[0])

    @pl.when(i >= n_prompt_steps)
    def _():
        for bb in range(ys_ref.shape[0]):
            ys_ref[bb] = batch_rows(bb, ys_ref.shape[0])


def _combine_call(x1t, grow, ytok, g2, be2, *, prompt_shape, sample_shape, alpha):
    bp, sp, _ = prompt_shape
    bs, ss, _ = sample_shape
    ts_p = COMBINE_ROWS // bp
    nb_s = COMBINE_ROWS // ss
    n_p = sp // ts_p
    nt = n_p + bs // nb_s
    plane = lambda k: pl.BlockSpec((COMBINE_ROWS, None, 8, LANES), lambda i, k=k: (i, k, 0, 0))
    vec = pl.BlockSpec((1, D_MODEL), lambda i: (0, 0))
    ytok4 = ytok.reshape(-1, TOP_K, 8, LANES)
    return pl.pallas_call(
        functools.partial(_combine_kernel, alpha=alpha, n_prompt_steps=n_p),
        grid=(nt,),
        in_specs=[pl.BlockSpec((COMBINE_ROWS * 8, LANES), lambda i: (i, 0)),
                  pl.BlockSpec((COMBINE_ROWS, LANES), lambda i: (i, 0)),
                  plane(0), plane(1), plane(2), plane(3), vec, vec],
        out_specs=(pl.BlockSpec((bp, ts_p, D_MODEL), lambda i: (0, jnp.minimum(i, n_p - 1), 0)),
                   pl.BlockSpec((nb_s, ss, D_MODEL), lambda i: (jnp.maximum(i - n_p, 0), 0, 0))),
        out_shape=(jax.ShapeDtypeStruct(prompt_shape, _F32), jax.ShapeDtypeStruct(sample_shape, _F32)),
        scratch_shapes=[pltpu.VMEM((8, COMBINE_ROWS, LANES), _F32)],
        compiler_params=pltpu.CompilerParams(
            dimension_semantics=("arbitrary",), vmem_limit_bytes=VMEM_LIMIT),
        name="combine",
    )(x1t, grow, ytok4, ytok4, ytok4, ytok4, g2, be2)


def _pack_block_diag(w):
    w = w.reshape(N_GATE_TILES, HEADS_PER_TILE, LRU_BLOCK, LRU_BLOCK)
    eye = jnp.eye(HEADS_PER_TILE, dtype=w.dtype)
    t = jnp.einsum("qhij,hg->qhigj", w, eye)
    return t.reshape(N_GATE_TILES, GATE_TILE, GATE_TILE)


def _layer(xp, xs, h_s0, ca_s0, cb_s0, p, *, alpha):
    bp, sp, _ = xp.shape
    bs, ss, _ = xs.shape
    tp, tsm = bp * sp, bs * ss
    total = tp + tsm
    row2 = lambda v: v.reshape(1, -1)

    rwt = jnp.transpose(p["router_w"])
    rwt_hi = rwt.astype(_BF16)
    rwt_lo = (rwt - rwt_hi.astype(_F32)).astype(_BF16)
    ii = jnp.arange(MIXER_ROWS)
    tri = (ii[:, None] < ii[None, :]).astype(_BF16)
    wts = (
        p["w_in"].astype(_BF16), row2(p["b_in"]), p["conv_a_w"], row2(p["conv_a_b"]),
        _pack_block_diag(p["lru_wa"]).astype(_BF16), row2(p["lru_ba"]),
        _pack_block_diag(p["lru_wx"]).astype(_BF16), row2(p["lru_bx"]), row2(p["lru_lambda"]),
        p["conv_b_w"], p["w_out"].astype(_BF16), row2(p["ln1_g"]), row2(p["ln1_b"]),
        rwt_hi, rwt_lo, jnp.broadcast_to(p["router_b"][:, None], (N_EXPERTS, LANES)), tri,
    )

    ts_p = MIXER_ROWS // bp
    xp4 = jnp.transpose(xp, (1, 0, 2))[None]
    zeros = lambda *s: jnp.zeros(s, _F32)
    n_assign = total * TOP_K
    n_spare = 3 * MOE_ROWS
    x1_buf = zeros((total + n_spare // TOP_K) * 8, LANES)
    outs_p = _mixer_call(xp4, zeros(1, bp, D_MODEL), zeros(1, CONV_A_WIDTH - 1, bp, D_MODEL),
                         zeros(1, CONV_B_WIDTH - 1, bp, D_MODEL), zeros(N_EXPERTS, LANES), x1_buf, wts,
                         ts=ts_p, blk_off=0, alpha=alpha)
    nb_s = MIXER_ROWS // ss
    nblk_s = bs // nb_s
    xs4 = jnp.transpose(xs.reshape(nblk_s, nb_s, ss, D_MODEL), (0, 2, 1, 3))
    h0 = h_s0.reshape(nblk_s, nb_s, D_MODEL)
    ca0 = jnp.transpose(ca_s0.reshape(nblk_s, nb_s, CONV_A_WIDTH - 1, D_MODEL), (0, 2, 1, 3))
    cb0 = jnp.transpose(cb_s0.reshape(nblk_s, nb_s, CONV_B_WIDTH - 1, D_MODEL), (0, 2, 1, 3))
    outs_s = _mixer_call(xs4, h0, ca0, cb0, outs_p[7], outs_p[0], wts,
                         ts=ss, blk_off=tp // MIXER_ROWS, alpha=alpha)
    x1t = outs_s[0]

    n_blocks = -(-n_assign // MOE_ROWS) + N_EXPERTS + 1
    n_rows = n_blocks * MOE_ROWS
    idx = jnp.concatenate([outs_p[1][:TOP_K], outs_s[1][:TOP_K]], axis=1)
    rank = jnp.concatenate([outs_p[2][:TOP_K], outs_s[2][:TOP_K]], axis=1)
    grow = jnp.concatenate([outs_p[3], outs_s[3]], axis=0)
    counts = outs_s[7][:, 0].astype(jnp.int32)

    padded = (counts + MOE_ROWS - 1) // MOE_ROWS * MOE_ROWS
    end_pad = jnp.cumsum(padded)
    start_pad = end_pad - padded
    experts = jnp.arange(N_EXPERTS, dtype=jnp.int32)
    start_of = jnp.sum(jnp.where(idx[:, :, None] == experts, start_pad, 0), axis=-1)
    dest = (start_of + rank).reshape(n_assign)
    blk_start = jnp.arange(n_blocks, dtype=jnp.int32) * MOE_ROWS
    blk_e = jnp.minimum(jnp.sum((blk_start[:, None] >= end_pad[None, :]).astype(jnp.int32), axis=1),
                        N_EXPERTS - 1)
    n_used = (end_pad[-1] // MOE_ROWS).astype(jnp.int32).reshape(1)

    blk_ids = jnp.arange(n_blocks, dtype=jnp.int32)
    changed = jnp.concatenate([jnp.ones((1,), bool), blk_e[1:] != blk_e[:-1]])
    first = changed & (blk_ids < n_used[0])
    wslot = (jnp.cumsum(first.astype(jnp.int32)) - 1) % 2
    later_first = first[None, :] & (blk_ids[None, :] > blk_ids[:, None])
    nxt_blk = jnp.min(jnp.where(later_first, blk_ids[None, :], n_blocks), axis=1)
    nxt_e = jnp.sum(jnp.where(nxt_blk[:, None] == blk_ids[None, :], blk_e[None, :], 0), axis=1)
    nxt_e = jnp.where(nxt_blk < n_blocks, nxt_e, -1)

    flat = jnp.arange(n_assign, dtype=jnp.int32)
    slot_vals = (flat % total) * TOP_K + flat // total
    ext = jnp.arange(n_rows + MOE_ROWS, dtype=jnp.int32)
    row = ext - MOE_ROWS
    holds = jnp.any((row[:, None] >= start_pad[None, :]) & (row[:, None] < (start_pad + counts)[None, :]),
                    axis=1)
    spare = jnp.where(row < 0, n_assign + 2 * MOE_ROWS + ext, n_assign + (row & (2 * MOE_ROWS - 1)))
    scattered = _invmap_sc_call(dest + MOE_ROWS, jnp.broadcast_to(slot_vals[:, None], (n_assign, SC_ROW)),
                                n_out=n_rows + MOE_ROWS)
    inv = jnp.where(holds, scattered[:, 0], spare)
    ytok = _moe_call(blk_e, first.astype(jnp.int32), wslot, nxt_e, n_used, inv, x1t.reshape(-1, 8, LANES),
                     p["w_gu"], p["b_gu"][:, None, :], p["w_down"], p["b_down"][:, None, :],
                     n_slots=n_assign)
    yp, ys = _combine_call(x1t, grow, ytok, row2(p["ln2_g"]), row2(p["ln2_b"]),
                           prompt_shape=xp.shape, sample_shape=xs.shape, alpha=alpha)

    def batch_major(v, nblk):
        return jnp.transpose(v, (0, 2, 1, 3)).reshape(nblk * v.shape[2], v.shape[1], D_MODEL)

    states_p = (outs_p[4].reshape(bp, D_MODEL), batch_major(outs_p[5], 1), batch_major(outs_p[6], 1))
    states_s = (outs_s[4].reshape(bs, D_MODEL), batch_major(outs_s[5], nblk_s), batch_major(outs_s[6], nblk_s))
    return yp, ys, states_p, states_s


def kernel(x_prompt, x_sample, state_rglru_h, state_rglru_conv, state_shortconv, w_in, b_in, conv_a_w, conv_a_b, lru_wa, lru_ba, lru_wx, lru_bx, lru_lambda, conv_b_w, w_out, ln1_g, ln1_b, router_w, router_b, w_gu, b_gu, w_down, b_down, ln2_g, ln2_b):
    depth = w_in.shape[0]
    alpha = (2.0 * depth) ** 0.25
    names = ("w_in", "b_in", "conv_a_w", "conv_a_b", "lru_wa", "lru_ba", "lru_wx", "lru_bx", "lru_lambda",
             "conv_b_w", "w_out", "ln1_g", "ln1_b", "router_w", "router_b", "w_gu", "b_gu", "w_down",
             "b_down", "ln2_g", "ln2_b")
    stacked = (w_in, b_in, conv_a_w, conv_a_b, lru_wa, lru_ba, lru_wx, lru_bx, lru_lambda, conv_b_w, w_out,
               ln1_g, ln1_b, router_w, router_b, w_gu, b_gu, w_down, b_down, ln2_g, ln2_b)
    xp, xs = x_prompt, x_sample
    hp_l, cp_l, sp_l, hs_l, cs_l, ss_l = [], [], [], [], [], []
    for l in range(depth):
        p = {n: v[l] for n, v in zip(names, stacked)}
        xp, xs, (hp, cp, sp), (hs, cs, ss) = _layer(
            xp, xs, state_rglru_h[l], state_rglru_conv[l], state_shortconv[l], p, alpha=alpha)
        hp_l.append(hp); cp_l.append(cp); sp_l.append(sp)
        hs_l.append(hs); cs_l.append(cs); ss_l.append(ss)
    return (xp, xs, jnp.stack(hp_l), jnp.stack(cp_l), jnp.stack(sp_l), jnp.stack(hs_l), jnp.stack(cs_l),
            jnp.stack(ss_l))
```

```python
import functools

import jax
import jax.numpy as jnp
from jax import lax
from jax.experimental import pallas as pl
from jax.experimental.pallas import tpu as pltpu
from jax.experimental.pallas import tpu_sc as plsc

D_MODEL = 1024
LRU_HEADS = 16
LRU_BLOCK = D_MODEL // LRU_HEADS
LRU_C = 8.0
CONV_A_WIDTH = 4
CONV_B_WIDTH = 3
N_GROUPS = 7
N_EXPERTS = 32
TOP_K = 4
D_FF = D_MODEL
SWIGLU_LIMIT = 7.0
SWIGLU_ALPHA = 1.702
LN_EPS = 1e-5

GATE_TILE = 256
HEADS_PER_TILE = GATE_TILE // LRU_BLOCK
N_GATE_TILES = D_MODEL // GATE_TILE
LANES = 128
MIXER_ROWS = 512
MOE_ROWS = 256
COMBINE_ROWS = 512
VMEM_LIMIT = 58 * 1024 * 1024

_F32 = jnp.float32
_BF16 = jnp.bfloat16
_NT = (((1,), (1,)), ((), ()))


def _sigmoid(v):
    return 0.5 * jnp.tanh(0.5 * v) + 0.5


def _gelu_tanh(v):
    c = 0.7978845608028654
    return 0.5 * v * (1.0 + jnp.tanh(c * (v + 0.044715 * (v * v * v))))


def _layernorm(z, g, b):
    mu = jnp.mean(z, axis=-1, keepdims=True)
    zc = z - mu
    var = jnp.mean(zc * zc, axis=-1, keepdims=True)
    return zc * lax.rsqrt(var + LN_EPS) * g + b


def _mixer_kernel(x_ref, h0_ref, ca0_ref, cb0_ref, cnt0_ref, x1_buf_ref,
                  w_in_ref, b_in_ref, wca_ref, bca_ref, wa_ref, ba_ref, wx_ref, bx_ref, lam_ref,
                  wcb_ref, w_out_ref, g1_ref, be1_ref, rwt_hi_ref, rwt_lo_ref, rb_ref, tri_ref,
                  x1_ref, idx_ref, rank_ref, grow_ref, hl_ref, ca_ref, cb_ref, cnt_ref,
                  xa_s, u_s, a_s, b_s, h_s, hst_s, cnt_s, *, ts, nb, alpha):
    i = pl.program_id(0)
    j = pl.program_id(1)
    rows = ts * nb
    ta = (CONV_A_WIDTH - 1) * nb
    tb = (CONV_B_WIDTH - 1) * nb

    @pl.when(j == 0)
    def _():
        hst_s[...] = h0_ref[...]
        xa_s[0:ta, :] = ca0_ref[...].reshape(ta, D_MODEL)
        u_s[0:tb, :] = cb0_ref[...].reshape(tb, D_MODEL)

    @pl.when((i == 0) & (j == 0))
    def _():
        cnt_s[...] = cnt0_ref[...]

    x = x_ref[...].reshape(rows, D_MODEL)
    xb = x.astype(_BF16)

    def proj(g):
        lo, hi = g * D_MODEL, (g + 1) * D_MODEL
        return jnp.dot(xb, w_in_ref[:, lo:hi], preferred_element_type=_F32) + b_in_ref[:, lo:hi]

    xa_s[ta:ta + rows, :] = proj(0)
    xc = bca_ref[...] + xa_s[0:rows, :] * wca_ref[0:1, :]
    for k in range(1, CONV_A_WIDTH):
        xc = xc + xa_s[k * nb:k * nb + rows, :] * wca_ref[k:k + 1, :]
    new_ta = xa_s[rows:rows + ta, :]
    xa_s[0:ta, :] = new_ta
    ca_ref[...] = new_ta.reshape(CONV_A_WIDTH - 1, nb, D_MODEL)

    xcb = xc.astype(_BF16)

    def block_diag(w_ref):
        return jnp.concatenate(
            [jnp.dot(xcb[:, q * GATE_TILE:(q + 1) * GATE_TILE], w_ref[q], preferred_element_type=_F32)
             for q in range(N_GATE_TILES)], axis=-1)

    r = _sigmoid(block_diag(wa_ref) + ba_ref[...])
    ig = _sigmoid(block_diag(wx_ref) + bx_ref[...])
    nlam = -lam_ref[...]
    softplus = jnp.maximum(nlam, 0.0) + jnp.log1p(jnp.exp(-jnp.abs(nlam)))
    log_a = (-LRU_C * softplus) * r
    a = jnp.exp(log_a)
    a_s[...] = a
    b_s[...] = jnp.sqrt(-jnp.tanh(log_a) * (a * a + 1.0)) * (ig * xc)

    h = hst_s[...]
    for t in range(ts):
        sl = slice(t * nb, (t + 1) * nb)
        h = a_s[sl, :] * h + b_s[sl, :]
        h_s[sl, :] = h
    hst_s[...] = h
    hl_ref[...] = h

    a_s[...] = h_s[...] * _gelu_tanh(proj(1))

    u_s[tb:tb + rows, :] = proj(3) * proj(4)
    uc = u_s[0:rows, :] * wcb_ref[0:1, :]
    for k in range(1, CONV_B_WIDTH):
        uc = uc + u_s[k * nb:k * nb + rows, :] * wcb_ref[k:k + 1, :]
    new_tb = u_s[rows:rows + tb, :]
    u_s[0:tb, :] = new_tb
    cb_ref[...] = new_tb.reshape(CONV_B_WIDTH - 1, nb, D_MODEL)
    y_b = proj(2) * uc

    merged = _sigmoid(proj(5)) * a_s[...] + _sigmoid(proj(6)) * y_b
    mixed = jnp.dot(merged.astype(_BF16), w_out_ref[...], preferred_element_type=_F32)
    x1 = _layernorm(alpha * x + mixed, g1_ref[...], be1_ref[...])
    for s in range(8):
        x1_ref[pl.ds(s, rows, stride=8), :] = x1[:, s * LANES:(s + 1) * LANES]

    x1_hi = x1.astype(_BF16)
    x1_lo = (x1 - x1_hi.astype(_F32)).astype(_BF16)
    logits = (lax.dot_general(rwt_hi_ref[...], x1_hi, _NT, preferred_element_type=_F32)
              + lax.dot_general(rwt_hi_ref[...], x1_lo, _NT, preferred_element_type=_F32)
              + lax.dot_general(rwt_lo_ref[...], x1_hi, _NT, preferred_element_type=_F32)
              + rb_ref[:, 0:1])
    e_iota = lax.broadcasted_iota(jnp.int32, (N_EXPERTS, rows), 0)
    work = logits
    vals, sels, idxs = [], [], []
    for _ in range(TOP_K):
        m = jnp.max(work, axis=0, keepdims=True)
        ik = jnp.min(jnp.where(work == m, e_iota, N_EXPERTS), axis=0, keepdims=True)
        sel = e_iota == ik
        work = jnp.where(sel, -jnp.inf, work)
        vals.append(m)
        sels.append(sel)
        idxs.append(ik)
    exps = [jnp.exp(v - vals[0]) for v in vals]
    denom = exps[0] + exps[1] + exps[2] + exps[3]
    gates = [ex / denom for ex in exps]

    onehot = jnp.zeros((N_EXPERTS, rows), _F32)
    for sel in sels:
        onehot = onehot + sel.astype(_F32)
    prefix = jnp.dot(onehot.astype(_BF16), tri_ref[...], preferred_element_type=_F32)
    pos = prefix + cnt_s[:, 0:1]
    ranks = [jnp.sum(jnp.where(sel, pos, 0.0), axis=0, keepdims=True) for sel in sels]
    new_cnt = cnt_s[...] + jnp.sum(onehot, axis=1, keepdims=True)
    cnt_s[...] = new_cnt
    cnt_ref[...] = new_cnt

    row8 = lax.broadcasted_iota(jnp.int32, (8, rows), 0)
    idx8 = jnp.zeros((8, rows), jnp.int32)
    rank8 = jnp.zeros((8, rows), jnp.int32)
    for k in range(TOP_K):
        idx8 = jnp.where(row8 == k, idxs[k], idx8)
        rank8 = jnp.where(row8 == k, ranks[k].astype(jnp.int32), rank8)
    idx_ref[...] = idx8
    rank_ref[...] = rank8

    row_l = lax.broadcasted_iota(jnp.int32, (LANES, rows), 0)
    g_t = jnp.zeros((LANES, rows), _F32)
    for k in range(TOP_K):
        g_t = jnp.where(row_l == k, gates[k], g_t)
    grow_ref[...] = jnp.transpose(g_t)


def _const_spec(shape):
    nd = len(shape)
    return pl.BlockSpec(shape, lambda i, j: (0,) * nd, pipeline_mode=pl.Buffered(1))


def _mixer_call(x4, h0, ca0, cb0, cnt0, x1_buf, wts, *, ts, blk_off, alpha):
    nblk, seq, nb, _ = x4.shape
    nt = seq // ts
    rows = ts * nb
    total = nblk * seq * nb
    f32 = lambda *s: jax.ShapeDtypeStruct(s, _F32)
    i32 = lambda *s: jax.ShapeDtypeStruct(s, jnp.int32)
    in_specs = [
        pl.BlockSpec((None, ts, nb, D_MODEL), lambda i, j: (i, j, 0, 0)),
        pl.BlockSpec((None, nb, D_MODEL), lambda i, j: (i, 0, 0)),
        pl.BlockSpec((None, CONV_A_WIDTH - 1, nb, D_MODEL), lambda i, j: (i, 0, 0, 0)),
        pl.BlockSpec((None, CONV_B_WIDTH - 1, nb, D_MODEL), lambda i, j: (i, 0, 0, 0)),
        _const_spec(cnt0.shape),
        pl.BlockSpec(memory_space=pl.ANY),
    ] + [_const_spec(w.shape) for w in wts]
    out_shape = (
        jax.ShapeDtypeStruct(x1_buf.shape, _F32),
        i32(8, total),
        i32(8, total),
        f32(total, LANES),
        f32(nblk, nb, D_MODEL),
        f32(nblk, CONV_A_WIDTH - 1, nb, D_MODEL),
        f32(nblk, CONV_B_WIDTH - 1, nb, D_MODEL),
        f32(N_EXPERTS, LANES),
    )
    out_specs = (
        pl.BlockSpec((rows * 8, LANES), lambda i, j: (blk_off + i * nt + j, 0)),
        pl.BlockSpec((8, rows), lambda i, j: (0, i * nt + j)),
        pl.BlockSpec((8, rows), lambda i, j: (0, i * nt + j)),
        pl.BlockSpec((rows, LANES), lambda i, j: (i * nt + j, 0)),
        pl.BlockSpec((None, nb, D_MODEL), lambda i, j: (i, 0, 0)),
        pl.BlockSpec((None, CONV_A_WIDTH - 1, nb, D_MODEL), lambda i, j: (i, 0, 0, 0)),
        pl.BlockSpec((None, CONV_B_WIDTH - 1, nb, D_MODEL), lambda i, j: (i, 0, 0, 0)),
        pl.BlockSpec((N_EXPERTS, LANES), lambda i, j: (0, 0)),
    )
    scratch = [
        pltpu.VMEM((rows + (CONV_A_WIDTH - 1) * nb, D_MODEL), _F32),
        pltpu.VMEM((rows + (CONV_B_WIDTH - 1) * nb, D_MODEL), _F32),
        pltpu.VMEM((rows, D_MODEL), _F32),
        pltpu.VMEM((rows, D_MODEL), _F32),
        pltpu.VMEM((rows, D_MODEL), _F32),
        pltpu.VMEM((nb, D_MODEL), _F32),
        pltpu.VMEM((N_EXPERTS, LANES), _F32),
    ]
    return pl.pallas_call(
        functools.partial(_mixer_kernel, ts=ts, nb=nb, alpha=alpha),
        grid=(nblk, nt),
        in_specs=in_specs,
        out_specs=out_specs,
        out_shape=out_shape,
        scratch_shapes=scratch,
        input_output_aliases={5: 0},
        compiler_params=pltpu.CompilerParams(
            dimension_semantics=("arbitrary", "arbitrary"), vmem_limit_bytes=VMEM_LIMIT),
        name="mixer",
    )(x4, h0, ca0, cb0, cnt0, x1_buf, *wts)


def _invmap_kernel(dest_ref, lo_ref, hi_ref, inv_ref, *, total, n_rows):
    n_slots = total * TOP_K

    for r in range(MOE_ROWS):
        inv_ref[r] = n_slots + 2 * MOE_ROWS + r

    def fill(lo, hi):
        def body(row, c):
            inv_ref[MOE_ROWS + row] = n_slots + (row & (2 * MOE_ROWS - 1))
            return c
        lax.fori_loop(lo, hi, body, 0)

    def per_expert(e, c):
        fill(lo_ref[e], hi_ref[e])
        return c
    lax.fori_loop(0, N_EXPERTS, per_expert, 0)
    fill(hi_ref[N_EXPERTS - 1], n_rows)

    unroll = 16
    for k in range(TOP_K):
        def body(it, c, k=k):
            base = it * unroll
            rows = [dest_ref[k * total + base + u] for u in range(unroll)]
            val = base * TOP_K + k
            for u in range(unroll):
                inv_ref[MOE_ROWS + rows[u]] = val + u * TOP_K
            return c
        lax.fori_loop(0, total // unroll, body, 0)


def _invmap_call(dest, pad_lo, pad_hi, *, total, n_rows):
    smem = pl.BlockSpec(memory_space=pltpu.SMEM)
    return pl.pallas_call(
        functools.partial(_invmap_kernel, total=total, n_rows=n_rows),
        in_specs=[smem, smem, smem],
        out_specs=smem,
        out_shape=jax.ShapeDtypeStruct((n_rows + MOE_ROWS,), jnp.int32),
        name="invmap",
    )(dest, pad_lo, pad_hi)


SC_WORKERS = 32
SC_WINDOW = 128
SC_LANES = 16


def _invmap_sc_call(dest, *, total, n_out):
    n = dest.shape[0]
    per_worker = n // SC_WORKERS
    n_windows = per_worker // SC_WINDOW
    assert per_worker * SC_WORKERS == n and n_windows * SC_WINDOW == per_worker
    assert total % per_worker == 0
    mesh = plsc.VectorSubcoreMesh(core_axis_name="c", subcore_axis_name="s")

    @functools.partial(
        pl.kernel, mesh=mesh,
        out_type=jax.ShapeDtypeStruct((n_out,), jnp.int32),
        scratch_types=[pltpu.VMEM((SC_WINDOW,), jnp.int32),
                       pltpu.VMEM((SC_WINDOW,), jnp.int32),
                       pltpu.SemaphoreType.DMA],
        compiler_params=pltpu.CompilerParams(use_tc_tiling_on_sc=False, needs_layout_passes=False),
        name="invmap_sc")
    def scatter(dest_hbm, out_hbm, idx_v, val_v, sem):
        worker = lax.axis_index("s") * 2 + lax.axis_index("c")
        first = worker * per_worker
        k = first // total
        lane = lax.iota(jnp.int32, SC_LANES)

        @pl.loop(0, n_windows)
        def _(w):
            base = first + w * SC_WINDOW
            pltpu.sync_copy(dest_hbm.at[pl.ds(base, SC_WINDOW)], idx_v)
            for j in range(SC_WINDOW // SC_LANES):
                tok = base - k * total + j * SC_LANES + lane
                val_v[pl.ds(j * SC_LANES, SC_LANES)] = tok * TOP_K + k
            pltpu.async_copy(val_v, out_hbm.at[idx_v], sem).wait()

    return scatter(dest)


def _moe_kernel(blk_e_ref, first_ref, wslot_ref, nxt_e_ref, n_used_ref, inv_ref,
                x1_hbm, wgu_hbm, bgu_ref, wd_hbm, bd_ref,
                ytok_hbm,
                xb0, xb1, xb2, yb0, yb1, yb2, wgu_f, wd_f, wgu_s, wd_s, bgu_s, bd_s,
                gsem, ssem, wsem, *, n_slots):
    n_used = n_used_ref[0]
    n_blk = blk_e_ref.shape[0]
    xbufs = (xb0, xb1, xb2)
    ybufs = (yb0, yb1, yb2)

    def gather_copy(v, s, r):
        return pltpu.make_async_copy(
            x1_hbm.at[v >> 2], xbufs[s].at[pl.ds(r * 8, 8), :], gsem.at[s])

    def scatter_copy(v, s, r):
        return pltpu.make_async_copy(
            ybufs[s].at[pl.ds(r * 8, 8), :], ytok_hbm.at[v], ssem.at[s])

    def weight_copies(e, ws):
        return (pltpu.make_async_copy(wgu_hbm.at[e], wgu_f.at[ws], wsem.at[ws]),
                pltpu.make_async_copy(wd_hbm.at[e], wd_f.at[ws], wsem.at[ws]))

    def wait_rows(copy_fn, s):
        for r in range(MOE_ROWS):
            copy_fn(0, s, r).wait()

    for cp in weight_copies(blk_e_ref[0], 0):
        cp.start()
    for yb in ybufs:
        yb[...] = jnp.zeros(yb.shape, _F32)

    def prime(r, c):
        for s in range(2):
            pltpu.make_async_copy(
                ybufs[s].at[pl.ds(pl.multiple_of(r * 8, 8), 8), :],
                ytok_hbm.at[n_slots + s * MOE_ROWS + r], ssem.at[s]).start()
            pltpu.make_async_copy(
                x1_hbm.at[inv_ref[(s + 1) * MOE_ROWS + r] >> 2],
                xbufs[s].at[pl.ds(pl.multiple_of(r * 8, 8), 8), :], gsem.at[s]).start()
        return c
    lax.fori_loop(0, MOE_ROWS, prime, 0)

    def run_block(b, s):
        nxt = (s + 2) % 3
        wait_rows(gather_copy, s)
        wait_rows(scatter_copy, s)
        x = jnp.concatenate(
            [xbufs[s][pl.ds(q, MOE_ROWS, stride=8), :] for q in range(8)], axis=-1)
        gbase = (jnp.minimum(b + 2, n_blk - 1) + 1) * MOE_ROWS
        sbase = b * MOE_ROWS
        for r in range(MOE_ROWS):
            gather_copy(inv_ref[gbase + r], nxt, r).start()
        for r in range(MOE_ROWS):
            scatter_copy(inv_ref[sbase + r], nxt, r).start()
        gu = jnp.dot(x.astype(_BF16), wgu_s[...], preferred_element_type=_F32) + bgu_s[...]
        gate = jnp.minimum(gu[:, :D_FF], SWIGLU_LIMIT)
        up = jnp.clip(gu[:, D_FF:], -SWIGLU_LIMIT, SWIGLU_LIMIT)
        hmid = (up + 1.0) * (gate * _sigmoid(SWIGLU_ALPHA * gate))
        y = jnp.dot(hmid.astype(_BF16), wd_s[...], preferred_element_type=_F32) + bd_s[...]
        for q in range(8):
            ybufs[s][pl.ds(q, MOE_ROWS, stride=8), :] = y[:, q * LANES:(q + 1) * LANES]

    def body(b, c):
        @pl.when(first_ref[b] == 1)
        def _():
            e = blk_e_ref[b]
            ws = wslot_ref[b]
            for cp in weight_copies(e, ws):
                cp.wait()
            nxt_e = nxt_e_ref[b]

            @pl.when(nxt_e >= 0)
            def _():
                for cp in weight_copies(nxt_e, 1 - ws):
                    cp.start()
            wgu_s[...] = wgu_f[ws].astype(_BF16)
            wd_s[...] = wd_f[ws].astype(_BF16)
            bgu_s[...] = bgu_ref[e]
            bd_s[...] = bd_ref[e]

        for s in range(3):
            pl.when(b % 3 == s)(functools.partial(run_block, b, s))
        return c

    lax.fori_loop(0, n_used + 1, body, 0)

    for d in (1, 2):
        for s in range(3):
            @pl.when((n_used + d) % 3 == s)
            def _():
                wait_rows(gather_copy, s)
                wait_rows(scatter_copy, s)


def _moe_call(blk_e, first, wslot, nxt_e, n_used, inv, x1t, w_gu, b_gu, w_down, b_down, *, n_slots):
    full = lambda shape: pl.BlockSpec(shape, lambda i, *_: (0,) * len(shape))
    grid_spec = pltpu.PrefetchScalarGridSpec(
        num_scalar_prefetch=6,
        grid=(1,),
        in_specs=[
            pl.BlockSpec(memory_space=pl.ANY),
            pl.BlockSpec(memory_space=pl.ANY),
            full(b_gu.shape),
            pl.BlockSpec(memory_space=pl.ANY),
            full(b_down.shape),
        ],
        out_specs=pl.BlockSpec(memory_space=pl.ANY),
        scratch_shapes=[pltpu.VMEM((MOE_ROWS * 8, LANES), _F32)] * 6 + [
            pltpu.VMEM((2, D_MODEL, 2 * D_FF), _F32),
            pltpu.VMEM((2, D_FF, D_MODEL), _F32),
            pltpu.VMEM((D_MODEL, 2 * D_FF), _BF16),
            pltpu.VMEM((D_FF, D_MODEL), _BF16),
            pltpu.VMEM((1, 2 * D_FF), _F32),
            pltpu.VMEM((1, D_MODEL), _F32),
            pltpu.SemaphoreType.DMA((3,)),
            pltpu.SemaphoreType.DMA((3,)),
            pltpu.SemaphoreType.DMA((2,)),
        ],
    )
    return pl.pallas_call(
        functools.partial(_moe_kernel, n_slots=n_slots),
        grid_spec=grid_spec,
        out_shape=jax.ShapeDtypeStruct((n_slots + 3 * MOE_ROWS, 8, LANES), _F32),
        compiler_params=pltpu.CompilerParams(
            dimension_semantics=("arbitrary",), vmem_limit_bytes=VMEM_LIMIT),
        name="moe",
    )(blk_e, first, wslot, nxt_e, n_used, inv, x1t, w_gu, b_gu, w_down, b_down)


def _combine_kernel(x1_ref, g_ref, y0_ref, y1_ref, y2_ref, y3_ref, g2_ref, be2_ref,
                    yp_ref, ys_ref, y_s, *, alpha, n_prompt_steps):
    i = pl.program_id(0)
    g = g_ref[...]
    rows = g.shape[0]

    def token_major(ref):
        return jnp.concatenate([ref[pl.ds(s, rows, stride=8), :] for s in range(8)], axis=-1)

    ys = [token_major(r.reshape(rows * 8, LANES)) for r in (y0_ref, y1_ref, y2_ref, y3_ref)]
    moe = g[:, 0:1] * ys[0] + g[:, 1:2] * ys[1] + g[:, 2:3] * ys[2] + g[:, 3:4] * ys[3]
    y = _layernorm(alpha * token_major(x1_ref) + moe, g2_ref[...], be2_ref[...])
    for q in range(8):
        y_s[q] = y[:, q * LANES:(q + 1) * LANES]

    def batch_rows(bb, nb):
        return jnp.concatenate(
            [y_s[q, pl.ds(bb, rows // nb, stride=nb), :] for q in range(8)], axis=-1)

    @pl.when(i < n_prompt_steps)
    def _():
        for bb in range(yp_ref.shape[0]):
            yp_ref[bb] = batch_rows(bb, yp_ref.shape[0])

    @pl.when(i >= n_prompt_steps)
    def _():
        for bb in range(ys_ref.shape[0]):
            ys_ref[bb] = batch_rows(bb, ys_ref.shape[0])


def _combine_call(x1t, grow, ytok, g2, be2, *, prompt_shape, sample_shape, alpha):
    bp, sp, _ = prompt_shape
    bs, ss, _ = sample_shape
    ts_p = COMBINE_ROWS // bp
    nb_s = COMBINE_ROWS // ss
    n_p = sp // ts_p
    nt = n_p + bs // nb_s
    plane = lambda k: pl.BlockSpec((COMBINE_ROWS, None, 8, LANES), lambda i, k=k: (i, k, 0, 0))
    vec = pl.BlockSpec((1, D_MODEL), lambda i: (0, 0))
    ytok4 = ytok.reshape(-1, TOP_K, 8, LANES)
    return pl.pallas_call(
        functools.partial(_combine_kernel, alpha=alpha, n_prompt_steps=n_p),
        grid=(nt,),
        in_specs=[pl.BlockSpec((COMBINE_ROWS * 8, LANES), lambda i: (i, 0)),
                  pl.BlockSpec((COMBINE_ROWS, LANES), lambda i: (i, 0)),
                  plane(0), plane(1), plane(2), plane(3), vec, vec],
        out_specs=(pl.BlockSpec((bp, ts_p, D_MODEL), lambda i: (0, jnp.minimum(i, n_p - 1), 0)),
                   pl.BlockSpec((nb_s, ss, D_MODEL), lambda i: (jnp.maximum(i - n_p, 0), 0, 0))),
        out_shape=(jax.ShapeDtypeStruct(prompt_shape, _F32), jax.ShapeDtypeStruct(sample_shape, _F32)),
        scratch_shapes=[pltpu.VMEM((8, COMBINE_ROWS, LANES), _F32)],
        compiler_params=pltpu.CompilerParams(
            dimension_semantics=("arbitrary",), vmem_limit_bytes=VMEM_LIMIT),
        name="combine",
    )(x1t, grow, ytok4, ytok4, ytok4, ytok4, g2, be2)


def _pack_block_diag(w):
    w = w.reshape(N_GATE_TILES, HEADS_PER_TILE, LRU_BLOCK, LRU_BLOCK)
    eye = jnp.eye(HEADS_PER_TILE, dtype=w.dtype)
    t = jnp.einsum("qhij,hg->qhigj", w, eye)
    return t.reshape(N_GATE_TILES, GATE_TILE, GATE_TILE)


def _layer(xp, xs, h_s0, ca_s0, cb_s0, p, *, alpha):
    bp, sp, _ = xp.shape
    bs, ss, _ = xs.shape
    tp, tsm = bp * sp, bs * ss
    total = tp + tsm
    row2 = lambda v: v.reshape(1, -1)

    rwt = jnp.transpose(p["router_w"])
    rwt_hi = rwt.astype(_BF16)
    rwt_lo = (rwt - rwt_hi.astype(_F32)).astype(_BF16)
    ii = jnp.arange(MIXER_ROWS)
    tri = (ii[:, None] < ii[None, :]).astype(_BF16)
    wts = (
        p["w_in"].astype(_BF16), row2(p["b_in"]), p["conv_a_w"], row2(p["conv_a_b"]),
        _pack_block_diag(p["lru_wa"]).astype(_BF16), row2(p["lru_ba"]),
        _pack_block_diag(p["lru_wx"]).astype(_BF16), row2(p["lru_bx"]), row2(p["lru_lambda"]),
        p["conv_b_w"], p["w_out"].astype(_BF16), row2(p["ln1_g"]), row2(p["ln1_b"]),
        rwt_hi, rwt_lo, jnp.broadcast_to(p["router_b"][:, None], (N_EXPERTS, LANES)), tri,
    )

    ts_p = MIXER_ROWS // bp
    xp4 = jnp.transpose(xp, (1, 0, 2))[None]
    zeros = lambda *s: jnp.zeros(s, _F32)
    n_assign = total * TOP_K
    n_spare = 3 * MOE_ROWS
    x1_buf = zeros((total + n_spare // TOP_K) * 8, LANES)
    outs_p = _mixer_call(xp4, zeros(1, bp, D_MODEL), zeros(1, CONV_A_WIDTH - 1, bp, D_MODEL),
                         zeros(1, CONV_B_WIDTH - 1, bp, D_MODEL), zeros(N_EXPERTS, LANES), x1_buf, wts,
                         ts=ts_p, blk_off=0, alpha=alpha)
    nb_s = MIXER_ROWS // ss
    nblk_s = bs // nb_s
    xs4 = jnp.transpose(xs.reshape(nblk_s, nb_s, ss, D_MODEL), (0, 2, 1, 3))
    h0 = h_s0.reshape(nblk_s, nb_s, D_MODEL)
    ca0 = jnp.transpose(ca_s0.reshape(nblk_s, nb_s, CONV_A_WIDTH - 1, D_MODEL), (0, 2, 1, 3))
    cb0 = jnp.transpose(cb_s0.reshape(nblk_s, nb_s, CONV_B_WIDTH - 1, D_MODEL), (0, 2, 1, 3))
    outs_s = _mixer_call(xs4, h0, ca0, cb0, outs_p[7], outs_p[0], wts,
                         ts=ss, blk_off=tp // MIXER_ROWS, alpha=alpha)
    x1t = outs_s[0]

    n_blocks = -(-n_assign // MOE_ROWS) + N_EXPERTS + 1
    n_rows = n_blocks * MOE_ROWS
    idx = jnp.concatenate([outs_p[1][:TOP_K], outs_s[1][:TOP_K]], axis=1)
    rank = jnp.concatenate([outs_p[2][:TOP_K], outs_s[2][:TOP_K]], axis=1)
    grow = jnp.concatenate([outs_p[3], outs_s[3]], axis=0)
    counts = outs_s[7][:, 0].astype(jnp.int32)

    padded = (counts + MOE_ROWS - 1) // MOE_ROWS * MOE_ROWS
    end_pad = jnp.cumsum(padded)
    start_pad = end_pad - padded
    experts = jnp.arange(N_EXPERTS, dtype=jnp.int32)
    start_of = jnp.sum(jnp.where(idx[:, :, None] == experts, start_pad, 0), axis=-1)
    dest = (start_of + rank).reshape(n_assign)
    blk_start = jnp.arange(n_blocks, dtype=jnp.int32) * MOE_ROWS
    blk_e = jnp.minimum(jnp.sum((blk_start[:, None] >= end_pad[None, :]).astype(jnp.int32), axis=1),
                        N_EXPERTS - 1)
    n_used = (end_pad[-1] // MOE_ROWS).astype(jnp.int32).reshape(1)

    blk_ids = jnp.arange(n_blocks, dtype=jnp.int32)
    changed = jnp.concatenate([jnp.ones((1,), bool), blk_e[1:] != blk_e[:-1]])
    first = changed & (blk_ids < n_used[0])
    wslot = (jnp.cumsum(first.astype(jnp.int32)) - 1) % 2
    later_first = first[None, :] & (blk_ids[None, :] > blk_ids[:, None])
    nxt_blk = jnp.min(jnp.where(later_first, blk_ids[None, :], n_blocks), axis=1)
    nxt_e = jnp.sum(jnp.where(nxt_blk[:, None] == blk_ids[None, :], blk_e[None, :], 0), axis=1)
    nxt_e = jnp.where(nxt_blk < n_blocks, nxt_e, -1)

    ext = jnp.arange(n_rows + MOE_ROWS, dtype=jnp.int32)
    row = ext - MOE_ROWS
    holds = jnp.any((row[:, None] >= start_pad[None, :]) & (row[:, None] < (start_pad + counts)[None, :]),
                    axis=1)
    spare = jnp.where(row < 0, n_assign + 2 * MOE_ROWS + ext, n_assign + (row & (2 * MOE_ROWS - 1)))
    scattered = _invmap_sc_call(dest + MOE_ROWS, total=total, n_out=n_rows + MOE_ROWS)
    inv = jnp.where(holds, scattered, spare)
    ytok = _moe_call(blk_e, first.astype(jnp.int32), wslot, nxt_e, n_used, inv, x1t.reshape(-1, 8, LANES),
                     p["w_gu"], p["b_gu"][:, None, :], p["w_down"], p["b_down"][:, None, :],
                     n_slots=n_assign)
    yp, ys = _combine_call(x1t, grow, ytok, row2(p["ln2_g"]), row2(p["ln2_b"]),
                           prompt_shape=xp.shape, sample_shape=xs.shape, alpha=alpha)

    def batch_major(v, nblk):
        return jnp.transpose(v, (0, 2, 1, 3)).reshape(nblk * v.shape[2], v.shape[1], D_MODEL)

    states_p = (outs_p[4].reshape(bp, D_MODEL), batch_major(outs_p[5], 1), batch_major(outs_p[6], 1))
    states_s = (outs_s[4].reshape(bs, D_MODEL), batch_major(outs_s[5], nblk_s), batch_major(outs_s[6], nblk_s))
    return yp, ys, states_p, states_s


def kernel(x_prompt, x_sample, state_rglru_h, state_rglru_conv, state_shortconv, w_in, b_in, conv_a_w, conv_a_b, lru_wa, lru_ba, lru_wx, lru_bx, lru_lambda, conv_b_w, w_out, ln1_g, ln1_b, router_w, router_b, w_gu, b_gu, w_down, b_down, ln2_g, ln2_b):
    depth = w_in.shape[0]
    alpha = (2.0 * depth) ** 0.25
    names = ("w_in", "b_in", "conv_a_w", "conv_a_b", "lru_wa", "lru_ba", "lru_wx", "lru_bx", "lru_lambda",
             "conv_b_w", "w_out", "ln1_g", "ln1_b", "router_w", "router_b", "w_gu", "b_gu", "w_down",
             "b_down", "ln2_g", "ln2_b")
    stacked = (w_in, b_in, conv_a_w, conv_a_b, lru_wa, lru_ba, lru_wx, lru_bx, lru_lambda, conv_b_w, w_out,
               ln1_g, ln1_b, router_w, router_b, w_gu, b_gu, w_down, b_down, ln2_g, ln2_b)
    xp, xs = x_prompt, x_sample
    hp_l, cp_l, sp_l, hs_l, cs_l, ss_l = [], [], [], [], [], []
    for l in range(depth):
        p = {n: v[l] for n, v in zip(names, stacked)}
        xp, xs, (hp, cp, sp), (hs, cs, ss) = _layer(
            xp, xs, state_rglru_h[l], state_rglru_conv[l], state_shortconv[l], p, alpha=alpha)
        hp_l.append(hp); cp_l.append(cp); sp_l.append(sp)
        hs_l.append(hs); cs_l.append(cs); ss_l.append(ss)
    return (xp, xs, jnp.stack(hp_l), jnp.stack(cp_l), jnp.stack(sp_l), jnp.stack(hs_l), jnp.stack(cs_l),
            jnp.stack(ss_l))
```

```python
import functools

import jax
import jax.numpy as jnp
from jax import lax
from jax.experimental import pallas as pl
from jax.experimental.pallas import tpu as pltpu
from jax.experimental.pallas import tpu_sc as plsc

D_MODEL = 1024
LRU_HEADS = 16
LRU_BLOCK = D_MODEL // LRU_HEADS
LRU_C = 8.0
CONV_A_WIDTH = 4
CONV_B_WIDTH = 3
N_GROUPS = 7
N_EXPERTS = 32
TOP_K = 4
D_FF = D_MODEL
SWIGLU_LIMIT = 7.0
SWIGLU_ALPHA = 1.702
LN_EPS = 1e-5

GATE_TILE = 256
HEADS_PER_TILE = GATE_TILE // LRU_BLOCK
N_GATE_TILES = D_MODEL // GATE_TILE
LANES = 128
MIXER_ROWS = 512
MOE_ROWS = 256
COMBINE_ROWS = 512
VMEM_LIMIT = 58 * 1024 * 1024

_F32 = jnp.float32
_BF16 = jnp.bfloat16
_NT = (((1,), (1,)), ((), ()))


def _sigmoid(v):
    return 0.5 * jnp.tanh(0.5 * v) + 0.5


def _gelu_tanh(v):
    c = 0.7978845608028654
    return 0.5 * v * (1.0 + jnp.tanh(c * (v + 0.044715 * (v * v * v))))


def _layernorm(z, g, b):
    mu = jnp.mean(z, axis=-1, keepdims=True)
    zc = z - mu
    var = jnp.mean(zc * zc, axis=-1, keepdims=True)
    return zc * lax.rsqrt(var + LN_EPS) * g + b


def _mixer_kernel(x_ref, h0_ref, ca0_ref, cb0_ref, cnt0_ref, x1_buf_ref,
                  w_in_ref, b_in_ref, wca_ref, bca_ref, wa_ref, ba_ref, wx_ref, bx_ref, lam_ref,
                  wcb_ref, w_out_ref, g1_ref, be1_ref, rwt_hi_ref, rwt_lo_ref, rb_ref, tri_ref,
                  x1_ref, idx_ref, rank_ref, grow_ref, hl_ref, ca_ref, cb_ref, cnt_ref,
                  xa_s, u_s, a_s, b_s, h_s, hst_s, cnt_s, *, ts, nb, alpha):
    i = pl.program_id(0)
    j = pl.program_id(1)
    rows = ts * nb
    ta = (CONV_A_WIDTH - 1) * nb
    tb = (CONV_B_WIDTH - 1) * nb

    @pl.when(j == 0)
    def _():
        hst_s[...] = h0_ref[...]
        xa_s[0:ta, :] = ca0_ref[...].reshape(ta, D_MODEL)
        u_s[0:tb, :] = cb0_ref[...].reshape(tb, D_MODEL)

    @pl.when((i == 0) & (j == 0))
    def _():
        cnt_s[...] = cnt0_ref[...]

    x = x_ref[...].reshape(rows, D_MODEL)
    xb = x.astype(_BF16)

    def proj(g):
        lo, hi = g * D_MODEL, (g + 1) * D_MODEL
        return jnp.dot(xb, w_in_ref[:, lo:hi], preferred_element_type=_F32) + b_in_ref[:, lo:hi]

    xa_s[ta:ta + rows, :] = proj(0)
    xc = bca_ref[...] + xa_s[0:rows, :] * wca_ref[0:1, :]
    for k in range(1, CONV_A_WIDTH):
        xc = xc + xa_s[k * nb:k * nb + rows, :] * wca_ref[k:k + 1, :]
    new_ta = xa_s[rows:rows + ta, :]
    xa_s[0:ta, :] = new_ta
    ca_ref[...] = new_ta.reshape(CONV_A_WIDTH - 1, nb, D_MODEL)

    xcb = xc.astype(_BF16)

    def block_diag(w_ref):
        return jnp.concatenate(
            [jnp.dot(xcb[:, q * GATE_TILE:(q + 1) * GATE_TILE], w_ref[q], preferred_element_type=_F32)
             for q in range(N_GATE_TILES)], axis=-1)

    r = _sigmoid(block_diag(wa_ref) + ba_ref[...])
    ig = _sigmoid(block_diag(wx_ref) + bx_ref[...])
    nlam = -lam_ref[...]
    softplus = jnp.maximum(nlam, 0.0) + jnp.log1p(jnp.exp(-jnp.abs(nlam)))
    log_a = (-LRU_C * softplus) * r
    a = jnp.exp(log_a)
    a_s[...] = a
    b_s[...] = jnp.sqrt(-jnp.tanh(log_a) * (a * a + 1.0)) * (ig * xc)

    h = hst_s[...]
    for t in range(ts):
        sl = slice(t * nb, (t + 1) * nb)
        h = a_s[sl, :] * h + b_s[sl, :]
        h_s[sl, :] = h
    hst_s[...] = h
    hl_ref[...] = h

    a_s[...] = h_s[...] * _gelu_tanh(proj(1))

    u_s[tb:tb + rows, :] = proj(3) * proj(4)
    uc = u_s[0:rows, :] * wcb_ref[0:1, :]
    for k in range(1, CONV_B_WIDTH):
        uc = uc + u_s[k * nb:k * nb + rows, :] * wcb_ref[k:k + 1, :]
    new_tb = u_s[rows:rows + tb, :]
    u_s[0:tb, :] = new_tb
    cb_ref[...] = new_tb.reshape(CONV_B_WIDTH - 1, nb, D_MODEL)
    y_b = proj(2) * uc

    merged = _sigmoid(proj(5)) * a_s[...] + _sigmoid(proj(6)) * y_b
    mixed = jnp.dot(merged.astype(_BF16), w_out_ref[...], preferred_element_type=_F32)
    x1 = _layernorm(alpha * x + mixed, g1_ref[...], be1_ref[...])
    for s in range(8):
        x1_ref[pl.ds(s, rows, stride=8), :] = x1[:, s * LANES:(s + 1) * LANES]

    x1_hi = x1.astype(_BF16)
    x1_lo = (x1 - x1_hi.astype(_F32)).astype(_BF16)
    logits = (lax.dot_general(rwt_hi_ref[...], x1_hi, _NT, preferred_element_type=_F32)
              + lax.dot_general(rwt_hi_ref[...], x1_lo, _NT, preferred_element_type=_F32)
              + lax.dot_general(rwt_lo_ref[...], x1_hi, _NT, preferred_element_type=_F32)
              + rb_ref[:, 0:1])
    e_iota = lax.broadcasted_iota(jnp.int32, (N_EXPERTS, rows), 0)
    work = logits
    vals, sels, idxs = [], [], []
    for _ in range(TOP_K):
        m = jnp.max(work, axis=0, keepdims=True)
        ik = jnp.min(jnp.where(work == m, e_iota, N_EXPERTS), axis=0, keepdims=True)
        sel = e_iota == ik
        work = jnp.where(sel, -jnp.inf, work)
        vals.append(m)
        sels.append(sel)
        idxs.append(ik)
    exps = [jnp.exp(v - vals[0]) for v in vals]
    denom = exps[0] + exps[1] + exps[2] + exps[3]
    gates = [ex / denom for ex in exps]

    onehot = jnp.zeros((N_EXPERTS, rows), _F32)
    for sel in sels:
        onehot = onehot + sel.astype(_F32)
    prefix = jnp.dot(onehot.astype(_BF16), tri_ref[...], preferred_element_type=_F32)
    pos = prefix + cnt_s[:, 0:1]
    ranks = [jnp.sum(jnp.where(sel, pos, 0.0), axis=0, keepdims=True) for sel in sels]
    new_cnt = cnt_s[...] + jnp.sum(onehot, axis=1, keepdims=True)
    cnt_s[...] = new_cnt
    cnt_ref[...] = new_cnt

    row8 = lax.broadcasted_iota(jnp.int32, (8, rows), 0)
    idx8 = jnp.zeros((8, rows), jnp.int32)
    rank8 = jnp.zeros((8, rows), jnp.int32)
    for k in range(TOP_K):
        idx8 = jnp.where(row8 == k, idxs[k], idx8)
        rank8 = jnp.where(row8 == k, ranks[k].astype(jnp.int32), rank8)
    idx_ref[...] = idx8
    rank_ref[...] = rank8

    row_l = lax.broadcasted_iota(jnp.int32, (LANES, rows), 0)
    g_t = jnp.zeros((LANES, rows), _F32)
    for k in range(TOP_K):
        g_t = jnp.where(row_l == k, gates[k], g_t)
    grow_ref[...] = jnp.transpose(g_t)


def _const_spec(shape):
    nd = len(shape)
    return pl.BlockSpec(shape, lambda i, j: (0,) * nd, pipeline_mode=pl.Buffered(1))


def _mixer_call(x4, h0, ca0, cb0, cnt0, x1_buf, wts, *, ts, blk_off, alpha):
    nblk, seq, nb, _ = x4.shape
    nt = seq // ts
    rows = ts * nb
    total = nblk * seq * nb
    f32 = lambda *s: jax.ShapeDtypeStruct(s, _F32)
    i32 = lambda *s: jax.ShapeDtypeStruct(s, jnp.int32)
    in_specs = [
        pl.BlockSpec((None, ts, nb, D_MODEL), lambda i, j: (i, j, 0, 0)),
        pl.BlockSpec((None, nb, D_MODEL), lambda i, j: (i, 0, 0)),
        pl.BlockSpec((None, CONV_A_WIDTH - 1, nb, D_MODEL), lambda i, j: (i, 0, 0, 0)),
        pl.BlockSpec((None, CONV_B_WIDTH - 1, nb, D_MODEL), lambda i, j: (i, 0, 0, 0)),
        _const_spec(cnt0.shape),
        pl.BlockSpec(memory_space=pl.ANY),
    ] + [_const_spec(w.shape) for w in wts]
    out_shape = (
        jax.ShapeDtypeStruct(x1_buf.shape, _F32),
        i32(8, total),
        i32(8, total),
        f32(total, LANES),
        f32(nblk, nb, D_MODEL),
        f32(nblk, CONV_A_WIDTH - 1, nb, D_MODEL),
        f32(nblk, CONV_B_WIDTH - 1, nb, D_MODEL),
        f32(N_EXPERTS, LANES),
    )
    out_specs = (
        pl.BlockSpec((rows * 8, LANES), lambda i, j: (blk_off + i * nt + j, 0)),
        pl.BlockSpec((8, rows), lambda i, j: (0, i * nt + j)),
        pl.BlockSpec((8, rows), lambda i, j: (0, i * nt + j)),
        pl.BlockSpec((rows, LANES), lambda i, j: (i * nt + j, 0)),
        pl.BlockSpec((None, nb, D_MODEL), lambda i, j: (i, 0, 0)),
        pl.BlockSpec((None, CONV_A_WIDTH - 1, nb, D_MODEL), lambda i, j: (i, 0, 0, 0)),
        pl.BlockSpec((None, CONV_B_WIDTH - 1, nb, D_MODEL), lambda i, j: (i, 0, 0, 0)),
        pl.BlockSpec((N_EXPERTS, LANES), lambda i, j: (0, 0)),
    )
    scratch = [
        pltpu.VMEM((rows + (CONV_A_WIDTH - 1) * nb, D_MODEL), _F32),
        pltpu.VMEM((rows + (CONV_B_WIDTH - 1) * nb, D_MODEL), _F32),
        pltpu.VMEM((rows, D_MODEL), _F32),
        pltpu.VMEM((rows, D_MODEL), _F32),
        pltpu.VMEM((rows, D_MODEL), _F32),
        pltpu.VMEM((nb, D_MODEL), _F32),
        pltpu.VMEM((N_EXPERTS, LANES), _F32),
    ]
    return pl.pallas_call(
        functools.partial(_mixer_kernel, ts=ts, nb=nb, alpha=alpha),
        grid=(nblk, nt),
        in_specs=in_specs,
        out_specs=out_specs,
        out_shape=out_shape,
        scratch_shapes=scratch,
        input_output_aliases={5: 0},
        compiler_params=pltpu.CompilerParams(
            dimension_semantics=("arbitrary", "arbitrary"), vmem_limit_bytes=VMEM_LIMIT),
        name="mixer",
    )(x4, h0, ca0, cb0, cnt0, x1_buf, *wts)


def _invmap_kernel(dest_ref, lo_ref, hi_ref, inv_ref, *, total, n_rows):
    n_slots = total * TOP_K

    for r in range(MOE_ROWS):
        inv_ref[r] = n_slots + 2 * MOE_ROWS + r

    def fill(lo, hi):
        def body(row, c):
            inv_ref[MOE_ROWS + row] = n_slots + (row & (2 * MOE_ROWS - 1))
            return c
        lax.fori_loop(lo, hi, body, 0)

    def per_expert(e, c):
        fill(lo_ref[e], hi_ref[e])
        return c
    lax.fori_loop(0, N_EXPERTS, per_expert, 0)
    fill(hi_ref[N_EXPERTS - 1], n_rows)

    unroll = 16
    for k in range(TOP_K):
        def body(it, c, k=k):
            base = it * unroll
            rows = [dest_ref[k * total + base + u] for u in range(unroll)]
            val = base * TOP_K + k
            for u in range(unroll):
                inv_ref[MOE_ROWS + rows[u]] = val + u * TOP_K
            return c
        lax.fori_loop(0, total // unroll, body, 0)


def _invmap_call(dest, pad_lo, pad_hi, *, total, n_rows):
    smem = pl.BlockSpec(memory_space=pltpu.SMEM)
    return pl.pallas_call(
        functools.partial(_invmap_kernel, total=total, n_rows=n_rows),
        in_specs=[smem, smem, smem],
        out_specs=smem,
        out_shape=jax.ShapeDtypeStruct((n_rows + MOE_ROWS,), jnp.int32),
        name="invmap",
    )(dest, pad_lo, pad_hi)


SC_WORKERS = 32
SC_WINDOW = 128
SC_LANES = 16


def _invmap_sc_call(dest, *, total, n_out):
    n = dest.shape[0]
    per_worker = n // SC_WORKERS
    n_windows = per_worker // SC_WINDOW
    assert per_worker * SC_WORKERS == n and n_windows * SC_WINDOW == per_worker
    assert total % per_worker == 0
    mesh = plsc.VectorSubcoreMesh(core_axis_name="c", subcore_axis_name="s")

    @functools.partial(
        pl.kernel, mesh=mesh,
        out_type=jax.ShapeDtypeStruct((n_out,), jnp.int32),
        scratch_types=[pltpu.VMEM((n_windows, SC_WINDOW), jnp.int32),
                       pltpu.VMEM((n_windows, SC_WINDOW), jnp.int32),
                       pltpu.SemaphoreType.DMA, pltpu.SemaphoreType.DMA],
        compiler_params=pltpu.CompilerParams(use_tc_tiling_on_sc=False, needs_layout_passes=False),
        name="invmap_sc")
    def scatter(dest_hbm, out_hbm, idx_v, val_v, sem_in, sem_out):
        worker = lax.axis_index("s") * 2 + lax.axis_index("c")
        first = worker * per_worker
        k = first // total
        lane = lax.iota(jnp.int32, SC_LANES)

        loads = [pltpu.make_async_copy(dest_hbm.at[pl.ds(first + w * SC_WINDOW, SC_WINDOW)],
                                       idx_v.at[w], sem_in) for w in range(n_windows)]
        for cp in loads:
            cp.start()
        for w in range(n_windows):
            for j in range(SC_WINDOW // SC_LANES):
                tok = first - k * total + w * SC_WINDOW + j * SC_LANES + lane
                val_v[w, pl.ds(j * SC_LANES, SC_LANES)] = tok * TOP_K + k
        for cp in loads:
            cp.wait()
        stores = [pltpu.make_async_copy(val_v.at[w], out_hbm.at[idx_v.at[w]], sem_out)
                  for w in range(n_windows)]
        for cp in stores:
            cp.start()
        for cp in stores:
            cp.wait()

    return scatter(dest)


def _moe_kernel(blk_e_ref, first_ref, wslot_ref, nxt_e_ref, n_used_ref, inv_ref,
                x1_hbm, wgu_hbm, bgu_ref, wd_hbm, bd_ref,
                ytok_hbm,
                xb0, xb1, xb2, yb0, yb1, yb2, wgu_f, wd_f, wgu_s, wd_s, bgu_s, bd_s,
                gsem, ssem, wsem, *, n_slots):
    n_used = n_used_ref[0]
    n_blk = blk_e_ref.shape[0]
    xbufs = (xb0, xb1, xb2)
    ybufs = (yb0, yb1, yb2)

    def gather_copy(v, s, r):
        return pltpu.make_async_copy(
            x1_hbm.at[v >> 2], xbufs[s].at[pl.ds(r * 8, 8), :], gsem.at[s])

    def scatter_copy(v, s, r):
        return pltpu.make_async_copy(
            ybufs[s].at[pl.ds(r * 8, 8), :], ytok_hbm.at[v], ssem.at[s])

    def weight_copies(e, ws):
        return (pltpu.make_async_copy(wgu_hbm.at[e], wgu_f.at[ws], wsem.at[ws]),
                pltpu.make_async_copy(wd_hbm.at[e], wd_f.at[ws], wsem.at[ws]))

    def wait_rows(copy_fn, s):
        for r in range(MOE_ROWS):
            copy_fn(0, s, r).wait()

    for cp in weight_copies(blk_e_ref[0], 0):
        cp.start()
    for yb in ybufs:
        yb[...] = jnp.zeros(yb.shape, _F32)

    def prime(r, c):
        for s in range(2):
            pltpu.make_async_copy(
                ybufs[s].at[pl.ds(pl.multiple_of(r * 8, 8), 8), :],
                ytok_hbm.at[n_slots + s * MOE_ROWS + r], ssem.at[s]).start()
            pltpu.make_async_copy(
                x1_hbm.at[inv_ref[(s + 1) * MOE_ROWS + r] >> 2],
                xbufs[s].at[pl.ds(pl.multiple_of(r * 8, 8), 8), :], gsem.at[s]).start()
        return c
    lax.fori_loop(0, MOE_ROWS, prime, 0)

    def run_block(b, s):
        nxt = (s + 2) % 3
        wait_rows(gather_copy, s)
        wait_rows(scatter_copy, s)
        x = jnp.concatenate(
            [xbufs[s][pl.ds(q, MOE_ROWS, stride=8), :] for q in range(8)], axis=-1)
        gbase = (jnp.minimum(b + 2, n_blk - 1) + 1) * MOE_ROWS
        sbase = b * MOE_ROWS
        for r in range(MOE_ROWS):
            gather_copy(inv_ref[gbase + r], nxt, r).start()
        for r in range(MOE_ROWS):
            scatter_copy(inv_ref[sbase + r], nxt, r).start()
        gu = jnp.dot(x.astype(_BF16), wgu_s[...], preferred_element_type=_F32) + bgu_s[...]
        gate = jnp.minimum(gu[:, :D_FF], SWIGLU_LIMIT)
        up = jnp.clip(gu[:, D_FF:], -SWIGLU_LIMIT, SWIGLU_LIMIT)
        hmid = (up + 1.0) * (gate * _sigmoid(SWIGLU_ALPHA * gate))
        y = jnp.dot(hmid.astype(_BF16), wd_s[...], preferred_element_type=_F32) + bd_s[...]
        for q in range(8):
            ybufs[s][pl.ds(q, MOE_ROWS, stride=8), :] = y[:, q * LANES:(q + 1) * LANES]

    def body(b, c):
        @pl.when(first_ref[b] == 1)
        def _():
            e = blk_e_ref[b]
            ws = wslot_ref[b]
            for cp in weight_copies(e, ws):
                cp.wait()
            nxt_e = nxt_e_ref[b]

            @pl.when(nxt_e >= 0)
            def _():
                for cp in weight_copies(nxt_e, 1 - ws):
                    cp.start()
            wgu_s[...] = wgu_f[ws].astype(_BF16)
            wd_s[...] = wd_f[ws].astype(_BF16)
            bgu_s[...] = bgu_ref[e]
            bd_s[...] = bd_ref[e]

        for s in range(3):
            pl.when(b % 3 == s)(functools.partial(run_block, b, s))
        return c

    lax.fori_loop(0, n_used + 1, body, 0)

    for d in (1, 2):
        for s in range(3):
            @pl.when((n_used + d) % 3 == s)
            def _():
                wait_rows(gather_copy, s)
                wait_rows(scatter_copy, s)


def _moe_call(blk_e, first, wslot, nxt_e, n_used, inv, x1t, w_gu, b_gu, w_down, b_down, *, n_slots):
    full = lambda shape: pl.BlockSpec(shape, lambda i, *_: (0,) * len(shape))
    grid_spec = pltpu.PrefetchScalarGridSpec(
        num_scalar_prefetch=6,
        grid=(1,),
        in_specs=[
            pl.BlockSpec(memory_space=pl.ANY),
            pl.BlockSpec(memory_space=pl.ANY),
            full(b_gu.shape),
            pl.BlockSpec(memory_space=pl.ANY),
            full(b_down.shape),
        ],
        out_specs=pl.BlockSpec(memory_space=pl.ANY),
        scratch_shapes=[pltpu.VMEM((MOE_ROWS * 8, LANES), _F32)] * 6 + [
            pltpu.VMEM((2, D_MODEL, 2 * D_FF), _F32),
            pltpu.VMEM((2, D_FF, D_MODEL), _F32),
            pltpu.VMEM((D_MODEL, 2 * D_FF), _BF16),
            pltpu.VMEM((D_FF, D_MODEL), _BF16),
            pltpu.VMEM((1, 2 * D_FF), _F32),
            pltpu.VMEM((1, D_MODEL), _F32),
            pltpu.SemaphoreType.DMA((3,)),
            pltpu.SemaphoreType.DMA((3,)),
            pltpu.SemaphoreType.DMA((2,)),
        ],
    )
    return pl.pallas_call(
        functools.partial(_moe_kernel, n_slots=n_slots),
        grid_spec=grid_spec,
        out_shape=jax.ShapeDtypeStruct((n_slots + 3 * MOE_ROWS, 8, LANES), _F32),
        compiler_params=pltpu.CompilerParams(
            dimension_semantics=("arbitrary",), vmem_limit_bytes=VMEM_LIMIT),
        name="moe",
    )(blk_e, first, wslot, nxt_e, n_used, inv, x1t, w_gu, b_gu, w_down, b_down)


def _combine_kernel(x1_ref, g_ref, y0_ref, y1_ref, y2_ref, y3_ref, g2_ref, be2_ref,
                    yp_ref, ys_ref, y_s, *, alpha, n_prompt_steps):
    i = pl.program_id(0)
    g = g_ref[...]
    rows = g.shape[0]

    def token_major(ref):
        return jnp.concatenate([ref[pl.ds(s, rows, stride=8), :] for s in range(8)], axis=-1)

    ys = [token_major(r.reshape(rows * 8, LANES)) for r in (y0_ref, y1_ref, y2_ref, y3_ref)]
    moe = g[:, 0:1] * ys[0] + g[:, 1:2] * ys[1] + g[:, 2:3] * ys[2] + g[:, 3:4] * ys[3]
    y = _layernorm(alpha * token_major(x1_ref) + moe, g2_ref[...], be2_ref[...])
    for q in range(8):
        y_s[q] = y[:, q * LANES:(q + 1) * LANES]

    def batch_rows(bb, nb):
        return jnp.concatenate(
            [y_s[q, pl.ds(bb, rows // nb, stride=nb), :] for q in range(8)], axis=-1)

    @pl.when(i < n_prompt_steps)
    def _():
        for bb in range(yp_ref.shape[0]):
            yp_ref[bb] = batch_rows(bb, yp_ref.shape[0])

    @pl.when(i >= n_prompt_steps)
    def _():
        for bb in range(ys_ref.shape[0]):
            ys_ref[bb] = batch_rows(bb, ys_ref.shape[0])


def _combine_call(x1t, grow, ytok, g2, be2, *, prompt_shape, sample_shape, alpha):
    bp, sp, _ = prompt_shape
    bs, ss, _ = sample_shape
    ts_p = COMBINE_ROWS // bp
    nb_s = COMBINE_ROWS // ss
    n_p = sp // ts_p
    nt = n_p + bs // nb_s
    plane = lambda k: pl.BlockSpec((COMBINE_ROWS, None, 8, LANES), lambda i, k=k: (i, k, 0, 0))
    vec = pl.BlockSpec((1, D_MODEL), lambda i: (0, 0))
    ytok4 = ytok.reshape(-1, TOP_K, 8, LANES)
    return pl.pallas_call(
        functools.partial(_combine_kernel, alpha=alpha, n_prompt_steps=n_p),
        grid=(nt,),
        in_specs=[pl.BlockSpec((COMBINE_ROWS * 8, LANES), lambda i: (i, 0)),
                  pl.BlockSpec((COMBINE_ROWS, LANES), lambda i: (i, 0)),
                  plane(0), plane(1), plane(2), plane(3), vec, vec],
        out_specs=(pl.BlockSpec((bp, ts_p, D_MODEL), lambda i: (0, jnp.minimum(i, n_p - 1), 0)),
                   pl.BlockSpec((nb_s, ss, D_MODEL), lambda i: (jnp.maximum(i - n_p, 0), 0, 0))),
        out_shape=(jax.ShapeDtypeStruct(prompt_shape, _F32), jax.ShapeDtypeStruct(sample_shape, _F32)),
        scratch_shapes=[pltpu.VMEM((8, COMBINE_ROWS, LANES), _F32)],
        compiler_params=pltpu.CompilerParams(
            dimension_semantics=("arbitrary",), vmem_limit_bytes=VMEM_LIMIT),
        name="combine",
    )(x1t, grow, ytok4, ytok4, ytok4, ytok4, g2, be2)


def _pack_block_diag(w):
    w = w.reshape(N_GATE_TILES, HEADS_PER_TILE, LRU_BLOCK, LRU_BLOCK)
    eye = jnp.eye(HEADS_PER_TILE, dtype=w.dtype)
    t = jnp.einsum("qhij,hg->qhigj", w, eye)
    return t.reshape(N_GATE_TILES, GATE_TILE, GATE_TILE)


def _layer(xp, xs, h_s0, ca_s0, cb_s0, p, *, alpha):
    bp, sp, _ = xp.shape
    bs, ss, _ = xs.shape
    tp, tsm = bp * sp, bs * ss
    total = tp + tsm
    row2 = lambda v: v.reshape(1, -1)

    rwt = jnp.transpose(p["router_w"])
    rwt_hi = rwt.astype(_BF16)
    rwt_lo = (rwt - rwt_hi.astype(_F32)).astype(_BF16)
    ii = jnp.arange(MIXER_ROWS)
    tri = (ii[:, None] < ii[None, :]).astype(_BF16)
    wts = (
        p["w_in"].astype(_BF16), row2(p["b_in"]), p["conv_a_w"], row2(p["conv_a_b"]),
        _pack_block_diag(p["lru_wa"]).astype(_BF16), row2(p["lru_ba"]),
        _pack_block_diag(p["lru_wx"]).astype(_BF16), row2(p["lru_bx"]), row2(p["lru_lambda"]),
        p["conv_b_w"], p["w_out"].astype(_BF16), row2(p["ln1_g"]), row2(p["ln1_b"]),
        rwt_hi, rwt_lo, jnp.broadcast_to(p["router_b"][:, None], (N_EXPERTS, LANES)), tri,
    )

    ts_p = MIXER_ROWS // bp
    xp4 = jnp.transpose(xp, (1, 0, 2))[None]
    zeros = lambda *s: jnp.zeros(s, _F32)
    n_assign = total * TOP_K
    n_spare = 3 * MOE_ROWS
    x1_buf = zeros((total + n_spare // TOP_K) * 8, LANES)
    outs_p = _mixer_call(xp4, zeros(1, bp, D_MODEL), zeros(1, CONV_A_WIDTH - 1, bp, D_MODEL),
                         zeros(1, CONV_B_WIDTH - 1, bp, D_MODEL), zeros(N_EXPERTS, LANES), x1_buf, wts,
                         ts=ts_p, blk_off=0, alpha=alpha)
    nb_s = MIXER_ROWS // ss
    nblk_s = bs // nb_s
    xs4 = jnp.transpose(xs.reshape(nblk_s, nb_s, ss, D_MODEL), (0, 2, 1, 3))
    h0 = h_s0.reshape(nblk_s, nb_s, D_MODEL)
    ca0 = jnp.transpose(ca_s0.reshape(nblk_s, nb_s, CONV_A_WIDTH - 1, D_MODEL), (0, 2, 1, 3))
    cb0 = jnp.transpose(cb_s0.reshape(nblk_s, nb_s, CONV_B_WIDTH - 1, D_MODEL), (0, 2, 1, 3))
    outs_s = _mixer_call(xs4, h0, ca0, cb0, outs_p[7], outs_p[0], wts,
                         ts=ss, blk_off=tp // MIXER_ROWS, alpha=alpha)
    x1t = outs_s[0]

    n_blocks = -(-n_assign // MOE_ROWS) + N_EXPERTS + 1
    n_rows = n_blocks * MOE_ROWS
    idx = jnp.concatenate([outs_p[1][:TOP_K], outs_s[1][:TOP_K]], axis=1)
    rank = jnp.concatenate([outs_p[2][:TOP_K], outs_s[2][:TOP_K]], axis=1)
    grow = jnp.concatenate([outs_p[3], outs_s[3]], axis=0)
    counts = outs_s[7][:, 0].astype(jnp.int32)

    padded = (counts + MOE_ROWS - 1) // MOE_ROWS * MOE_ROWS
    end_pad = jnp.cumsum(padded)
    start_pad = end_pad - padded
    experts = jnp.arange(N_EXPERTS, dtype=jnp.int32)
    start_of = jnp.sum(jnp.where(idx[:, :, None] == experts, start_pad, 0), axis=-1)
    dest = (start_of + rank).reshape(n_assign)
    blk_start = jnp.arange(n_blocks, dtype=jnp.int32) * MOE_ROWS
    blk_e = jnp.minimum(jnp.sum((blk_start[:, None] >= end_pad[None, :]).astype(jnp.int32), axis=1),
                        N_EXPERTS - 1)
    n_used = (end_pad[-1] // MOE_ROWS).astype(jnp.int32).reshape(1)

    blk_ids = jnp.arange(n_blocks, dtype=jnp.int32)
    changed = jnp.concatenate([jnp.ones((1,), bool), blk_e[1:] != blk_e[:-1]])
    first = changed & (blk_ids < n_used[0])
    wslot = (jnp.cumsum(first.astype(jnp.int32)) - 1) % 2
    later_first = first[None, :] & (blk_ids[None, :] > blk_ids[:, None])
    nxt_blk = jnp.min(jnp.where(later_first, blk_ids[None, :], n_blocks), axis=1)
    nxt_e = jnp.sum(jnp.where(nxt_blk[:, None] == blk_ids[None, :], blk_e[None, :], 0), axis=1)
    nxt_e = jnp.where(nxt_blk < n_blocks, nxt_e, -1)

    ext = jnp.arange(n_rows + MOE_ROWS, dtype=jnp.int32)
    row = ext - MOE_ROWS
    holds = jnp.any((row[:, None] >= start_pad[None, :]) & (row[:, None] < (start_pad + counts)[None, :]),
                    axis=1)
    spare = jnp.where(row < 0, n_assign + 2 * MOE_ROWS + ext, n_assign + (row & (2 * MOE_ROWS - 1)))
    scattered = _invmap_sc_call(dest + MOE_ROWS, total=total, n_out=n_rows + MOE_ROWS)
    inv = jnp.where(holds, scattered, spare)
    ytok = _moe_call(blk_e, first.astype(jnp.int32), wslot, nxt_e, n_used, inv, x1t.reshape(-1, 8, LANES),
                     p["w_gu"], p["b_gu"][:, None, :], p["w_down"], p["b_down"][:, None, :],
                     n_slots=n_assign)
    yp, ys = _combine_call(x1t, grow, ytok, row2(p["ln2_g"]), row2(p["ln2_b"]),
                           prompt_shape=xp.shape, sample_shape=xs.shape, alpha=alpha)

    def batch_major(v, nblk):
        return jnp.transpose(v, (0, 2, 1, 3)).reshape(nblk * v.shape[2], v.shape[1], D_MODEL)

    states_p = (outs_p[4].reshape(bp, D_MODEL), batch_major(outs_p[5], 1), batch_major(outs_p[6], 1))
    states_s = (outs_s[4].reshape(bs, D_MODEL), batch_major(outs_s[5], nblk_s), batch_major(outs_s[6], nblk_s))
    return yp, ys, states_p, states_s


def kernel(x_prompt, x_sample, state_rglru_h, state_rglru_conv, state_shortconv, w_in, b_in, conv_a_w, conv_a_b, lru_wa, lru_ba, lru_wx, lru_bx, lru_lambda, conv_b_w, w_out, ln1_g, ln1_b, router_w, router_b, w_gu, b_gu, w_down, b_down, ln2_g, ln2_b):
    depth = w_in.shape[0]
    alpha = (2.0 * depth) ** 0.25
    names = ("w_in", "b_in", "conv_a_w", "conv_a_b", "lru_wa", "lru_ba", "lru_wx", "lru_bx", "lru_lambda",
             "conv_b_w", "w_out", "ln1_g", "ln1_b", "router_w", "router_b", "w_gu", "b_gu", "w_down",
             "b_down", "ln2_g", "ln2_b")
    stacked = (w_in, b_in, conv_a_w, conv_a_b, lru_wa, lru_ba, lru_wx, lru_bx, lru_lambda, conv_b_w, w_out,
               ln1_g, ln1_b, router_w, router_b, w_gu, b_gu, w_down, b_down, ln2_g, ln2_b)
    xp, xs = x_prompt, x_sample
    hp_l, cp_l, sp_l, hs_l, cs_l, ss_l = [], [], [], [], [], []
    for l in range(depth):
        p = {n: v[l] for n, v in zip(names, stacked)}
        xp, xs, (hp, cp, sp), (hs, cs, ss) = _layer(
            xp, xs, state_rglru_h[l], state_rglru_conv[l], state_shortconv[l], p, alpha=alpha)
        hp_l.append(hp); cp_l.append(cp); sp_l.append(sp)
        hs_l.append(hs); cs_l.append(cs); ss_l.append(ss)
    return (xp, xs, jnp.stack(hp_l), jnp.stack(cp_l), jnp.stack(sp_l), jnp.stack(hs_l), jnp.stack(cs_l),
            jnp.stack(ss_l))
```

```python
import functools

import jax
import jax.numpy as jnp
from jax import lax
from jax.experimental import pallas as pl
from jax.experimental.pallas import tpu as pltpu
from jax.experimental.pallas import tpu_sc as plsc

D_MODEL = 1024
LRU_HEADS = 16
LRU_BLOCK = D_MODEL // LRU_HEADS
LRU_C = 8.0
CONV_A_WIDTH = 4
CONV_B_WIDTH = 3
N_GROUPS = 7
N_EXPERTS = 32
TOP_K = 4
D_FF = D_MODEL
SWIGLU_LIMIT = 7.0
SWIGLU_ALPHA = 1.702
LN_EPS = 1e-5

GATE_TILE = 256
HEADS_PER_TILE = GATE_TILE // LRU_BLOCK
N_GATE_TILES = D_MODEL // GATE_TILE
LANES = 128
MIXER_ROWS = 512
MOE_ROWS = 256
COMBINE_ROWS = 512
VMEM_LIMIT = 58 * 1024 * 1024

_F32 = jnp.float32
_BF16 = jnp.bfloat16
_NT = (((1,), (1,)), ((), ()))


def _sigmoid(v):
    return 0.5 * jnp.tanh(0.5 * v) + 0.5


def _gelu_tanh(v):
    c = 0.7978845608028654
    return 0.5 * v * (1.0 + jnp.tanh(c * (v + 0.044715 * (v * v * v))))


def _layernorm(z, g, b):
    mu = jnp.mean(z, axis=-1, keepdims=True)
    zc = z - mu
    var = jnp.mean(zc * zc, axis=-1, keepdims=True)
    return zc * lax.rsqrt(var + LN_EPS) * g + b


def _mixer_kernel(x_ref, h0_ref, ca0_ref, cb0_ref, cnt0_ref, x1_buf_ref,
                  w_in_ref, b_in_ref, wca_ref, bca_ref, wa_ref, ba_ref, wx_ref, bx_ref, lam_ref,
                  wcb_ref, w_out_ref, g1_ref, be1_ref, rwt_hi_ref, rwt_lo_ref, rb_ref, tri_ref,
                  x1_ref, idx_ref, rank_ref, grow_ref, hl_ref, ca_ref, cb_ref, cnt_ref,
                  xa_s, u_s, a_s, b_s, h_s, hst_s, cnt_s, *, ts, nb, alpha):
    i = pl.program_id(0)
    j = pl.program_id(1)
    rows = ts * nb
    ta = (CONV_A_WIDTH - 1) * nb
    tb = (CONV_B_WIDTH - 1) * nb

    @pl.when(j == 0)
    def _():
        hst_s[...] = h0_ref[...]
        xa_s[0:ta, :] = ca0_ref[...].reshape(ta, D_MODEL)
        u_s[0:tb, :] = cb0_ref[...].reshape(tb, D_MODEL)

    @pl.when((i == 0) & (j == 0))
    def _():
        cnt_s[...] = cnt0_ref[...]

    x = x_ref[...].reshape(rows, D_MODEL)
    xb = x.astype(_BF16)

    def proj(g):
        lo, hi = g * D_MODEL, (g + 1) * D_MODEL
        return jnp.dot(xb, w_in_ref[:, lo:hi], preferred_element_type=_F32) + b_in_ref[:, lo:hi]

    xa_s[ta:ta + rows, :] = proj(0)
    xc = bca_ref[...] + xa_s[0:rows, :] * wca_ref[0:1, :]
    for k in range(1, CONV_A_WIDTH):
        xc = xc + xa_s[k * nb:k * nb + rows, :] * wca_ref[k:k + 1, :]
    new_ta = xa_s[rows:rows + ta, :]
    xa_s[0:ta, :] = new_ta
    ca_ref[...] = new_ta.reshape(CONV_A_WIDTH - 1, nb, D_MODEL)

    xcb = xc.astype(_BF16)

    def block_diag(w_ref):
        return jnp.concatenate(
            [jnp.dot(xcb[:, q * GATE_TILE:(q + 1) * GATE_TILE], w_ref[q], preferred_element_type=_F32)
             for q in range(N_GATE_TILES)], axis=-1)

    r = _sigmoid(block_diag(wa_ref) + ba_ref[...])
    ig = _sigmoid(block_diag(wx_ref) + bx_ref[...])
    nlam = -lam_ref[...]
    softplus = jnp.maximum(nlam, 0.0) + jnp.log1p(jnp.exp(-jnp.abs(nlam)))
    log_a = (-LRU_C * softplus) * r
    a = jnp.exp(log_a)
    a_s[...] = a
    b_s[...] = jnp.sqrt(-jnp.tanh(log_a) * (a * a + 1.0)) * (ig * xc)

    h = hst_s[...]
    for t in range(ts):
        sl = slice(t * nb, (t + 1) * nb)
        h = a_s[sl, :] * h + b_s[sl, :]
        h_s[sl, :] = h
    hst_s[...] = h
    hl_ref[...] = h

    a_s[...] = h_s[...] * _gelu_tanh(proj(1))

    u_s[tb:tb + rows, :] = proj(3) * proj(4)
    uc = u_s[0:rows, :] * wcb_ref[0:1, :]
    for k in range(1, CONV_B_WIDTH):
        uc = uc + u_s[k * nb:k * nb + rows, :] * wcb_ref[k:k + 1, :]
    new_tb = u_s[rows:rows + tb, :]
    u_s[0:tb, :] = new_tb
    cb_ref[...] = new_tb.reshape(CONV_B_WIDTH - 1, nb, D_MODEL)
    y_b = proj(2) * uc

    merged = _sigmoid(proj(5)) * a_s[...] + _sigmoid(proj(6)) * y_b
    mixed = jnp.dot(merged.astype(_BF16), w_out_ref[...], preferred_element_type=_F32)
    x1 = _layernorm(alpha * x + mixed, g1_ref[...], be1_ref[...])
    for s in range(8):
        x1_ref[pl.ds(s, rows, stride=8), :] = x1[:, s * LANES:(s + 1) * LANES]

    x1_hi = x1.astype(_BF16)
    x1_lo = (x1 - x1_hi.astype(_F32)).astype(_BF16)
    logits = (lax.dot_general(rwt_hi_ref[...], x1_hi, _NT, preferred_element_type=_F32)
              + lax.dot_general(rwt_hi_ref[...], x1_lo, _NT, preferred_element_type=_F32)
              + lax.dot_general(rwt_lo_ref[...], x1_hi, _NT, preferred_element_type=_F32)
              + rb_ref[:, 0:1])
    e_iota = lax.broadcasted_iota(jnp.int32, (N_EXPERTS, rows), 0)
    work = logits
    vals, sels, idxs = [], [], []
    for _ in range(TOP_K):
        m = jnp.max(work, axis=0, keepdims=True)
        ik = jnp.min(jnp.where(work == m, e_iota, N_EXPERTS), axis=0, keepdims=True)
        sel = e_iota == ik
        work = jnp.where(sel, -jnp.inf, work)
        vals.append(m)
        sels.append(sel)
        idxs.append(ik)
    exps = [jnp.exp(v - vals[0]) for v in vals]
    denom = exps[0] + exps[1] + exps[2] + exps[3]
    gates = [ex / denom for ex in exps]

    onehot = jnp.zeros((N_EXPERTS, rows), _F32)
    for sel in sels:
        onehot = onehot + sel.astype(_F32)
    prefix = jnp.dot(onehot.astype(_BF16), tri_ref[...], preferred_element_type=_F32)
    pos = prefix + cnt_s[:, 0:1]
    ranks = [jnp.sum(jnp.where(sel, pos, 0.0), axis=0, keepdims=True) for sel in sels]
    new_cnt = cnt_s[...] + jnp.sum(onehot, axis=1, keepdims=True)
    cnt_s[...] = new_cnt
    cnt_ref[...] = new_cnt

    row8 = lax.broadcasted_iota(jnp.int32, (8, rows), 0)
    idx8 = jnp.zeros((8, rows), jnp.int32)
    rank8 = jnp.zeros((8, rows), jnp.int32)
    for k in range(TOP_K):
        idx8 = jnp.where(row8 == k, idxs[k], idx8)
        rank8 = jnp.where(row8 == k, ranks[k].astype(jnp.int32), rank8)
    idx_ref[...] = idx8
    rank_ref[...] = rank8

    row_l = lax.broadcasted_iota(jnp.int32, (LANES, rows), 0)
    g_t = jnp.zeros((LANES, rows), _F32)
    for k in range(TOP_K):
        g_t = jnp.where(row_l == k, gates[k], g_t)
    grow_ref[...] = jnp.transpose(g_t)


def _const_spec(shape):
    nd = len(shape)
    return pl.BlockSpec(shape, lambda i, j: (0,) * nd, pipeline_mode=pl.Buffered(1))


def _mixer_call(x4, h0, ca0, cb0, cnt0, x1_buf, wts, *, ts, blk_off, alpha):
    nblk, seq, nb, _ = x4.shape
    nt = seq // ts
    rows = ts * nb
    total = nblk * seq * nb
    f32 = lambda *s: jax.ShapeDtypeStruct(s, _F32)
    i32 = lambda *s: jax.ShapeDtypeStruct(s, jnp.int32)
    in_specs = [
        pl.BlockSpec((None, ts, nb, D_MODEL), lambda i, j: (i, j, 0, 0)),
        pl.BlockSpec((None, nb, D_MODEL), lambda i, j: (i, 0, 0)),
        pl.BlockSpec((None, CONV_A_WIDTH - 1, nb, D_MODEL), lambda i, j: (i, 0, 0, 0)),
        pl.BlockSpec((None, CONV_B_WIDTH - 1, nb, D_MODEL), lambda i, j: (i, 0, 0, 0)),
        _const_spec(cnt0.shape),
        pl.BlockSpec(memory_space=pl.ANY),
    ] + [_const_spec(w.shape) for w in wts]
    out_shape = (
        jax.ShapeDtypeStruct(x1_buf.shape, _F32),
        i32(8, total),
        i32(8, total),
        f32(total, LANES),
        f32(nblk, nb, D_MODEL),
        f32(nblk, CONV_A_WIDTH - 1, nb, D_MODEL),
        f32(nblk, CONV_B_WIDTH - 1, nb, D_MODEL),
        f32(N_EXPERTS, LANES),
    )
    out_specs = (
        pl.BlockSpec((rows * 8, LANES), lambda i, j: (blk_off + i * nt + j, 0)),
        pl.BlockSpec((8, rows), lambda i, j: (0, i * nt + j)),
        pl.BlockSpec((8, rows), lambda i, j: (0, i * nt + j)),
        pl.BlockSpec((rows, LANES), lambda i, j: (i * nt + j, 0)),
        pl.BlockSpec((None, nb, D_MODEL), lambda i, j: (i, 0, 0)),
        pl.BlockSpec((None, CONV_A_WIDTH - 1, nb, D_MODEL), lambda i, j: (i, 0, 0, 0)),
        pl.BlockSpec((None, CONV_B_WIDTH - 1, nb, D_MODEL), lambda i, j: (i, 0, 0, 0)),
        pl.BlockSpec((N_EXPERTS, LANES), lambda i, j: (0, 0)),
    )
    scratch = [
        pltpu.VMEM((rows + (CONV_A_WIDTH - 1) * nb, D_MODEL), _F32),
        pltpu.VMEM((rows + (CONV_B_WIDTH - 1) * nb, D_MODEL), _F32),
        pltpu.VMEM((rows, D_MODEL), _F32),
        pltpu.VMEM((rows, D_MODEL), _F32),
        pltpu.VMEM((rows, D_MODEL), _F32),
        pltpu.VMEM((nb, D_MODEL), _F32),
        pltpu.VMEM((N_EXPERTS, LANES), _F32),
    ]
    return pl.pallas_call(
        functools.partial(_mixer_kernel, ts=ts, nb=nb, alpha=alpha),
        grid=(nblk, nt),
        in_specs=in_specs,
        out_specs=out_specs,
        out_shape=out_shape,
        scratch_shapes=scratch,
        input_output_aliases={5: 0},
        compiler_params=pltpu.CompilerParams(
            dimension_semantics=("arbitrary", "arbitrary"), vmem_limit_bytes=VMEM_LIMIT),
        name="mixer",
    )(x4, h0, ca0, cb0, cnt0, x1_buf, *wts)


def _invmap_kernel(dest_ref, lo_ref, hi_ref, inv_ref, *, total, n_rows):
    n_slots = total * TOP_K

    for r in range(MOE_ROWS):
        inv_ref[r] = n_slots + 2 * MOE_ROWS + r

    def fill(lo, hi):
        def body(row, c):
            inv_ref[MOE_ROWS + row] = n_slots + (row & (2 * MOE_ROWS - 1))
            return c
        lax.fori_loop(lo, hi, body, 0)

    def per_expert(e, c):
        fill(lo_ref[e], hi_ref[e])
        return c
    lax.fori_loop(0, N_EXPERTS, per_expert, 0)
    fill(hi_ref[N_EXPERTS - 1], n_rows)

    unroll = 16
    for k in range(TOP_K):
        def body(it, c, k=k):
            base = it * unroll
            rows = [dest_ref[k * total + base + u] for u in range(unroll)]
            val = base * TOP_K + k
            for u in range(unroll):
                inv_ref[MOE_ROWS + rows[u]] = val + u * TOP_K
            return c
        lax.fori_loop(0, total // unroll, body, 0)


def _invmap_call(dest, pad_lo, pad_hi, *, total, n_rows):
    smem = pl.BlockSpec(memory_space=pltpu.SMEM)
    return pl.pallas_call(
        functools.partial(_invmap_kernel, total=total, n_rows=n_rows),
        in_specs=[smem, smem, smem],
        out_specs=smem,
        out_shape=jax.ShapeDtypeStruct((n_rows + MOE_ROWS,), jnp.int32),
        name="invmap",
    )(dest, pad_lo, pad_hi)


SC_WORKERS = 32
SC_WINDOW = 128
SC_LANES = 16


def _invmap_sc_call(dest, *, total, n_out):
    n = dest.shape[0]
    per_worker = n // SC_WORKERS
    n_windows = per_worker // SC_WINDOW
    assert per_worker * SC_WORKERS == n and n_windows * SC_WINDOW == per_worker
    assert total % per_worker == 0
    mesh = plsc.VectorSubcoreMesh(core_axis_name="c", subcore_axis_name="s")

    @functools.partial(
        pl.kernel, mesh=mesh,
        out_type=jax.ShapeDtypeStruct((SC_WORKERS, n_out), jnp.int32),
        scratch_types=[pltpu.VMEM((n_out,), jnp.int32),
                       pltpu.VMEM((per_worker,), jnp.int32),
                       pltpu.SemaphoreType.DMA],
        compiler_params=pltpu.CompilerParams(use_tc_tiling_on_sc=False, needs_layout_passes=False),
        name="invmap_sc")
    def scatter(dest_hbm, out_hbm, loc_v, idx_v, sem):
        worker = lax.axis_index("s") * 2 + lax.axis_index("c")
        first = worker * per_worker
        k = first // total
        lane = lax.iota(jnp.int32, SC_LANES)
        load = pltpu.make_async_copy(dest_hbm.at[pl.ds(first, per_worker)], idx_v, sem)
        load.start()
        empty = jnp.full((SC_LANES,), -1, jnp.int32)
        unroll = 8

        @pl.loop(0, n_out // (SC_LANES * unroll))
        def _(i):
            for u in range(unroll):
                loc_v[pl.ds((i * unroll + u) * SC_LANES, SC_LANES)] = empty
        load.wait()

        @pl.loop(0, per_worker // SC_LANES)
        def _(j):
            idx = idx_v[pl.ds(j * SC_LANES, SC_LANES)]
            tok = first - k * total + j * SC_LANES + lane
            plsc.store_scatter(loc_v, [idx], tok * TOP_K + k)
        pltpu.sync_copy(loc_v, out_hbm.at[worker])

    assert n_out % (SC_LANES * 8) == 0
    return jnp.max(scatter(dest), axis=0)


def _moe_kernel(blk_e_ref, first_ref, wslot_ref, nxt_e_ref, n_used_ref, inv_ref,
                x1_hbm, wgu_hbm, bgu_ref, wd_hbm, bd_ref,
                ytok_hbm,
                xb0, xb1, xb2, yb0, yb1, yb2, wgu_f, wd_f, wgu_s, wd_s, bgu_s, bd_s,
                gsem, ssem, wsem, *, n_slots):
    n_used = n_used_ref[0]
    n_blk = blk_e_ref.shape[0]
    xbufs = (xb0, xb1, xb2)
    ybufs = (yb0, yb1, yb2)

    def gather_copy(v, s, r):
        return pltpu.make_async_copy(
            x1_hbm.at[v >> 2], xbufs[s].at[pl.ds(r * 8, 8), :], gsem.at[s])

    def scatter_copy(v, s, r):
        return pltpu.make_async_copy(
            ybufs[s].at[pl.ds(r * 8, 8), :], ytok_hbm.at[v], ssem.at[s])

    def weight_copies(e, ws):
        return (pltpu.make_async_copy(wgu_hbm.at[e], wgu_f.at[ws], wsem.at[ws]),
                pltpu.make_async_copy(wd_hbm.at[e], wd_f.at[ws], wsem.at[ws]))

    def wait_rows(copy_fn, s):
        for r in range(MOE_ROWS):
            copy_fn(0, s, r).wait()

    for cp in weight_copies(blk_e_ref[0], 0):
        cp.start()
    for yb in ybufs:
        yb[...] = jnp.zeros(yb.shape, _F32)

    def prime(r, c):
        for s in range(2):
            pltpu.make_async_copy(
                ybufs[s].at[pl.ds(pl.multiple_of(r * 8, 8), 8), :],
                ytok_hbm.at[n_slots + s * MOE_ROWS + r], ssem.at[s]).start()
            pltpu.make_async_copy(
                x1_hbm.at[inv_ref[(s + 1) * MOE_ROWS + r] >> 2],
                xbufs[s].at[pl.ds(pl.multiple_of(r * 8, 8), 8), :], gsem.at[s]).start()
        return c
    lax.fori_loop(0, MOE_ROWS, prime, 0)

    def run_block(b, s):
        nxt = (s + 2) % 3
        wait_rows(gather_copy, s)
        wait_rows(scatter_copy, s)
        x = jnp.concatenate(
            [xbufs[s][pl.ds(q, MOE_ROWS, stride=8), :] for q in range(8)], axis=-1)
        gbase = (jnp.minimum(b + 2, n_blk - 1) + 1) * MOE_ROWS
        sbase = b * MOE_ROWS
        for r in range(MOE_ROWS):
            gather_copy(inv_ref[gbase + r], nxt, r).start()
        for r in range(MOE_ROWS):
            scatter_copy(inv_ref[sbase + r], nxt, r).start()
        gu = jnp.dot(x.astype(_BF16), wgu_s[...], preferred_element_type=_F32) + bgu_s[...]
        gate = jnp.minimum(gu[:, :D_FF], SWIGLU_LIMIT)
        up = jnp.clip(gu[:, D_FF:], -SWIGLU_LIMIT, SWIGLU_LIMIT)
        hmid = (up + 1.0) * (gate * _sigmoid(SWIGLU_ALPHA * gate))
        y = jnp.dot(hmid.astype(_BF16), wd_s[...], preferred_element_type=_F32) + bd_s[...]
        for q in range(8):
            ybufs[s][pl.ds(q, MOE_ROWS, stride=8), :] = y[:, q * LANES:(q + 1) * LANES]

    def body(b, c):
        @pl.when(first_ref[b] == 1)
        def _():
            e = blk_e_ref[b]
            ws = wslot_ref[b]
            for cp in weight_copies(e, ws):
                cp.wait()
            nxt_e = nxt_e_ref[b]

            @pl.when(nxt_e >= 0)
            def _():
                for cp in weight_copies(nxt_e, 1 - ws):
                    cp.start()
            wgu_s[...] = wgu_f[ws].astype(_BF16)
            wd_s[...] = wd_f[ws].astype(_BF16)
            bgu_s[...] = bgu_ref[e]
            bd_s[...] = bd_ref[e]

        for s in range(3):
            pl.when(b % 3 == s)(functools.partial(run_block, b, s))
        return c

    lax.fori_loop(0, n_used + 1, body, 0)

    for d in (1, 2):
        for s in range(3):
            @pl.when((n_used + d) % 3 == s)
            def _():
                wait_rows(gather_copy, s)
                wait_rows(scatter_copy, s)


def _moe_call(blk_e, first, wslot, nxt_e, n_used, inv, x1t, w_gu, b_gu, w_down, b_down, *, n_slots):
    full = lambda shape: pl.BlockSpec(shape, lambda i, *_: (0,) * len(shape))
    grid_spec = pltpu.PrefetchScalarGridSpec(
        num_scalar_prefetch=6,
        grid=(1,),
        in_specs=[
            pl.BlockSpec(memory_space=pl.ANY),
            pl.BlockSpec(memory_space=pl.ANY),
            full(b_gu.shape),
            pl.BlockSpec(memory_space=pl.ANY),
            full(b_down.shape),
        ],
        out_specs=pl.BlockSpec(memory_space=pl.ANY),
        scratch_shapes=[pltpu.VMEM((MOE_ROWS * 8, LANES), _F32)] * 6 + [
            pltpu.VMEM((2, D_MODEL, 2 * D_FF), _F32),
            pltpu.VMEM((2, D_FF, D_MODEL), _F32),
            pltpu.VMEM((D_MODEL, 2 * D_FF), _BF16),
            pltpu.VMEM((D_FF, D_MODEL), _BF16),
            pltpu.VMEM((1, 2 * D_FF), _F32),
            pltpu.VMEM((1, D_MODEL), _F32),
            pltpu.SemaphoreType.DMA((3,)),
            pltpu.SemaphoreType.DMA((3,)),
            pltpu.SemaphoreType.DMA((2,)),
        ],
    )
    return pl.pallas_call(
        functools.partial(_moe_kernel, n_slots=n_slots),
        grid_spec=grid_spec,
        out_shape=jax.ShapeDtypeStruct((n_slots + 3 * MOE_ROWS, 8, LANES), _F32),
        compiler_params=pltpu.CompilerParams(
            dimension_semantics=("arbitrary",), vmem_limit_bytes=VMEM_LIMIT),
        name="moe",
    )(blk_e, first, wslot, nxt_e, n_used, inv, x1t, w_gu, b_gu, w_down, b_down)


def _combine_kernel(x1_ref, g_ref, y0_ref, y1_ref, y2_ref, y3_ref, g2_ref, be2_ref,
                    yp_ref, ys_ref, y_s, *, alpha, n_prompt_steps):
    i = pl.program_id(0)
    g = g_ref[...]
    rows = g.shape[0]

    def token_major(ref):
        return jnp.concatenate([ref[pl.ds(s, rows, stride=8), :] for s in range(8)], axis=-1)

    ys = [token_major(r.reshape(rows * 8, LANES)) for r in (y0_ref, y1_ref, y2_ref, y3_ref)]
    moe = g[:, 0:1] * ys[0] + g[:, 1:2] * ys[1] + g[:, 2:3] * ys[2] + g[:, 3:4] * ys[3]
    y = _layernorm(alpha * token_major(x1_ref) + moe, g2_ref[...], be2_ref[...])
    for q in range(8):
        y_s[q] = y[:, q * LANES:(q + 1) * LANES]

    def batch_rows(bb, nb):
        return jnp.concatenate(
            [y_s[q, pl.ds(bb, rows // nb, stride=nb), :] for q in range(8)], axis=-1)

    @pl.when(i < n_prompt_steps)
    def _():
        for bb in range(yp_ref.shape[0]):
            yp_ref[bb] = batch_rows(bb, yp_ref.shape[0])

    @pl.when(i >= n_prompt_steps)
    def _():
        for bb in range(ys_ref.shape[0]):
            ys_ref[bb] = batch_rows(bb, ys_ref.shape[0])


def _combine_call(x1t, grow, ytok, g2, be2, *, prompt_shape, sample_shape, alpha):
    bp, sp, _ = prompt_shape
    bs, ss, _ = sample_shape
    ts_p = COMBINE_ROWS // bp
    nb_s = COMBINE_ROWS // ss
    n_p = sp // ts_p
    nt = n_p + bs // nb_s
    plane = lambda k: pl.BlockSpec((COMBINE_ROWS, None, 8, LANES), lambda i, k=k: (i, k, 0, 0))
    vec = pl.BlockSpec((1, D_MODEL), lambda i: (0, 0))
    ytok4 = ytok.reshape(-1, TOP_K, 8, LANES)
    return pl.pallas_call(
        functools.partial(_combine_kernel, alpha=alpha, n_prompt_steps=n_p),
        grid=(nt,),
        in_specs=[pl.BlockSpec((COMBINE_ROWS * 8, LANES), lambda i: (i, 0)),
                  pl.BlockSpec((COMBINE_ROWS, LANES), lambda i: (i, 0)),
                  plane(0), plane(1), plane(2), plane(3), vec, vec],
        out_specs=(pl.BlockSpec((bp, ts_p, D_MODEL), lambda i: (0, jnp.minimum(i, n_p - 1), 0)),
                   pl.BlockSpec((nb_s, ss, D_MODEL), lambda i: (jnp.maximum(i - n_p, 0), 0, 0))),
        out_shape=(jax.ShapeDtypeStruct(prompt_shape, _F32), jax.ShapeDtypeStruct(sample_shape, _F32)),
        scratch_shapes=[pltpu.VMEM((8, COMBINE_ROWS, LANES), _F32)],
        compiler_params=pltpu.CompilerParams(
            dimension_semantics=("arbitrary",), vmem_limit_bytes=VMEM_LIMIT),
        name="combine",
    )(x1t, grow, ytok4, ytok4, ytok4, ytok4, g2, be2)


def _pack_block_diag(w):
    w = w.reshape(N_GATE_TILES, HEADS_PER_TILE, LRU_BLOCK, LRU_BLOCK)
    eye = jnp.eye(HEADS_PER_TILE, dtype=w.dtype)
    t = jnp.einsum("qhij,hg->qhigj", w, eye)
    return t.reshape(N_GATE_TILES, GATE_TILE, GATE_TILE)


def _layer(xp, xs, h_s0, ca_s0, cb_s0, p, *, alpha):
    bp, sp, _ = xp.shape
    bs, ss, _ = xs.shape
    tp, tsm = bp * sp, bs * ss
    total = tp + tsm
    row2 = lambda v: v.reshape(1, -1)

    rwt = jnp.transpose(p["router_w"])
    rwt_hi = rwt.astype(_BF16)
    rwt_lo = (rwt - rwt_hi.astype(_F32)).astype(_BF16)
    ii = jnp.arange(MIXER_ROWS)
    tri = (ii[:, None] < ii[None, :]).astype(_BF16)
    wts = (
        p["w_in"].astype(_BF16), row2(p["b_in"]), p["conv_a_w"], row2(p["conv_a_b"]),
        _pack_block_diag(p["lru_wa"]).astype(_BF16), row2(p["lru_ba"]),
        _pack_block_diag(p["lru_wx"]).astype(_BF16), row2(p["lru_bx"]), row2(p["lru_lambda"]),
        p["conv_b_w"], p["w_out"].astype(_BF16), row2(p["ln1_g"]), row2(p["ln1_b"]),
        rwt_hi, rwt_lo, jnp.broadcast_to(p["router_b"][:, None], (N_EXPERTS, LANES)), tri,
    )

    ts_p = MIXER_ROWS // bp
    xp4 = jnp.transpose(xp, (1, 0, 2))[None]
    zeros = lambda *s: jnp.zeros(s, _F32)
    n_assign = total * TOP_K
    n_spare = 3 * MOE_ROWS
    x1_buf = zeros((total + n_spare // TOP_K) * 8, LANES)
    outs_p = _mixer_call(xp4, zeros(1, bp, D_MODEL), zeros(1, CONV_A_WIDTH - 1, bp, D_MODEL),
                         zeros(1, CONV_B_WIDTH - 1, bp, D_MODEL), zeros(N_EXPERTS, LANES), x1_buf, wts,
                         ts=ts_p, blk_off=0, alpha=alpha)
    nb_s = MIXER_ROWS // ss
    nblk_s = bs // nb_s
    xs4 = jnp.transpose(xs.reshape(nblk_s, nb_s, ss, D_MODEL), (0, 2, 1, 3))
    h0 = h_s0.reshape(nblk_s, nb_s, D_MODEL)
    ca0 = jnp.transpose(ca_s0.reshape(nblk_s, nb_s, CONV_A_WIDTH - 1, D_MODEL), (0, 2, 1, 3))
    cb0 = jnp.transpose(cb_s0.reshape(nblk_s, nb_s, CONV_B_WIDTH - 1, D_MODEL), (0, 2, 1, 3))
    outs_s = _mixer_call(xs4, h0, ca0, cb0, outs_p[7], outs_p[0], wts,
                         ts=ss, blk_off=tp // MIXER_ROWS, alpha=alpha)
    x1t = outs_s[0]

    n_blocks = -(-n_assign // MOE_ROWS) + N_EXPERTS + 1
    n_rows = n_blocks * MOE_ROWS
    idx = jnp.concatenate([outs_p[1][:TOP_K], outs_s[1][:TOP_K]], axis=1)
    rank = jnp.concatenate([outs_p[2][:TOP_K], outs_s[2][:TOP_K]], axis=1)
    grow = jnp.concatenate([outs_p[3], outs_s[3]], axis=0)
    counts = outs_s[7][:, 0].astype(jnp.int32)

    padded = (counts + MOE_ROWS - 1) // MOE_ROWS * MOE_ROWS
    end_pad = jnp.cumsum(padded)
    start_pad = end_pad - padded
    experts = jnp.arange(N_EXPERTS, dtype=jnp.int32)
    start_of = jnp.sum(jnp.where(idx[:, :, None] == experts, start_pad, 0), axis=-1)
    dest = (start_of + rank).reshape(n_assign)
    blk_start = jnp.arange(n_blocks, dtype=jnp.int32) * MOE_ROWS
    blk_e = jnp.minimum(jnp.sum((blk_start[:, None] >= end_pad[None, :]).astype(jnp.int32), axis=1),
                        N_EXPERTS - 1)
    n_used = (end_pad[-1] // MOE_ROWS).astype(jnp.int32).reshape(1)

    blk_ids = jnp.arange(n_blocks, dtype=jnp.int32)
    changed = jnp.concatenate([jnp.ones((1,), bool), blk_e[1:] != blk_e[:-1]])
    first = changed & (blk_ids < n_used[0])
    wslot = (jnp.cumsum(first.astype(jnp.int32)) - 1) % 2
    later_first = first[None, :] & (blk_ids[None, :] > blk_ids[:, None])
    nxt_blk = jnp.min(jnp.where(later_first, blk_ids[None, :], n_blocks), axis=1)
    nxt_e = jnp.sum(jnp.where(nxt_blk[:, None] == blk_ids[None, :], blk_e[None, :], 0), axis=1)
    nxt_e = jnp.where(nxt_blk < n_blocks, nxt_e, -1)

    ext = jnp.arange(n_rows + MOE_ROWS, dtype=jnp.int32)
    row = ext - MOE_ROWS
    holds = jnp.any((row[:, None] >= start_pad[None, :]) & (row[:, None] < (start_pad + counts)[None, :]),
                    axis=1)
    spare = jnp.where(row < 0, n_assign + 2 * MOE_ROWS + ext, n_assign + (row & (2 * MOE_ROWS - 1)))
    scattered = _invmap_sc_call(dest + MOE_ROWS, total=total, n_out=n_rows + MOE_ROWS)
    inv = jnp.where(holds, scattered, spare)
    ytok = _moe_call(blk_e, first.astype(jnp.int32), wslot, nxt_e, n_used, inv, x1t.reshape(-1, 8, LANES),
                     p["w_gu"], p["b_gu"][:, None, :], p["w_down"], p["b_down"][:, None, :],
                     n_slots=n_assign)
    yp, ys = _combine_call(x1t, grow, ytok, row2(p["ln2_g"]), row2(p["ln2_b"]),
                           prompt_shape=xp.shape, sample_shape=xs.shape, alpha=alpha)

    def batch_major(v, nblk):
        return jnp.transpose(v, (0, 2, 1, 3)).reshape(nblk * v.shape[2], v.shape[1], D_MODEL)

    states_p = (outs_p[4].reshape(bp, D_MODEL), batch_major(outs_p[5], 1), batch_major(outs_p[6], 1))
    states_s = (outs_s[4].reshape(bs, D_MODEL), batch_major(outs_s[5], nblk_s), batch_major(outs_s[6], nblk_s))
    return yp, ys, states_p, states_s


def kernel(x_prompt, x_sample, state_rglru_h, state_rglru_conv, state_shortconv, w_in, b_in, conv_a_w, conv_a_b, lru_wa, lru_ba, lru_wx, lru_bx, lru_lambda, conv_b_w, w_out, ln1_g, ln1_b, router_w, router_b, w_gu, b_gu, w_down, b_down, ln2_g, ln2_b):
    depth = w_in.shape[0]
    alpha = (2.0 * depth) ** 0.25
    names = ("w_in", "b_in", "conv_a_w", "conv_a_b", "lru_wa", "lru_ba", "lru_wx", "lru_bx", "lru_lambda",
             "conv_b_w", "w_out", "ln1_g", "ln1_b", "router_w", "router_b", "w_gu", "b_gu", "w_down",
             "b_down", "ln2_g", "ln2_b")
    stacked = (w_in, b_in, conv_a_w, conv_a_b, lru_wa, lru_ba, lru_wx, lru_bx, lru_lambda, conv_b_w, w_out,
               ln1_g, ln1_b, router_w, router_b, w_gu, b_gu, w_down, b_down, ln2_g, ln2_b)
    xp, xs = x_prompt, x_sample
    hp_l, cp_l, sp_l, hs_l, cs_l, ss_l = [], [], [], [], [], []
    for l in range(depth):
        p = {n: v[l] for n, v in zip(names, stacked)}
        xp, xs, (hp, cp, sp), (hs, cs, ss) = _layer(
            xp, xs, state_rglru_h[l], state_rglru_conv[l], state_shortconv[l], p, alpha=alpha)
        hp_l.append(hp); cp_l.append(cp); sp_l.append(sp)
        hs_l.append(hs); cs_l.append(cs); ss_l.append(ss)
    return (xp, xs, jnp.stack(hp_l), jnp.stack(cp_l), jnp.stack(sp_l), jnp.stack(hs_l), jnp.stack(cs_l),
            jnp.stack(ss_l))
```

```python
import functools

import jax
import jax.numpy as jnp
from jax import lax
from jax.experimental import pallas as pl
from jax.experimental.pallas import tpu as pltpu
from jax.experimental.pallas import tpu_sc as plsc

D_MODEL = 1024
LRU_HEADS = 16
LRU_BLOCK = D_MODEL // LRU_HEADS
LRU_C = 8.0
CONV_A_WIDTH = 4
CONV_B_WIDTH = 3
N_GROUPS = 7
N_EXPERTS = 32
TOP_K = 4
D_FF = D_MODEL
SWIGLU_LIMIT = 7.0
SWIGLU_ALPHA = 1.702
LN_EPS = 1e-5

GATE_TILE = 256
HEADS_PER_TILE = GATE_TILE // LRU_BLOCK
N_GATE_TILES = D_MODEL // GATE_TILE
LANES = 128
MIXER_ROWS = 512
MOE_ROWS = 256
COMBINE_ROWS = 512
VMEM_LIMIT = 58 * 1024 * 1024

_F32 = jnp.float32
_BF16 = jnp.bfloat16
_NT = (((1,), (1,)), ((), ()))


def _sigmoid(v):
    return 0.5 * jnp.tanh(0.5 * v) + 0.5


def _gelu_tanh(v):
    c = 0.7978845608028654
    return 0.5 * v * (1.0 + jnp.tanh(c * (v + 0.044715 * (v * v * v))))


def _layernorm(z, g, b):
    mu = jnp.mean(z, axis=-1, keepdims=True)
    zc = z - mu
    var = jnp.mean(zc * zc, axis=-1, keepdims=True)
    return zc * lax.rsqrt(var + LN_EPS) * g + b


def _mixer_kernel(x_ref, h0_ref, ca0_ref, cb0_ref, cnt0_ref, x1_buf_ref,
                  w_in_ref, b_in_ref, wca_ref, bca_ref, wa_ref, ba_ref, wx_ref, bx_ref, lam_ref,
                  wcb_ref, w_out_ref, g1_ref, be1_ref, rwt_hi_ref, rwt_lo_ref, rb_ref, tri_ref,
                  x1_ref, idx_ref, rank_ref, grow_ref, hl_ref, ca_ref, cb_ref, cnt_ref,
                  xa_s, u_s, a_s, b_s, h_s, hst_s, cnt_s, *maybe_xin, ts, nb, alpha, batch_major_x):
    i = pl.program_id(0)
    j = pl.program_id(1)
    rows = ts * nb
    ta = (CONV_A_WIDTH - 1) * nb
    tb = (CONV_B_WIDTH - 1) * nb

    @pl.when(j == 0)
    def _():
        hst_s[...] = h0_ref[...]
        xa_s[0:ta, :] = ca0_ref[...].reshape(ta, D_MODEL)
        u_s[0:tb, :] = cb0_ref[...].reshape(tb, D_MODEL)

    @pl.when((i == 0) & (j == 0))
    def _():
        cnt_s[...] = cnt0_ref[...]

    if batch_major_x:
        xin_s, xsem = maybe_xin
        nt = pl.num_programs(1)

        def chunk_copies(c, slot):
            return [pltpu.make_async_copy(x_ref.at[bb, pl.ds(c * ts, ts), :],
                                          xin_s.at[slot, :, bb, :], xsem.at[slot]) for bb in range(nb)]

        @pl.when(j == 0)
        def _():
            for cp in chunk_copies(0, 0):
                cp.start()

        @pl.when(j + 1 < nt)
        def _():
            for cp in chunk_copies(j + 1, (j + 1) % 2):
                cp.start()
        for cp in chunk_copies(j, j % 2):
            cp.wait()
        x = xin_s[j % 2].reshape(rows, D_MODEL)
    else:
        x = x_ref[...].reshape(rows, D_MODEL)
    xb = x.astype(_BF16)

    def proj(g):
        lo, hi = g * D_MODEL, (g + 1) * D_MODEL
        return jnp.dot(xb, w_in_ref[:, lo:hi], preferred_element_type=_F32) + b_in_ref[:, lo:hi]

    xa_s[ta:ta + rows, :] = proj(0)
    xc = bca_ref[...] + xa_s[0:rows, :] * wca_ref[0:1, :]
    for k in range(1, CONV_A_WIDTH):
        xc = xc + xa_s[k * nb:k * nb + rows, :] * wca_ref[k:k + 1, :]
    new_ta = xa_s[rows:rows + ta, :]
    xa_s[0:ta, :] = new_ta
    ca_ref[...] = new_ta.reshape(CONV_A_WIDTH - 1, nb, D_MODEL)

    xcb = xc.astype(_BF16)

    def block_diag(w_ref):
        return jnp.concatenate(
            [jnp.dot(xcb[:, q * GATE_TILE:(q + 1) * GATE_TILE], w_ref[q], preferred_element_type=_F32)
             for q in range(N_GATE_TILES)], axis=-1)

    r = _sigmoid(block_diag(wa_ref) + ba_ref[...])
    ig = _sigmoid(block_diag(wx_ref) + bx_ref[...])
    nlam = -lam_ref[...]
    softplus = jnp.maximum(nlam, 0.0) + jnp.log1p(jnp.exp(-jnp.abs(nlam)))
    log_a = (-LRU_C * softplus) * r
    a = jnp.exp(log_a)
    a_s[...] = a
    b_s[...] = jnp.sqrt(-jnp.tanh(log_a) * (a * a + 1.0)) * (ig * xc)

    h = hst_s[...]
    for t in range(ts):
        sl = slice(t * nb, (t + 1) * nb)
        h = a_s[sl, :] * h + b_s[sl, :]
        h_s[sl, :] = h
    hst_s[...] = h
    hl_ref[...] = h

    a_s[...] = h_s[...] * _gelu_tanh(proj(1))

    u_s[tb:tb + rows, :] = proj(3) * proj(4)
    uc = u_s[0:rows, :] * wcb_ref[0:1, :]
    for k in range(1, CONV_B_WIDTH):
        uc = uc + u_s[k * nb:k * nb + rows, :] * wcb_ref[k:k + 1, :]
    new_tb = u_s[rows:rows + tb, :]
    u_s[0:tb, :] = new_tb
    cb_ref[...] = new_tb.reshape(CONV_B_WIDTH - 1, nb, D_MODEL)
    y_b = proj(2) * uc

    merged = _sigmoid(proj(5)) * a_s[...] + _sigmoid(proj(6)) * y_b
    mixed = jnp.dot(merged.astype(_BF16), w_out_ref[...], preferred_element_type=_F32)
    x1 = _layernorm(alpha * x + mixed, g1_ref[...], be1_ref[...])
    for s in range(8):
        x1_ref[pl.ds(s, rows, stride=8), :] = x1[:, s * LANES:(s + 1) * LANES]

    x1_hi = x1.astype(_BF16)
    x1_lo = (x1 - x1_hi.astype(_F32)).astype(_BF16)
    logits = (lax.dot_general(rwt_hi_ref[...], x1_hi, _NT, preferred_element_type=_F32)
              + lax.dot_general(rwt_hi_ref[...], x1_lo, _NT, preferred_element_type=_F32)
              + lax.dot_general(rwt_lo_ref[...], x1_hi, _NT, preferred_element_type=_F32)
              + rb_ref[:, 0:1])
    e_iota = lax.broadcasted_iota(jnp.int32, (N_EXPERTS, rows), 0)
    work = logits
    vals, sels, idxs = [], [], []
    for _ in range(TOP_K):
        m = jnp.max(work, axis=0, keepdims=True)
        ik = jnp.min(jnp.where(work == m, e_iota, N_EXPERTS), axis=0, keepdims=True)
        sel = e_iota == ik
        work = jnp.where(sel, -jnp.inf, work)
        vals.append(m)
        sels.append(sel)
        idxs.append(ik)
    exps = [jnp.exp(v - vals[0]) for v in vals]
    denom = exps[0] + exps[1] + exps[2] + exps[3]
    gates = [ex / denom for ex in exps]

    onehot = jnp.zeros((N_EXPERTS, rows), _F32)
    for sel in sels:
        onehot = onehot + sel.astype(_F32)
    prefix = jnp.dot(onehot.astype(_BF16), tri_ref[...], preferred_element_type=_F32)
    pos = prefix + cnt_s[:, 0:1]
    ranks = [jnp.sum(jnp.where(sel, pos, 0.0), axis=0, keepdims=True) for sel in sels]
    new_cnt = cnt_s[...] + jnp.sum(onehot, axis=1, keepdims=True)
    cnt_s[...] = new_cnt
    cnt_ref[...] = new_cnt

    row8 = lax.broadcasted_iota(jnp.int32, (8, rows), 0)
    idx8 = jnp.zeros((8, rows), jnp.int32)
    rank8 = jnp.zeros((8, rows), jnp.int32)
    for k in range(TOP_K):
        idx8 = jnp.where(row8 == k, idxs[k], idx8)
        rank8 = jnp.where(row8 == k, ranks[k].astype(jnp.int32), rank8)
    idx_ref[...] = idx8
    rank_ref[...] = rank8

    row_l = lax.broadcasted_iota(jnp.int32, (LANES, rows), 0)
    g_t = jnp.zeros((LANES, rows), _F32)
    for k in range(TOP_K):
        g_t = jnp.where(row_l == k, gates[k], g_t)
    grow_ref[...] = jnp.transpose(g_t)


def _const_spec(shape):
    nd = len(shape)
    return pl.BlockSpec(shape, lambda i, j: (0,) * nd, pipeline_mode=pl.Buffered(1))


def _mixer_call(x4, h0, ca0, cb0, cnt0, x1_buf, wts, *, ts, blk_off, alpha, batch_major_x=False):
    if batch_major_x:
        (nb, seq, _), nblk = x4.shape, 1
    else:
        nblk, seq, nb, _ = x4.shape
    nt = seq // ts
    rows = ts * nb
    total = nblk * seq * nb
    f32 = lambda *s: jax.ShapeDtypeStruct(s, _F32)
    i32 = lambda *s: jax.ShapeDtypeStruct(s, jnp.int32)
    in_specs = [
        (pl.BlockSpec(memory_space=pl.ANY) if batch_major_x
         else pl.BlockSpec((None, ts, nb, D_MODEL), lambda i, j: (i, j, 0, 0))),
        pl.BlockSpec((None, nb, D_MODEL), lambda i, j: (i, 0, 0)),
        pl.BlockSpec((None, CONV_A_WIDTH - 1, nb, D_MODEL), lambda i, j: (i, 0, 0, 0)),
        pl.BlockSpec((None, CONV_B_WIDTH - 1, nb, D_MODEL), lambda i, j: (i, 0, 0, 0)),
        _const_spec(cnt0.shape),
        pl.BlockSpec(memory_space=pl.ANY),
    ] + [_const_spec(w.shape) for w in wts]
    out_shape = (
        jax.ShapeDtypeStruct(x1_buf.shape, _F32),
        i32(8, total),
        i32(8, total),
        f32(total, LANES),
        f32(nblk, nb, D_MODEL),
        f32(nblk, CONV_A_WIDTH - 1, nb, D_MODEL),
        f32(nblk, CONV_B_WIDTH - 1, nb, D_MODEL),
        f32(N_EXPERTS, LANES),
    )
    out_specs = (
        pl.BlockSpec((rows * 8, LANES), lambda i, j: (blk_off + i * nt + j, 0)),
        pl.BlockSpec((8, rows), lambda i, j: (0, i * nt + j)),
        pl.BlockSpec((8, rows), lambda i, j: (0, i * nt + j)),
        pl.BlockSpec((rows, LANES), lambda i, j: (i * nt + j, 0)),
        pl.BlockSpec((None, nb, D_MODEL), lambda i, j: (i, 0, 0)),
        pl.BlockSpec((None, CONV_A_WIDTH - 1, nb, D_MODEL), lambda i, j: (i, 0, 0, 0)),
        pl.BlockSpec((None, CONV_B_WIDTH - 1, nb, D_MODEL), lambda i, j: (i, 0, 0, 0)),
        pl.BlockSpec((N_EXPERTS, LANES), lambda i, j: (0, 0)),
    )
    scratch = [
        pltpu.VMEM((rows + (CONV_A_WIDTH - 1) * nb, D_MODEL), _F32),
        pltpu.VMEM((rows + (CONV_B_WIDTH - 1) * nb, D_MODEL), _F32),
        pltpu.VMEM((rows, D_MODEL), _F32),
        pltpu.VMEM((rows, D_MODEL), _F32),
        pltpu.VMEM((rows, D_MODEL), _F32),
        pltpu.VMEM((nb, D_MODEL), _F32),
        pltpu.VMEM((N_EXPERTS, LANES), _F32),
    ]
    if batch_major_x:
        scratch += [pltpu.VMEM((2, ts, nb, D_MODEL), _F32), pltpu.SemaphoreType.DMA((2,))]
    return pl.pallas_call(
        functools.partial(_mixer_kernel, ts=ts, nb=nb, alpha=alpha, batch_major_x=batch_major_x),
        grid=(nblk, nt),
        in_specs=in_specs,
        out_specs=out_specs,
        out_shape=out_shape,
        scratch_shapes=scratch,
        input_output_aliases={5: 0},
        compiler_params=pltpu.CompilerParams(
            dimension_semantics=("arbitrary", "arbitrary"), vmem_limit_bytes=VMEM_LIMIT),
        name="mixer",
    )(x4, h0, ca0, cb0, cnt0, x1_buf, *wts)


def _invmap_kernel(dest_ref, lo_ref, hi_ref, inv_ref, *, total, n_rows):
    n_slots = total * TOP_K

    for r in range(MOE_ROWS):
        inv_ref[r] = n_slots + 2 * MOE_ROWS + r

    def fill(lo, hi):
        def body(row, c):
            inv_ref[MOE_ROWS + row] = n_slots + (row & (2 * MOE_ROWS - 1))
            return c
        lax.fori_loop(lo, hi, body, 0)

    def per_expert(e, c):
        fill(lo_ref[e], hi_ref[e])
        return c
    lax.fori_loop(0, N_EXPERTS, per_expert, 0)
    fill(hi_ref[N_EXPERTS - 1], n_rows)

    unroll = 16
    for k in range(TOP_K):
        def body(it, c, k=k):
            base = it * unroll
            rows = [dest_ref[k * total + base + u] for u in range(unroll)]
            val = base * TOP_K + k
            for u in range(unroll):
                inv_ref[MOE_ROWS + rows[u]] = val + u * TOP_K
            return c
        lax.fori_loop(0, total // unroll, body, 0)


def _invmap_call(dest, pad_lo, pad_hi, *, total, n_rows):
    smem = pl.BlockSpec(memory_space=pltpu.SMEM)
    return pl.pallas_call(
        functools.partial(_invmap_kernel, total=total, n_rows=n_rows),
        in_specs=[smem, smem, smem],
        out_specs=smem,
        out_shape=jax.ShapeDtypeStruct((n_rows + MOE_ROWS,), jnp.int32),
        name="invmap",
    )(dest, pad_lo, pad_hi)


SC_WORKERS = 32
SC_WINDOW = 128
SC_LANES = 16


def _invmap_sc_call(dest, *, total, n_out):
    n = dest.shape[0]
    per_worker = n // SC_WORKERS
    n_windows = per_worker // SC_WINDOW
    assert per_worker * SC_WORKERS == n and n_windows * SC_WINDOW == per_worker
    assert total % per_worker == 0
    mesh = plsc.VectorSubcoreMesh(core_axis_name="c", subcore_axis_name="s")

    @functools.partial(
        pl.kernel, mesh=mesh,
        out_type=jax.ShapeDtypeStruct((SC_WORKERS, n_out), jnp.int32),
        scratch_types=[pltpu.VMEM((n_out,), jnp.int32),
                       pltpu.VMEM((per_worker,), jnp.int32),
                       pltpu.SemaphoreType.DMA],
        compiler_params=pltpu.CompilerParams(use_tc_tiling_on_sc=False, needs_layout_passes=False),
        name="invmap_sc")
    def scatter(dest_hbm, out_hbm, loc_v, idx_v, sem):
        worker = lax.axis_index("s") * 2 + lax.axis_index("c")
        first = worker * per_worker
        k = first // total
        lane = lax.iota(jnp.int32, SC_LANES)
        load = pltpu.make_async_copy(dest_hbm.at[pl.ds(first, per_worker)], idx_v, sem)
        load.start()
        empty = jnp.full((SC_LANES,), -1, jnp.int32)
        unroll = 8

        @pl.loop(0, n_out // (SC_LANES * unroll))
        def _(i):
            for u in range(unroll):
                loc_v[pl.ds((i * unroll + u) * SC_LANES, SC_LANES)] = empty
        load.wait()

        @pl.loop(0, per_worker // SC_LANES)
        def _(j):
            idx = idx_v[pl.ds(j * SC_LANES, SC_LANES)]
            tok = first - k * total + j * SC_LANES + lane
            plsc.store_scatter(loc_v, [idx], tok * TOP_K + k)
        pltpu.sync_copy(loc_v, out_hbm.at[worker])

    assert n_out % (SC_LANES * 8) == 0
    return jnp.max(scatter(dest), axis=0)


def _moe_kernel(blk_e_ref, first_ref, wslot_ref, nxt_e_ref, n_used_ref, inv_ref,
                x1_hbm, wgu_hbm, bgu_ref, wd_hbm, bd_ref,
                ytok_hbm,
                xb0, xb1, xb2, yb0, yb1, yb2, wgu_f, wd_f, wgu_s, wd_s, bgu_s, bd_s,
                gsem, ssem, wsem, *, n_slots):
    n_used = n_used_ref[0]
    n_blk = blk_e_ref.shape[0]
    xbufs = (xb0, xb1, xb2)
    ybufs = (yb0, yb1, yb2)

    def gather_copy(v, s, r):
        return pltpu.make_async_copy(
            x1_hbm.at[v >> 2], xbufs[s].at[pl.ds(r * 8, 8), :], gsem.at[s])

    def scatter_copy(v, s, r):
        return pltpu.make_async_copy(
            ybufs[s].at[pl.ds(r * 8, 8), :], ytok_hbm.at[v], ssem.at[s])

    def weight_copies(e, ws):
        return (pltpu.make_async_copy(wgu_hbm.at[e], wgu_f.at[ws], wsem.at[ws]),
                pltpu.make_async_copy(wd_hbm.at[e], wd_f.at[ws], wsem.at[ws]))

    def wait_rows(copy_fn, s):
        for r in range(MOE_ROWS):
            copy_fn(0, s, r).wait()

    for cp in weight_copies(blk_e_ref[0], 0):
        cp.start()
    for yb in ybufs:
        yb[...] = jnp.zeros(yb.shape, _F32)

    def prime(r, c):
        for s in range(2):
            pltpu.make_async_copy(
                ybufs[s].at[pl.ds(pl.multiple_of(r * 8, 8), 8), :],
                ytok_hbm.at[n_slots + s * MOE_ROWS + r], ssem.at[s]).start()
            pltpu.make_async_copy(
                x1_hbm.at[inv_ref[(s + 1) * MOE_ROWS + r] >> 2],
                xbufs[s].at[pl.ds(pl.multiple_of(r * 8, 8), 8), :], gsem.at[s]).start()
        return c
    lax.fori_loop(0, MOE_ROWS, prime, 0)

    def run_block(b, s):
        nxt = (s + 2) % 3
        wait_rows(gather_copy, s)
        wait_rows(scatter_copy, s)
        x = jnp.concatenate(
            [xbufs[s][pl.ds(q, MOE_ROWS, stride=8), :] for q in range(8)], axis=-1)
        gbase = (jnp.minimum(b + 2, n_blk - 1) + 1) * MOE_ROWS
        sbase = b * MOE_ROWS
        for r in range(MOE_ROWS):
            gather_copy(inv_ref[gbase + r], nxt, r).start()
        for r in range(MOE_ROWS):
            scatter_copy(inv_ref[sbase + r], nxt, r).start()
        gu = jnp.dot(x.astype(_BF16), wgu_s[...], preferred_element_type=_F32) + bgu_s[...]
        gate = jnp.minimum(gu[:, :D_FF], SWIGLU_LIMIT)
        up = jnp.clip(gu[:, D_FF:], -SWIGLU_LIMIT, SWIGLU_LIMIT)
        hmid = (up + 1.0) * (gate * _sigmoid(SWIGLU_ALPHA * gate))
        y = jnp.dot(hmid.astype(_BF16), wd_s[...], preferred_element_type=_F32) + bd_s[...]
        for q in range(8):
            ybufs[s][pl.ds(q, MOE_ROWS, stride=8), :] = y[:, q * LANES:(q + 1) * LANES]

    def body(b, c):
        @pl.when(first_ref[b] == 1)
        def _():
            e = blk_e_ref[b]
            ws = wslot_ref[b]
            for cp in weight_copies(e, ws):
                cp.wait()
            nxt_e = nxt_e_ref[b]

            @pl.when(nxt_e >= 0)
            def _():
                for cp in weight_copies(nxt_e, 1 - ws):
                    cp.start()
            wgu_s[...] = wgu_f[ws].astype(_BF16)
            wd_s[...] = wd_f[ws].astype(_BF16)
            bgu_s[...] = bgu_ref[e]
            bd_s[...] = bd_ref[e]

        for s in range(3):
            pl.when(b % 3 == s)(functools.partial(run_block, b, s))
        return c

    lax.fori_loop(0, n_used + 1, body, 0)

    for d in (1, 2):
        for s in range(3):
            @pl.when((n_used + d) % 3 == s)
            def _():
                wait_rows(gather_copy, s)
                wait_rows(scatter_copy, s)


def _moe_call(blk_e, first, wslot, nxt_e, n_used, inv, x1t, w_gu, b_gu, w_down, b_down, *, n_slots):
    full = lambda shape: pl.BlockSpec(shape, lambda i, *_: (0,) * len(shape))
    grid_spec = pltpu.PrefetchScalarGridSpec(
        num_scalar_prefetch=6,
        grid=(1,),
        in_specs=[
            pl.BlockSpec(memory_space=pl.ANY),
            pl.BlockSpec(memory_space=pl.ANY),
            full(b_gu.shape),
            pl.BlockSpec(memory_space=pl.ANY),
            full(b_down.shape),
        ],
        out_specs=pl.BlockSpec(memory_space=pl.ANY),
        scratch_shapes=[pltpu.VMEM((MOE_ROWS * 8, LANES), _F32)] * 6 + [
            pltpu.VMEM((2, D_MODEL, 2 * D_FF), _F32),
            pltpu.VMEM((2, D_FF, D_MODEL), _F32),
            pltpu.VMEM((D_MODEL, 2 * D_FF), _BF16),
            pltpu.VMEM((D_FF, D_MODEL), _BF16),
            pltpu.VMEM((1, 2 * D_FF), _F32),
            pltpu.VMEM((1, D_MODEL), _F32),
            pltpu.SemaphoreType.DMA((3,)),
            pltpu.SemaphoreType.DMA((3,)),
            pltpu.SemaphoreType.DMA((2,)),
        ],
    )
    return pl.pallas_call(
        functools.partial(_moe_kernel, n_slots=n_slots),
        grid_spec=grid_spec,
        out_shape=jax.ShapeDtypeStruct((n_slots + 3 * MOE_ROWS, 8, LANES), _F32),
        compiler_params=pltpu.CompilerParams(
            dimension_semantics=("arbitrary",), vmem_limit_bytes=VMEM_LIMIT),
        name="moe",
    )(blk_e, first, wslot, nxt_e, n_used, inv, x1t, w_gu, b_gu, w_down, b_down)


def _combine_kernel(x1_ref, g_ref, y0_ref, y1_ref, y2_ref, y3_ref, g2_ref, be2_ref,
                    yp_ref, ys_ref, y_s, *, alpha, n_prompt_steps):
    i = pl.program_id(0)
    g = g_ref[...]
    rows = g.shape[0]

    def token_major(ref):
        return jnp.concatenate([ref[pl.ds(s, rows, stride=8), :] for s in range(8)], axis=-1)

    ys = [token_major(r.reshape(rows * 8, LANES)) for r in (y0_ref, y1_ref, y2_ref, y3_ref)]
    moe = g[:, 0:1] * ys[0] + g[:, 1:2] * ys[1] + g[:, 2:3] * ys[2] + g[:, 3:4] * ys[3]
    y = _layernorm(alpha * token_major(x1_ref) + moe, g2_ref[...], be2_ref[...])
    for q in range(8):
        y_s[q] = y[:, q * LANES:(q + 1) * LANES]

    def batch_rows(bb, nb):
        return jnp.concatenate(
            [y_s[q, pl.ds(bb, rows // nb, stride=nb), :] for q in range(8)], axis=-1)

    @pl.when(i < n_prompt_steps)
    def _():
        for bb in range(yp_ref.shape[0]):
            yp_ref[bb] = batch_rows(bb, yp_ref.shape[0])

    @pl.when(i >= n_prompt_steps)
    def _():
        for bb in range(ys_ref.shape[0]):
            ys_ref[bb] = batch_rows(bb, ys_ref.shape[0])


def _combine_call(x1t, grow, ytok, g2, be2, *, prompt_shape, sample_shape, alpha):
    bp, sp, _ = prompt_shape
    bs, ss, _ = sample_shape
    ts_p = COMBINE_ROWS // bp
    nb_s = COMBINE_ROWS // ss
    n_p = sp // ts_p
    nt = n_p + bs // nb_s
    plane = lambda k: pl.BlockSpec((COMBINE_ROWS, None, 8, LANES), lambda i, k=k: (i, k, 0, 0))
    vec = pl.BlockSpec((1, D_MODEL), lambda i: (0, 0))
    ytok4 = ytok.reshape(-1, TOP_K, 8, LANES)
    return pl.pallas_call(
        functools.partial(_combine_kernel, alpha=alpha, n_prompt_steps=n_p),
        grid=(nt,),
        in_specs=[pl.BlockSpec((COMBINE_ROWS * 8, LANES), lambda i: (i, 0)),
                  pl.BlockSpec((COMBINE_ROWS, LANES), lambda i: (i, 0)),
                  plane(0), plane(1), plane(2), plane(3), vec, vec],
        out_specs=(pl.BlockSpec((bp, ts_p, D_MODEL), lambda i: (0, jnp.minimum(i, n_p - 1), 0)),
                   pl.BlockSpec((nb_s, ss, D_MODEL), lambda i: (jnp.maximum(i - n_p, 0), 0, 0))),
        out_shape=(jax.ShapeDtypeStruct(prompt_shape, _F32), jax.ShapeDtypeStruct(sample_shape, _F32)),
        scratch_shapes=[pltpu.VMEM((8, COMBINE_ROWS, LANES), _F32)],
        compiler_params=pltpu.CompilerParams(
            dimension_semantics=("arbitrary",), vmem_limit_bytes=VMEM_LIMIT),
        name="combine",
    )(x1t, grow, ytok4, ytok4, ytok4, ytok4, g2, be2)


def _pack_block_diag(w):
    w = w.reshape(N_GATE_TILES, HEADS_PER_TILE, LRU_BLOCK, LRU_BLOCK)
    eye = jnp.eye(HEADS_PER_TILE, dtype=w.dtype)
    t = jnp.einsum("qhij,hg->qhigj", w, eye)
    return t.reshape(N_GATE_TILES, GATE_TILE, GATE_TILE)


def _layer(xp, xs, h_s0, ca_s0, cb_s0, p, *, alpha):
    bp, sp, _ = xp.shape
    bs, ss, _ = xs.shape
    tp, tsm = bp * sp, bs * ss
    total = tp + tsm
    row2 = lambda v: v.reshape(1, -1)

    rwt = jnp.transpose(p["router_w"])
    rwt_hi = rwt.astype(_BF16)
    rwt_lo = (rwt - rwt_hi.astype(_F32)).astype(_BF16)
    ii = jnp.arange(MIXER_ROWS)
    tri = (ii[:, None] < ii[None, :]).astype(_BF16)
    wts = (
        p["w_in"].astype(_BF16), row2(p["b_in"]), p["conv_a_w"], row2(p["conv_a_b"]),
        _pack_block_diag(p["lru_wa"]).astype(_BF16), row2(p["lru_ba"]),
        _pack_block_diag(p["lru_wx"]).astype(_BF16), row2(p["lru_bx"]), row2(p["lru_lambda"]),
        p["conv_b_w"], p["w_out"].astype(_BF16), row2(p["ln1_g"]), row2(p["ln1_b"]),
        rwt_hi, rwt_lo, jnp.broadcast_to(p["router_b"][:, None], (N_EXPERTS, LANES)), tri,
    )

    ts_p = MIXER_ROWS // bp
    zeros = lambda *s: jnp.zeros(s, _F32)
    n_assign = total * TOP_K
    n_spare = 3 * MOE_ROWS
    x1_buf = zeros((total + n_spare // TOP_K) * 8, LANES)
    outs_p = _mixer_call(xp, zeros(1, bp, D_MODEL), zeros(1, CONV_A_WIDTH - 1, bp, D_MODEL),
                         zeros(1, CONV_B_WIDTH - 1, bp, D_MODEL), zeros(N_EXPERTS, LANES), x1_buf, wts,
                         ts=ts_p, blk_off=0, alpha=alpha, batch_major_x=True)
    nb_s = MIXER_ROWS // ss
    nblk_s = bs // nb_s
    xs4 = jnp.transpose(xs.reshape(nblk_s, nb_s, ss, D_MODEL), (0, 2, 1, 3))
    h0 = h_s0.reshape(nblk_s, nb_s, D_MODEL)
    ca0 = jnp.transpose(ca_s0.reshape(nblk_s, nb_s, CONV_A_WIDTH - 1, D_MODEL), (0, 2, 1, 3))
    cb0 = jnp.transpose(cb_s0.reshape(nblk_s, nb_s, CONV_B_WIDTH - 1, D_MODEL), (0, 2, 1, 3))
    outs_s = _mixer_call(xs4, h0, ca0, cb0, outs_p[7], outs_p[0], wts,
                         ts=ss, blk_off=tp // MIXER_ROWS, alpha=alpha)
    x1t = outs_s[0]

    n_blocks = -(-n_assign // MOE_ROWS) + N_EXPERTS + 1
    n_rows = n_blocks * MOE_ROWS
    idx = jnp.concatenate([outs_p[1][:TOP_K], outs_s[1][:TOP_K]], axis=1)
    rank = jnp.concatenate([outs_p[2][:TOP_K], outs_s[2][:TOP_K]], axis=1)
    grow = jnp.concatenate([outs_p[3], outs_s[3]], axis=0)
    counts = outs_s[7][:, 0].astype(jnp.int32)

    padded = (counts + MOE_ROWS - 1) // MOE_ROWS * MOE_ROWS
    end_pad = jnp.cumsum(padded)
    start_pad = end_pad - padded
    experts = jnp.arange(N_EXPERTS, dtype=jnp.int32)
    start_of = jnp.sum(jnp.where(idx[:, :, None] == experts, start_pad, 0), axis=-1)
    dest = (start_of + rank).reshape(n_assign)
    blk_start = jnp.arange(n_blocks, dtype=jnp.int32) * MOE_ROWS
    blk_e = jnp.minimum(jnp.sum((blk_start[:, None] >= end_pad[None, :]).astype(jnp.int32), axis=1),
                        N_EXPERTS - 1)
    n_used = (end_pad[-1] // MOE_ROWS).astype(jnp.int32).reshape(1)

    blk_ids = jnp.arange(n_blocks, dtype=jnp.int32)
    changed = jnp.concatenate([jnp.ones((1,), bool), blk_e[1:] != blk_e[:-1]])
    first = changed & (blk_ids < n_used[0])
    wslot = (jnp.cumsum(first.astype(jnp.int32)) - 1) % 2
    later_first = first[None, :] & (blk_ids[None, :] > blk_ids[:, None])
    nxt_blk = jnp.min(jnp.where(later_first, blk_ids[None, :], n_blocks), axis=1)
    nxt_e = jnp.sum(jnp.where(nxt_blk[:, None] == blk_ids[None, :], blk_e[None, :], 0), axis=1)
    nxt_e = jnp.where(nxt_blk < n_blocks, nxt_e, -1)

    ext = jnp.arange(n_rows + MOE_ROWS, dtype=jnp.int32)
    row = ext - MOE_ROWS
    holds = jnp.any((row[:, None] >= start_pad[None, :]) & (row[:, None] < (start_pad + counts)[None, :]),
                    axis=1)
    spare = jnp.where(row < 0, n_assign + 2 * MOE_ROWS + ext, n_assign + (row & (2 * MOE_ROWS - 1)))
    scattered = _invmap_sc_call(dest + MOE_ROWS, total=total, n_out=n_rows + MOE_ROWS)
    inv = jnp.where(holds, scattered, spare)
    ytok = _moe_call(blk_e, first.astype(jnp.int32), wslot, nxt_e, n_used, inv, x1t.reshape(-1, 8, LANES),
                     p["w_gu"], p["b_gu"][:, None, :], p["w_down"], p["b_down"][:, None, :],
                     n_slots=n_assign)
    yp, ys = _combine_call(x1t, grow, ytok, row2(p["ln2_g"]), row2(p["ln2_b"]),
                           prompt_shape=xp.shape, sample_shape=xs.shape, alpha=alpha)

    def batch_major(v, nblk):
        return jnp.transpose(v, (0, 2, 1, 3)).reshape(nblk * v.shape[2], v.shape[1], D_MODEL)

    states_p = (outs_p[4].reshape(bp, D_MODEL), batch_major(outs_p[5], 1), batch_major(outs_p[6], 1))
    states_s = (outs_s[4].reshape(bs, D_MODEL), batch_major(outs_s[5], nblk_s), batch_major(outs_s[6], nblk_s))
    return yp, ys, states_p, states_s


def kernel(x_prompt, x_sample, state_rglru_h, state_rglru_conv, state_shortconv, w_in, b_in, conv_a_w, conv_a_b, lru_wa, lru_ba, lru_wx, lru_bx, lru_lambda, conv_b_w, w_out, ln1_g, ln1_b, router_w, router_b, w_gu, b_gu, w_down, b_down, ln2_g, ln2_b):
    depth = w_in.shape[0]
    alpha = (2.0 * depth) ** 0.25
    names = ("w_in", "b_in", "conv_a_w", "conv_a_b", "lru_wa", "lru_ba", "lru_wx", "lru_bx", "lru_lambda",
             "conv_b_w", "w_out", "ln1_g", "ln1_b", "router_w", "router_b", "w_gu", "b_gu", "w_down",
             "b_down", "ln2_g", "ln2_b")
    stacked = (w_in, b_in, conv_a_w, conv_a_b, lru_wa, lru_ba, lru_wx, lru_bx, lru_lambda, conv_b_w, w_out,
               ln1_g, ln1_b, router_w, router_b, w_gu, b_gu, w_down, b_down, ln2_g, ln2_b)
    xp, xs = x_prompt, x_sample
    hp_l, cp_l, sp_l, hs_l, cs_l, ss_l = [], [], [], [], [], []
    for l in range(depth):
        p = {n: v[l] for n, v in zip(names, stacked)}
        xp, xs, (hp, cp, sp), (hs, cs, ss) = _layer(
            xp, xs, state_rglru_h[l], state_rglru_conv[l], state_shortconv[l], p, alpha=alpha)
        hp_l.append(hp); cp_l.append(cp); sp_l.append(sp)
        hs_l.append(hs); cs_l.append(cs); ss_l.append(ss)
    return (xp, xs, jnp.stack(hp_l), jnp.stack(cp_l), jnp.stack(sp_l), jnp.stack(hs_l), jnp.stack(cs_l),
            jnp.stack(ss_l))
```

```python
import functools

import jax
import jax.numpy as jnp
from jax import lax
from jax.experimental import pallas as pl
from jax.experimental.pallas import tpu as pltpu
from jax.experimental.pallas import tpu_sc as plsc

D_MODEL = 1024
LRU_HEADS = 16
LRU_BLOCK = D_MODEL // LRU_HEADS
LRU_C = 8.0
CONV_A_WIDTH = 4
CONV_B_WIDTH = 3
N_GROUPS = 7
N_EXPERTS = 32
TOP_K = 4
D_FF = D_MODEL
SWIGLU_LIMIT = 7.0
SWIGLU_ALPHA = 1.702
LN_EPS = 1e-5

GATE_TILE = 256
HEADS_PER_TILE = GATE_TILE // LRU_BLOCK
N_GATE_TILES = D_MODEL // GATE_TILE
LANES = 128
MIXER_ROWS = 512
MOE_ROWS = 256
COMBINE_ROWS = 512
YROW = 4
VMEM_LIMIT = 58 * 1024 * 1024

_F32 = jnp.float32
_BF16 = jnp.bfloat16
_NT = (((1,), (1,)), ((), ()))


def _sigmoid(v):
    return 0.5 * jnp.tanh(0.5 * v) + 0.5


def _gelu_tanh(v):
    c = 0.7978845608028654
    return 0.5 * v * (1.0 + jnp.tanh(c * (v + 0.044715 * (v * v * v))))


def _layernorm(z, g, b):
    mu = jnp.mean(z, axis=-1, keepdims=True)
    zc = z - mu
    var = jnp.mean(zc * zc, axis=-1, keepdims=True)
    return zc * lax.rsqrt(var + LN_EPS) * g + b


def _mixer_kernel(x_ref, h0_ref, ca0_ref, cb0_ref, cnt0_ref, x1_buf_ref,
                  w_in_ref, b_in_ref, wca_ref, bca_ref, wa_ref, ba_ref, wx_ref, bx_ref, lam_ref,
                  wcb_ref, w_out_ref, g1_ref, be1_ref, rwt_hi_ref, rwt_lo_ref, rb_ref, tri_ref,
                  x1_ref, idx_ref, rank_ref, grow_ref, hl_ref, ca_ref, cb_ref, cnt_ref,
                  xa_s, u_s, a_s, b_s, h_s, hst_s, cnt_s, *maybe_xin, ts, nb, alpha, batch_major_x):
    i = pl.program_id(0)
    j = pl.program_id(1)
    rows = ts * nb
    ta = (CONV_A_WIDTH - 1) * nb
    tb = (CONV_B_WIDTH - 1) * nb

    @pl.when(j == 0)
    def _():
        hst_s[...] = h0_ref[...]
        xa_s[0:ta, :] = ca0_ref[...].reshape(ta, D_MODEL)
        u_s[0:tb, :] = cb0_ref[...].reshape(tb, D_MODEL)

    @pl.when((i == 0) & (j == 0))
    def _():
        cnt_s[...] = cnt0_ref[...]

    if batch_major_x:
        xin_s, xsem = maybe_xin
        nt = pl.num_programs(1)

        def chunk_copies(c, slot):
            return [pltpu.make_async_copy(x_ref.at[bb, pl.ds(c * ts, ts), :],
                                          xin_s.at[slot, :, bb, :], xsem.at[slot]) for bb in range(nb)]

        @pl.when(j == 0)
        def _():
            for cp in chunk_copies(0, 0):
                cp.start()

        @pl.when(j + 1 < nt)
        def _():
            for cp in chunk_copies(j + 1, (j + 1) % 2):
                cp.start()
        for cp in chunk_copies(j, j % 2):
            cp.wait()
        x = xin_s[j % 2].reshape(rows, D_MODEL)
    else:
        x = x_ref[...].reshape(rows, D_MODEL)
    xb = x.astype(_BF16)

    def proj(g):
        lo, hi = g * D_MODEL, (g + 1) * D_MODEL
        return jnp.dot(xb, w_in_ref[:, lo:hi], preferred_element_type=_F32) + b_in_ref[:, lo:hi]

    xa_s[ta:ta + rows, :] = proj(0)
    xc = bca_ref[...] + xa_s[0:rows, :] * wca_ref[0:1, :]
    for k in range(1, CONV_A_WIDTH):
        xc = xc + xa_s[k * nb:k * nb + rows, :] * wca_ref[k:k + 1, :]
    new_ta = xa_s[rows:rows + ta, :]
    xa_s[0:ta, :] = new_ta
    ca_ref[...] = new_ta.reshape(CONV_A_WIDTH - 1, nb, D_MODEL)

    xcb = xc.astype(_BF16)

    def block_diag(w_ref):
        return jnp.concatenate(
            [jnp.dot(xcb[:, q * GATE_TILE:(q + 1) * GATE_TILE], w_ref[q], preferred_element_type=_F32)
             for q in range(N_GATE_TILES)], axis=-1)

    r = _sigmoid(block_diag(wa_ref) + ba_ref[...])
    ig = _sigmoid(block_diag(wx_ref) + bx_ref[...])
    nlam = -lam_ref[...]
    softplus = jnp.maximum(nlam, 0.0) + jnp.log1p(jnp.exp(-jnp.abs(nlam)))
    log_a = (-LRU_C * softplus) * r
    a = jnp.exp(log_a)
    a_s[...] = a
    b_s[...] = jnp.sqrt(-jnp.tanh(log_a) * (a * a + 1.0)) * (ig * xc)

    h = hst_s[...]
    for t in range(ts):
        sl = slice(t * nb, (t + 1) * nb)
        h = a_s[sl, :] * h + b_s[sl, :]
        h_s[sl, :] = h
    hst_s[...] = h
    hl_ref[...] = h

    a_s[...] = h_s[...] * _gelu_tanh(proj(1))

    u_s[tb:tb + rows, :] = proj(3) * proj(4)
    uc = u_s[0:rows, :] * wcb_ref[0:1, :]
    for k in range(1, CONV_B_WIDTH):
        uc = uc + u_s[k * nb:k * nb + rows, :] * wcb_ref[k:k + 1, :]
    new_tb = u_s[rows:rows + tb, :]
    u_s[0:tb, :] = new_tb
    cb_ref[...] = new_tb.reshape(CONV_B_WIDTH - 1, nb, D_MODEL)
    y_b = proj(2) * uc

    merged = _sigmoid(proj(5)) * a_s[...] + _sigmoid(proj(6)) * y_b
    mixed = jnp.dot(merged.astype(_BF16), w_out_ref[...], preferred_element_type=_F32)
    x1 = _layernorm(alpha * x + mixed, g1_ref[...], be1_ref[...])
    for s in range(8):
        x1_ref[pl.ds(s, rows, stride=8), :] = x1[:, s * LANES:(s + 1) * LANES]

    x1_hi = x1.astype(_BF16)
    x1_lo = (x1 - x1_hi.astype(_F32)).astype(_BF16)
    logits = (lax.dot_general(rwt_hi_ref[...], x1_hi, _NT, preferred_element_type=_F32)
              + lax.dot_general(rwt_hi_ref[...], x1_lo, _NT, preferred_element_type=_F32)
              + lax.dot_general(rwt_lo_ref[...], x1_hi, _NT, preferred_element_type=_F32)
              + rb_ref[:, 0:1])
    e_iota = lax.broadcasted_iota(jnp.int32, (N_EXPERTS, rows), 0)
    work = logits
    vals, sels, idxs = [], [], []
    for _ in range(TOP_K):
        m = jnp.max(work, axis=0, keepdims=True)
        ik = jnp.min(jnp.where(work == m, e_iota, N_EXPERTS), axis=0, keepdims=True)
        sel = e_iota == ik
        work = jnp.where(sel, -jnp.inf, work)
        vals.append(m)
        sels.append(sel)
        idxs.append(ik)
    exps = [jnp.exp(v - vals[0]) for v in vals]
    denom = exps[0] + exps[1] + exps[2] + exps[3]
    gates = [ex / denom for ex in exps]

    onehot = jnp.zeros((N_EXPERTS, rows), _F32)
    for sel in sels:
        onehot = onehot + sel.astype(_F32)
    prefix = jnp.dot(onehot.astype(_BF16), tri_ref[...], preferred_element_type=_F32)
    pos = prefix + cnt_s[:, 0:1]
    ranks = [jnp.sum(jnp.where(sel, pos, 0.0), axis=0, keepdims=True) for sel in sels]
    new_cnt = cnt_s[...] + jnp.sum(onehot, axis=1, keepdims=True)
    cnt_s[...] = new_cnt
    cnt_ref[...] = new_cnt

    row8 = lax.broadcasted_iota(jnp.int32, (8, rows), 0)
    idx8 = jnp.zeros((8, rows), jnp.int32)
    rank8 = jnp.zeros((8, rows), jnp.int32)
    for k in range(TOP_K):
        idx8 = jnp.where(row8 == k, idxs[k], idx8)
        rank8 = jnp.where(row8 == k, ranks[k].astype(jnp.int32), rank8)
    idx_ref[...] = idx8
    rank_ref[...] = rank8

    row_l = lax.broadcasted_iota(jnp.int32, (LANES, rows), 0)
    g_t = jnp.zeros((LANES, rows), _F32)
    for k in range(TOP_K):
        g_t = jnp.where(row_l == k, gates[k], g_t)
    grow_ref[...] = jnp.transpose(g_t)


def _const_spec(shape):
    nd = len(shape)
    return pl.BlockSpec(shape, lambda i, j: (0,) * nd, pipeline_mode=pl.Buffered(1))


def _mixer_call(x4, h0, ca0, cb0, cnt0, x1_buf, wts, *, ts, blk_off, alpha, batch_major_x=False):
    if batch_major_x:
        (nb, seq, _), nblk = x4.shape, 1
    else:
        nblk, seq, nb, _ = x4.shape
    nt = seq // ts
    rows = ts * nb
    total = nblk * seq * nb
    f32 = lambda *s: jax.ShapeDtypeStruct(s, _F32)
    i32 = lambda *s: jax.ShapeDtypeStruct(s, jnp.int32)
    in_specs = [
        (pl.BlockSpec(memory_space=pl.ANY) if batch_major_x
         else pl.BlockSpec((None, ts, nb, D_MODEL), lambda i, j: (i, j, 0, 0))),
        pl.BlockSpec((None, nb, D_MODEL), lambda i, j: (i, 0, 0)),
        pl.BlockSpec((None, CONV_A_WIDTH - 1, nb, D_MODEL), lambda i, j: (i, 0, 0, 0)),
        pl.BlockSpec((None, CONV_B_WIDTH - 1, nb, D_MODEL), lambda i, j: (i, 0, 0, 0)),
        _const_spec(cnt0.shape),
        pl.BlockSpec(memory_space=pl.ANY),
    ] + [_const_spec(w.shape) for w in wts]
    out_shape = (
        jax.ShapeDtypeStruct(x1_buf.shape, _F32),
        i32(8, total),
        i32(8, total),
        f32(total, LANES),
        f32(nblk, nb, D_MODEL),
        f32(nblk, CONV_A_WIDTH - 1, nb, D_MODEL),
        f32(nblk, CONV_B_WIDTH - 1, nb, D_MODEL),
        f32(N_EXPERTS, LANES),
    )
    out_specs = (
        pl.BlockSpec((rows * 8, LANES), lambda i, j: (blk_off + i * nt + j, 0)),
        pl.BlockSpec((8, rows), lambda i, j: (0, i * nt + j)),
        pl.BlockSpec((8, rows), lambda i, j: (0, i * nt + j)),
        pl.BlockSpec((rows, LANES), lambda i, j: (i * nt + j, 0)),
        pl.BlockSpec((None, nb, D_MODEL), lambda i, j: (i, 0, 0)),
        pl.BlockSpec((None, CONV_A_WIDTH - 1, nb, D_MODEL), lambda i, j: (i, 0, 0, 0)),
        pl.BlockSpec((None, CONV_B_WIDTH - 1, nb, D_MODEL), lambda i, j: (i, 0, 0, 0)),
        pl.BlockSpec((N_EXPERTS, LANES), lambda i, j: (0, 0)),
    )
    scratch = [
        pltpu.VMEM((rows + (CONV_A_WIDTH - 1) * nb, D_MODEL), _F32),
        pltpu.VMEM((rows + (CONV_B_WIDTH - 1) * nb, D_MODEL), _F32),
        pltpu.VMEM((rows, D_MODEL), _F32),
        pltpu.VMEM((rows, D_MODEL), _F32),
        pltpu.VMEM((rows, D_MODEL), _F32),
        pltpu.VMEM((nb, D_MODEL), _F32),
        pltpu.VMEM((N_EXPERTS, LANES), _F32),
    ]
    if batch_major_x:
        scratch += [pltpu.VMEM((2, ts, nb, D_MODEL), _F32), pltpu.SemaphoreType.DMA((2,))]
    return pl.pallas_call(
        functools.partial(_mixer_kernel, ts=ts, nb=nb, alpha=alpha, batch_major_x=batch_major_x),
        grid=(nblk, nt),
        in_specs=in_specs,
        out_specs=out_specs,
        out_shape=out_shape,
        scratch_shapes=scratch,
        input_output_aliases={5: 0},
        compiler_params=pltpu.CompilerParams(
            dimension_semantics=("arbitrary", "arbitrary"), vmem_limit_bytes=VMEM_LIMIT),
        name="mixer",
    )(x4, h0, ca0, cb0, cnt0, x1_buf, *wts)


SC_WORKERS = 32
SC_WINDOW = 128
SC_LANES = 16


def _invmap_sc_call(dest, *, total, n_out):
    n = dest.shape[0]
    per_worker = n // SC_WORKERS
    n_windows = per_worker // SC_WINDOW
    assert per_worker * SC_WORKERS == n and n_windows * SC_WINDOW == per_worker
    assert total % per_worker == 0
    mesh = plsc.VectorSubcoreMesh(core_axis_name="c", subcore_axis_name="s")

    @functools.partial(
        pl.kernel, mesh=mesh,
        out_type=jax.ShapeDtypeStruct((SC_WORKERS, n_out), jnp.int32),
        scratch_types=[pltpu.VMEM((n_out,), jnp.int32),
                       pltpu.VMEM((per_worker,), jnp.int32),
                       pltpu.SemaphoreType.DMA],
        compiler_params=pltpu.CompilerParams(use_tc_tiling_on_sc=False, needs_layout_passes=False),
        name="invmap_sc")
    def scatter(dest_hbm, out_hbm, loc_v, idx_v, sem):
        worker = lax.axis_index("s") * 2 + lax.axis_index("c")
        first = worker * per_worker
        k = first // total
        lane = lax.iota(jnp.int32, SC_LANES)
        load = pltpu.make_async_copy(dest_hbm.at[pl.ds(first, per_worker)], idx_v, sem)
        load.start()
        empty = jnp.full((SC_LANES,), -1, jnp.int32)
        unroll = 8

        @pl.loop(0, n_out // (SC_LANES * unroll))
        def _(i):
            for u in range(unroll):
                loc_v[pl.ds((i * unroll + u) * SC_LANES, SC_LANES)] = empty
        load.wait()

        @pl.loop(0, per_worker // SC_LANES)
        def _(j):
            idx = idx_v[pl.ds(j * SC_LANES, SC_LANES)]
            tok = first - k * total + j * SC_LANES + lane
            plsc.store_scatter(loc_v, [idx], tok * TOP_K + k)
        pltpu.sync_copy(loc_v, out_hbm.at[worker])

    assert n_out % (SC_LANES * 8) == 0
    return jnp.max(scatter(dest), axis=0)


def _moe_kernel(blk_e_ref, first_ref, wslot_ref, nxt_e_ref, n_used_ref, inv_ref,
                x1_hbm, wgu_hbm, bgu_ref, wd_hbm, bd_ref,
                ytok_hbm,
                xb0, xb1, xb2, yb0, yb1, yb2, wgu_f, wd_f, wgu_s, wd_s, bgu_s, bd_s,
                gsem, ssem, wsem, *, n_slots):
    n_used = n_used_ref[0]
    n_blk = blk_e_ref.shape[0]
    xbufs = (xb0, xb1, xb2)
    ybufs = (yb0, yb1, yb2)

    def gather_copy(v, s, r):
        return pltpu.make_async_copy(
            x1_hbm.at[v >> 2], xbufs[s].at[pl.ds(r * 8, 8), :], gsem.at[s])

    def scatter_copy(v, s, r):
        return pltpu.make_async_copy(
            ybufs[s].at[pl.ds(r * YROW, YROW), :],
            ytok_hbm.at[v >> 1, pl.ds(pl.multiple_of((v & 1) * YROW, YROW), YROW), :], ssem.at[s])

    def weight_copies(e, ws):
        return (pltpu.make_async_copy(wgu_hbm.at[e], wgu_f.at[ws], wsem.at[ws]),
                pltpu.make_async_copy(wd_hbm.at[e], wd_f.at[ws], wsem.at[ws]))

    def wait_rows(copy_fn, s):
        for r in range(MOE_ROWS):
            copy_fn(0, s, r).wait()

    for cp in weight_copies(blk_e_ref[0], 0):
        cp.start()
    for yb in ybufs:
        yb[...] = jnp.zeros(yb.shape, jnp.uint32)

    def prime(r, c):
        for s in range(2):
            spare = n_slots + s * MOE_ROWS + r
            pltpu.make_async_copy(
                ybufs[s].at[pl.ds(pl.multiple_of(r * YROW, YROW), YROW), :],
                ytok_hbm.at[spare >> 1, pl.ds(pl.multiple_of((spare & 1) * YROW, YROW), YROW), :],
                ssem.at[s]).start()
            pltpu.make_async_copy(
                x1_hbm.at[inv_ref[(s + 1) * MOE_ROWS + r] >> 2],
                xbufs[s].at[pl.ds(pl.multiple_of(r * 8, 8), 8), :], gsem.at[s]).start()
        return c
    lax.fori_loop(0, MOE_ROWS, prime, 0)

    def run_block(b, s):
        nxt = (s + 2) % 3
        wait_rows(gather_copy, s)
        wait_rows(scatter_copy, s)
        x = jnp.concatenate(
            [xbufs[s][pl.ds(q, MOE_ROWS, stride=8), :] for q in range(8)], axis=-1)
        gbase = (jnp.minimum(b + 2, n_blk - 1) + 1) * MOE_ROWS
        sbase = b * MOE_ROWS
        for r in range(MOE_ROWS):
            gather_copy(inv_ref[gbase + r], nxt, r).start()
        for r in range(MOE_ROWS):
            scatter_copy(inv_ref[sbase + r], nxt, r).start()
        gu = jnp.dot(x.astype(_BF16), wgu_s[...], preferred_element_type=_F32) + bgu_s[...]
        gate = jnp.minimum(gu[:, :D_FF], SWIGLU_LIMIT)
        up = jnp.clip(gu[:, D_FF:], -SWIGLU_LIMIT, SWIGLU_LIMIT)
        hmid = (up + 1.0) * (gate * _sigmoid(SWIGLU_ALPHA * gate))
        y = jnp.dot(hmid.astype(_BF16), wd_s[...], preferred_element_type=_F32) + bd_s[...]
        bits = lax.bitcast_convert_type(y.astype(_BF16).astype(_F32), jnp.uint32)
        packed = (bits[:, :D_MODEL // 2] & jnp.uint32(0xFFFF0000)) | (bits[:, D_MODEL // 2:] >> 16)
        for q in range(YROW):
            ybufs[s][pl.ds(q, MOE_ROWS, stride=YROW), :] = packed[:, q * LANES:(q + 1) * LANES]

    def body(b, c):
        @pl.when(first_ref[b] == 1)
        def _():
            e = blk_e_ref[b]
            ws = wslot_ref[b]
            for cp in weight_copies(e, ws):
                cp.wait()
            nxt_e = nxt_e_ref[b]

            @pl.when(nxt_e >= 0)
            def _():
                for cp in weight_copies(nxt_e, 1 - ws):
                    cp.start()
            wgu_s[...] = wgu_f[ws].astype(_BF16)
            wd_s[...] = wd_f[ws].astype(_BF16)
            bgu_s[...] = bgu_ref[e]
            bd_s[...] = bd_ref[e]

        for s in range(3):
            pl.when(b % 3 == s)(functools.partial(run_block, b, s))
        return c

    lax.fori_loop(0, n_used + 1, body, 0)

    for d in (1, 2):
        for s in range(3):
            @pl.when((n_used + d) % 3 == s)
            def _():
                wait_rows(gather_copy, s)
                wait_rows(scatter_copy, s)


def _moe_call(blk_e, first, wslot, nxt_e, n_used, inv, x1t, w_gu, b_gu, w_down, b_down, *, n_slots):
    full = lambda shape: pl.BlockSpec(shape, lambda i, *_: (0,) * len(shape))
    grid_spec = pltpu.PrefetchScalarGridSpec(
        num_scalar_prefetch=6,
        grid=(1,),
        in_specs=[
            pl.BlockSpec(memory_space=pl.ANY),
            pl.BlockSpec(memory_space=pl.ANY),
            full(b_gu.shape),
            pl.BlockSpec(memory_space=pl.ANY),
            full(b_down.shape),
        ],
        out_specs=pl.BlockSpec(memory_space=pl.ANY),
        scratch_shapes=[pltpu.VMEM((MOE_ROWS * 8, LANES), _F32)] * 3
        + [pltpu.VMEM((MOE_ROWS * YROW, LANES), jnp.uint32)] * 3 + [
            pltpu.VMEM((2, D_MODEL, 2 * D_FF), _F32),
            pltpu.VMEM((2, D_FF, D_MODEL), _F32),
            pltpu.VMEM((D_MODEL, 2 * D_FF), _BF16),
            pltpu.VMEM((D_FF, D_MODEL), _BF16),
            pltpu.VMEM((1, 2 * D_FF), _F32),
            pltpu.VMEM((1, D_MODEL), _F32),
            pltpu.SemaphoreType.DMA((3,)),
            pltpu.SemaphoreType.DMA((3,)),
            pltpu.SemaphoreType.DMA((2,)),
        ],
    )
    return pl.pallas_call(
        functools.partial(_moe_kernel, n_slots=n_slots),
        grid_spec=grid_spec,
        out_shape=jax.ShapeDtypeStruct(((n_slots + 3 * MOE_ROWS) // 2, 8, LANES), jnp.uint32),
        compiler_params=pltpu.CompilerParams(
            dimension_semantics=("arbitrary",), vmem_limit_bytes=VMEM_LIMIT),
        name="moe",
    )(blk_e, first, wslot, nxt_e, n_used, inv, x1t, w_gu, b_gu, w_down, b_down)


def _combine_kernel(x1_ref, g_ref, y01_ref, y23_ref, g2_ref, be2_ref,
                    yp_ref, ys_ref, y_s, *, alpha, n_prompt_steps):
    i = pl.program_id(0)
    g = g_ref[...]
    rows = g.shape[0]

    def token_major(ref):
        return jnp.concatenate([ref[pl.ds(s, rows, stride=8), :] for s in range(8)], axis=-1)

    def unpack(ref, half):
        u = jnp.concatenate([ref[pl.ds(half * YROW + q, rows, stride=8), :] for q in range(YROW)], axis=-1)
        hi = lax.bitcast_convert_type(u & jnp.uint32(0xFFFF0000), _F32)
        lo = lax.bitcast_convert_type(u << 16, _F32)
        return jnp.concatenate([hi, lo], axis=-1)

    pairs = [r.reshape(rows * 8, LANES) for r in (y01_ref, y23_ref)]
    ys = [unpack(pairs[k // 2], k % 2) for k in range(TOP_K)]
    moe = g[:, 0:1] * ys[0] + g[:, 1:2] * ys[1] + g[:, 2:3] * ys[2] + g[:, 3:4] * ys[3]
    y = _layernorm(alpha * token_major(x1_ref) + moe, g2_ref[...], be2_ref[...])
    for q in range(8):
        y_s[q] = y[:, q * LANES:(q + 1) * LANES]

    def batch_rows(bb, nb):
        return jnp.concatenate(
            [y_s[q, pl.ds(bb, rows // nb, stride=nb), :] for q in range(8)], axis=-1)

    @pl.when(i < n_prompt_steps)
    def _():
        for bb in range(yp_ref.shape[0]):
            yp_ref[bb] = batch_rows(bb, yp_ref.shape[0])

    @pl.when(i >= n_prompt_steps)
    def _():
        for bb in range(ys_ref.shape[0]):
            ys_ref[bb] = batch_rows(bb, ys_ref.shape[0])


def _combine_call(x1t, grow, ytok, g2, be2, *, prompt_shape, sample_shape, alpha):
    bp, sp, _ = prompt_shape
    bs, ss, _ = sample_shape
    ts_p = COMBINE_ROWS // bp
    nb_s = COMBINE_ROWS // ss
    n_p = sp // ts_p
    nt = n_p + bs // nb_s
    pair = lambda h: pl.BlockSpec((COMBINE_ROWS, None, 8, LANES), lambda i, h=h: (i, h, 0, 0))
    vec = pl.BlockSpec((1, D_MODEL), lambda i: (0, 0))
    ytok4 = ytok.reshape(-1, TOP_K // 2, 8, LANES)
    return pl.pallas_call(
        functools.partial(_combine_kernel, alpha=alpha, n_prompt_steps=n_p),
        grid=(nt,),
        in_specs=[pl.BlockSpec((COMBINE_ROWS * 8, LANES), lambda i: (i, 0)),
                  pl.BlockSpec((COMBINE_ROWS, LANES), lambda i: (i, 0)),
                  pair(0), pair(1), vec, vec],
        out_specs=(pl.BlockSpec((bp, ts_p, D_MODEL), lambda i: (0, jnp.minimum(i, n_p - 1), 0)),
                   pl.BlockSpec((nb_s, ss, D_MODEL), lambda i: (jnp.maximum(i - n_p, 0), 0, 0))),
        out_shape=(jax.ShapeDtypeStruct(prompt_shape, _F32), jax.ShapeDtypeStruct(sample_shape, _F32)),
        scratch_shapes=[pltpu.VMEM((8, COMBINE_ROWS, LANES), _F32)],
        compiler_params=pltpu.CompilerParams(
            dimension_semantics=("arbitrary",), vmem_limit_bytes=VMEM_LIMIT),
        name="combine",
    )(x1t, grow, ytok4, ytok4, g2, be2)


def _pack_block_diag(w):
    w = w.reshape(N_GATE_TILES, HEADS_PER_TILE, LRU_BLOCK, LRU_BLOCK)
    eye = jnp.eye(HEADS_PER_TILE, dtype=w.dtype)
    t = jnp.einsum("qhij,hg->qhigj", w, eye)
    return t.reshape(N_GATE_TILES, GATE_TILE, GATE_TILE)


def _layer(xp, xs, h_s0, ca_s0, cb_s0, p, *, alpha):
    bp, sp, _ = xp.shape
    bs, ss, _ = xs.shape
    tp, tsm = bp * sp, bs * ss
    total = tp + tsm
    row2 = lambda v: v.reshape(1, -1)

    rwt = jnp.transpose(p["router_w"])
    rwt_hi = rwt.astype(_BF16)
    rwt_lo = (rwt - rwt_hi.astype(_F32)).astype(_BF16)
    ii = jnp.arange(MIXER_ROWS)
    tri = (ii[:, None] < ii[None, :]).astype(_BF16)
    wts = (
        p["w_in"].astype(_BF16), row2(p["b_in"]), p["conv_a_w"], row2(p["conv_a_b"]),
        _pack_block_diag(p["lru_wa"]).astype(_BF16), row2(p["lru_ba"]),
        _pack_block_diag(p["lru_wx"]).astype(_BF16), row2(p["lru_bx"]), row2(p["lru_lambda"]),
        p["conv_b_w"], p["w_out"].astype(_BF16), row2(p["ln1_g"]), row2(p["ln1_b"]),
        rwt_hi, rwt_lo, jnp.broadcast_to(p["router_b"][:, None], (N_EXPERTS, LANES)), tri,
    )

    ts_p = MIXER_ROWS // bp
    zeros = lambda *s: jnp.zeros(s, _F32)
    n_assign = total * TOP_K
    n_spare = 3 * MOE_ROWS
    x1_buf = zeros((total + n_spare // TOP_K) * 8, LANES)
    outs_p = _mixer_call(xp, zeros(1, bp, D_MODEL), zeros(1, CONV_A_WIDTH - 1, bp, D_MODEL),
                         zeros(1, CONV_B_WIDTH - 1, bp, D_MODEL), zeros(N_EXPERTS, LANES), x1_buf, wts,
                         ts=ts_p, blk_off=0, alpha=alpha, batch_major_x=True)
    nb_s = MIXER_ROWS // ss
    nblk_s = bs // nb_s
    xs4 = jnp.transpose(xs.reshape(nblk_s, nb_s, ss, D_MODEL), (0, 2, 1, 3))
    h0 = h_s0.reshape(nblk_s, nb_s, D_MODEL)
    ca0 = jnp.transpose(ca_s0.reshape(nblk_s, nb_s, CONV_A_WIDTH - 1, D_MODEL), (0, 2, 1, 3))
    cb0 = jnp.transpose(cb_s0.reshape(nblk_s, nb_s, CONV_B_WIDTH - 1, D_MODEL), (0, 2, 1, 3))
    outs_s = _mixer_call(xs4, h0, ca0, cb0, outs_p[7], outs_p[0], wts,
                         ts=ss, blk_off=tp // MIXER_ROWS, alpha=alpha)
    x1t = outs_s[0]

    n_blocks = -(-n_assign // MOE_ROWS) + N_EXPERTS + 1
    n_rows = n_blocks * MOE_ROWS
    idx = jnp.concatenate([outs_p[1][:TOP_K], outs_s[1][:TOP_K]], axis=1)
    rank = jnp.concatenate([outs_p[2][:TOP_K], outs_s[2][:TOP_K]], axis=1)
    grow = jnp.concatenate([outs_p[3], outs_s[3]], axis=0)
    counts = outs_s[7][:, 0].astype(jnp.int32)

    padded = (counts + MOE_ROWS - 1) // MOE_ROWS * MOE_ROWS
    end_pad = jnp.cumsum(padded)
    start_pad = end_pad - padded
    experts = jnp.arange(N_EXPERTS, dtype=jnp.int32)
    start_of = jnp.sum(jnp.where(idx[:, :, None] == experts, start_pad, 0), axis=-1)
    dest = (start_of + rank).reshape(n_assign)
    blk_start = jnp.arange(n_blocks, dtype=jnp.int32) * MOE_ROWS
    blk_e = jnp.minimum(jnp.sum((blk_start[:, None] >= end_pad[None, :]).astype(jnp.int32), axis=1),
                        N_EXPERTS - 1)
    n_used = (end_pad[-1] // MOE_ROWS).astype(jnp.int32).reshape(1)

    blk_ids = jnp.arange(n_blocks, dtype=jnp.int32)
    changed = jnp.concatenate([jnp.ones((1,), bool), blk_e[1:] != blk_e[:-1]])
    first = changed & (blk_ids < n_used[0])
    wslot = (jnp.cumsum(first.astype(jnp.int32)) - 1) % 2
    later_first = first[None, :] & (blk_ids[None, :] > blk_ids[:, None])
    nxt_blk = jnp.min(jnp.where(later_first, blk_ids[None, :], n_blocks), axis=1)
    nxt_e = jnp.sum(jnp.where(nxt_blk[:, None] == blk_ids[None, :], blk_e[None, :], 0), axis=1)
    nxt_e = jnp.where(nxt_blk < n_blocks, nxt_e, -1)

    ext = jnp.arange(n_rows + MOE_ROWS, dtype=jnp.int32)
    row = ext - MOE_ROWS
    holds = jnp.any((row[:, None] >= start_pad[None, :]) & (row[:, None] < (start_pad + counts)[None, :]),
                    axis=1)
    spare = jnp.where(row < 0, n_assign + 2 * MOE_ROWS + ext, n_assign + (row & (2 * MOE_ROWS - 1)))
    scattered = _invmap_sc_call(dest + MOE_ROWS, total=total, n_out=n_rows + MOE_ROWS)
    inv = jnp.where(holds, scattered, spare)
    ytok = _moe_call(blk_e, first.astype(jnp.int32), wslot, nxt_e, n_used, inv, x1t.reshape(-1, 8, LANES),
                     p["w_gu"], p["b_gu"][:, None, :], p["w_down"], p["b_down"][:, None, :],
                     n_slots=n_assign)
    yp, ys = _combine_call(x1t, grow, ytok, row2(p["ln2_g"]), row2(p["ln2_b"]),
                           prompt_shape=xp.shape, sample_shape=xs.shape, alpha=alpha)

    def batch_major(v, nblk):
        return jnp.transpose(v, (0, 2, 1, 3)).reshape(nblk * v.shape[2], v.shape[1], D_MODEL)

    states_p = (outs_p[4].reshape(bp, D_MODEL), batch_major(outs_p[5], 1), batch_major(outs_p[6], 1))
    states_s = (outs_s[4].reshape(bs, D_MODEL), batch_major(outs_s[5], nblk_s), batch_major(outs_s[6], nblk_s))
    return yp, ys, states_p, states_s


def kernel(x_prompt, x_sample, state_rglru_h, state_rglru_conv, state_shortconv, w_in, b_in, conv_a_w, conv_a_b, lru_wa, lru_ba, lru_wx, lru_bx, lru_lambda, conv_b_w, w_out, ln1_g, ln1_b, router_w, router_b, w_gu, b_gu, w_down, b_down, ln2_g, ln2_b):
    depth = w_in.shape[0]
    alpha = (2.0 * depth) ** 0.25
    names = ("w_in", "b_in", "conv_a_w", "conv_a_b", "lru_wa", "lru_ba", "lru_wx", "lru_bx", "lru_lambda",
             "conv_b_w", "w_out", "ln1_g", "ln1_b", "router_w", "router_b", "w_gu", "b_gu", "w_down",
             "b_down", "ln2_g", "ln2_b")
    stacked = (w_in, b_in, conv_a_w, conv_a_b, lru_wa, lru_ba, lru_wx, lru_bx, lru_lambda, conv_b_w, w_out,
               ln1_g, ln1_b, router_w, router_b, w_gu, b_gu, w_down, b_down, ln2_g, ln2_b)
    xp, xs = x_prompt, x_sample
    hp_l, cp_l, sp_l, hs_l, cs_l, ss_l = [], [], [], [], [], []
    for l in range(depth):
        p = {n: v[l] for n, v in zip(names, stacked)}
        xp, xs, (hp, cp, sp), (hs, cs, ss) = _layer(
            xp, xs, state_rglru_h[l], state_rglru_conv[l], state_shortconv[l], p, alpha=alpha)
        hp_l.append(hp); cp_l.append(cp); sp_l.append(sp)
        hs_l.append(hs); cs_l.append(cs); ss_l.append(ss)
    return (xp, xs, jnp.stack(hp_l), jnp.stack(cp_l), jnp.stack(sp_l), jnp.stack(hs_l), jnp.stack(cs_l),
            jnp.stack(ss_l))
```

```python
import functools

import jax
import jax.numpy as jnp
from jax import lax
from jax.experimental import pallas as pl
from jax.experimental.pallas import tpu as pltpu
from jax.experimental.pallas import tpu_sc as plsc

D_MODEL = 1024
LRU_HEADS = 16
LRU_BLOCK = D_MODEL // LRU_HEADS
LRU_C = 8.0
CONV_A_WIDTH = 4
CONV_B_WIDTH = 3
N_GROUPS = 7
N_EXPERTS = 32
TOP_K = 4
D_FF = D_MODEL
SWIGLU_LIMIT = 7.0
SWIGLU_ALPHA = 1.702
LN_EPS = 1e-5

GATE_TILE = 256
HEADS_PER_TILE = GATE_TILE // LRU_BLOCK
N_GATE_TILES = D_MODEL // GATE_TILE
LANES = 128
MIXER_ROWS = 512
MOE_ROWS = 256
COMBINE_ROWS = 512
YROW = 4
VMEM_LIMIT = 58 * 1024 * 1024

_F32 = jnp.float32
_BF16 = jnp.bfloat16
_NT = (((1,), (1,)), ((), ()))


def _sigmoid(v):
    return 0.5 * jnp.tanh(0.5 * v) + 0.5


def _gelu_tanh(v):
    c = 0.7978845608028654
    return 0.5 * v * (1.0 + jnp.tanh(c * (v + 0.044715 * (v * v * v))))


def _layernorm(z, g, b):
    mu = jnp.mean(z, axis=-1, keepdims=True)
    zc = z - mu
    var = jnp.mean(zc * zc, axis=-1, keepdims=True)
    return zc * lax.rsqrt(var + LN_EPS) * g + b


def _mixer_kernel(x_ref, h0_ref, ca0_ref, cb0_ref, cnt0_ref, x1_buf_ref,
                  w_in_ref, b_in_ref, wca_ref, bca_ref, wa_ref, ba_ref, wx_ref, bx_ref, lam_ref,
                  wcb_ref, w_out_ref, g1_ref, be1_ref, rwt_hi_ref, rwt_lo_ref, rb_ref, tri_ref,
                  x1_ref, idx_ref, rank_ref, grow_ref, hl_ref, ca_ref, cb_ref, cnt_ref,
                  xa_s, u_s, a_s, b_s, h_s, hst_s, cnt_s, *maybe_xin, ts, nb, alpha, batch_major_x):
    i = pl.program_id(0)
    j = pl.program_id(1)
    rows = ts * nb
    ta = (CONV_A_WIDTH - 1) * nb
    tb = (CONV_B_WIDTH - 1) * nb

    @pl.when(j == 0)
    def _():
        hst_s[...] = h0_ref[...]
        xa_s[0:ta, :] = ca0_ref[...].reshape(ta, D_MODEL)
        u_s[0:tb, :] = cb0_ref[...].reshape(tb, D_MODEL)

    @pl.when((i == 0) & (j == 0))
    def _():
        cnt_s[...] = cnt0_ref[...]

    if batch_major_x:
        xin_s, xsem, x1_o, osem = maybe_xin
        nt = pl.num_programs(1)

        def chunk_copies(c, slot):
            return [pltpu.make_async_copy(x_ref.at[bb, pl.ds(c * ts, ts), :],
                                          xin_s.at[slot, :, bb, :], xsem.at[slot]) for bb in range(nb)]

        @pl.when(j == 0)
        def _():
            for cp in chunk_copies(0, 0):
                cp.start()

        @pl.when(j + 1 < nt)
        def _():
            for cp in chunk_copies(j + 1, (j + 1) % 2):
                cp.start()
        for cp in chunk_copies(j, j % 2):
            cp.wait()
        x = xin_s[j % 2].reshape(rows, D_MODEL)

        tile_rows = rows * 8

        def out_copy(c, sl):
            return pltpu.make_async_copy(
                x1_o.at[sl], x1_ref.at[pl.ds(pl.multiple_of(c * tile_rows, tile_rows), tile_rows), :],
                osem.at[sl])

        @pl.when(j == 0)
        def _():
            x1_o[1] = jnp.zeros((tile_rows, LANES), _F32)
            lo = nt * tile_rows
            tails = []
            while lo < x1_ref.shape[0]:
                n = min(tile_rows, x1_ref.shape[0] - lo)
                tails.append(pltpu.make_async_copy(
                    x1_o.at[1, pl.ds(0, n), :], x1_ref.at[pl.ds(lo, n), :], osem.at[1]))
                lo += n
            for cp in tails:
                cp.start()
            for cp in tails:
                cp.wait()

        @pl.when(j >= 1)
        def _():
            out_copy(j - 1, (j - 1) % 2).start()

        @pl.when(j >= 2)
        def _():
            out_copy(0, j % 2).wait()
    else:
        x = x_ref[...].reshape(rows, D_MODEL)
    xb = x.astype(_BF16)

    def proj(g):
        lo, hi = g * D_MODEL, (g + 1) * D_MODEL
        return jnp.dot(xb, w_in_ref[:, lo:hi], preferred_element_type=_F32) + b_in_ref[:, lo:hi]

    xa_s[ta:ta + rows, :] = proj(0)
    xc = bca_ref[...] + xa_s[0:rows, :] * wca_ref[0:1, :]
    for k in range(1, CONV_A_WIDTH):
        xc = xc + xa_s[k * nb:k * nb + rows, :] * wca_ref[k:k + 1, :]
    new_ta = xa_s[rows:rows + ta, :]
    xa_s[0:ta, :] = new_ta
    ca_ref[...] = new_ta.reshape(CONV_A_WIDTH - 1, nb, D_MODEL)

    xcb = xc.astype(_BF16)

    def block_diag(w_ref):
        return jnp.concatenate(
            [jnp.dot(xcb[:, q * GATE_TILE:(q + 1) * GATE_TILE], w_ref[q], preferred_element_type=_F32)
             for q in range(N_GATE_TILES)], axis=-1)

    r = _sigmoid(block_diag(wa_ref) + ba_ref[...])
    ig = _sigmoid(block_diag(wx_ref) + bx_ref[...])
    nlam = -lam_ref[...]
    softplus = jnp.maximum(nlam, 0.0) + jnp.log1p(jnp.exp(-jnp.abs(nlam)))
    log_a = (-LRU_C * softplus) * r
    a = jnp.exp(log_a)
    a_s[...] = a
    b_s[...] = jnp.sqrt(-jnp.tanh(log_a) * (a * a + 1.0)) * (ig * xc)

    h = hst_s[...]
    for t in range(ts):
        sl = slice(t * nb, (t + 1) * nb)
        h = a_s[sl, :] * h + b_s[sl, :]
        h_s[sl, :] = h
    hst_s[...] = h
    hl_ref[...] = h

    a_s[...] = h_s[...] * _gelu_tanh(proj(1))

    u_s[tb:tb + rows, :] = proj(3) * proj(4)
    uc = u_s[0:rows, :] * wcb_ref[0:1, :]
    for k in range(1, CONV_B_WIDTH):
        uc = uc + u_s[k * nb:k * nb + rows, :] * wcb_ref[k:k + 1, :]
    new_tb = u_s[rows:rows + tb, :]
    u_s[0:tb, :] = new_tb
    cb_ref[...] = new_tb.reshape(CONV_B_WIDTH - 1, nb, D_MODEL)
    y_b = proj(2) * uc

    merged = _sigmoid(proj(5)) * a_s[...] + _sigmoid(proj(6)) * y_b
    mixed = jnp.dot(merged.astype(_BF16), w_out_ref[...], preferred_element_type=_F32)
    x1 = _layernorm(alpha * x + mixed, g1_ref[...], be1_ref[...])
    x1_dst = x1_o.at[j % 2] if batch_major_x else x1_ref
    for s in range(8):
        x1_dst[pl.ds(s, rows, stride=8), :] = x1[:, s * LANES:(s + 1) * LANES]

    x1_hi = x1.astype(_BF16)
    x1_lo = (x1 - x1_hi.astype(_F32)).astype(_BF16)
    logits = (lax.dot_general(rwt_hi_ref[...], x1_hi, _NT, preferred_element_type=_F32)
              + lax.dot_general(rwt_hi_ref[...], x1_lo, _NT, preferred_element_type=_F32)
              + lax.dot_general(rwt_lo_ref[...], x1_hi, _NT, preferred_element_type=_F32)
              + rb_ref[:, 0:1])
    e_iota = lax.broadcasted_iota(jnp.int32, (N_EXPERTS, rows), 0)
    work = logits
    vals, sels, idxs = [], [], []
    for _ in range(TOP_K):
        m = jnp.max(work, axis=0, keepdims=True)
        ik = jnp.min(jnp.where(work == m, e_iota, N_EXPERTS), axis=0, keepdims=True)
        sel = e_iota == ik
        work = jnp.where(sel, -jnp.inf, work)
        vals.append(m)
        sels.append(sel)
        idxs.append(ik)
    exps = [jnp.exp(v - vals[0]) for v in vals]
    denom = exps[0] + exps[1] + exps[2] + exps[3]
    gates = [ex / denom for ex in exps]

    onehot = jnp.zeros((N_EXPERTS, rows), _F32)
    for sel in sels:
        onehot = onehot + sel.astype(_F32)
    prefix = jnp.dot(onehot.astype(_BF16), tri_ref[...], preferred_element_type=_F32)
    pos = prefix + cnt_s[:, 0:1]
    ranks = [jnp.sum(jnp.where(sel, pos, 0.0), axis=0, keepdims=True) for sel in sels]
    new_cnt = cnt_s[...] + jnp.sum(onehot, axis=1, keepdims=True)
    cnt_s[...] = new_cnt
    cnt_ref[...] = new_cnt

    row8 = lax.broadcasted_iota(jnp.int32, (8, rows), 0)
    idx8 = jnp.zeros((8, rows), jnp.int32)
    rank8 = jnp.zeros((8, rows), jnp.int32)
    for k in range(TOP_K):
        idx8 = jnp.where(row8 == k, idxs[k], idx8)
        rank8 = jnp.where(row8 == k, ranks[k].astype(jnp.int32), rank8)
    idx_ref[...] = idx8
    rank_ref[...] = rank8

    row_l = lax.broadcasted_iota(jnp.int32, (LANES, rows), 0)
    g_t = jnp.zeros((LANES, rows), _F32)
    for k in range(TOP_K):
        g_t = jnp.where(row_l == k, gates[k], g_t)
    grow_ref[...] = jnp.transpose(g_t)

    if batch_major_x:
        @pl.when(j == nt - 1)
        def _():
            out_copy(j, j % 2).start()
            out_copy(0, j % 2).wait()
            if nt > 1:
                out_copy(0, (j + 1) % 2).wait()


def _const_spec(shape):
    nd = len(shape)
    return pl.BlockSpec(shape, lambda i, j: (0,) * nd, pipeline_mode=pl.Buffered(1))


def _mixer_call(x4, h0, ca0, cb0, cnt0, x1_buf, wts, *, ts, blk_off, alpha, batch_major_x=False,
                x1_rows=None):
    if batch_major_x:
        (nb, seq, _), nblk = x4.shape, 1
    else:
        nblk, seq, nb, _ = x4.shape
    nt = seq // ts
    rows = ts * nb
    total = nblk * seq * nb
    f32 = lambda *s: jax.ShapeDtypeStruct(s, _F32)
    i32 = lambda *s: jax.ShapeDtypeStruct(s, jnp.int32)
    in_specs = [
        (pl.BlockSpec(memory_space=pl.ANY) if batch_major_x
         else pl.BlockSpec((None, ts, nb, D_MODEL), lambda i, j: (i, j, 0, 0))),
        pl.BlockSpec((None, nb, D_MODEL), lambda i, j: (i, 0, 0)),
        pl.BlockSpec((None, CONV_A_WIDTH - 1, nb, D_MODEL), lambda i, j: (i, 0, 0, 0)),
        pl.BlockSpec((None, CONV_B_WIDTH - 1, nb, D_MODEL), lambda i, j: (i, 0, 0, 0)),
        _const_spec(cnt0.shape),
        pl.BlockSpec(memory_space=pl.ANY),
    ] + [_const_spec(w.shape) for w in wts]
    out_shape = (
        f32(*((x1_rows, LANES) if batch_major_x else x1_buf.shape)),
        i32(8, total),
        i32(8, total),
        f32(total, LANES),
        f32(nblk, nb, D_MODEL),
        f32(nblk, CONV_A_WIDTH - 1, nb, D_MODEL),
        f32(nblk, CONV_B_WIDTH - 1, nb, D_MODEL),
        f32(N_EXPERTS, LANES),
    )
    out_specs = (
        (pl.BlockSpec(memory_space=pl.ANY) if batch_major_x
         else pl.BlockSpec((rows * 8, LANES), lambda i, j: (blk_off + i * nt + j, 0))),
        pl.BlockSpec((8, rows), lambda i, j: (0, i * nt + j)),
        pl.BlockSpec((8, rows), lambda i, j: (0, i * nt + j)),
        pl.BlockSpec((rows, LANES), lambda i, j: (i * nt + j, 0)),
        pl.BlockSpec((None, nb, D_MODEL), lambda i, j: (i, 0, 0)),
        pl.BlockSpec((None, CONV_A_WIDTH - 1, nb, D_MODEL), lambda i, j: (i, 0, 0, 0)),
        pl.BlockSpec((None, CONV_B_WIDTH - 1, nb, D_MODEL), lambda i, j: (i, 0, 0, 0)),
        pl.BlockSpec((N_EXPERTS, LANES), lambda i, j: (0, 0)),
    )
    scratch = [
        pltpu.VMEM((rows + (CONV_A_WIDTH - 1) * nb, D_MODEL), _F32),
        pltpu.VMEM((rows + (CONV_B_WIDTH - 1) * nb, D_MODEL), _F32),
        pltpu.VMEM((rows, D_MODEL), _F32),
        pltpu.VMEM((rows, D_MODEL), _F32),
        pltpu.VMEM((rows, D_MODEL), _F32),
        pltpu.VMEM((nb, D_MODEL), _F32),
        pltpu.VMEM((N_EXPERTS, LANES), _F32),
    ]
    if batch_major_x:
        scratch += [pltpu.VMEM((2, ts, nb, D_MODEL), _F32), pltpu.SemaphoreType.DMA((2,)),
                    pltpu.VMEM((2, rows * 8, LANES), _F32), pltpu.SemaphoreType.DMA((2,))]
    return pl.pallas_call(
        functools.partial(_mixer_kernel, ts=ts, nb=nb, alpha=alpha, batch_major_x=batch_major_x),
        grid=(nblk, nt),
        in_specs=in_specs,
        out_specs=out_specs,
        out_shape=out_shape,
        scratch_shapes=scratch,
        input_output_aliases={} if batch_major_x else {5: 0},
        compiler_params=pltpu.CompilerParams(
            dimension_semantics=("arbitrary", "arbitrary"), vmem_limit_bytes=VMEM_LIMIT),
        name="mixer",
    )(x4, h0, ca0, cb0, cnt0, x1_buf, *wts)


SC_WORKERS = 32
SC_WINDOW = 128
SC_LANES = 16


def _invmap_sc_call(dest, *, total, n_out):
    n = dest.shape[0]
    per_worker = n // SC_WORKERS
    n_windows = per_worker // SC_WINDOW
    assert per_worker * SC_WORKERS == n and n_windows * SC_WINDOW == per_worker
    assert total % per_worker == 0
    mesh = plsc.VectorSubcoreMesh(core_axis_name="c", subcore_axis_name="s")

    @functools.partial(
        pl.kernel, mesh=mesh,
        out_type=jax.ShapeDtypeStruct((SC_WORKERS, n_out), jnp.int32),
        scratch_types=[pltpu.VMEM((n_out,), jnp.int32),
                       pltpu.VMEM((per_worker,), jnp.int32),
                       pltpu.SemaphoreType.DMA],
        compiler_params=pltpu.CompilerParams(use_tc_tiling_on_sc=False, needs_layout_passes=False),
        name="invmap_sc")
    def scatter(dest_hbm, out_hbm, loc_v, idx_v, sem):
        worker = lax.axis_index("s") * 2 + lax.axis_index("c")
        first = worker * per_worker
        k = first // total
        lane = lax.iota(jnp.int32, SC_LANES)
        load = pltpu.make_async_copy(dest_hbm.at[pl.ds(first, per_worker)], idx_v, sem)
        load.start()
        empty = jnp.full((SC_LANES,), -1, jnp.int32)
        unroll = 8

        @pl.loop(0, n_out // (SC_LANES * unroll))
        def _(i):
            for u in range(unroll):
                loc_v[pl.ds((i * unroll + u) * SC_LANES, SC_LANES)] = empty
        load.wait()

        @pl.loop(0, per_worker // SC_LANES)
        def _(j):
            idx = idx_v[pl.ds(j * SC_LANES, SC_LANES)]
            tok = first - k * total + j * SC_LANES + lane
            plsc.store_scatter(loc_v, [idx], tok * TOP_K + k)
        pltpu.sync_copy(loc_v, out_hbm.at[worker])

    assert n_out % (SC_LANES * 8) == 0
    return jnp.max(scatter(dest), axis=0)


def _moe_kernel(blk_e_ref, first_ref, wslot_ref, nxt_e_ref, n_used_ref, inv_ref,
                x1_hbm, wgu_hbm, bgu_ref, wd_hbm, bd_ref,
                ytok_hbm,
                xb0, xb1, xb2, yb0, yb1, yb2, wgu_f, wd_f, wgu_s, wd_s, bgu_s, bd_s,
                gsem, ssem, wsem, *, n_slots):
    n_used = n_used_ref[0]
    n_blk = blk_e_ref.shape[0]
    xbufs = (xb0, xb1, xb2)
    ybufs = (yb0, yb1, yb2)

    def gather_copy(v, s, r):
        return pltpu.make_async_copy(
            x1_hbm.at[v >> 2], xbufs[s].at[pl.ds(r * 8, 8), :], gsem.at[s])

    def scatter_copy(v, s, r):
        return pltpu.make_async_copy(
            ybufs[s].at[pl.ds(r * YROW, YROW), :],
            ytok_hbm.at[v >> 1, pl.ds(pl.multiple_of((v & 1) * YROW, YROW), YROW), :], ssem.at[s])

    def weight_copies(e, ws):
        return (pltpu.make_async_copy(wgu_hbm.at[e], wgu_f.at[ws], wsem.at[ws]),
                pltpu.make_async_copy(wd_hbm.at[e], wd_f.at[ws], wsem.at[ws]))

    def wait_rows(copy_fn, s):
        for r in range(MOE_ROWS):
            copy_fn(0, s, r).wait()

    for cp in weight_copies(blk_e_ref[0], 0):
        cp.start()
    for yb in ybufs:
        yb[...] = jnp.zeros(yb.shape, jnp.uint32)

    def prime(r, c):
        for s in range(2):
            spare = n_slots + s * MOE_ROWS + r
            pltpu.make_async_copy(
                ybufs[s].at[pl.ds(pl.multiple_of(r * YROW, YROW), YROW), :],
                ytok_hbm.at[spare >> 1, pl.ds(pl.multiple_of((spare & 1) * YROW, YROW), YROW), :],
                ssem.at[s]).start()
            pltpu.make_async_copy(
                x1_hbm.at[inv_ref[(s + 1) * MOE_ROWS + r] >> 2],
                xbufs[s].at[pl.ds(pl.multiple_of(r * 8, 8), 8), :], gsem.at[s]).start()
        return c
    lax.fori_loop(0, MOE_ROWS, prime, 0)

    def run_block(b, s):
        nxt = (s + 2) % 3
        wait_rows(gather_copy, s)
        wait_rows(scatter_copy, s)
        x = jnp.concatenate(
            [xbufs[s][pl.ds(q, MOE_ROWS, stride=8), :] for q in range(8)], axis=-1)
        gbase = (jnp.minimum(b + 2, n_blk - 1) + 1) * MOE_ROWS
        sbase = b * MOE_ROWS
        for r in range(MOE_ROWS):
            gather_copy(inv_ref[gbase + r], nxt, r).start()
        for r in range(MOE_ROWS):
            scatter_copy(inv_ref[sbase + r], nxt, r).start()
        gu = jnp.dot(x.astype(_BF16), wgu_s[...], preferred_element_type=_F32) + bgu_s[...]
        gate = jnp.minimum(gu[:, :D_FF], SWIGLU_LIMIT)
        up = jnp.clip(gu[:, D_FF:], -SWIGLU_LIMIT, SWIGLU_LIMIT)
        hmid = (up + 1.0) * (gate * _sigmoid(SWIGLU_ALPHA * gate))
        y = jnp.dot(hmid.astype(_BF16), wd_s[...], preferred_element_type=_F32) + bd_s[...]
        bits = lax.bitcast_convert_type(y.astype(_BF16).astype(_F32), jnp.uint32)
        packed = (bits[:, :D_MODEL // 2] & jnp.uint32(0xFFFF0000)) | (bits[:, D_MODEL // 2:] >> 16)
        for q in range(YROW):
            ybufs[s][pl.ds(q, MOE_ROWS, stride=YROW), :] = packed[:, q * LANES:(q + 1) * LANES]

    def body(b, c):
        @pl.when(first_ref[b] == 1)
        def _():
            e = blk_e_ref[b]
            ws = wslot_ref[b]
            for cp in weight_copies(e, ws):
                cp.wait()
            nxt_e = nxt_e_ref[b]

            @pl.when(nxt_e >= 0)
            def _():
                for cp in weight_copies(nxt_e, 1 - ws):
                    cp.start()
            wgu_s[...] = wgu_f[ws].astype(_BF16)
            wd_s[...] = wd_f[ws].astype(_BF16)
            bgu_s[...] = bgu_ref[e]
            bd_s[...] = bd_ref[e]

        for s in range(3):
            pl.when(b % 3 == s)(functools.partial(run_block, b, s))
        return c

    lax.fori_loop(0, n_used + 1, body, 0)

    for d in (1, 2):
        for s in range(3):
            @pl.when((n_used + d) % 3 == s)
            def _():
                wait_rows(gather_copy, s)
                wait_rows(scatter_copy, s)


def _moe_call(blk_e, first, wslot, nxt_e, n_used, inv, x1t, w_gu, b_gu, w_down, b_down, *, n_slots):
    full = lambda shape: pl.BlockSpec(shape, lambda i, *_: (0,) * len(shape))
    grid_spec = pltpu.PrefetchScalarGridSpec(
        num_scalar_prefetch=6,
        grid=(1,),
        in_specs=[
            pl.BlockSpec(memory_space=pl.ANY),
            pl.BlockSpec(memory_space=pl.ANY),
            full(b_gu.shape),
            pl.BlockSpec(memory_space=pl.ANY),
            full(b_down.shape),
        ],
        out_specs=pl.BlockSpec(memory_space=pl.ANY),
        scratch_shapes=[pltpu.VMEM((MOE_ROWS * 8, LANES), _F32)] * 3
        + [pltpu.VMEM((MOE_ROWS * YROW, LANES), jnp.uint32)] * 3 + [
            pltpu.VMEM((2, D_MODEL, 2 * D_FF), _F32),
            pltpu.VMEM((2, D_FF, D_MODEL), _F32),
            pltpu.VMEM((D_MODEL, 2 * D_FF), _BF16),
            pltpu.VMEM((D_FF, D_MODEL), _BF16),
            pltpu.VMEM((1, 2 * D_FF), _F32),
            pltpu.VMEM((1, D_MODEL), _F32),
            pltpu.SemaphoreType.DMA((3,)),
            pltpu.SemaphoreType.DMA((3,)),
            pltpu.SemaphoreType.DMA((2,)),
        ],
    )
    return pl.pallas_call(
        functools.partial(_moe_kernel, n_slots=n_slots),
        grid_spec=grid_spec,
        out_shape=jax.ShapeDtypeStruct(((n_slots + 3 * MOE_ROWS) // 2, 8, LANES), jnp.uint32),
        compiler_params=pltpu.CompilerParams(
            dimension_semantics=("arbitrary",), vmem_limit_bytes=VMEM_LIMIT),
        name="moe",
    )(blk_e, first, wslot, nxt_e, n_used, inv, x1t, w_gu, b_gu, w_down, b_down)


def _combine_kernel(x1_ref, g_ref, y01_ref, y23_ref, g2_ref, be2_ref,
                    yp_ref, ys_ref, y_s, *, alpha, n_prompt_steps):
    i = pl.program_id(0)
    g = g_ref[...]
    rows = g.shape[0]

    def token_major(ref):
        return jnp.concatenate([ref[pl.ds(s, rows, stride=8), :] for s in range(8)], axis=-1)

    def unpack(ref, half):
        u = jnp.concatenate([ref[pl.ds(half * YROW + q, rows, stride=8), :] for q in range(YROW)], axis=-1)
        hi = lax.bitcast_convert_type(u & jnp.uint32(0xFFFF0000), _F32)
        lo = lax.bitcast_convert_type(u << 16, _F32)
        return jnp.concatenate([hi, lo], axis=-1)

    pairs = [r.reshape(rows * 8, LANES) for r in (y01_ref, y23_ref)]
    ys = [unpack(pairs[k // 2], k % 2) for k in range(TOP_K)]
    moe = g[:, 0:1] * ys[0] + g[:, 1:2] * ys[1] + g[:, 2:3] * ys[2] + g[:, 3:4] * ys[3]
    y = _layernorm(alpha * token_major(x1_ref) + moe, g2_ref[...], be2_ref[...])
    for q in range(8):
        y_s[q] = y[:, q * LANES:(q + 1) * LANES]

    def batch_rows(bb, nb):
        return jnp.concatenate(
            [y_s[q, pl.ds(bb, rows // nb, stride=nb), :] for q in range(8)], axis=-1)

    @pl.when(i < n_prompt_steps)
    def _():
        for bb in range(yp_ref.shape[0]):
            yp_ref[bb] = batch_rows(bb, yp_ref.shape[0])

    @pl.when(i >= n_prompt_steps)
    def _():
        for bb in range(ys_ref.shape[0]):
            ys_ref[bb] = batch_rows(bb, ys_ref.shape[0])


def _combine_call(x1t, grow, ytok, g2, be2, *, prompt_shape, sample_shape, alpha):
    bp, sp, _ = prompt_shape
    bs, ss, _ = sample_shape
    ts_p = COMBINE_ROWS // bp
    nb_s = COMBINE_ROWS // ss
    n_p = sp // ts_p
    nt = n_p + bs // nb_s
    pair = lambda h: pl.BlockSpec((COMBINE_ROWS, None, 8, LANES), lambda i, h=h: (i, h, 0, 0))
    vec = pl.BlockSpec((1, D_MODEL), lambda i: (0, 0))
    ytok4 = ytok.reshape(-1, TOP_K // 2, 8, LANES)
    return pl.pallas_call(
        functools.partial(_combine_kernel, alpha=alpha, n_prompt_steps=n_p),
        grid=(nt,),
        in_specs=[pl.BlockSpec((COMBINE_ROWS * 8, LANES), lambda i: (i, 0)),
                  pl.BlockSpec((COMBINE_ROWS, LANES), lambda i: (i, 0)),
                  pair(0), pair(1), vec, vec],
        out_specs=(pl.BlockSpec((bp, ts_p, D_MODEL), lambda i: (0, jnp.minimum(i, n_p - 1), 0)),
                   pl.BlockSpec((nb_s, ss, D_MODEL), lambda i: (jnp.maximum(i - n_p, 0), 0, 0))),
        out_shape=(jax.ShapeDtypeStruct(prompt_shape, _F32), jax.ShapeDtypeStruct(sample_shape, _F32)),
        scratch_shapes=[pltpu.VMEM((8, COMBINE_ROWS, LANES), _F32)],
        compiler_params=pltpu.CompilerParams(
            dimension_semantics=("arbitrary",), vmem_limit_bytes=VMEM_LIMIT),
        name="combine",
    )(x1t, grow, ytok4, ytok4, g2, be2)


def _pack_block_diag(w):
    w = w.reshape(N_GATE_TILES, HEADS_PER_TILE, LRU_BLOCK, LRU_BLOCK)
    eye = jnp.eye(HEADS_PER_TILE, dtype=w.dtype)
    t = jnp.einsum("qhij,hg->qhigj", w, eye)
    return t.reshape(N_GATE_TILES, GATE_TILE, GATE_TILE)


def _layer(xp, xs, h_s0, ca_s0, cb_s0, p, *, alpha):
    bp, sp, _ = xp.shape
    bs, ss, _ = xs.shape
    tp, tsm = bp * sp, bs * ss
    total = tp + tsm
    row2 = lambda v: v.reshape(1, -1)

    rwt = jnp.transpose(p["router_w"])
    rwt_hi = rwt.astype(_BF16)
    rwt_lo = (rwt - rwt_hi.astype(_F32)).astype(_BF16)
    ii = jnp.arange(MIXER_ROWS)
    tri = (ii[:, None] < ii[None, :]).astype(_BF16)
    wts = (
        p["w_in"].astype(_BF16), row2(p["b_in"]), p["conv_a_w"], row2(p["conv_a_b"]),
        _pack_block_diag(p["lru_wa"]).astype(_BF16), row2(p["lru_ba"]),
        _pack_block_diag(p["lru_wx"]).astype(_BF16), row2(p["lru_bx"]), row2(p["lru_lambda"]),
        p["conv_b_w"], p["w_out"].astype(_BF16), row2(p["ln1_g"]), row2(p["ln1_b"]),
        rwt_hi, rwt_lo, jnp.broadcast_to(p["router_b"][:, None], (N_EXPERTS, LANES)), tri,
    )

    ts_p = MIXER_ROWS // bp
    zeros = lambda *s: jnp.zeros(s, _F32)
    n_assign = total * TOP_K
    n_spare = 3 * MOE_ROWS
    outs_p = _mixer_call(xp, zeros(1, bp, D_MODEL), zeros(1, CONV_A_WIDTH - 1, bp, D_MODEL),
                         zeros(1, CONV_B_WIDTH - 1, bp, D_MODEL), zeros(N_EXPERTS, LANES), zeros(8, LANES), wts,
                         ts=ts_p, blk_off=0, alpha=alpha, batch_major_x=True,
                         x1_rows=(total + n_spare // TOP_K) * 8)
    nb_s = MIXER_ROWS // ss
    nblk_s = bs // nb_s
    xs4 = jnp.transpose(xs.reshape(nblk_s, nb_s, ss, D_MODEL), (0, 2, 1, 3))
    h0 = h_s0.reshape(nblk_s, nb_s, D_MODEL)
    ca0 = jnp.transpose(ca_s0.reshape(nblk_s, nb_s, CONV_A_WIDTH - 1, D_MODEL), (0, 2, 1, 3))
    cb0 = jnp.transpose(cb_s0.reshape(nblk_s, nb_s, CONV_B_WIDTH - 1, D_MODEL), (0, 2, 1, 3))
    outs_s = _mixer_call(xs4, h0, ca0, cb0, outs_p[7], outs_p[0], wts,
                         ts=ss, blk_off=tp // MIXER_ROWS, alpha=alpha)
    x1t = outs_s[0]

    n_blocks = -(-n_assign // MOE_ROWS) + N_EXPERTS + 1
    n_rows = n_blocks * MOE_ROWS
    idx = jnp.concatenate([outs_p[1][:TOP_K], outs_s[1][:TOP_K]], axis=1)
    rank = jnp.concatenate([outs_p[2][:TOP_K], outs_s[2][:TOP_K]], axis=1)
    grow = jnp.concatenate([outs_p[3], outs_s[3]], axis=0)
    counts = outs_s[7][:, 0].astype(jnp.int32)

    padded = (counts + MOE_ROWS - 1) // MOE_ROWS * MOE_ROWS
    end_pad = jnp.cumsum(padded)
    start_pad = end_pad - padded
    experts = jnp.arange(N_EXPERTS, dtype=jnp.int32)
    start_of = jnp.sum(jnp.where(idx[:, :, None] == experts, start_pad, 0), axis=-1)
    dest = (start_of + rank).reshape(n_assign)
    blk_start = jnp.arange(n_blocks, dtype=jnp.int32) * MOE_ROWS
    blk_e = jnp.minimum(jnp.sum((blk_start[:, None] >= end_pad[None, :]).astype(jnp.int32), axis=1),
                        N_EXPERTS - 1)
    n_used = (end_pad[-1] // MOE_ROWS).astype(jnp.int32).reshape(1)

    blk_ids = jnp.arange(n_blocks, dtype=jnp.int32)
    changed = jnp.concatenate([jnp.ones((1,), bool), blk_e[1:] != blk_e[:-1]])
    first = changed & (blk_ids < n_used[0])
    wslot = (jnp.cumsum(first.astype(jnp.int32)) - 1) % 2
    later_first = first[None, :] & (blk_ids[None, :] > blk_ids[:, None])
    nxt_blk = jnp.min(jnp.where(later_first, blk_ids[None, :], n_blocks), axis=1)
    nxt_e = jnp.sum(jnp.where(nxt_blk[:, None] == blk_ids[None, :], blk_e[None, :], 0), axis=1)
    nxt_e = jnp.where(nxt_blk < n_blocks, nxt_e, -1)

    ext = jnp.arange(n_rows + MOE_ROWS, dtype=jnp.int32)
    row = ext - MOE_ROWS
    holds = jnp.any((row[:, None] >= start_pad[None, :]) & (row[:, None] < (start_pad + counts)[None, :]),
                    axis=1)
    spare = jnp.where(row < 0, n_assign + 2 * MOE_ROWS + ext, n_assign + (row & (2 * MOE_ROWS - 1)))
    scattered = _invmap_sc_call(dest + MOE_ROWS, total=total, n_out=n_rows + MOE_ROWS)
    inv = jnp.where(holds, scattered, spare)
    ytok = _moe_call(blk_e, first.astype(jnp.int32), wslot, nxt_e, n_used, inv, x1t.reshape(-1, 8, LANES),
                     p["w_gu"], p["b_gu"][:, None, :], p["w_down"], p["b_down"][:, None, :],
                     n_slots=n_assign)
    yp, ys = _combine_call(x1t, grow, ytok, row2(p["ln2_g"]), row2(p["ln2_b"]),
                           prompt_shape=xp.shape, sample_shape=xs.shape, alpha=alpha)

    def batch_major(v, nblk):
        return jnp.transpose(v, (0, 2, 1, 3)).reshape(nblk * v.shape[2], v.shape[1], D_MODEL)

    states_p = (outs_p[4].reshape(bp, D_MODEL), batch_major(outs_p[5], 1), batch_major(outs_p[6], 1))
    states_s = (outs_s[4].reshape(bs, D_MODEL), batch_major(outs_s[5], nblk_s), batch_major(outs_s[6], nblk_s))
    return yp, ys, states_p, states_s


def kernel(x_prompt, x_sample, state_rglru_h, state_rglru_conv, state_shortconv, w_in, b_in, conv_a_w, conv_a_b, lru_wa, lru_ba, lru_wx, lru_bx, lru_lambda, conv_b_w, w_out, ln1_g, ln1_b, router_w, router_b, w_gu, b_gu, w_down, b_down, ln2_g, ln2_b):
    depth = w_in.shape[0]
    alpha = (2.0 * depth) ** 0.25
    names = ("w_in", "b_in", "conv_a_w", "conv_a_b", "lru_wa", "lru_ba", "lru_wx", "lru_bx", "lru_lambda",
             "conv_b_w", "w_out", "ln1_g", "ln1_b", "router_w", "router_b", "w_gu", "b_gu", "w_down",
             "b_down", "ln2_g", "ln2_b")
    stacked = (w_in, b_in, conv_a_w, conv_a_b, lru_wa, lru_ba, lru_wx, lru_bx, lru_lambda, conv_b_w, w_out,
               ln1_g, ln1_b, router_w, router_b, w_gu, b_gu, w_down, b_down, ln2_g, ln2_b)
    xp, xs = x_prompt, x_sample
    hp_l, cp_l, sp_l, hs_l, cs_l, ss_l = [], [], [], [], [], []
    for l in range(depth):
        p = {n: v[l] for n, v in zip(names, stacked)}
        xp, xs, (hp, cp, sp), (hs, cs, ss) = _layer(
            xp, xs, state_rglru_h[l], state_rglru_conv[l], state_shortconv[l], p, alpha=alpha)
        hp_l.append(hp); cp_l.append(cp); sp_l.append(sp)
        hs_l.append(hs); cs_l.append(cs); ss_l.append(ss)
    return (xp, xs, jnp.stack(hp_l), jnp.stack(cp_l), jnp.stack(sp_l), jnp.stack(hs_l), jnp.stack(cs_l),
            jnp.stack(ss_l))
```

```python
import functools

import jax
import jax.numpy as jnp
from jax import lax
from jax.experimental import pallas as pl
from jax.experimental.pallas import tpu as pltpu
from jax.experimental.pallas import tpu_sc as plsc

D_MODEL = 1024
LRU_HEADS = 16
LRU_BLOCK = D_MODEL // LRU_HEADS
LRU_C = 8.0
CONV_A_WIDTH = 4
CONV_B_WIDTH = 3
N_GROUPS = 7
N_EXPERTS = 32
TOP_K = 4
D_FF = D_MODEL
SWIGLU_LIMIT = 7.0
SWIGLU_ALPHA = 1.702
LN_EPS = 1e-5

GATE_TILE = 256
HEADS_PER_TILE = GATE_TILE // LRU_BLOCK
N_GATE_TILES = D_MODEL // GATE_TILE
LANES = 128
MIXER_ROWS = 512
MOE_ROWS = 256
COMBINE_ROWS = 512
YROW = 4
VMEM_LIMIT = 58 * 1024 * 1024

_F32 = jnp.float32
_BF16 = jnp.bfloat16
_NT = (((1,), (1,)), ((), ()))


def _sigmoid(v):
    return 0.5 * jnp.tanh(0.5 * v) + 0.5


def _gelu_tanh(v):
    c = 0.7978845608028654
    return 0.5 * v * (1.0 + jnp.tanh(v * (c + (c * 0.044715) * (v * v))))


def _layernorm(z, g, b):
    mu = jnp.mean(z, axis=-1, keepdims=True)
    zc = z - mu
    var = jnp.mean(zc * zc, axis=-1, keepdims=True)
    return zc * lax.rsqrt(var + LN_EPS) * g + b


def _mixer_kernel(x_ref, h0_ref, ca0_ref, cb0_ref, cnt0_ref, x1_buf_ref,
                  w_in_ref, b_in_ref, wca_ref, bca_ref, wa_ref, ba_ref, wx_ref, bx_ref, lam_ref,
                  wcb_ref, w_out_ref, g1_ref, be1_ref, rwt_hi_ref, rwt_lo_ref, rb_ref, tri_ref,
                  x1_ref, idx_ref, rank_ref, grow_ref, hl_ref, ca_ref, cb_ref, cnt_ref,
                  xa_s, u_s, a_s, b_s, h_s, hst_s, cnt_s, *maybe_xin, ts, nb, alpha, batch_major_x):
    i = pl.program_id(0)
    j = pl.program_id(1)
    rows = ts * nb
    ta = (CONV_A_WIDTH - 1) * nb
    tb = (CONV_B_WIDTH - 1) * nb

    @pl.when(j == 0)
    def _():
        hst_s[...] = h0_ref[...]
        xa_s[0:ta, :] = ca0_ref[...].reshape(ta, D_MODEL)
        u_s[0:tb, :] = cb0_ref[...].reshape(tb, D_MODEL)

    @pl.when((i == 0) & (j == 0))
    def _():
        cnt_s[...] = cnt0_ref[...]

    if batch_major_x:
        xin_s, xsem, x1_o, osem = maybe_xin
        nt = pl.num_programs(1)

        def chunk_copies(c, slot):
            return [pltpu.make_async_copy(x_ref.at[bb, pl.ds(c * ts, ts), :],
                                          xin_s.at[slot, :, bb, :], xsem.at[slot]) for bb in range(nb)]

        @pl.when(j == 0)
        def _():
            for cp in chunk_copies(0, 0):
                cp.start()

        @pl.when(j + 1 < nt)
        def _():
            for cp in chunk_copies(j + 1, (j + 1) % 2):
                cp.start()
        for cp in chunk_copies(j, j % 2):
            cp.wait()
        x = xin_s[j % 2].reshape(rows, D_MODEL)

        tile_rows = rows * 8

        def out_copy(c, sl):
            return pltpu.make_async_copy(
                x1_o.at[sl], x1_ref.at[pl.ds(pl.multiple_of(c * tile_rows, tile_rows), tile_rows), :],
                osem.at[sl])

        @pl.when(j == 0)
        def _():
            x1_o[1] = jnp.zeros((tile_rows, LANES), _F32)
            lo = nt * tile_rows
            tails = []
            while lo < x1_ref.shape[0]:
                n = min(tile_rows, x1_ref.shape[0] - lo)
                tails.append(pltpu.make_async_copy(
                    x1_o.at[1, pl.ds(0, n), :], x1_ref.at[pl.ds(lo, n), :], osem.at[1]))
                lo += n
            for cp in tails:
                cp.start()
            for cp in tails:
                cp.wait()

        @pl.when(j >= 1)
        def _():
            out_copy(j - 1, (j - 1) % 2).start()

        @pl.when(j >= 2)
        def _():
            out_copy(0, j % 2).wait()
    else:
        x = x_ref[...].reshape(rows, D_MODEL)
    xb = x.astype(_BF16)

    def proj(g):
        lo, hi = g * D_MODEL, (g + 1) * D_MODEL
        return jnp.dot(xb, w_in_ref[:, lo:hi], preferred_element_type=_F32) + b_in_ref[:, lo:hi]

    xa_s[ta:ta + rows, :] = proj(0)
    xc = bca_ref[...] + xa_s[0:rows, :] * wca_ref[0:1, :]
    for k in range(1, CONV_A_WIDTH):
        xc = xc + xa_s[k * nb:k * nb + rows, :] * wca_ref[k:k + 1, :]
    new_ta = xa_s[rows:rows + ta, :]
    xa_s[0:ta, :] = new_ta
    ca_ref[...] = new_ta.reshape(CONV_A_WIDTH - 1, nb, D_MODEL)

    xcb = xc.astype(_BF16)

    def block_diag(w_ref):
        return jnp.concatenate(
            [jnp.dot(xcb[:, q * GATE_TILE:(q + 1) * GATE_TILE], w_ref[q], preferred_element_type=_F32)
             for q in range(N_GATE_TILES)], axis=-1)

    r = _sigmoid(block_diag(wa_ref) + ba_ref[...])
    ig = _sigmoid(block_diag(wx_ref) + bx_ref[...])
    nlam = -lam_ref[...]
    softplus = jnp.maximum(nlam, 0.0) + jnp.log1p(jnp.exp(-jnp.abs(nlam)))
    log_a = (-LRU_C * softplus) * r
    a = jnp.exp(log_a)
    a_s[...] = a
    z = -jnp.tanh(log_a) * (a * a + 1.0)
    b_s[...] = jnp.where(z > 0.0, z * lax.rsqrt(z), 0.0) * (ig * xc)

    h = hst_s[...]
    for t in range(ts):
        sl = slice(t * nb, (t + 1) * nb)
        h = a_s[sl, :] * h + b_s[sl, :]
        h_s[sl, :] = h
    hst_s[...] = h
    hl_ref[...] = h

    a_s[...] = h_s[...] * _gelu_tanh(proj(1))

    u_s[tb:tb + rows, :] = proj(3) * proj(4)
    uc = u_s[0:rows, :] * wcb_ref[0:1, :]
    for k in range(1, CONV_B_WIDTH):
        uc = uc + u_s[k * nb:k * nb + rows, :] * wcb_ref[k:k + 1, :]
    new_tb = u_s[rows:rows + tb, :]
    u_s[0:tb, :] = new_tb
    cb_ref[...] = new_tb.reshape(CONV_B_WIDTH - 1, nb, D_MODEL)
    y_b = proj(2) * uc

    merged = _sigmoid(proj(5)) * a_s[...] + _sigmoid(proj(6)) * y_b
    mixed = jnp.dot(merged.astype(_BF16), w_out_ref[...], preferred_element_type=_F32)
    x1 = _layernorm(alpha * x + mixed, g1_ref[...], be1_ref[...])
    x1_dst = x1_o.at[j % 2] if batch_major_x else x1_ref
    for s in range(8):
        x1_dst[pl.ds(s, rows, stride=8), :] = x1[:, s * LANES:(s + 1) * LANES]

    x1_hi = x1.astype(_BF16)
    x1_lo = (x1 - x1_hi.astype(_F32)).astype(_BF16)
    logits = (lax.dot_general(rwt_hi_ref[...], x1_hi, _NT, preferred_element_type=_F32)
              + lax.dot_general(rwt_hi_ref[...], x1_lo, _NT, preferred_element_type=_F32)
              + lax.dot_general(rwt_lo_ref[...], x1_hi, _NT, preferred_element_type=_F32)
              + rb_ref[:, 0:1])
    e_iota = lax.broadcasted_iota(jnp.int32, (N_EXPERTS, rows), 0)
    work = logits
    vals, sels, idxs = [], [], []
    for _ in range(TOP_K):
        m = jnp.max(work, axis=0, keepdims=True)
        ik = jnp.min(jnp.where(work == m, e_iota, N_EXPERTS), axis=0, keepdims=True)
        sel = e_iota == ik
        work = jnp.where(sel, -jnp.inf, work)
        vals.append(m)
        sels.append(sel)
        idxs.append(ik)
    exps = [jnp.exp(v - vals[0]) for v in vals]
    denom = exps[0] + exps[1] + exps[2] + exps[3]
    gates = [ex / denom for ex in exps]

    onehot = jnp.zeros((N_EXPERTS, rows), _F32)
    for sel in sels:
        onehot = onehot + sel.astype(_F32)
    prefix = jnp.dot(onehot.astype(_BF16), tri_ref[...], preferred_element_type=_F32)
    pos = prefix + cnt_s[:, 0:1]
    ranks = [jnp.sum(jnp.where(sel, pos, 0.0), axis=0, keepdims=True) for sel in sels]
    new_cnt = cnt_s[...] + jnp.sum(onehot, axis=1, keepdims=True)
    cnt_s[...] = new_cnt
    cnt_ref[...] = new_cnt

    row8 = lax.broadcasted_iota(jnp.int32, (8, rows), 0)
    idx8 = jnp.zeros((8, rows), jnp.int32)
    rank8 = jnp.zeros((8, rows), jnp.int32)
    for k in range(TOP_K):
        idx8 = jnp.where(row8 == k, idxs[k], idx8)
        rank8 = jnp.where(row8 == k, ranks[k].astype(jnp.int32), rank8)
    idx_ref[...] = idx8
    rank_ref[...] = rank8

    row_l = lax.broadcasted_iota(jnp.int32, (LANES, rows), 0)
    g_t = jnp.zeros((LANES, rows), _F32)
    for k in range(TOP_K):
        g_t = jnp.where(row_l == k, gates[k], g_t)
    grow_ref[...] = jnp.transpose(g_t)

    if batch_major_x:
        @pl.when(j == nt - 1)
        def _():
            out_copy(j, j % 2).start()
            out_copy(0, j % 2).wait()
            if nt > 1:
                out_copy(0, (j + 1) % 2).wait()


def _const_spec(shape):
    nd = len(shape)
    return pl.BlockSpec(shape, lambda i, j: (0,) * nd, pipeline_mode=pl.Buffered(1))


def _mixer_call(x4, h0, ca0, cb0, cnt0, x1_buf, wts, *, ts, blk_off, alpha, batch_major_x=False,
                x1_rows=None):
    if batch_major_x:
        (nb, seq, _), nblk = x4.shape, 1
    else:
        nblk, seq, nb, _ = x4.shape
    nt = seq // ts
    rows = ts * nb
    total = nblk * seq * nb
    f32 = lambda *s: jax.ShapeDtypeStruct(s, _F32)
    i32 = lambda *s: jax.ShapeDtypeStruct(s, jnp.int32)
    in_specs = [
        (pl.BlockSpec(memory_space=pl.ANY) if batch_major_x
         else pl.BlockSpec((None, ts, nb, D_MODEL), lambda i, j: (i, j, 0, 0))),
        pl.BlockSpec((None, nb, D_MODEL), lambda i, j: (i, 0, 0)),
        pl.BlockSpec((None, CONV_A_WIDTH - 1, nb, D_MODEL), lambda i, j: (i, 0, 0, 0)),
        pl.BlockSpec((None, CONV_B_WIDTH - 1, nb, D_MODEL), lambda i, j: (i, 0, 0, 0)),
        _const_spec(cnt0.shape),
        pl.BlockSpec(memory_space=pl.ANY),
    ] + [_const_spec(w.shape) for w in wts]
    out_shape = (
        f32(*((x1_rows, LANES) if batch_major_x else x1_buf.shape)),
        i32(8, total),
        i32(8, total),
        f32(total, LANES),
        f32(nblk, nb, D_MODEL),
        f32(nblk, CONV_A_WIDTH - 1, nb, D_MODEL),
        f32(nblk, CONV_B_WIDTH - 1, nb, D_MODEL),
        f32(N_EXPERTS, LANES),
    )
    out_specs = (
        (pl.BlockSpec(memory_space=pl.ANY) if batch_major_x
         else pl.BlockSpec((rows * 8, LANES), lambda i, j: (blk_off + i * nt + j, 0))),
        pl.BlockSpec((8, rows), lambda i, j: (0, i * nt + j)),
        pl.BlockSpec((8, rows), lambda i, j: (0, i * nt + j)),
        pl.BlockSpec((rows, LANES), lambda i, j: (i * nt + j, 0)),
        pl.BlockSpec((None, nb, D_MODEL), lambda i, j: (i, 0, 0)),
        pl.BlockSpec((None, CONV_A_WIDTH - 1, nb, D_MODEL), lambda i, j: (i, 0, 0, 0)),
        pl.BlockSpec((None, CONV_B_WIDTH - 1, nb, D_MODEL), lambda i, j: (i, 0, 0, 0)),
        pl.BlockSpec((N_EXPERTS, LANES), lambda i, j: (0, 0)),
    )
    scratch = [
        pltpu.VMEM((rows + (CONV_A_WIDTH - 1) * nb, D_MODEL), _F32),
        pltpu.VMEM((rows + (CONV_B_WIDTH - 1) * nb, D_MODEL), _F32),
        pltpu.VMEM((rows, D_MODEL), _F32),
        pltpu.VMEM((rows, D_MODEL), _F32),
        pltpu.VMEM((rows, D_MODEL), _F32),
        pltpu.VMEM((nb, D_MODEL), _F32),
        pltpu.VMEM((N_EXPERTS, LANES), _F32),
    ]
    if batch_major_x:
        scratch += [pltpu.VMEM((2, ts, nb, D_MODEL), _F32), pltpu.SemaphoreType.DMA((2,)),
                    pltpu.VMEM((2, rows * 8, LANES), _F32), pltpu.SemaphoreType.DMA((2,))]
    return pl.pallas_call(
        functools.partial(_mixer_kernel, ts=ts, nb=nb, alpha=alpha, batch_major_x=batch_major_x),
        grid=(nblk, nt),
        in_specs=in_specs,
        out_specs=out_specs,
        out_shape=out_shape,
        scratch_shapes=scratch,
        input_output_aliases={} if batch_major_x else {5: 0},
        compiler_params=pltpu.CompilerParams(
            dimension_semantics=("arbitrary", "arbitrary"), vmem_limit_bytes=VMEM_LIMIT),
        name="mixer",
    )(x4, h0, ca0, cb0, cnt0, x1_buf, *wts)


SC_WORKERS = 8
SC_WINDOW = 128
SC_LANES = 16


def _invmap_sc_call(dest, *, total, n_out):
    n = dest.shape[0]
    per_worker = n // SC_WORKERS
    n_windows = per_worker // SC_WINDOW
    assert per_worker * SC_WORKERS == n and n_windows * SC_WINDOW == per_worker
    assert total % per_worker == 0
    mesh = plsc.VectorSubcoreMesh(core_axis_name="c", subcore_axis_name="s")

    @functools.partial(
        pl.kernel, mesh=mesh,
        out_type=jax.ShapeDtypeStruct((SC_WORKERS, n_out), jnp.int32),
        scratch_types=[pltpu.VMEM((n_out,), jnp.int32),
                       pltpu.VMEM((per_worker,), jnp.int32),
                       pltpu.SemaphoreType.DMA],
        compiler_params=pltpu.CompilerParams(use_tc_tiling_on_sc=False, needs_layout_passes=False),
        name="invmap_sc")
    def scatter(dest_hbm, out_hbm, loc_v, idx_v, sem):
        worker = lax.axis_index("s") * 2 + lax.axis_index("c")

        @pl.when(worker < SC_WORKERS)
        def _():
            first = worker * per_worker
            k = first // total
            lane = lax.iota(jnp.int32, SC_LANES)
            load = pltpu.make_async_copy(dest_hbm.at[pl.ds(first, per_worker)], idx_v, sem)
            load.start()
            empty = jnp.full((SC_LANES,), -1, jnp.int32)
            unroll = 8

            @pl.loop(0, n_out // (SC_LANES * unroll))
            def _(i):
                for u in range(unroll):
                    loc_v[pl.ds((i * unroll + u) * SC_LANES, SC_LANES)] = empty
            load.wait()

            @pl.loop(0, per_worker // SC_LANES)
            def _(j):
                idx = idx_v[pl.ds(j * SC_LANES, SC_LANES)]
                tok = first - k * total + j * SC_LANES + lane
                plsc.store_scatter(loc_v, [idx], tok * TOP_K + k)
            pltpu.sync_copy(loc_v, out_hbm.at[worker])

    assert n_out % (SC_LANES * 8) == 0
    return jnp.max(scatter(dest), axis=0)


def _moe_kernel(blk_e_ref, first_ref, wslot_ref, nxt_e_ref, n_used_ref, inv_ref,
                x1_hbm, wgu_hbm, bgu_ref, wd_hbm, bd_ref,
                ytok_hbm,
                xb0, xb1, xb2, yb0, yb1, yb2, wgu_f, wd_f, wgu_s, wd_s, bgu_s, bd_s,
                gsem, ssem, wsem, *, n_slots):
    n_used = n_used_ref[0]
    n_blk = blk_e_ref.shape[0]
    xbufs = (xb0, xb1, xb2)
    ybufs = (yb0, yb1, yb2)

    def gather_copy(v, s, r):
        return pltpu.make_async_copy(
            x1_hbm.at[v >> 2], xbufs[s].at[pl.ds(r * 8, 8), :], gsem.at[s])

    def scatter_copy(v, s, r):
        return pltpu.make_async_copy(
            ybufs[s].at[pl.ds(r * YROW, YROW), :],
            ytok_hbm.at[v >> 1, pl.ds(pl.multiple_of((v & 1) * YROW, YROW), YROW), :], ssem.at[s])

    def weight_copies(e, ws):
        return (pltpu.make_async_copy(wgu_hbm.at[e], wgu_f.at[ws], wsem.at[ws]),
                pltpu.make_async_copy(wd_hbm.at[e], wd_f.at[ws], wsem.at[ws]))

    def wait_rows(copy_fn, s):
        for r in range(MOE_ROWS):
            copy_fn(0, s, r).wait()

    for cp in weight_copies(blk_e_ref[0], 0):
        cp.start()
    for yb in ybufs:
        yb[...] = jnp.zeros(yb.shape, jnp.uint32)

    def prime(r, c):
        for s in range(2):
            spare = n_slots + s * MOE_ROWS + r
            pltpu.make_async_copy(
                ybufs[s].at[pl.ds(pl.multiple_of(r * YROW, YROW), YROW), :],
                ytok_hbm.at[spare >> 1, pl.ds(pl.multiple_of((spare & 1) * YROW, YROW), YROW), :],
                ssem.at[s]).start()
            pltpu.make_async_copy(
                x1_hbm.at[inv_ref[(s + 1) * MOE_ROWS + r] >> 2],
                xbufs[s].at[pl.ds(pl.multiple_of(r * 8, 8), 8), :], gsem.at[s]).start()
        return c
    lax.fori_loop(0, MOE_ROWS, prime, 0)

    def run_block(b, s):
        nxt = (s + 2) % 3
        wait_rows(gather_copy, s)
        wait_rows(scatter_copy, s)
        x = jnp.concatenate(
            [xbufs[s][pl.ds(q, MOE_ROWS, stride=8), :] for q in range(8)], axis=-1)
        gbase = (jnp.minimum(b + 2, n_blk - 1) + 1) * MOE_ROWS
        sbase = b * MOE_ROWS
        for r in range(MOE_ROWS):
            gather_copy(inv_ref[gbase + r], nxt, r).start()
        for r in range(MOE_ROWS):
            scatter_copy(inv_ref[sbase + r], nxt, r).start()
        gu = jnp.dot(x.astype(_BF16), wgu_s[...], preferred_element_type=_F32) + bgu_s[...]
        gate = jnp.minimum(gu[:, :D_FF], SWIGLU_LIMIT)
        up = jnp.clip(gu[:, D_FF:], -SWIGLU_LIMIT, SWIGLU_LIMIT)
        hmid = (up + 1.0) * (gate * _sigmoid(SWIGLU_ALPHA * gate))
        y = jnp.dot(hmid.astype(_BF16), wd_s[...], preferred_element_type=_F32) + bd_s[...]
        bits = lax.bitcast_convert_type(y.astype(_BF16).astype(_F32), jnp.uint32)
        packed = (bits[:, :D_MODEL // 2] & jnp.uint32(0xFFFF0000)) | (bits[:, D_MODEL // 2:] >> 16)
        for q in range(YROW):
            ybufs[s][pl.ds(q, MOE_ROWS, stride=YROW), :] = packed[:, q * LANES:(q + 1) * LANES]

    def body(b, c):
        @pl.when(first_ref[b] == 1)
        def _():
            e = blk_e_ref[b]
            ws = wslot_ref[b]
            for cp in weight_copies(e, ws):
                cp.wait()
            nxt_e = nxt_e_ref[b]

            @pl.when(nxt_e >= 0)
            def _():
                for cp in weight_copies(nxt_e, 1 - ws):
                    cp.start()
            wgu_s[...] = wgu_f[ws].astype(_BF16)
            wd_s[...] = wd_f[ws].astype(_BF16)
            bgu_s[...] = bgu_ref[e]
            bd_s[...] = bd_ref[e]

        for s in range(3):
            pl.when(b % 3 == s)(functools.partial(run_block, b, s))
        return c

    lax.fori_loop(0, n_used + 1, body, 0)

    for d in (1, 2):
        for s in range(3):
            @pl.when((n_used + d) % 3 == s)
            def _():
                wait_rows(gather_copy, s)
                wait_rows(scatter_copy, s)


def _moe_call(blk_e, first, wslot, nxt_e, n_used, inv, x1t, w_gu, b_gu, w_down, b_down, *, n_slots):
    full = lambda shape: pl.BlockSpec(shape, lambda i, *_: (0,) * len(shape))
    grid_spec = pltpu.PrefetchScalarGridSpec(
        num_scalar_prefetch=6,
        grid=(1,),
        in_specs=[
            pl.BlockSpec(memory_space=pl.ANY),
            pl.BlockSpec(memory_space=pl.ANY),
            full(b_gu.shape),
            pl.BlockSpec(memory_space=pl.ANY),
            full(b_down.shape),
        ],
        out_specs=pl.BlockSpec(memory_space=pl.ANY),
        scratch_shapes=[pltpu.VMEM((MOE_ROWS * 8, LANES), _F32)] * 3
        + [pltpu.VMEM((MOE_ROWS * YROW, LANES), jnp.uint32)] * 3 + [
            pltpu.VMEM((2, D_MODEL, 2 * D_FF), _F32),
            pltpu.VMEM((2, D_FF, D_MODEL), _F32),
            pltpu.VMEM((D_MODEL, 2 * D_FF), _BF16),
            pltpu.VMEM((D_FF, D_MODEL), _BF16),
            pltpu.VMEM((1, 2 * D_FF), _F32),
            pltpu.VMEM((1, D_MODEL), _F32),
            pltpu.SemaphoreType.DMA((3,)),
            pltpu.SemaphoreType.DMA((3,)),
            pltpu.SemaphoreType.DMA((2,)),
        ],
    )
    return pl.pallas_call(
        functools.partial(_moe_kernel, n_slots=n_slots),
        grid_spec=grid_spec,
        out_shape=jax.ShapeDtypeStruct(((n_slots + 3 * MOE_ROWS) // 2, 8, LANES), jnp.uint32),
        compiler_params=pltpu.CompilerParams(
            dimension_semantics=("arbitrary",), vmem_limit_bytes=VMEM_LIMIT),
        name="moe",
    )(blk_e, first, wslot, nxt_e, n_used, inv, x1t, w_gu, b_gu, w_down, b_down)


def _combine_kernel(x1_ref, g_ref, y01_ref, y23_ref, g2_ref, be2_ref,
                    yp_ref, ys_ref, y_s, *, alpha, n_prompt_steps):
    i = pl.program_id(0)
    g = g_ref[...]
    rows = g.shape[0]

    def token_major(ref):
        return jnp.concatenate([ref[pl.ds(s, rows, stride=8), :] for s in range(8)], axis=-1)

    def unpack(ref, half):
        u = jnp.concatenate([ref[pl.ds(half * YROW + q, rows, stride=8), :] for q in range(YROW)], axis=-1)
        hi = lax.bitcast_convert_type(u & jnp.uint32(0xFFFF0000), _F32)
        lo = lax.bitcast_convert_type(u << 16, _F32)
        return jnp.concatenate([hi, lo], axis=-1)

    pairs = [r.reshape(rows * 8, LANES) for r in (y01_ref, y23_ref)]
    ys = [unpack(pairs[k // 2], k % 2) for k in range(TOP_K)]
    moe = g[:, 0:1] * ys[0] + g[:, 1:2] * ys[1] + g[:, 2:3] * ys[2] + g[:, 3:4] * ys[3]
    y = _layernorm(alpha * token_major(x1_ref) + moe, g2_ref[...], be2_ref[...])
    for q in range(8):
        y_s[q] = y[:, q * LANES:(q + 1) * LANES]

    def batch_rows(bb, nb):
        return jnp.concatenate(
            [y_s[q, pl.ds(bb, rows // nb, stride=nb), :] for q in range(8)], axis=-1)

    @pl.when(i < n_prompt_steps)
    def _():
        for bb in range(yp_ref.shape[0]):
            yp_ref[bb] = batch_rows(bb, yp_ref.shape[0])

    @pl.when(i >= n_prompt_steps)
    def _():
        for bb in range(ys_ref.shape[0]):
            ys_ref[bb] = batch_rows(bb, ys_ref.shape[0])


def _combine_call(x1t, grow, ytok, g2, be2, *, prompt_shape, sample_shape, alpha):
    bp, sp, _ = prompt_shape
    bs, ss, _ = sample_shape
    ts_p = COMBINE_ROWS // bp
    nb_s = COMBINE_ROWS // ss
    n_p = sp // ts_p
    nt = n_p + bs // nb_s
    pair = lambda h: pl.BlockSpec((COMBINE_ROWS, None, 8, LANES), lambda i, h=h: (i, h, 0, 0))
    vec = pl.BlockSpec((1, D_MODEL), lambda i: (0, 0))
    ytok4 = ytok.reshape(-1, TOP_K // 2, 8, LANES)
    return pl.pallas_call(
        functools.partial(_combine_kernel, alpha=alpha, n_prompt_steps=n_p),
        grid=(nt,),
        in_specs=[pl.BlockSpec((COMBINE_ROWS * 8, LANES), lambda i: (i, 0)),
                  pl.BlockSpec((COMBINE_ROWS, LANES), lambda i: (i, 0)),
                  pair(0), pair(1), vec, vec],
        out_specs=(pl.BlockSpec((bp, ts_p, D_MODEL), lambda i: (0, jnp.minimum(i, n_p - 1), 0)),
                   pl.BlockSpec((nb_s, ss, D_MODEL), lambda i: (jnp.maximum(i - n_p, 0), 0, 0))),
        out_shape=(jax.ShapeDtypeStruct(prompt_shape, _F32), jax.ShapeDtypeStruct(sample_shape, _F32)),
        scratch_shapes=[pltpu.VMEM((8, COMBINE_ROWS, LANES), _F32)],
        compiler_params=pltpu.CompilerParams(
            dimension_semantics=("arbitrary",), vmem_limit_bytes=VMEM_LIMIT),
        name="combine",
    )(x1t, grow, ytok4, ytok4, g2, be2)


def _pack_block_diag(w):
    w = w.reshape(N_GATE_TILES, HEADS_PER_TILE, LRU_BLOCK, LRU_BLOCK)
    eye = jnp.eye(HEADS_PER_TILE, dtype=w.dtype)
    t = jnp.einsum("qhij,hg->qhigj", w, eye)
    return t.reshape(N_GATE_TILES, GATE_TILE, GATE_TILE)


def _layer(xp, xs, h_s0, ca_s0, cb_s0, p, *, alpha):
    bp, sp, _ = xp.shape
    bs, ss, _ = xs.shape
    tp, tsm = bp * sp, bs * ss
    total = tp + tsm
    row2 = lambda v: v.reshape(1, -1)

    rwt = jnp.transpose(p["router_w"])
    rwt_hi = rwt.astype(_BF16)
    rwt_lo = (rwt - rwt_hi.astype(_F32)).astype(_BF16)
    ii = jnp.arange(MIXER_ROWS)
    tri = (ii[:, None] < ii[None, :]).astype(_BF16)
    wts = (
        p["w_in"].astype(_BF16), row2(p["b_in"]), p["conv_a_w"], row2(p["conv_a_b"]),
        _pack_block_diag(p["lru_wa"]).astype(_BF16), row2(p["lru_ba"]),
        _pack_block_diag(p["lru_wx"]).astype(_BF16), row2(p["lru_bx"]), row2(p["lru_lambda"]),
        p["conv_b_w"], p["w_out"].astype(_BF16), row2(p["ln1_g"]), row2(p["ln1_b"]),
        rwt_hi, rwt_lo, jnp.broadcast_to(p["router_b"][:, None], (N_EXPERTS, LANES)), tri,
    )

    ts_p = MIXER_ROWS // bp
    zeros = lambda *s: jnp.zeros(s, _F32)
    n_assign = total * TOP_K
    n_spare = 3 * MOE_ROWS
    outs_p = _mixer_call(xp, zeros(1, bp, D_MODEL), zeros(1, CONV_A_WIDTH - 1, bp, D_MODEL),
                         zeros(1, CONV_B_WIDTH - 1, bp, D_MODEL), zeros(N_EXPERTS, LANES), zeros(8, LANES), wts,
                         ts=ts_p, blk_off=0, alpha=alpha, batch_major_x=True,
                         x1_rows=(total + n_spare // TOP_K) * 8)
    nb_s = MIXER_ROWS // ss
    nblk_s = bs // nb_s
    xs4 = jnp.transpose(xs.reshape(nblk_s, nb_s, ss, D_MODEL), (0, 2, 1, 3))
    h0 = h_s0.reshape(nblk_s, nb_s, D_MODEL)
    ca0 = jnp.transpose(ca_s0.reshape(nblk_s, nb_s, CONV_A_WIDTH - 1, D_MODEL), (0, 2, 1, 3))
    cb0 = jnp.transpose(cb_s0.reshape(nblk_s, nb_s, CONV_B_WIDTH - 1, D_MODEL), (0, 2, 1, 3))
    outs_s = _mixer_call(xs4, h0, ca0, cb0, outs_p[7], outs_p[0], wts,
                         ts=ss, blk_off=tp // MIXER_ROWS, alpha=alpha)
    x1t = outs_s[0]

    n_blocks = -(-n_assign // MOE_ROWS) + N_EXPERTS + 1
    n_rows = n_blocks * MOE_ROWS
    idx = jnp.concatenate([outs_p[1][:TOP_K], outs_s[1][:TOP_K]], axis=1)
    rank = jnp.concatenate([outs_p[2][:TOP_K], outs_s[2][:TOP_K]], axis=1)
    grow = jnp.concatenate([outs_p[3], outs_s[3]], axis=0)
    counts = outs_s[7][:, 0].astype(jnp.int32)

    padded = (counts + MOE_ROWS - 1) // MOE_ROWS * MOE_ROWS
    end_pad = jnp.cumsum(padded)
    start_pad = end_pad - padded
    experts = jnp.arange(N_EXPERTS, dtype=jnp.int32)
    start_of = jnp.sum(jnp.where(idx[:, :, None] == experts, start_pad, 0), axis=-1)
    dest = (start_of + rank).reshape(n_assign)
    blk_start = jnp.arange(n_blocks, dtype=jnp.int32) * MOE_ROWS
    blk_e = jnp.minimum(jnp.sum((blk_start[:, None] >= end_pad[None, :]).astype(jnp.int32), axis=1),
                        N_EXPERTS - 1)
    n_used = (end_pad[-1] // MOE_ROWS).astype(jnp.int32).reshape(1)

    blk_ids = jnp.arange(n_blocks, dtype=jnp.int32)
    changed = jnp.concatenate([jnp.ones((1,), bool), blk_e[1:] != blk_e[:-1]])
    first = changed & (blk_ids < n_used[0])
    wslot = (jnp.cumsum(first.astype(jnp.int32)) - 1) % 2
    later_first = first[None, :] & (blk_ids[None, :] > blk_ids[:, None])
    nxt_blk = jnp.min(jnp.where(later_first, blk_ids[None, :], n_blocks), axis=1)
    nxt_e = jnp.sum(jnp.where(nxt_blk[:, None] == blk_ids[None, :], blk_e[None, :], 0), axis=1)
    nxt_e = jnp.where(nxt_blk < n_blocks, nxt_e, -1)

    ext = jnp.arange(n_rows + MOE_ROWS, dtype=jnp.int32)
    row = ext - MOE_ROWS
    holds = jnp.any((row[:, None] >= start_pad[None, :]) & (row[:, None] < (start_pad + counts)[None, :]),
                    axis=1)
    spare = jnp.where(row < 0, n_assign + 2 * MOE_ROWS + ext, n_assign + (row & (2 * MOE_ROWS - 1)))
    scattered = _invmap_sc_call(dest + MOE_ROWS, total=total, n_out=n_rows + MOE_ROWS)
    inv = jnp.where(holds, scattered, spare)
    ytok = _moe_call(blk_e, first.astype(jnp.int32), wslot, nxt_e, n_used, inv, x1t.reshape(-1, 8, LANES),
                     p["w_gu"], p["b_gu"][:, None, :], p["w_down"], p["b_down"][:, None, :],
                     n_slots=n_assign)
    yp, ys = _combine_call(x1t, grow, ytok, row2(p["ln2_g"]), row2(p["ln2_b"]),
                           prompt_shape=xp.shape, sample_shape=xs.shape, alpha=alpha)

    def batch_major(v, nblk):
        return jnp.transpose(v, (0, 2, 1, 3)).reshape(nblk * v.shape[2], v.shape[1], D_MODEL)

    states_p = (outs_p[4].reshape(bp, D_MODEL), batch_major(outs_p[5], 1), batch_major(outs_p[6], 1))
    states_s = (outs_s[4].reshape(bs, D_MODEL), batch_major(outs_s[5], nblk_s), batch_major(outs_s[6], nblk_s))
    return yp, ys, states_p, states_s


def kernel(x_prompt, x_sample, state_rglru_h, state_rglru_conv, state_shortconv, w_in, b_in, conv_a_w, conv_a_b, lru_wa, lru_ba, lru_wx, lru_bx, lru_lambda, conv_b_w, w_out, ln1_g, ln1_b, router_w, router_b, w_gu, b_gu, w_down, b_down, ln2_g, ln2_b):
    depth = w_in.shape[0]
    alpha = (2.0 * depth) ** 0.25
    names = ("w_in", "b_in", "conv_a_w", "conv_a_b", "lru_wa", "lru_ba", "lru_wx", "lru_bx", "lru_lambda",
             "conv_b_w", "w_out", "ln1_g", "ln1_b", "router_w", "router_b", "w_gu", "b_gu", "w_down",
             "b_down", "ln2_g", "ln2_b")
    stacked = (w_in, b_in, conv_a_w, conv_a_b, lru_wa, lru_ba, lru_wx, lru_bx, lru_lambda, conv_b_w, w_out,
               ln1_g, ln1_b, router_w, router_b, w_gu, b_gu, w_down, b_down, ln2_g, ln2_b)
    xp, xs = x_prompt, x_sample
    hp_l, cp_l, sp_l, hs_l, cs_l, ss_l = [], [], [], [], [], []
    for l in range(depth):
        p = {n: v[l] for n, v in zip(names, stacked)}
        xp, xs, (hp, cp, sp), (hs, cs, ss) = _layer(
            xp, xs, state_rglru_h[l], state_rglru_conv[l], state_shortconv[l], p, alpha=alpha)
        hp_l.append(hp); cp_l.append(cp); sp_l.append(sp)
        hs_l.append(hs); cs_l.append(cs); ss_l.append(ss)
    return (xp, xs, jnp.stack(hp_l), jnp.stack(cp_l), jnp.stack(sp_l), jnp.stack(hs_l), jnp.stack(cs_l),
            jnp.stack(ss_l))
```

```python
import functools

import jax
import jax.numpy as jnp
from jax import lax
from jax.experimental import pallas as pl
from jax.experimental.pallas import tpu as pltpu
from jax.experimental.pallas import tpu_sc as plsc

D_MODEL = 1024
LRU_HEADS = 16
LRU_BLOCK = D_MODEL // LRU_HEADS
LRU_C = 8.0
CONV_A_WIDTH = 4
CONV_B_WIDTH = 3
N_GROUPS = 7
N_EXPERTS = 32
TOP_K = 4
D_FF = D_MODEL
SWIGLU_LIMIT = 7.0
SWIGLU_ALPHA = 1.702
LN_EPS = 1e-5

GATE_TILE = 256
HEADS_PER_TILE = GATE_TILE // LRU_BLOCK
N_GATE_TILES = D_MODEL // GATE_TILE
LANES = 128
MIXER_ROWS = 512
MOE_ROWS = 256
COMBINE_ROWS = 512
YROW = 4
VMEM_LIMIT = 58 * 1024 * 1024

_F32 = jnp.float32
_BF16 = jnp.bfloat16
_NT = (((1,), (1,)), ((), ()))


def _sigmoid(v):
    return 0.5 * jnp.tanh(0.5 * v) + 0.5


def _gelu_tanh(v):
    c = 0.7978845608028654
    return 0.5 * v * (1.0 + jnp.tanh(v * (c + (c * 0.044715) * (v * v))))


def _layernorm(z, g, b):
    mu = jnp.mean(z, axis=-1, keepdims=True)
    zc = z - mu
    var = jnp.mean(zc * zc, axis=-1, keepdims=True)
    return zc * lax.rsqrt(var + LN_EPS) * g + b


def _mixer_kernel(x_ref, h0_ref, ca0_ref, cb0_ref, cnt0_ref, x1_buf_ref,
                  w_in_ref, b_in_ref, wca_ref, bca_ref, wa_ref, ba_ref, wx_ref, bx_ref, lam_ref,
                  wcb_ref, w_out_ref, g1_ref, be1_ref, rwt_hi_ref, rwt_lo_ref, rb_ref, tri_ref,
                  x1_ref, idx_ref, rank_ref, grow_ref, hl_ref, ca_ref, cb_ref, cnt_ref,
                  xa_s, u_s, a_s, b_s, h_s, hst_s, cnt_s, *maybe_xin, ts, nb, alpha, batch_major_x):
    i = pl.program_id(0)
    j = pl.program_id(1)
    rows = ts * nb
    ta = (CONV_A_WIDTH - 1) * nb
    tb = (CONV_B_WIDTH - 1) * nb

    @pl.when(j == 0)
    def _():
        hst_s[...] = h0_ref[...]
        xa_s[0:ta, :] = ca0_ref[...].reshape(ta, D_MODEL)
        u_s[0:tb, :] = cb0_ref[...].reshape(tb, D_MODEL)

    @pl.when((i == 0) & (j == 0))
    def _():
        cnt_s[...] = cnt0_ref[...]

    if batch_major_x:
        xin_s, xsem, x1_o, osem = maybe_xin
        nt = pl.num_programs(1)

        def chunk_copies(c, slot):
            return [pltpu.make_async_copy(x_ref.at[bb, pl.ds(c * ts, ts), :],
                                          xin_s.at[slot, :, bb, :], xsem.at[slot]) for bb in range(nb)]

        @pl.when(j == 0)
        def _():
            for cp in chunk_copies(0, 0):
                cp.start()

        @pl.when(j + 1 < nt)
        def _():
            for cp in chunk_copies(j + 1, (j + 1) % 2):
                cp.start()
        for cp in chunk_copies(j, j % 2):
            cp.wait()
        x = xin_s[j % 2].reshape(rows, D_MODEL)

        tile_rows = rows * 8

        def out_copy(c, sl):
            return pltpu.make_async_copy(
                x1_o.at[sl], x1_ref.at[pl.ds(pl.multiple_of(c * tile_rows, tile_rows), tile_rows), :],
                osem.at[sl])

        @pl.when(j == 0)
        def _():
            x1_o[1] = jnp.zeros((tile_rows, LANES), _F32)
            lo = nt * tile_rows
            tails = []
            while lo < x1_ref.shape[0]:
                n = min(tile_rows, x1_ref.shape[0] - lo)
                tails.append(pltpu.make_async_copy(
                    x1_o.at[1, pl.ds(0, n), :], x1_ref.at[pl.ds(lo, n), :], osem.at[1]))
                lo += n
            for cp in tails:
                cp.start()
            for cp in tails:
                cp.wait()

        @pl.when(j >= 1)
        def _():
            out_copy(j - 1, (j - 1) % 2).start()

        @pl.when(j >= 2)
        def _():
            out_copy(0, j % 2).wait()
    else:
        x = x_ref[...].reshape(rows, D_MODEL)
    xb = x.astype(_BF16)

    def proj(g):
        lo, hi = g * D_MODEL, (g + 1) * D_MODEL
        return jnp.dot(xb, w_in_ref[:, lo:hi], preferred_element_type=_F32) + b_in_ref[:, lo:hi]

    xa_s[ta:ta + rows, :] = proj(0)
    xc = bca_ref[...] + xa_s[0:rows, :] * wca_ref[0:1, :]
    for k in range(1, CONV_A_WIDTH):
        xc = xc + xa_s[k * nb:k * nb + rows, :] * wca_ref[k:k + 1, :]
    new_ta = xa_s[rows:rows + ta, :]
    xa_s[0:ta, :] = new_ta
    ca_ref[...] = new_ta.reshape(CONV_A_WIDTH - 1, nb, D_MODEL)

    xcb = xc.astype(_BF16)

    def block_diag(w_ref):
        return jnp.concatenate(
            [jnp.dot(xcb[:, q * GATE_TILE:(q + 1) * GATE_TILE], w_ref[q], preferred_element_type=_F32)
             for q in range(N_GATE_TILES)], axis=-1)

    r = _sigmoid(block_diag(wa_ref) + ba_ref[...])
    ig = _sigmoid(block_diag(wx_ref) + bx_ref[...])
    nlam = -lam_ref[...]
    softplus = jnp.maximum(nlam, 0.0) + jnp.log1p(jnp.exp(-jnp.abs(nlam)))
    log_a = (-LRU_C * softplus) * r
    a = jnp.exp(log_a)
    a_s[...] = a
    z = -jnp.tanh(log_a) * (a * a + 1.0)
    b_s[...] = jnp.where(z > 0.0, z * lax.rsqrt(z), 0.0) * (ig * xc)

    h = hst_s[...]
    for t in range(ts):
        sl = slice(t * nb, (t + 1) * nb)
        h = a_s[sl, :] * h + b_s[sl, :]
        h_s[sl, :] = h
    hst_s[...] = h
    hl_ref[...] = h

    a_s[...] = h_s[...] * _gelu_tanh(proj(1))

    u_s[tb:tb + rows, :] = proj(3) * proj(4)
    uc = u_s[0:rows, :] * wcb_ref[0:1, :]
    for k in range(1, CONV_B_WIDTH):
        uc = uc + u_s[k * nb:k * nb + rows, :] * wcb_ref[k:k + 1, :]
    new_tb = u_s[rows:rows + tb, :]
    u_s[0:tb, :] = new_tb
    cb_ref[...] = new_tb.reshape(CONV_B_WIDTH - 1, nb, D_MODEL)
    y_b = proj(2) * uc

    merged = _sigmoid(proj(5)) * a_s[...] + _sigmoid(proj(6)) * y_b
    mixed = jnp.dot(merged.astype(_BF16), w_out_ref[...], preferred_element_type=_F32)
    x1 = _layernorm(alpha * x + mixed, g1_ref[...], be1_ref[...])
    x1_dst = x1_o.at[j % 2] if batch_major_x else x1_ref
    for s in range(8):
        x1_dst[pl.ds(s, rows, stride=8), :] = x1[:, s * LANES:(s + 1) * LANES]

    x1_hi = x1.astype(_BF16)
    x1_lo = (x1 - x1_hi.astype(_F32)).astype(_BF16)
    logits = (lax.dot_general(rwt_hi_ref[...], x1_hi, _NT, preferred_element_type=_F32)
              + lax.dot_general(rwt_hi_ref[...], x1_lo, _NT, preferred_element_type=_F32)
              + lax.dot_general(rwt_lo_ref[...], x1_hi, _NT, preferred_element_type=_F32)
              + rb_ref[:, 0:1])
    e_iota = lax.broadcasted_iota(jnp.int32, (N_EXPERTS, rows), 0)
    work = logits
    vals, sels, idxs = [], [], []
    for _ in range(TOP_K):
        m = jnp.max(work, axis=0, keepdims=True)
        ik = jnp.min(jnp.where(work == m, e_iota, N_EXPERTS), axis=0, keepdims=True)
        sel = e_iota == ik
        work = jnp.where(sel, -jnp.inf, work)
        vals.append(m)
        sels.append(sel)
        idxs.append(ik)
    exps = [jnp.exp(v - vals[0]) for v in vals]
    denom = exps[0] + exps[1] + exps[2] + exps[3]
    gates = [ex / denom for ex in exps]

    onehot = jnp.zeros((N_EXPERTS, rows), _F32)
    for sel in sels:
        onehot = onehot + sel.astype(_F32)
    prefix = jnp.dot(onehot.astype(_BF16), tri_ref[...], preferred_element_type=_F32)
    pos = prefix + cnt_s[:, 0:1]
    ranks = [jnp.sum(jnp.where(sel, pos, 0.0), axis=0, keepdims=True) for sel in sels]
    new_cnt = cnt_s[...] + jnp.sum(onehot, axis=1, keepdims=True)
    cnt_s[...] = new_cnt
    cnt_ref[...] = new_cnt

    row8 = lax.broadcasted_iota(jnp.int32, (8, rows), 0)
    idx8 = jnp.zeros((8, rows), jnp.int32)
    rank8 = jnp.zeros((8, rows), jnp.int32)
    for k in range(TOP_K):
        idx8 = jnp.where(row8 == k, idxs[k], idx8)
        rank8 = jnp.where(row8 == k, ranks[k].astype(jnp.int32), rank8)
    idx_ref[...] = idx8
    rank_ref[...] = rank8

    row_l = lax.broadcasted_iota(jnp.int32, (LANES, rows), 0)
    g_t = jnp.zeros((LANES, rows), _F32)
    for k in range(TOP_K):
        g_t = jnp.where(row_l == k, gates[k], g_t)
    grow_ref[...] = jnp.transpose(g_t)

    if batch_major_x:
        @pl.when(j == nt - 1)
        def _():
            out_copy(j, j % 2).start()
            out_copy(0, j % 2).wait()
            if nt > 1:
                out_copy(0, (j + 1) % 2).wait()


def _const_spec(shape):
    nd = len(shape)
    return pl.BlockSpec(shape, lambda i, j: (0,) * nd, pipeline_mode=pl.Buffered(1))


def _mixer_call(x4, h0, ca0, cb0, cnt0, x1_buf, wts, *, ts, blk_off, alpha, batch_major_x=False,
                x1_rows=None):
    if batch_major_x:
        (nb, seq, _), nblk = x4.shape, 1
    else:
        nblk, seq, nb, _ = x4.shape
    nt = seq // ts
    rows = ts * nb
    total = nblk * seq * nb
    f32 = lambda *s: jax.ShapeDtypeStruct(s, _F32)
    i32 = lambda *s: jax.ShapeDtypeStruct(s, jnp.int32)
    in_specs = [
        (pl.BlockSpec(memory_space=pl.ANY) if batch_major_x
         else pl.BlockSpec((None, ts, nb, D_MODEL), lambda i, j: (i, j, 0, 0))),
        pl.BlockSpec((None, nb, D_MODEL), lambda i, j: (i, 0, 0)),
        pl.BlockSpec((None, CONV_A_WIDTH - 1, nb, D_MODEL), lambda i, j: (i, 0, 0, 0)),
        pl.BlockSpec((None, CONV_B_WIDTH - 1, nb, D_MODEL), lambda i, j: (i, 0, 0, 0)),
        _const_spec(cnt0.shape),
        pl.BlockSpec(memory_space=pl.ANY),
    ] + [_const_spec(w.shape) for w in wts]
    out_shape = (
        f32(*((x1_rows, LANES) if batch_major_x else x1_buf.shape)),
        i32(8, total),
        i32(8, total),
        f32(total, LANES),
        f32(nblk, nb, D_MODEL),
        f32(nblk, CONV_A_WIDTH - 1, nb, D_MODEL),
        f32(nblk, CONV_B_WIDTH - 1, nb, D_MODEL),
        f32(N_EXPERTS, LANES),
    )
    out_specs = (
        (pl.BlockSpec(memory_space=pl.ANY) if batch_major_x
         else pl.BlockSpec((rows * 8, LANES), lambda i, j: (blk_off + i * nt + j, 0))),
        pl.BlockSpec((8, rows), lambda i, j: (0, i * nt + j)),
        pl.BlockSpec((8, rows), lambda i, j: (0, i * nt + j)),
        pl.BlockSpec((rows, LANES), lambda i, j: (i * nt + j, 0)),
        pl.BlockSpec((None, nb, D_MODEL), lambda i, j: (i, 0, 0)),
        pl.BlockSpec((None, CONV_A_WIDTH - 1, nb, D_MODEL), lambda i, j: (i, 0, 0, 0)),
        pl.BlockSpec((None, CONV_B_WIDTH - 1, nb, D_MODEL), lambda i, j: (i, 0, 0, 0)),
        pl.BlockSpec((N_EXPERTS, LANES), lambda i, j: (0, 0)),
    )
    scratch = [
        pltpu.VMEM((rows + (CONV_A_WIDTH - 1) * nb, D_MODEL), _F32),
        pltpu.VMEM((rows + (CONV_B_WIDTH - 1) * nb, D_MODEL), _F32),
        pltpu.VMEM((rows, D_MODEL), _F32),
        pltpu.VMEM((rows, D_MODEL), _F32),
        pltpu.VMEM((rows, D_MODEL), _F32),
        pltpu.VMEM((nb, D_MODEL), _F32),
        pltpu.VMEM((N_EXPERTS, LANES), _F32),
    ]
    if batch_major_x:
        scratch += [pltpu.VMEM((2, ts, nb, D_MODEL), _F32), pltpu.SemaphoreType.DMA((2,)),
                    pltpu.VMEM((2, rows * 8, LANES), _F32), pltpu.SemaphoreType.DMA((2,))]
    return pl.pallas_call(
        functools.partial(_mixer_kernel, ts=ts, nb=nb, alpha=alpha, batch_major_x=batch_major_x),
        grid=(nblk, nt),
        in_specs=in_specs,
        out_specs=out_specs,
        out_shape=out_shape,
        scratch_shapes=scratch,
        input_output_aliases={} if batch_major_x else {5: 0},
        compiler_params=pltpu.CompilerParams(
            dimension_semantics=("arbitrary", "arbitrary"), vmem_limit_bytes=VMEM_LIMIT),
        name="mixer",
    )(x4, h0, ca0, cb0, cnt0, x1_buf, *wts)


SC_WORKERS = 8
SC_WINDOW = 128
SC_LANES = 16


def _invmap_sc_call(dest, *, total, n_out):
    n = dest.shape[0]
    per_worker = n // SC_WORKERS
    n_windows = per_worker // SC_WINDOW
    assert per_worker * SC_WORKERS == n and n_windows * SC_WINDOW == per_worker
    assert total % per_worker == 0
    mesh = plsc.VectorSubcoreMesh(core_axis_name="c", subcore_axis_name="s")

    @functools.partial(
        pl.kernel, mesh=mesh,
        out_type=jax.ShapeDtypeStruct((SC_WORKERS, n_out), jnp.int32),
        scratch_types=[pltpu.VMEM((n_out,), jnp.int32),
                       pltpu.VMEM((per_worker,), jnp.int32),
                       pltpu.SemaphoreType.DMA],
        compiler_params=pltpu.CompilerParams(use_tc_tiling_on_sc=False, needs_layout_passes=False),
        name="invmap_sc")
    def scatter(dest_hbm, out_hbm, loc_v, idx_v, sem):
        worker = lax.axis_index("s") * 2 + lax.axis_index("c")

        @pl.when(worker < SC_WORKERS)
        def _():
            first = worker * per_worker
            k = first // total
            lane = lax.iota(jnp.int32, SC_LANES)
            load = pltpu.make_async_copy(dest_hbm.at[pl.ds(first, per_worker)], idx_v, sem)
            load.start()
            empty = jnp.full((SC_LANES,), -1, jnp.int32)
            unroll = 8

            @pl.loop(0, n_out // (SC_LANES * unroll))
            def _(i):
                for u in range(unroll):
                    loc_v[pl.ds((i * unroll + u) * SC_LANES, SC_LANES)] = empty
            load.wait()

            @pl.loop(0, per_worker // SC_LANES)
            def _(j):
                idx = idx_v[pl.ds(j * SC_LANES, SC_LANES)]
                tok = first - k * total + j * SC_LANES + lane
                plsc.store_scatter(loc_v, [idx], tok * TOP_K + k)
            pltpu.sync_copy(loc_v, out_hbm.at[worker])

    assert n_out % (SC_LANES * 8) == 0
    return jnp.max(scatter(dest), axis=0)


def _moe_kernel(blk_e_ref, first_ref, wslot_ref, nxt_e_ref, n_used_ref, inv_ref,
                x1_hbm, wgu_hbm, bgu_ref, wd_hbm, bd_ref,
                ytok_hbm,
                xb0, xb1, xb2, yb0, yb1, yb2, wgu_f, wd_f, wgu_s, wd_s, bgu_s, bd_s,
                gsem, ssem, wsem, *, n_slots):
    n_used = n_used_ref[0]
    n_blk = blk_e_ref.shape[0]
    xbufs = (xb0, xb1, xb2)
    ybufs = (yb0, yb1, yb2)

    def gather_copy(v, s, r):
        return pltpu.make_async_copy(
            x1_hbm.at[v >> 2], xbufs[s].at[pl.ds(r * 8, 8), :], gsem.at[s])

    def scatter_copy(v, s, r):
        return pltpu.make_async_copy(
            ybufs[s].at[pl.ds(r * YROW, YROW), :],
            ytok_hbm.at[v >> 1, pl.ds(pl.multiple_of((v & 1) * YROW, YROW), YROW), :], ssem.at[s])

    def weight_copies(e, ws):
        return (pltpu.make_async_copy(wgu_hbm.at[e], wgu_f.at[ws], wsem.at[ws]),
                pltpu.make_async_copy(wd_hbm.at[e], wd_f.at[ws], wsem.at[ws]))

    def wait_rows(copy_fn, s):
        for r in range(MOE_ROWS):
            copy_fn(0, s, r).wait()

    for cp in weight_copies(blk_e_ref[0], 0):
        cp.start()
    for yb in ybufs:
        yb[...] = jnp.zeros(yb.shape, jnp.uint32)

    def prime(r, c):
        for s in range(2):
            spare = n_slots + s * MOE_ROWS + r
            pltpu.make_async_copy(
                ybufs[s].at[pl.ds(pl.multiple_of(r * YROW, YROW), YROW), :],
                ytok_hbm.at[spare >> 1, pl.ds(pl.multiple_of((spare & 1) * YROW, YROW), YROW), :],
                ssem.at[s]).start()
            pltpu.make_async_copy(
                x1_hbm.at[inv_ref[(s + 1) * MOE_ROWS + r] >> 2],
                xbufs[s].at[pl.ds(pl.multiple_of(r * 8, 8), 8), :], gsem.at[s]).start()
        return c
    lax.fori_loop(0, MOE_ROWS, prime, 0)

    def run_block(b, s):
        nxt = (s + 2) % 3
        wait_rows(gather_copy, s)
        wait_rows(scatter_copy, s)
        x = jnp.concatenate(
            [xbufs[s][pl.ds(q, MOE_ROWS, stride=8), :] for q in range(8)], axis=-1)
        gbase = (jnp.minimum(b + 2, n_blk - 1) + 1) * MOE_ROWS
        sbase = b * MOE_ROWS
        for r in range(MOE_ROWS):
            gather_copy(inv_ref[gbase + r], nxt, r).start()
        for r in range(MOE_ROWS):
            scatter_copy(inv_ref[sbase + r], nxt, r).start()
        gu = jnp.dot(x.astype(_BF16), wgu_s[...], preferred_element_type=_F32) + bgu_s[...]
        gate = jnp.minimum(gu[:, :D_FF], SWIGLU_LIMIT)
        up = jnp.clip(gu[:, D_FF:], -SWIGLU_LIMIT, SWIGLU_LIMIT)
        hmid = (up + 1.0) * (gate * _sigmoid(SWIGLU_ALPHA * gate))
        y = jnp.dot(hmid.astype(_BF16), wd_s[...], preferred_element_type=_F32) + bd_s[...]
        bits = lax.bitcast_convert_type(y.astype(_BF16).astype(_F32), jnp.uint32)
        packed = (bits[:, :D_MODEL // 2] & jnp.uint32(0xFFFF0000)) | (bits[:, D_MODEL // 2:] >> 16)
        for q in range(YROW):
            ybufs[s][pl.ds(q, MOE_ROWS, stride=YROW), :] = packed[:, q * LANES:(q + 1) * LANES]

    def body(b, c):
        @pl.when(first_ref[b] == 1)
        def _():
            e = blk_e_ref[b]
            ws = wslot_ref[b]
            for cp in weight_copies(e, ws):
                cp.wait()
            nxt_e = nxt_e_ref[b]

            @pl.when(nxt_e >= 0)
            def _():
                for cp in weight_copies(nxt_e, 1 - ws):
                    cp.start()
            wgu_s[...] = wgu_f[ws].astype(_BF16)
            wd_s[...] = wd_f[ws].astype(_BF16)
            bgu_s[...] = bgu_ref[e]
            bd_s[...] = bd_ref[e]

        for s in range(3):
            pl.when(b % 3 == s)(functools.partial(run_block, b, s))
        return c

    lax.fori_loop(0, n_used + 1, body, 0)

    for d in (1, 2):
        for s in range(3):
            @pl.when((n_used + d) % 3 == s)
            def _():
                wait_rows(gather_copy, s)
                wait_rows(scatter_copy, s)


def _moe_call(blk_e, first, wslot, nxt_e, n_used, inv, x1t, w_gu, b_gu, w_down, b_down, *, n_slots):
    full = lambda shape: pl.BlockSpec(shape, lambda i, *_: (0,) * len(shape))
    grid_spec = pltpu.PrefetchScalarGridSpec(
        num_scalar_prefetch=6,
        grid=(1,),
        in_specs=[
            pl.BlockSpec(memory_space=pl.ANY),
            pl.BlockSpec(memory_space=pl.ANY),
            full(b_gu.shape),
            pl.BlockSpec(memory_space=pl.ANY),
            full(b_down.shape),
        ],
        out_specs=pl.BlockSpec(memory_space=pl.ANY),
        scratch_shapes=[pltpu.VMEM((MOE_ROWS * 8, LANES), _F32)] * 3
        + [pltpu.VMEM((MOE_ROWS * YROW, LANES), jnp.uint32)] * 3 + [
            pltpu.VMEM((2, D_MODEL, 2 * D_FF), _F32),
            pltpu.VMEM((2, D_FF, D_MODEL), _F32),
            pltpu.VMEM((D_MODEL, 2 * D_FF), _BF16),
            pltpu.VMEM((D_FF, D_MODEL), _BF16),
            pltpu.VMEM((1, 2 * D_FF), _F32),
            pltpu.VMEM((1, D_MODEL), _F32),
            pltpu.SemaphoreType.DMA((3,)),
            pltpu.SemaphoreType.DMA((3,)),
            pltpu.SemaphoreType.DMA((2,)),
        ],
    )
    return pl.pallas_call(
        functools.partial(_moe_kernel, n_slots=n_slots),
        grid_spec=grid_spec,
        out_shape=jax.ShapeDtypeStruct(((n_slots + 3 * MOE_ROWS) // 2, 8, LANES), jnp.uint32),
        compiler_params=pltpu.CompilerParams(
            dimension_semantics=("arbitrary",), vmem_limit_bytes=VMEM_LIMIT),
        name="moe",
    )(blk_e, first, wslot, nxt_e, n_used, inv, x1t, w_gu, b_gu, w_down, b_down)


def _combine_kernel(x1_ref, g_ref, y01_ref, y23_ref, g2_ref, be2_ref,
                    yp_hbm, ys_hbm, y_o, osem, *, alpha, n_p, bp, ts_p, nb_s, ss):
    i = pl.program_id(0)
    nt = pl.num_programs(0)
    rows = g_ref.shape[0]

    def prompt_copies(step, sl):
        return [pltpu.make_async_copy(y_o.at[sl, :, bb, :],
                                      yp_hbm.at[bb, pl.ds(step * ts_p, ts_p), :], osem.at[sl])
                for bb in range(bp)]

    def sample_copies(step, sl):
        flat = y_o.at[sl].reshape(rows, D_MODEL)
        return [pltpu.make_async_copy(flat.at[pl.ds(t * nb_s, nb_s), :],
                                      ys_hbm.at[pl.ds((step - n_p) * nb_s, nb_s), t, :], osem.at[sl])
                for t in range(ss)]

    def start_step(step, sl):
        @pl.when(step < n_p)
        def _():
            for cp in prompt_copies(step, sl):
                cp.start()

        @pl.when(step >= n_p)
        def _():
            for cp in sample_copies(step, sl):
                cp.start()

    def wait_step(sl):
        for cp in prompt_copies(0, sl):
            cp.wait()

    @pl.when(i >= 1)
    def _():
        start_step(i - 1, (i - 1) % 2)

    @pl.when(i >= 2)
    def _():
        wait_step(i % 2)

    g = g_ref[...]

    def token_major(ref):
        return jnp.concatenate([ref[pl.ds(s, rows, stride=8), :] for s in range(8)], axis=-1)

    def unpack(ref, half):
        u = jnp.concatenate([ref[pl.ds(half * YROW + q, rows, stride=8), :] for q in range(YROW)], axis=-1)
        hi = lax.bitcast_convert_type(u & jnp.uint32(0xFFFF0000), _F32)
        lo = lax.bitcast_convert_type(u << 16, _F32)
        return jnp.concatenate([hi, lo], axis=-1)

    pairs = [r.reshape(rows * 8, LANES) for r in (y01_ref, y23_ref)]
    ys = [unpack(pairs[k // 2], k % 2) for k in range(TOP_K)]
    moe = g[:, 0:1] * ys[0] + g[:, 1:2] * ys[1] + g[:, 2:3] * ys[2] + g[:, 3:4] * ys[3]
    y = _layernorm(alpha * token_major(x1_ref) + moe, g2_ref[...], be2_ref[...])
    y_o[i % 2] = y.reshape(ts_p, bp, D_MODEL)

    @pl.when(i == nt - 1)
    def _():
        start_step(i, i % 2)
        wait_step(i % 2)
        wait_step((i + 1) % 2)


def _combine_call(x1t, grow, ytok, g2, be2, *, prompt_shape, sample_shape, alpha):
    bp, sp, _ = prompt_shape
    bs, ss, _ = sample_shape
    ts_p = COMBINE_ROWS // bp
    nb_s = COMBINE_ROWS // ss
    n_p = sp // ts_p
    nt = n_p + bs // nb_s
    assert nt >= 2 and bp == ss
    pair = lambda h: pl.BlockSpec((COMBINE_ROWS, None, 8, LANES), lambda i, h=h: (i, h, 0, 0))
    vec = pl.BlockSpec((1, D_MODEL), lambda i: (0, 0))
    ytok4 = ytok.reshape(-1, TOP_K // 2, 8, LANES)
    return pl.pallas_call(
        functools.partial(_combine_kernel, alpha=alpha, n_p=n_p, bp=bp, ts_p=ts_p, nb_s=nb_s, ss=ss),
        grid=(nt,),
        in_specs=[pl.BlockSpec((COMBINE_ROWS * 8, LANES), lambda i: (i, 0)),
                  pl.BlockSpec((COMBINE_ROWS, LANES), lambda i: (i, 0)),
                  pair(0), pair(1), vec, vec],
        out_specs=(pl.BlockSpec(memory_space=pl.ANY), pl.BlockSpec(memory_space=pl.ANY)),
        out_shape=(jax.ShapeDtypeStruct(prompt_shape, _F32), jax.ShapeDtypeStruct(sample_shape, _F32)),
        scratch_shapes=[pltpu.VMEM((2, ts_p, bp, D_MODEL), _F32), pltpu.SemaphoreType.DMA((2,))],
        compiler_params=pltpu.CompilerParams(
            dimension_semantics=("arbitrary",), vmem_limit_bytes=VMEM_LIMIT),
        name="combine",
    )(x1t, grow, ytok4, ytok4, g2, be2)


def _pack_block_diag(w):
    w = w.reshape(N_GATE_TILES, HEADS_PER_TILE, LRU_BLOCK, LRU_BLOCK)
    eye = jnp.eye(HEADS_PER_TILE, dtype=w.dtype)
    t = jnp.einsum("qhij,hg->qhigj", w, eye)
    return t.reshape(N_GATE_TILES, GATE_TILE, GATE_TILE)


def _layer(xp, xs, h_s0, ca_s0, cb_s0, p, *, alpha):
    bp, sp, _ = xp.shape
    bs, ss, _ = xs.shape
    tp, tsm = bp * sp, bs * ss
    total = tp + tsm
    row2 = lambda v: v.reshape(1, -1)

    rwt = jnp.transpose(p["router_w"])
    rwt_hi = rwt.astype(_BF16)
    rwt_lo = (rwt - rwt_hi.astype(_F32)).astype(_BF16)
    ii = jnp.arange(MIXER_ROWS)
    tri = (ii[:, None] < ii[None, :]).astype(_BF16)
    wts = (
        p["w_in"].astype(_BF16), row2(p["b_in"]), p["conv_a_w"], row2(p["conv_a_b"]),
        _pack_block_diag(p["lru_wa"]).astype(_BF16), row2(p["lru_ba"]),
        _pack_block_diag(p["lru_wx"]).astype(_BF16), row2(p["lru_bx"]), row2(p["lru_lambda"]),
        p["conv_b_w"], p["w_out"].astype(_BF16), row2(p["ln1_g"]), row2(p["ln1_b"]),
        rwt_hi, rwt_lo, jnp.broadcast_to(p["router_b"][:, None], (N_EXPERTS, LANES)), tri,
    )

    ts_p = MIXER_ROWS // bp
    zeros = lambda *s: jnp.zeros(s, _F32)
    n_assign = total * TOP_K
    n_spare = 3 * MOE_ROWS
    outs_p = _mixer_call(xp, zeros(1, bp, D_MODEL), zeros(1, CONV_A_WIDTH - 1, bp, D_MODEL),
                         zeros(1, CONV_B_WIDTH - 1, bp, D_MODEL), zeros(N_EXPERTS, LANES), zeros(8, LANES), wts,
                         ts=ts_p, blk_off=0, alpha=alpha, batch_major_x=True,
                         x1_rows=(total + n_spare // TOP_K) * 8)
    nb_s = MIXER_ROWS // ss
    nblk_s = bs // nb_s
    xs4 = jnp.transpose(xs.reshape(nblk_s, nb_s, ss, D_MODEL), (0, 2, 1, 3))
    h0 = h_s0.reshape(nblk_s, nb_s, D_MODEL)
    ca0 = jnp.transpose(ca_s0.reshape(nblk_s, nb_s, CONV_A_WIDTH - 1, D_MODEL), (0, 2, 1, 3))
    cb0 = jnp.transpose(cb_s0.reshape(nblk_s, nb_s, CONV_B_WIDTH - 1, D_MODEL), (0, 2, 1, 3))
    outs_s = _mixer_call(xs4, h0, ca0, cb0, outs_p[7], outs_p[0], wts,
                         ts=ss, blk_off=tp // MIXER_ROWS, alpha=alpha)
    x1t = outs_s[0]

    n_blocks = -(-n_assign // MOE_ROWS) + N_EXPERTS + 1
    n_rows = n_blocks * MOE_ROWS
    idx = jnp.concatenate([outs_p[1][:TOP_K], outs_s[1][:TOP_K]], axis=1)
    rank = jnp.concatenate([outs_p[2][:TOP_K], outs_s[2][:TOP_K]], axis=1)
    grow = jnp.concatenate([outs_p[3], outs_s[3]], axis=0)
    counts = outs_s[7][:, 0].astype(jnp.int32)

    padded = (counts + MOE_ROWS - 1) // MOE_ROWS * MOE_ROWS
    end_pad = jnp.cumsum(padded)
    start_pad = end_pad - padded
    experts = jnp.arange(N_EXPERTS, dtype=jnp.int32)
    start_of = jnp.sum(jnp.where(idx[:, :, None] == experts, start_pad, 0), axis=-1)
    dest = (start_of + rank).reshape(n_assign)
    blk_start = jnp.arange(n_blocks, dtype=jnp.int32) * MOE_ROWS
    blk_e = jnp.minimum(jnp.sum((blk_start[:, None] >= end_pad[None, :]).astype(jnp.int32), axis=1),
                        N_EXPERTS - 1)
    n_used = (end_pad[-1] // MOE_ROWS).astype(jnp.int32).reshape(1)

    blk_ids = jnp.arange(n_blocks, dtype=jnp.int32)
    changed = jnp.concatenate([jnp.ones((1,), bool), blk_e[1:] != blk_e[:-1]])
    first = changed & (blk_ids < n_used[0])
    wslot = (jnp.cumsum(first.astype(jnp.int32)) - 1) % 2
    later_first = first[None, :] & (blk_ids[None, :] > blk_ids[:, None])
    nxt_blk = jnp.min(jnp.where(later_first, blk_ids[None, :], n_blocks), axis=1)
    nxt_e = jnp.sum(jnp.where(nxt_blk[:, None] == blk_ids[None, :], blk_e[None, :], 0), axis=1)
    nxt_e = jnp.where(nxt_blk < n_blocks, nxt_e, -1)

    ext = jnp.arange(n_rows + MOE_ROWS, dtype=jnp.int32)
    row = ext - MOE_ROWS
    holds = jnp.any((row[:, None] >= start_pad[None, :]) & (row[:, None] < (start_pad + counts)[None, :]),
                    axis=1)
    spare = jnp.where(row < 0, n_assign + 2 * MOE_ROWS + ext, n_assign + (row & (2 * MOE_ROWS - 1)))
    scattered = _invmap_sc_call(dest + MOE_ROWS, total=total, n_out=n_rows + MOE_ROWS)
    inv = jnp.where(holds, scattered, spare)
    ytok = _moe_call(blk_e, first.astype(jnp.int32), wslot, nxt_e, n_used, inv, x1t.reshape(-1, 8, LANES),
                     p["w_gu"], p["b_gu"][:, None, :], p["w_down"], p["b_down"][:, None, :],
                     n_slots=n_assign)
    yp, ys = _combine_call(x1t, grow, ytok, row2(p["ln2_g"]), row2(p["ln2_b"]),
                           prompt_shape=xp.shape, sample_shape=xs.shape, alpha=alpha)

    def batch_major(v, nblk):
        return jnp.transpose(v, (0, 2, 1, 3)).reshape(nblk * v.shape[2], v.shape[1], D_MODEL)

    states_p = (outs_p[4].reshape(bp, D_MODEL), batch_major(outs_p[5], 1), batch_major(outs_p[6], 1))
    states_s = (outs_s[4].reshape(bs, D_MODEL), batch_major(outs_s[5], nblk_s), batch_major(outs_s[6], nblk_s))
    return yp, ys, states_p, states_s


def kernel(x_prompt, x_sample, state_rglru_h, state_rglru_conv, state_shortconv, w_in, b_in, conv_a_w, conv_a_b, lru_wa, lru_ba, lru_wx, lru_bx, lru_lambda, conv_b_w, w_out, ln1_g, ln1_b, router_w, router_b, w_gu, b_gu, w_down, b_down, ln2_g, ln2_b):
    depth = w_in.shape[0]
    alpha = (2.0 * depth) ** 0.25
    names = ("w_in", "b_in", "conv_a_w", "conv_a_b", "lru_wa", "lru_ba", "lru_wx", "lru_bx", "lru_lambda",
             "conv_b_w", "w_out", "ln1_g", "ln1_b", "router_w", "router_b", "w_gu", "b_gu", "w_down",
             "b_down", "ln2_g", "ln2_b")
    stacked = (w_in, b_in, conv_a_w, conv_a_b, lru_wa, lru_ba, lru_wx, lru_bx, lru_lambda, conv_b_w, w_out,
               ln1_g, ln1_b, router_w, router_b, w_gu, b_gu, w_down, b_down, ln2_g, ln2_b)
    xp, xs = x_prompt, x_sample
    hp_l, cp_l, sp_l, hs_l, cs_l, ss_l = [], [], [], [], [], []
    for l in range(depth):
        p = {n: v[l] for n, v in zip(names, stacked)}
        xp, xs, (hp, cp, sp), (hs, cs, ss) = _layer(
            xp, xs, state_rglru_h[l], state_rglru_conv[l], state_shortconv[l], p, alpha=alpha)
        hp_l.append(hp); cp_l.append(cp); sp_l.append(sp)
        hs_l.append(hs); cs_l.append(cs); ss_l.append(ss)
    return (xp, xs, jnp.stack(hp_l), jnp.stack(cp_l), jnp.stack(sp_l), jnp.stack(hs_l), jnp.stack(cs_l),
            jnp.stack(ss_l))
```

```python
import functools

import jax
import jax.numpy as jnp
from jax import lax
from jax.experimental import pallas as pl
from jax.experimental.pallas import tpu as pltpu
from jax.experimental.pallas import tpu_sc as plsc

D_MODEL = 1024
LRU_HEADS = 16
LRU_BLOCK = D_MODEL // LRU_HEADS
LRU_C = 8.0
CONV_A_WIDTH = 4
CONV_B_WIDTH = 3
N_GROUPS = 7
N_EXPERTS = 32
TOP_K = 4
D_FF = D_MODEL
SWIGLU_LIMIT = 7.0
SWIGLU_ALPHA = 1.702
LN_EPS = 1e-5

GATE_TILE = 256
HEADS_PER_TILE = GATE_TILE // LRU_BLOCK
N_GATE_TILES = D_MODEL // GATE_TILE
LANES = 128
MIXER_ROWS = 512
MOE_ROWS = 256
COMBINE_ROWS = 512
YROW = 4
VMEM_LIMIT = 58 * 1024 * 1024

_F32 = jnp.float32
_BF16 = jnp.bfloat16
_NT = (((1,), (1,)), ((), ()))


def _sigmoid(v):
    return 0.5 * jnp.tanh(0.5 * v) + 0.5


def _gelu_tanh(v):
    c = 0.7978845608028654
    return 0.5 * v * (1.0 + jnp.tanh(v * (c + (c * 0.044715) * (v * v))))


def _layernorm(z, g, b):
    mu = jnp.mean(z, axis=-1, keepdims=True)
    zc = z - mu
    var = jnp.mean(zc * zc, axis=-1, keepdims=True)
    return zc * lax.rsqrt(var + LN_EPS) * g + b


def _mixer_kernel(x_ref, h0_ref, ca0_ref, cb0_ref, cnt0_ref, x1_buf_ref,
                  w_in_ref, b_in_ref, wca_ref, bca_ref, wa_ref, ba_ref, wx_ref, bx_ref, lam_ref,
                  wcb_ref, w_out_ref, g1_ref, be1_ref, rwt_hi_ref, rwt_lo_ref, rb_ref, tri_ref,
                  x1_ref, idx_ref, rank_ref, grow_ref, hl_ref, ca_ref, cb_ref, cnt_ref,
                  xa_s, u_s, a_s, b_s, h_s, hst_s, cnt_s, *maybe_xin, ts, nb, alpha, batch_major_x):
    i = pl.program_id(0)
    j = pl.program_id(1)
    rows = ts * nb
    ta = (CONV_A_WIDTH - 1) * nb
    tb = (CONV_B_WIDTH - 1) * nb

    @pl.when(j == 0)
    def _():
        hst_s[...] = h0_ref[...]
        xa_s[0:ta, :] = ca0_ref[...].reshape(ta, D_MODEL)
        u_s[0:tb, :] = cb0_ref[...].reshape(tb, D_MODEL)

    @pl.when((i == 0) & (j == 0))
    def _():
        cnt_s[...] = cnt0_ref[...]

    if batch_major_x:
        xin_s, xsem, x1_o, osem = maybe_xin
        nt = pl.num_programs(1)

        def chunk_copies(c, slot):
            return [pltpu.make_async_copy(x_ref.at[bb, pl.ds(c * ts, ts), :],
                                          xin_s.at[slot, :, bb, :], xsem.at[slot]) for bb in range(nb)]

        @pl.when(j == 0)
        def _():
            for cp in chunk_copies(0, 0):
                cp.start()

        @pl.when(j + 1 < nt)
        def _():
            for cp in chunk_copies(j + 1, (j + 1) % 2):
                cp.start()
        for cp in chunk_copies(j, j % 2):
            cp.wait()
        x = xin_s[j % 2].reshape(rows, D_MODEL)

        tile_rows = rows * 8

        def out_copy(c, sl):
            return pltpu.make_async_copy(
                x1_o.at[sl], x1_ref.at[pl.ds(pl.multiple_of(c * tile_rows, tile_rows), tile_rows), :],
                osem.at[sl])

        @pl.when(j == 0)
        def _():
            x1_o[1] = jnp.zeros((tile_rows, LANES), _F32)
            lo = nt * tile_rows
            tails = []
            while lo < x1_ref.shape[0]:
                n = min(tile_rows, x1_ref.shape[0] - lo)
                tails.append(pltpu.make_async_copy(
                    x1_o.at[1, pl.ds(0, n), :], x1_ref.at[pl.ds(lo, n), :], osem.at[1]))
                lo += n
            for cp in tails:
                cp.start()
            for cp in tails:
                cp.wait()

        @pl.when(j >= 1)
        def _():
            out_copy(j - 1, (j - 1) % 2).start()

        @pl.when(j >= 2)
        def _():
            out_copy(0, j % 2).wait()
    else:
        x = x_ref[...].reshape(rows, D_MODEL)
    xb = x.astype(_BF16)

    def proj(g):
        lo, hi = g * D_MODEL, (g + 1) * D_MODEL
        return jnp.dot(xb, w_in_ref[:, lo:hi], preferred_element_type=_F32) + b_in_ref[:, lo:hi]

    xa_s[ta:ta + rows, :] = proj(0)
    xc = bca_ref[...] + xa_s[0:rows, :] * wca_ref[0:1, :]
    for k in range(1, CONV_A_WIDTH):
        xc = xc + xa_s[k * nb:k * nb + rows, :] * wca_ref[k:k + 1, :]
    new_ta = xa_s[rows:rows + ta, :]
    xa_s[0:ta, :] = new_ta
    ca_ref[...] = new_ta.reshape(CONV_A_WIDTH - 1, nb, D_MODEL)

    xcb = xc.astype(_BF16)

    def block_diag(w_ref):
        return jnp.concatenate(
            [jnp.dot(xcb[:, q * GATE_TILE:(q + 1) * GATE_TILE], w_ref[q], preferred_element_type=_F32)
             for q in range(N_GATE_TILES)], axis=-1)

    r = _sigmoid(block_diag(wa_ref) + ba_ref[...])
    ig = _sigmoid(block_diag(wx_ref) + bx_ref[...])
    nlam = -lam_ref[...]
    softplus = jnp.maximum(nlam, 0.0) + jnp.log1p(jnp.exp(-jnp.abs(nlam)))
    log_a = (-LRU_C * softplus) * r
    a = jnp.exp(log_a)
    a_s[...] = a
    z = -jnp.tanh(log_a) * (a * a + 1.0)
    b_s[...] = jnp.where(z > 0.0, z * lax.rsqrt(z), 0.0) * (ig * xc)

    h = hst_s[...]
    for t in range(ts):
        sl = slice(t * nb, (t + 1) * nb)
        h = a_s[sl, :] * h + b_s[sl, :]
        h_s[sl, :] = h
    hst_s[...] = h
    hl_ref[...] = h

    a_s[...] = h_s[...] * _gelu_tanh(proj(1))

    u_s[tb:tb + rows, :] = proj(3) * proj(4)
    uc = u_s[0:rows, :] * wcb_ref[0:1, :]
    for k in range(1, CONV_B_WIDTH):
        uc = uc + u_s[k * nb:k * nb + rows, :] * wcb_ref[k:k + 1, :]
    new_tb = u_s[rows:rows + tb, :]
    u_s[0:tb, :] = new_tb
    cb_ref[...] = new_tb.reshape(CONV_B_WIDTH - 1, nb, D_MODEL)
    y_b = proj(2) * uc

    merged = _sigmoid(proj(5)) * a_s[...] + _sigmoid(proj(6)) * y_b
    mixed = jnp.dot(merged.astype(_BF16), w_out_ref[...], preferred_element_type=_F32)
    x1 = _layernorm(alpha * x + mixed, g1_ref[...], be1_ref[...])
    x1_dst = x1_o.at[j % 2] if batch_major_x else x1_ref
    for s in range(8):
        x1_dst[pl.ds(s, rows, stride=8), :] = x1[:, s * LANES:(s + 1) * LANES]

    x1_hi = x1.astype(_BF16)
    x1_lo = (x1 - x1_hi.astype(_F32)).astype(_BF16)
    logits = (lax.dot_general(rwt_hi_ref[...], x1_hi, _NT, preferred_element_type=_F32)
              + lax.dot_general(rwt_hi_ref[...], x1_lo, _NT, preferred_element_type=_F32)
              + lax.dot_general(rwt_lo_ref[...], x1_hi, _NT, preferred_element_type=_F32)
              + rb_ref[:, 0:1])
    e_iota = lax.broadcasted_iota(jnp.int32, (N_EXPERTS, rows), 0)
    work = logits
    vals, sels, idxs = [], [], []
    for _ in range(TOP_K):
        m = jnp.max(work, axis=0, keepdims=True)
        ik = jnp.min(jnp.where(work == m, e_iota, N_EXPERTS), axis=0, keepdims=True)
        sel = e_iota == ik
        work = jnp.where(sel, -jnp.inf, work)
        vals.append(m)
        sels.append(sel)
        idxs.append(ik)
    exps = [jnp.exp(v - vals[0]) for v in vals]
    denom = exps[0] + exps[1] + exps[2] + exps[3]
    gates = [ex / denom for ex in exps]

    onehot = jnp.zeros((N_EXPERTS, rows), _F32)
    for sel in sels:
        onehot = onehot + sel.astype(_F32)
    prefix = jnp.dot(onehot.astype(_BF16), tri_ref[...], preferred_element_type=_F32)
    pos = prefix + cnt_s[:, 0:1]
    ranks = [jnp.sum(jnp.where(sel, pos, 0.0), axis=0, keepdims=True) for sel in sels]
    new_cnt = cnt_s[...] + jnp.sum(onehot, axis=1, keepdims=True)
    cnt_s[...] = new_cnt
    cnt_ref[...] = new_cnt

    row8 = lax.broadcasted_iota(jnp.int32, (8, rows), 0)
    idx8 = jnp.zeros((8, rows), jnp.int32)
    rank8 = jnp.zeros((8, rows), jnp.int32)
    for k in range(TOP_K):
        idx8 = jnp.where(row8 == k, idxs[k], idx8)
        rank8 = jnp.where(row8 == k, ranks[k].astype(jnp.int32), rank8)
    idx_ref[...] = idx8
    rank_ref[...] = rank8

    row_l = lax.broadcasted_iota(jnp.int32, (LANES, rows), 0)
    g_t = jnp.zeros((LANES, rows), _F32)
    for k in range(TOP_K):
        g_t = jnp.where(row_l == k, gates[k], g_t)
    grow_ref[...] = jnp.transpose(g_t)

    if batch_major_x:
        @pl.when(j == nt - 1)
        def _():
            out_copy(j, j % 2).start()
            out_copy(0, j % 2).wait()
            if nt > 1:
                out_copy(0, (j + 1) % 2).wait()


def _const_spec(shape):
    nd = len(shape)
    return pl.BlockSpec(shape, lambda i, j: (0,) * nd, pipeline_mode=pl.Buffered(1))


def _mixer_call(x4, h0, ca0, cb0, cnt0, x1_buf, wts, *, ts, blk_off, alpha, batch_major_x=False,
                x1_rows=None):
    if batch_major_x:
        (nb, seq, _), nblk = x4.shape, 1
    else:
        nblk, seq, nb, _ = x4.shape
    nt = seq // ts
    rows = ts * nb
    total = nblk * seq * nb
    f32 = lambda *s: jax.ShapeDtypeStruct(s, _F32)
    i32 = lambda *s: jax.ShapeDtypeStruct(s, jnp.int32)
    in_specs = [
        (pl.BlockSpec(memory_space=pl.ANY) if batch_major_x
         else pl.BlockSpec((None, ts, nb, D_MODEL), lambda i, j: (i, j, 0, 0))),
        pl.BlockSpec((None, nb, D_MODEL), lambda i, j: (i, 0, 0)),
        pl.BlockSpec((None, CONV_A_WIDTH - 1, nb, D_MODEL), lambda i, j: (i, 0, 0, 0)),
        pl.BlockSpec((None, CONV_B_WIDTH - 1, nb, D_MODEL), lambda i, j: (i, 0, 0, 0)),
        _const_spec(cnt0.shape),
        pl.BlockSpec(memory_space=pl.ANY),
    ] + [_const_spec(w.shape) for w in wts]
    out_shape = (
        f32(*((x1_rows, LANES) if batch_major_x else x1_buf.shape)),
        i32(8, total),
        i32(8, total),
        f32(total, LANES),
        f32(nblk, nb, D_MODEL),
        f32(nblk, CONV_A_WIDTH - 1, nb, D_MODEL),
        f32(nblk, CONV_B_WIDTH - 1, nb, D_MODEL),
        f32(N_EXPERTS, LANES),
    )
    out_specs = (
        (pl.BlockSpec(memory_space=pl.ANY) if batch_major_x
         else pl.BlockSpec((rows * 8, LANES), lambda i, j: (blk_off + i * nt + j, 0))),
        pl.BlockSpec((8, rows), lambda i, j: (0, i * nt + j)),
        pl.BlockSpec((8, rows), lambda i, j: (0, i * nt + j)),
        pl.BlockSpec((rows, LANES), lambda i, j: (i * nt + j, 0)),
        pl.BlockSpec((None, nb, D_MODEL), lambda i, j: (i, 0, 0)),
        pl.BlockSpec((None, CONV_A_WIDTH - 1, nb, D_MODEL), lambda i, j: (i, 0, 0, 0)),
        pl.BlockSpec((None, CONV_B_WIDTH - 1, nb, D_MODEL), lambda i, j: (i, 0, 0, 0)),
        pl.BlockSpec((N_EXPERTS, LANES), lambda i, j: (0, 0)),
    )
    scratch = [
        pltpu.VMEM((rows + (CONV_A_WIDTH - 1) * nb, D_MODEL), _F32),
        pltpu.VMEM((rows + (CONV_B_WIDTH - 1) * nb, D_MODEL), _F32),
        pltpu.VMEM((rows, D_MODEL), _F32),
        pltpu.VMEM((rows, D_MODEL), _F32),
        pltpu.VMEM((rows, D_MODEL), _F32),
        pltpu.VMEM((nb, D_MODEL), _F32),
        pltpu.VMEM((N_EXPERTS, LANES), _F32),
    ]
    if batch_major_x:
        scratch += [pltpu.VMEM((2, ts, nb, D_MODEL), _F32), pltpu.SemaphoreType.DMA((2,)),
                    pltpu.VMEM((2, rows * 8, LANES), _F32), pltpu.SemaphoreType.DMA((2,))]
    return pl.pallas_call(
        functools.partial(_mixer_kernel, ts=ts, nb=nb, alpha=alpha, batch_major_x=batch_major_x),
        grid=(nblk, nt),
        in_specs=in_specs,
        out_specs=out_specs,
        out_shape=out_shape,
        scratch_shapes=scratch,
        input_output_aliases={} if batch_major_x else {5: 0},
        compiler_params=pltpu.CompilerParams(
            dimension_semantics=("arbitrary", "arbitrary"), vmem_limit_bytes=VMEM_LIMIT),
        name="mixer",
    )(x4, h0, ca0, cb0, cnt0, x1_buf, *wts)


SC_WORKERS = 8
SC_WINDOW = 128
SC_LANES = 16


def _invmap_sc_call(dest, *, total, n_out):
    n = dest.shape[0]
    per_worker = n // SC_WORKERS
    n_windows = per_worker // SC_WINDOW
    assert per_worker * SC_WORKERS == n and n_windows * SC_WINDOW == per_worker
    assert total % per_worker == 0
    mesh = plsc.VectorSubcoreMesh(core_axis_name="c", subcore_axis_name="s")

    @functools.partial(
        pl.kernel, mesh=mesh,
        out_type=jax.ShapeDtypeStruct((SC_WORKERS, n_out), jnp.int32),
        scratch_types=[pltpu.VMEM((n_out,), jnp.int32),
                       pltpu.VMEM((per_worker,), jnp.int32),
                       pltpu.SemaphoreType.DMA],
        compiler_params=pltpu.CompilerParams(use_tc_tiling_on_sc=False, needs_layout_passes=False),
        name="invmap_sc")
    def scatter(dest_hbm, out_hbm, loc_v, idx_v, sem):
        worker = lax.axis_index("s") * 2 + lax.axis_index("c")

        @pl.when(worker < SC_WORKERS)
        def _():
            first = worker * per_worker
            k = first // total
            lane = lax.iota(jnp.int32, SC_LANES)
            load = pltpu.make_async_copy(dest_hbm.at[pl.ds(first, per_worker)], idx_v, sem)
            load.start()
            empty = jnp.full((SC_LANES,), -1, jnp.int32)
            unroll = 8

            @pl.loop(0, n_out // (SC_LANES * unroll))
            def _(i):
                for u in range(unroll):
                    loc_v[pl.ds((i * unroll + u) * SC_LANES, SC_LANES)] = empty
            load.wait()

            @pl.loop(0, per_worker // SC_LANES)
            def _(j):
                idx = idx_v[pl.ds(j * SC_LANES, SC_LANES)]
                tok = first - k * total + j * SC_LANES + lane
                plsc.store_scatter(loc_v, [idx], tok * TOP_K + k)
            pltpu.sync_copy(loc_v, out_hbm.at[worker])

    assert n_out % (SC_LANES * 8) == 0
    return jnp.max(scatter(dest), axis=0)


def _moe_kernel(blk_e_ref, first_ref, wslot_ref, nxt_e_ref, n_used_ref, inv_ref,
                x1_hbm, wgu_hbm, bgu_ref, wd_hbm, bd_ref,
                ytok_hbm,
                xb0, xb1, xb2, yb0, yb1, yb2, wgu_f, wd_f, wgu_s, wd_s, bgu_s, bd_s,
                gsem, ssem, wsem, *, n_slots):
    n_used = n_used_ref[0]
    n_blk = blk_e_ref.shape[0]
    xbufs = (xb0, xb1, xb2)
    ybufs = (yb0, yb1, yb2)

    def gather_copy(v, s, r):
        return pltpu.make_async_copy(
            x1_hbm.at[v >> 2], xbufs[s].at[pl.ds(r * 8, 8), :], gsem.at[s])

    def scatter_copy(v, s, r):
        return pltpu.make_async_copy(
            ybufs[s].at[pl.ds(r * YROW, YROW), :],
            ytok_hbm.at[v >> 1, pl.ds(pl.multiple_of((v & 1) * YROW, YROW), YROW), :], ssem.at[s])

    def weight_copies(e, ws):
        return (pltpu.make_async_copy(wgu_hbm.at[e], wgu_f.at[ws], wsem.at[ws]),
                pltpu.make_async_copy(wd_hbm.at[e], wd_f.at[ws], wsem.at[ws]))

    def wait_rows(copy_fn, s):
        for r in range(MOE_ROWS):
            copy_fn(0, s, r).wait()

    for cp in weight_copies(blk_e_ref[0], 0):
        cp.start(priority=1)
    for yb in ybufs:
        yb[...] = jnp.zeros(yb.shape, jnp.uint32)

    def prime(r, c):
        for s in range(2):
            spare = n_slots + s * MOE_ROWS + r
            pltpu.make_async_copy(
                ybufs[s].at[pl.ds(pl.multiple_of(r * YROW, YROW), YROW), :],
                ytok_hbm.at[spare >> 1, pl.ds(pl.multiple_of((spare & 1) * YROW, YROW), YROW), :],
                ssem.at[s]).start()
            pltpu.make_async_copy(
                x1_hbm.at[inv_ref[(s + 1) * MOE_ROWS + r] >> 2],
                xbufs[s].at[pl.ds(pl.multiple_of(r * 8, 8), 8), :], gsem.at[s]).start()
        return c
    lax.fori_loop(0, MOE_ROWS, prime, 0)

    def run_block(b, s):
        nxt = (s + 2) % 3
        wait_rows(gather_copy, s)
        wait_rows(scatter_copy, s)
        x = jnp.concatenate(
            [xbufs[s][pl.ds(q, MOE_ROWS, stride=8), :] for q in range(8)], axis=-1)
        gbase = (jnp.minimum(b + 2, n_blk - 1) + 1) * MOE_ROWS
        sbase = b * MOE_ROWS
        for r in range(MOE_ROWS):
            gather_copy(inv_ref[gbase + r], nxt, r).start()
        for r in range(MOE_ROWS):
            scatter_copy(inv_ref[sbase + r], nxt, r).start(priority=r % 2)
        gu = jnp.dot(x.astype(_BF16), wgu_s[...], preferred_element_type=_F32) + bgu_s[...]
        gate = jnp.minimum(gu[:, :D_FF], SWIGLU_LIMIT)
        up = jnp.clip(gu[:, D_FF:], -SWIGLU_LIMIT, SWIGLU_LIMIT)
        hmid = (up + 1.0) * (gate * _sigmoid(SWIGLU_ALPHA * gate))
        y = jnp.dot(hmid.astype(_BF16), wd_s[...], preferred_element_type=_F32) + bd_s[...]
        bits = lax.bitcast_convert_type(y.astype(_BF16).astype(_F32), jnp.uint32)
        packed = (bits[:, :D_MODEL // 2] & jnp.uint32(0xFFFF0000)) | (bits[:, D_MODEL // 2:] >> 16)
        for q in range(YROW):
            ybufs[s][pl.ds(q, MOE_ROWS, stride=YROW), :] = packed[:, q * LANES:(q + 1) * LANES]

    def body(b, c):
        @pl.when(first_ref[b] == 1)
        def _():
            e = blk_e_ref[b]
            ws = wslot_ref[b]
            for cp in weight_copies(e, ws):
                cp.wait()
            nxt_e = nxt_e_ref[b]

            @pl.when(nxt_e >= 0)
            def _():
                for cp in weight_copies(nxt_e, 1 - ws):
                    cp.start(priority=1)
            wgu_s[...] = wgu_f[ws].astype(_BF16)
            wd_s[...] = wd_f[ws].astype(_BF16)
            bgu_s[...] = bgu_ref[e]
            bd_s[...] = bd_ref[e]

        for s in range(3):
            pl.when(b % 3 == s)(functools.partial(run_block, b, s))
        return c

    lax.fori_loop(0, n_used + 1, body, 0)

    for d in (1, 2):
        for s in range(3):
            @pl.when((n_used + d) % 3 == s)
            def _():
                wait_rows(gather_copy, s)
                wait_rows(scatter_copy, s)


def _moe_call(blk_e, first, wslot, nxt_e, n_used, inv, x1t, w_gu, b_gu, w_down, b_down, *, n_slots):
    full = lambda shape: pl.BlockSpec(shape, lambda i, *_: (0,) * len(shape))
    grid_spec = pltpu.PrefetchScalarGridSpec(
        num_scalar_prefetch=6,
        grid=(1,),
        in_specs=[
            pl.BlockSpec(memory_space=pl.ANY),
            pl.BlockSpec(memory_space=pl.ANY),
            full(b_gu.shape),
            pl.BlockSpec(memory_space=pl.ANY),
            full(b_down.shape),
        ],
        out_specs=pl.BlockSpec(memory_space=pl.ANY),
        scratch_shapes=[pltpu.VMEM((MOE_ROWS * 8, LANES), _F32)] * 3
        + [pltpu.VMEM((MOE_ROWS * YROW, LANES), jnp.uint32)] * 3 + [
            pltpu.VMEM((2, D_MODEL, 2 * D_FF), _F32),
            pltpu.VMEM((2, D_FF, D_MODEL), _F32),
            pltpu.VMEM((D_MODEL, 2 * D_FF), _BF16),
            pltpu.VMEM((D_FF, D_MODEL), _BF16),
            pltpu.VMEM((1, 2 * D_FF), _F32),
            pltpu.VMEM((1, D_MODEL), _F32),
            pltpu.SemaphoreType.DMA((3,)),
            pltpu.SemaphoreType.DMA((3,)),
            pltpu.SemaphoreType.DMA((2,)),
        ],
    )
    return pl.pallas_call(
        functools.partial(_moe_kernel, n_slots=n_slots),
        grid_spec=grid_spec,
        out_shape=jax.ShapeDtypeStruct(((n_slots + 3 * MOE_ROWS) // 2, 8, LANES), jnp.uint32),
        compiler_params=pltpu.CompilerParams(
            dimension_semantics=("arbitrary",), vmem_limit_bytes=VMEM_LIMIT),
        name="moe",
    )(blk_e, first, wslot, nxt_e, n_used, inv, x1t, w_gu, b_gu, w_down, b_down)


def _combine_kernel(x1_ref, g_ref, y01_ref, y23_ref, g2_ref, be2_ref,
                    yp_hbm, ys_hbm, y_o, osem, *, alpha, n_p, bp, ts_p, nb_s, ss):
    i = pl.program_id(0)
    nt = pl.num_programs(0)
    rows = g_ref.shape[0]

    def prompt_copies(step, sl):
        return [pltpu.make_async_copy(y_o.at[sl, :, bb, :],
                                      yp_hbm.at[bb, pl.ds(step * ts_p, ts_p), :], osem.at[sl])
                for bb in range(bp)]

    def sample_copies(step, sl):
        flat = y_o.at[sl].reshape(rows, D_MODEL)
        return [pltpu.make_async_copy(flat.at[pl.ds(t * nb_s, nb_s), :],
                                      ys_hbm.at[pl.ds((step - n_p) * nb_s, nb_s), t, :], osem.at[sl])
                for t in range(ss)]

    def start_step(step, sl):
        @pl.when(step < n_p)
        def _():
            for cp in prompt_copies(step, sl):
                cp.start()

        @pl.when(step >= n_p)
        def _():
            for cp in sample_copies(step, sl):
                cp.start()

    def wait_step(sl):
        for cp in prompt_copies(0, sl):
            cp.wait()

    @pl.when(i >= 1)
    def _():
        start_step(i - 1, (i - 1) % 2)

    @pl.when(i >= 2)
    def _():
        wait_step(i % 2)

    g = g_ref[...]

    def token_major(ref):
        return jnp.concatenate([ref[pl.ds(s, rows, stride=8), :] for s in range(8)], axis=-1)

    def unpack(ref, half):
        u = jnp.concatenate([ref[pl.ds(half * YROW + q, rows, stride=8), :] for q in range(YROW)], axis=-1)
        hi = lax.bitcast_convert_type(u & jnp.uint32(0xFFFF0000), _F32)
        lo = lax.bitcast_convert_type(u << 16, _F32)
        return jnp.concatenate([hi, lo], axis=-1)

    pairs = [r.reshape(rows * 8, LANES) for r in (y01_ref, y23_ref)]
    ys = [unpack(pairs[k // 2], k % 2) for k in range(TOP_K)]
    moe = g[:, 0:1] * ys[0] + g[:, 1:2] * ys[1] + g[:, 2:3] * ys[2] + g[:, 3:4] * ys[3]
    y = _layernorm(alpha * token_major(x1_ref) + moe, g2_ref[...], be2_ref[...])
    y_o[i % 2] = y.reshape(ts_p, bp, D_MODEL)

    @pl.when(i == nt - 1)
    def _():
        start_step(i, i % 2)
        wait_step(i % 2)
        wait_step((i + 1) % 2)


def _combine_call(x1t, grow, ytok, g2, be2, *, prompt_shape, sample_shape, alpha):
    bp, sp, _ = prompt_shape
    bs, ss, _ = sample_shape
    ts_p = COMBINE_ROWS // bp
    nb_s = COMBINE_ROWS // ss
    n_p = sp // ts_p
    nt = n_p + bs // nb_s
    assert nt >= 2 and bp == ss
    pair = lambda h: pl.BlockSpec((COMBINE_ROWS, None, 8, LANES), lambda i, h=h: (i, h, 0, 0))
    vec = pl.BlockSpec((1, D_MODEL), lambda i: (0, 0))
    ytok4 = ytok.reshape(-1, TOP_K // 2, 8, LANES)
    return pl.pallas_call(
        functools.partial(_combine_kernel, alpha=alpha, n_p=n_p, bp=bp, ts_p=ts_p, nb_s=nb_s, ss=ss),
        grid=(nt,),
        in_specs=[pl.BlockSpec((COMBINE_ROWS * 8, LANES), lambda i: (i, 0)),
                  pl.BlockSpec((COMBINE_ROWS, LANES), lambda i: (i, 0)),
                  pair(0), pair(1), vec, vec],
        out_specs=(pl.BlockSpec(memory_space=pl.ANY), pl.BlockSpec(memory_space=pl.ANY)),
        out_shape=(jax.ShapeDtypeStruct(prompt_shape, _F32), jax.ShapeDtypeStruct(sample_shape, _F32)),
        scratch_shapes=[pltpu.VMEM((2, ts_p, bp, D_MODEL), _F32), pltpu.SemaphoreType.DMA((2,))],
        compiler_params=pltpu.CompilerParams(
            dimension_semantics=("arbitrary",), vmem_limit_bytes=VMEM_LIMIT),
        name="combine",
    )(x1t, grow, ytok4, ytok4, g2, be2)


def _pack_block_diag(w):
    w = w.reshape(N_GATE_TILES, HEADS_PER_TILE, LRU_BLOCK, LRU_BLOCK)
    eye = jnp.eye(HEADS_PER_TILE, dtype=w.dtype)
    t = jnp.einsum("qhij,hg->qhigj", w, eye)
    return t.reshape(N_GATE_TILES, GATE_TILE, GATE_TILE)


def _layer(xp, xs, h_s0, ca_s0, cb_s0, p, *, alpha):
    bp, sp, _ = xp.shape
    bs, ss, _ = xs.shape
    tp, tsm = bp * sp, bs * ss
    total = tp + tsm
    row2 = lambda v: v.reshape(1, -1)

    rwt = jnp.transpose(p["router_w"])
    rwt_hi = rwt.astype(_BF16)
    rwt_lo = (rwt - rwt_hi.astype(_F32)).astype(_BF16)
    ii = jnp.arange(MIXER_ROWS)
    tri = (ii[:, None] < ii[None, :]).astype(_BF16)
    wts = (
        p["w_in"].astype(_BF16), row2(p["b_in"]), p["conv_a_w"], row2(p["conv_a_b"]),
        _pack_block_diag(p["lru_wa"]).astype(_BF16), row2(p["lru_ba"]),
        _pack_block_diag(p["lru_wx"]).astype(_BF16), row2(p["lru_bx"]), row2(p["lru_lambda"]),
        p["conv_b_w"], p["w_out"].astype(_BF16), row2(p["ln1_g"]), row2(p["ln1_b"]),
        rwt_hi, rwt_lo, jnp.broadcast_to(p["router_b"][:, None], (N_EXPERTS, LANES)), tri,
    )

    ts_p = MIXER_ROWS // bp
    zeros = lambda *s: jnp.zeros(s, _F32)
    n_assign = total * TOP_K
    n_spare = 3 * MOE_ROWS
    outs_p = _mixer_call(xp, zeros(1, bp, D_MODEL), zeros(1, CONV_A_WIDTH - 1, bp, D_MODEL),
                         zeros(1, CONV_B_WIDTH - 1, bp, D_MODEL), zeros(N_EXPERTS, LANES), zeros(8, LANES), wts,
                         ts=ts_p, blk_off=0, alpha=alpha, batch_major_x=True,
                         x1_rows=(total + n_spare // TOP_K) * 8)
    nb_s = MIXER_ROWS // ss
    nblk_s = bs // nb_s
    xs4 = jnp.transpose(xs.reshape(nblk_s, nb_s, ss, D_MODEL), (0, 2, 1, 3))
    h0 = h_s0.reshape(nblk_s, nb_s, D_MODEL)
    ca0 = jnp.transpose(ca_s0.reshape(nblk_s, nb_s, CONV_A_WIDTH - 1, D_MODEL), (0, 2, 1, 3))
    cb0 = jnp.transpose(cb_s0.reshape(nblk_s, nb_s, CONV_B_WIDTH - 1, D_MODEL), (0, 2, 1, 3))
    outs_s = _mixer_call(xs4, h0, ca0, cb0, outs_p[7], outs_p[0], wts,
                         ts=ss, blk_off=tp // MIXER_ROWS, alpha=alpha)
    x1t = outs_s[0]

    n_blocks = -(-n_assign // MOE_ROWS) + N_EXPERTS + 1
    n_rows = n_blocks * MOE_ROWS
    idx = jnp.concatenate([outs_p[1][:TOP_K], outs_s[1][:TOP_K]], axis=1)
    rank = jnp.concatenate([outs_p[2][:TOP_K], outs_s[2][:TOP_K]], axis=1)
    grow = jnp.concatenate([outs_p[3], outs_s[3]], axis=0)
    counts = outs_s[7][:, 0].astype(jnp.int32)

    padded = (counts + MOE_ROWS - 1) // MOE_ROWS * MOE_ROWS
    end_pad = jnp.cumsum(padded)
    start_pad = end_pad - padded
    experts = jnp.arange(N_EXPERTS, dtype=jnp.int32)
    start_of = jnp.sum(jnp.where(idx[:, :, None] == experts, start_pad, 0), axis=-1)
    dest = (start_of + rank).reshape(n_assign)
    blk_start = jnp.arange(n_blocks, dtype=jnp.int32) * MOE_ROWS
    blk_e = jnp.minimum(jnp.sum((blk_start[:, None] >= end_pad[None, :]).astype(jnp.int32), axis=1),
                        N_EXPERTS - 1)
    n_used = (end_pad[-1] // MOE_ROWS).astype(jnp.int32).reshape(1)

    blk_ids = jnp.arange(n_blocks, dtype=jnp.int32)
    changed = jnp.concatenate([jnp.ones((1,), bool), blk_e[1:] != blk_e[:-1]])
    first = changed & (blk_ids < n_used[0])
    wslot = (jnp.cumsum(first.astype(jnp.int32)) - 1) % 2
    later_first = first[None, :] & (blk_ids[None, :] > blk_ids[:, None])
    nxt_blk = jnp.min(jnp.where(later_first, blk_ids[None, :], n_blocks), axis=1)
    nxt_e = jnp.sum(jnp.where(nxt_blk[:, None] == blk_ids[None, :], blk_e[None, :], 0), axis=1)
    nxt_e = jnp.where(nxt_blk < n_blocks, nxt_e, -1)

    ext = jnp.arange(n_rows + MOE_ROWS, dtype=jnp.int32)
    row = ext - MOE_ROWS
    holds = jnp.any((row[:, None] >= start_pad[None, :]) & (row[:, None] < (start_pad + counts)[None, :]),
                    axis=1)
    spare = jnp.where(row < 0, n_assign + 2 * MOE_ROWS + ext, n_assign + (row & (2 * MOE_ROWS - 1)))
    scattered = _invmap_sc_call(dest + MOE_ROWS, total=total, n_out=n_rows + MOE_ROWS)
    inv = jnp.where(holds, scattered, spare)
    ytok = _moe_call(blk_e, first.astype(jnp.int32), wslot, nxt_e, n_used, inv, x1t.reshape(-1, 8, LANES),
                     p["w_gu"], p["b_gu"][:, None, :], p["w_down"], p["b_down"][:, None, :],
                     n_slots=n_assign)
    yp, ys = _combine_call(x1t, grow, ytok, row2(p["ln2_g"]), row2(p["ln2_b"]),
                           prompt_shape=xp.shape, sample_shape=xs.shape, alpha=alpha)

    def batch_major(v, nblk):
        return jnp.transpose(v, (0, 2, 1, 3)).reshape(nblk * v.shape[2], v.shape[1], D_MODEL)

    states_p = (outs_p[4].reshape(bp, D_MODEL), batch_major(outs_p[5], 1), batch_major(outs_p[6], 1))
    states_s = (outs_s[4].reshape(bs, D_MODEL), batch_major(outs_s[5], nblk_s), batch_major(outs_s[6], nblk_s))
    return yp, ys, states_p, states_s


def kernel(x_prompt, x_sample, state_rglru_h, state_rglru_conv, state_shortconv, w_in, b_in, conv_a_w, conv_a_b, lru_wa, lru_ba, lru_wx, lru_bx, lru_lambda, conv_b_w, w_out, ln1_g, ln1_b, router_w, router_b, w_gu, b_gu, w_down, b_down, ln2_g, ln2_b):
    depth = w_in.shape[0]
    alpha = (2.0 * depth) ** 0.25
    names = ("w_in", "b_in", "conv_a_w", "conv_a_b", "lru_wa", "lru_ba", "lru_wx", "lru_bx", "lru_lambda",
             "conv_b_w", "w_out", "ln1_g", "ln1_b", "router_w", "router_b", "w_gu", "b_gu", "w_down",
             "b_down", "ln2_g", "ln2_b")
    stacked = (w_in, b_in, conv_a_w, conv_a_b, lru_wa, lru_ba, lru_wx, lru_bx, lru_lambda, conv_b_w, w_out,
               ln1_g, ln1_b, router_w, router_b, w_gu, b_gu, w_down, b_down, ln2_g, ln2_b)
    xp, xs = x_prompt, x_sample
    hp_l, cp_l, sp_l, hs_l, cs_l, ss_l = [], [], [], [], [], []
    for l in range(depth):
        p = {n: v[l] for n, v in zip(names, stacked)}
        xp, xs, (hp, cp, sp), (hs, cs, ss) = _layer(
            xp, xs, state_rglru_h[l], state_rglru_conv[l], state_shortconv[l], p, alpha=alpha)
        hp_l.append(hp); cp_l.append(cp); sp_l.append(sp)
        hs_l.append(hs); cs_l.append(cs); ss_l.append(ss)
    return (xp, xs, jnp.stack(hp_l), jnp.stack(cp_l), jnp.stack(sp_l), jnp.stack(hs_l), jnp.stack(cs_l),
            jnp.stack(ss_l))
```

```python
import functools

import jax
import jax.numpy as jnp
from jax import lax
from jax.experimental import pallas as pl
from jax.experimental.pallas import tpu as pltpu
from jax.experimental.pallas import tpu_sc as plsc

D_MODEL = 1024
LRU_HEADS = 16
LRU_BLOCK = D_MODEL // LRU_HEADS
LRU_C = 8.0
CONV_A_WIDTH = 4
CONV_B_WIDTH = 3
N_GROUPS = 7
N_EXPERTS = 32
TOP_K = 4
D_FF = D_MODEL
SWIGLU_LIMIT = 7.0
SWIGLU_ALPHA = 1.702
LN_EPS = 1e-5

GATE_TILE = 256
HEADS_PER_TILE = GATE_TILE // LRU_BLOCK
N_GATE_TILES = D_MODEL // GATE_TILE
LANES = 128
MIXER_ROWS = 512
MOE_ROWS = 256
COMBINE_ROWS = 512
YROW = 4
VMEM_LIMIT = 58 * 1024 * 1024

_F32 = jnp.float32
_BF16 = jnp.bfloat16
_NT = (((1,), (1,)), ((), ()))


def _sigmoid(v):
    return 0.5 * jnp.tanh(0.5 * v) + 0.5


def _twice_gelu_tanh(v):
    c = 0.7978845608028654
    return v * (1.0 + jnp.tanh(v * (c + (c * 0.044715) * (v * v))))


def _layernorm(z, g, b):
    mu = jnp.mean(z, axis=-1, keepdims=True)
    zc = z - mu
    var = jnp.mean(zc * zc, axis=-1, keepdims=True)
    return zc * lax.rsqrt(var + LN_EPS) * g + b


def _mixer_kernel(x_ref, h0_ref, ca0_ref, cb0_ref, cnt0_ref, x1_buf_ref,
                  w_in_ref, b_in_ref, wca_ref, bca_ref, wa_ref, ba_ref, wx_ref, bx_ref, lam_ref,
                  wcb_ref, w_out_ref, g1_ref, be1_ref, rwt_hi_ref, rwt_lo_ref, rb_ref, tri_ref,
                  x1_ref, idx_ref, rank_ref, grow_ref, hl_ref, ca_ref, cb_ref, cnt_ref,
                  xa_s, u_s, a_s, b_s, h_s, hst_s, cnt_s, *maybe_xin, ts, nb, alpha, batch_major_x):
    i = pl.program_id(0)
    j = pl.program_id(1)
    rows = ts * nb
    ta = (CONV_A_WIDTH - 1) * nb
    tb = (CONV_B_WIDTH - 1) * nb

    @pl.when(j == 0)
    def _():
        hst_s[...] = h0_ref[...]
        xa_s[0:ta, :] = ca0_ref[...].reshape(ta, D_MODEL)
        u_s[0:tb, :] = cb0_ref[...].reshape(tb, D_MODEL)

    @pl.when((i == 0) & (j == 0))
    def _():
        cnt_s[...] = cnt0_ref[...]

    if batch_major_x:
        xin_s, xsem, x1_o, osem = maybe_xin
        nt = pl.num_programs(1)

        def chunk_copies(c, slot):
            return [pltpu.make_async_copy(x_ref.at[bb, pl.ds(c * ts, ts), :],
                                          xin_s.at[slot, :, bb, :], xsem.at[slot]) for bb in range(nb)]

        @pl.when(j == 0)
        def _():
            for cp in chunk_copies(0, 0):
                cp.start()

        @pl.when(j + 1 < nt)
        def _():
            for cp in chunk_copies(j + 1, (j + 1) % 2):
                cp.start()
        for cp in chunk_copies(j, j % 2):
            cp.wait()
        x = xin_s[j % 2].reshape(rows, D_MODEL)

        tile_rows = rows * 8

        def out_copy(c, sl):
            return pltpu.make_async_copy(
                x1_o.at[sl], x1_ref.at[pl.ds(pl.multiple_of(c * tile_rows, tile_rows), tile_rows), :],
                osem.at[sl])

        @pl.when(j == 0)
        def _():
            x1_o[1] = jnp.zeros((tile_rows, LANES), _F32)
            lo = nt * tile_rows
            tails = []
            while lo < x1_ref.shape[0]:
                n = min(tile_rows, x1_ref.shape[0] - lo)
                tails.append(pltpu.make_async_copy(
                    x1_o.at[1, pl.ds(0, n), :], x1_ref.at[pl.ds(lo, n), :], osem.at[1]))
                lo += n
            for cp in tails:
                cp.start()
            for cp in tails:
                cp.wait()

        @pl.when(j >= 1)
        def _():
            out_copy(j - 1, (j - 1) % 2).start()

        @pl.when(j >= 2)
        def _():
            out_copy(0, j % 2).wait()
    else:
        x = x_ref[...].reshape(rows, D_MODEL)
    xb = x.astype(_BF16)

    def proj(g):
        lo, hi = g * D_MODEL, (g + 1) * D_MODEL
        return jnp.dot(xb, w_in_ref[:, lo:hi], preferred_element_type=_F32) + b_in_ref[:, lo:hi]

    xa_s[ta:ta + rows, :] = proj(0)
    xc = bca_ref[...] + xa_s[0:rows, :] * wca_ref[0:1, :]
    for k in range(1, CONV_A_WIDTH):
        xc = xc + xa_s[k * nb:k * nb + rows, :] * wca_ref[k:k + 1, :]
    new_ta = xa_s[rows:rows + ta, :]
    xa_s[0:ta, :] = new_ta
    ca_ref[...] = new_ta.reshape(CONV_A_WIDTH - 1, nb, D_MODEL)

    xcb = xc.astype(_BF16)

    def block_diag(w_ref):
        return jnp.concatenate(
            [jnp.dot(xcb[:, q * GATE_TILE:(q + 1) * GATE_TILE], w_ref[q], preferred_element_type=_F32)
             for q in range(N_GATE_TILES)], axis=-1)

    t_r = jnp.tanh(block_diag(wa_ref) + ba_ref[...])
    ig = 0.5 * jnp.tanh(block_diag(wx_ref) + bx_ref[...]) + 0.5
    nlam = -lam_ref[...]
    softplus = jnp.maximum(nlam, 0.0) + jnp.log1p(jnp.exp(-jnp.abs(nlam)))
    half_c = (-0.5 * LRU_C) * softplus
    log_a = half_c * t_r + half_c
    a = jnp.exp(log_a)
    a_s[...] = a
    z = -jnp.tanh(log_a) * (a * a + 1.0)
    b_s[...] = jnp.where(z > 0.0, z * lax.rsqrt(z), 0.0) * (ig * xc)

    h = hst_s[...]
    for t in range(ts):
        sl = slice(t * nb, (t + 1) * nb)
        h = a_s[sl, :] * h + b_s[sl, :]
        h_s[sl, :] = h
    hst_s[...] = h
    hl_ref[...] = h

    a_s[...] = h_s[...] * _twice_gelu_tanh(proj(1))

    u_s[tb:tb + rows, :] = proj(3) * proj(4)
    uc = u_s[0:rows, :] * wcb_ref[0:1, :]
    for k in range(1, CONV_B_WIDTH):
        uc = uc + u_s[k * nb:k * nb + rows, :] * wcb_ref[k:k + 1, :]
    new_tb = u_s[rows:rows + tb, :]
    u_s[0:tb, :] = new_tb
    cb_ref[...] = new_tb.reshape(CONV_B_WIDTH - 1, nb, D_MODEL)
    y_b = proj(2) * uc

    merged = (jnp.tanh(proj(5)) + 1.0) * a_s[...] + (jnp.tanh(proj(6)) + 1.0) * y_b
    mixed = jnp.dot(merged.astype(_BF16), w_out_ref[...], preferred_element_type=_F32)
    x1 = _layernorm(alpha * x + mixed, g1_ref[...], be1_ref[...])
    x1_dst = x1_o.at[j % 2] if batch_major_x else x1_ref
    for s in range(8):
        x1_dst[pl.ds(s, rows, stride=8), :] = x1[:, s * LANES:(s + 1) * LANES]

    x1_hi = x1.astype(_BF16)
    x1_lo = (x1 - x1_hi.astype(_F32)).astype(_BF16)
    logits = (lax.dot_general(rwt_hi_ref[...], x1_hi, _NT, preferred_element_type=_F32)
              + lax.dot_general(rwt_hi_ref[...], x1_lo, _NT, preferred_element_type=_F32)
              + lax.dot_general(rwt_lo_ref[...], x1_hi, _NT, preferred_element_type=_F32)
              + rb_ref[:, 0:1])
    e_iota = lax.broadcasted_iota(jnp.int32, (N_EXPERTS, rows), 0)
    work = logits
    vals, sels, idxs = [], [], []
    for _ in range(TOP_K):
        m = jnp.max(work, axis=0, keepdims=True)
        ik = jnp.min(jnp.where(work == m, e_iota, N_EXPERTS), axis=0, keepdims=True)
        sel = e_iota == ik
        work = jnp.where(sel, -jnp.inf, work)
        vals.append(m)
        sels.append(sel)
        idxs.append(ik)
    exps = [jnp.exp(v - vals[0]) for v in vals]
    denom = exps[0] + exps[1] + exps[2] + exps[3]
    gates = [ex / denom for ex in exps]

    onehot = jnp.zeros((N_EXPERTS, rows), _F32)
    for sel in sels:
        onehot = onehot + sel.astype(_F32)
    prefix = jnp.dot(onehot.astype(_BF16), tri_ref[...], preferred_element_type=_F32)
    pos = prefix + cnt_s[:, 0:1]
    ranks = [jnp.sum(jnp.where(sel, pos, 0.0), axis=0, keepdims=True) for sel in sels]
    new_cnt = cnt_s[...] + jnp.sum(onehot, axis=1, keepdims=True)
    cnt_s[...] = new_cnt
    cnt_ref[...] = new_cnt

    row8 = lax.broadcasted_iota(jnp.int32, (8, rows), 0)
    idx8 = jnp.zeros((8, rows), jnp.int32)
    rank8 = jnp.zeros((8, rows), jnp.int32)
    for k in range(TOP_K):
        idx8 = jnp.where(row8 == k, idxs[k], idx8)
        rank8 = jnp.where(row8 == k, ranks[k].astype(jnp.int32), rank8)
    idx_ref[...] = idx8
    rank_ref[...] = rank8

    row_l = lax.broadcasted_iota(jnp.int32, (LANES, rows), 0)
    g_t = jnp.zeros((LANES, rows), _F32)
    for k in range(TOP_K):
        g_t = jnp.where(row_l == k, gates[k], g_t)
    grow_ref[...] = jnp.transpose(g_t)

    if batch_major_x:
        @pl.when(j == nt - 1)
        def _():
            out_copy(j, j % 2).start()
            out_copy(0, j % 2).wait()
            if nt > 1:
                out_copy(0, (j + 1) % 2).wait()


def _const_spec(shape):
    nd = len(shape)
    return pl.BlockSpec(shape, lambda i, j: (0,) * nd, pipeline_mode=pl.Buffered(1))


def _mixer_call(x4, h0, ca0, cb0, cnt0, x1_buf, wts, *, ts, blk_off, alpha, batch_major_x=False,
                x1_rows=None):
    if batch_major_x:
        (nb, seq, _), nblk = x4.shape, 1
    else:
        nblk, seq, nb, _ = x4.shape
    nt = seq // ts
    rows = ts * nb
    total = nblk * seq * nb
    f32 = lambda *s: jax.ShapeDtypeStruct(s, _F32)
    i32 = lambda *s: jax.ShapeDtypeStruct(s, jnp.int32)
    in_specs = [
        (pl.BlockSpec(memory_space=pl.ANY) if batch_major_x
         else pl.BlockSpec((None, ts, nb, D_MODEL), lambda i, j: (i, j, 0, 0))),
        pl.BlockSpec((None, nb, D_MODEL), lambda i, j: (i, 0, 0)),
        pl.BlockSpec((None, CONV_A_WIDTH - 1, nb, D_MODEL), lambda i, j: (i, 0, 0, 0)),
        pl.BlockSpec((None, CONV_B_WIDTH - 1, nb, D_MODEL), lambda i, j: (i, 0, 0, 0)),
        _const_spec(cnt0.shape),
        pl.BlockSpec(memory_space=pl.ANY),
    ] + [_const_spec(w.shape) for w in wts]
    out_shape = (
        f32(*((x1_rows, LANES) if batch_major_x else x1_buf.shape)),
        i32(8, total),
        i32(8, total),
        f32(total, LANES),
        f32(nblk, nb, D_MODEL),
        f32(nblk, CONV_A_WIDTH - 1, nb, D_MODEL),
        f32(nblk, CONV_B_WIDTH - 1, nb, D_MODEL),
        f32(N_EXPERTS, LANES),
    )
    out_specs = (
        (pl.BlockSpec(memory_space=pl.ANY) if batch_major_x
         else pl.BlockSpec((rows * 8, LANES), lambda i, j: (blk_off + i * nt + j, 0))),
        pl.BlockSpec((8, rows), lambda i, j: (0, i * nt + j)),
        pl.BlockSpec((8, rows), lambda i, j: (0, i * nt + j)),
        pl.BlockSpec((rows, LANES), lambda i, j: (i * nt + j, 0)),
        pl.BlockSpec((None, nb, D_MODEL), lambda i, j: (i, 0, 0)),
        pl.BlockSpec((None, CONV_A_WIDTH - 1, nb, D_MODEL), lambda i, j: (i, 0, 0, 0)),
        pl.BlockSpec((None, CONV_B_WIDTH - 1, nb, D_MODEL), lambda i, j: (i, 0, 0, 0)),
        pl.BlockSpec((N_EXPERTS, LANES), lambda i, j: (0, 0)),
    )
    scratch = [
        pltpu.VMEM((rows + (CONV_A_WIDTH - 1) * nb, D_MODEL), _F32),
        pltpu.VMEM((rows + (CONV_B_WIDTH - 1) * nb, D_MODEL), _F32),
        pltpu.VMEM((rows, D_MODEL), _F32),
        pltpu.VMEM((rows, D_MODEL), _F32),
        pltpu.VMEM((rows, D_MODEL), _F32),
        pltpu.VMEM((nb, D_MODEL), _F32),
        pltpu.VMEM((N_EXPERTS, LANES), _F32),
    ]
    if batch_major_x:
        scratch += [pltpu.VMEM((2, ts, nb, D_MODEL), _F32), pltpu.SemaphoreType.DMA((2,)),
                    pltpu.VMEM((2, rows * 8, LANES), _F32), pltpu.SemaphoreType.DMA((2,))]
    return pl.pallas_call(
        functools.partial(_mixer_kernel, ts=ts, nb=nb, alpha=alpha, batch_major_x=batch_major_x),
        grid=(nblk, nt),
        in_specs=in_specs,
        out_specs=out_specs,
        out_shape=out_shape,
        scratch_shapes=scratch,
        input_output_aliases={} if batch_major_x else {5: 0},
        compiler_params=pltpu.CompilerParams(
            dimension_semantics=("arbitrary", "arbitrary"), vmem_limit_bytes=VMEM_LIMIT),
        name="mixer",
    )(x4, h0, ca0, cb0, cnt0, x1_buf, *wts)


SC_WORKERS = 8
SC_WINDOW = 128
SC_LANES = 16


def _invmap_sc_call(dest, *, total, n_out):
    n = dest.shape[0]
    per_worker = n // SC_WORKERS
    n_windows = per_worker // SC_WINDOW
    assert per_worker * SC_WORKERS == n and n_windows * SC_WINDOW == per_worker
    assert total % per_worker == 0
    mesh = plsc.VectorSubcoreMesh(core_axis_name="c", subcore_axis_name="s")

    @functools.partial(
        pl.kernel, mesh=mesh,
        out_type=jax.ShapeDtypeStruct((SC_WORKERS, n_out), jnp.int32),
        scratch_types=[pltpu.VMEM((n_out,), jnp.int32),
                       pltpu.VMEM((per_worker,), jnp.int32),
                       pltpu.SemaphoreType.DMA],
        compiler_params=pltpu.CompilerParams(use_tc_tiling_on_sc=False, needs_layout_passes=False),
        name="invmap_sc")
    def scatter(dest_hbm, out_hbm, loc_v, idx_v, sem):
        worker = lax.axis_index("s") * 2 + lax.axis_index("c")

        @pl.when(worker < SC_WORKERS)
        def _():
            first = worker * per_worker
            k = first // total
            lane = lax.iota(jnp.int32, SC_LANES)
            load = pltpu.make_async_copy(dest_hbm.at[pl.ds(first, per_worker)], idx_v, sem)
            load.start()
            empty = jnp.full((SC_LANES,), -1, jnp.int32)
            unroll = 8

            @pl.loop(0, n_out // (SC_LANES * unroll))
            def _(i):
                for u in range(unroll):
                    loc_v[pl.ds((i * unroll + u) * SC_LANES, SC_LANES)] = empty
            load.wait()

            @pl.loop(0, per_worker // SC_LANES)
            def _(j):
                idx = idx_v[pl.ds(j * SC_LANES, SC_LANES)]
                tok = first - k * total + j * SC_LANES + lane
                plsc.store_scatter(loc_v, [idx], tok * TOP_K + k)
            pltpu.sync_copy(loc_v, out_hbm.at[worker])

    assert n_out % (SC_LANES * 8) == 0
    return jnp.max(scatter(dest), axis=0)


def _moe_kernel(blk_e_ref, first_ref, wslot_ref, nxt_e_ref, n_used_ref, inv_ref,
                x1_hbm, wgu_hbm, bgu_ref, wd_hbm, bd_ref,
                ytok_hbm,
                xb0, xb1, xb2, yb0, yb1, yb2, wgu_f, wd_f, wgu_s, wd_s, bgu_s, bd_s,
                gsem, ssem, wsem, *, n_slots):
    n_used = n_used_ref[0]
    n_blk = blk_e_ref.shape[0]
    xbufs = (xb0, xb1, xb2)
    ybufs = (yb0, yb1, yb2)

    def gather_copy(v, s, r):
        return pltpu.make_async_copy(
            x1_hbm.at[v >> 2], xbufs[s].at[pl.ds(r * 8, 8), :], gsem.at[s])

    def scatter_copy(v, s, r):
        return pltpu.make_async_copy(
            ybufs[s].at[pl.ds(r * YROW, YROW), :],
            ytok_hbm.at[v >> 1, pl.ds(pl.multiple_of((v & 1) * YROW, YROW), YROW), :], ssem.at[s])

    def weight_copies(e, ws):
        return (pltpu.make_async_copy(wgu_hbm.at[e], wgu_f.at[ws], wsem.at[ws]),
                pltpu.make_async_copy(wd_hbm.at[e], wd_f.at[ws], wsem.at[ws]))

    def wait_rows(copy_fn, s):
        for r in range(MOE_ROWS):
            copy_fn(0, s, r).wait()

    for cp in weight_copies(blk_e_ref[0], 0):
        cp.start(priority=1)
    for yb in ybufs:
        yb[...] = jnp.zeros(yb.shape, jnp.uint32)

    def prime(r, c):
        for s in range(2):
            spare = n_slots + s * MOE_ROWS + r
            pltpu.make_async_copy(
                ybufs[s].at[pl.ds(pl.multiple_of(r * YROW, YROW), YROW), :],
                ytok_hbm.at[spare >> 1, pl.ds(pl.multiple_of((spare & 1) * YROW, YROW), YROW), :],
                ssem.at[s]).start()
            pltpu.make_async_copy(
                x1_hbm.at[inv_ref[(s + 1) * MOE_ROWS + r] >> 2],
                xbufs[s].at[pl.ds(pl.multiple_of(r * 8, 8), 8), :], gsem.at[s]).start()
        return c
    lax.fori_loop(0, MOE_ROWS, prime, 0)

    def run_block(b, s):
        nxt = (s + 2) % 3
        wait_rows(gather_copy, s)
        wait_rows(scatter_copy, s)
        x = jnp.concatenate(
            [xbufs[s][pl.ds(q, MOE_ROWS, stride=8), :] for q in range(8)], axis=-1)
        gbase = (jnp.minimum(b + 2, n_blk - 1) + 1) * MOE_ROWS
        sbase = b * MOE_ROWS
        for r in range(MOE_ROWS):
            gather_copy(inv_ref[gbase + r], nxt, r).start()
        for r in range(MOE_ROWS):
            scatter_copy(inv_ref[sbase + r], nxt, r).start(priority=r % 2)
        gu = jnp.dot(x.astype(_BF16), wgu_s[...], preferred_element_type=_F32) + bgu_s[...]
        gate = jnp.minimum(gu[:, :D_FF], SWIGLU_LIMIT)
        up = jnp.clip(gu[:, D_FF:], -SWIGLU_LIMIT, SWIGLU_LIMIT)
        hmid = (up + 1.0) * (gate * _sigmoid(SWIGLU_ALPHA * gate))
        y = jnp.dot(hmid.astype(_BF16), wd_s[...], preferred_element_type=_F32) + bd_s[...]
        bits = lax.bitcast_convert_type(y.astype(_BF16).astype(_F32), jnp.uint32)
        packed = (bits[:, :D_MODEL // 2] & jnp.uint32(0xFFFF0000)) | (bits[:, D_MODEL // 2:] >> 16)
        for q in range(YROW):
            ybufs[s][pl.ds(q, MOE_ROWS, stride=YROW), :] = packed[:, q * LANES:(q + 1) * LANES]

    def body(b, c):
        @pl.when(first_ref[b] == 1)
        def _():
            e = blk_e_ref[b]
            ws = wslot_ref[b]
            for cp in weight_copies(e, ws):
                cp.wait()
            nxt_e = nxt_e_ref[b]

            @pl.when(nxt_e >= 0)
            def _():
                for cp in weight_copies(nxt_e, 1 - ws):
                    cp.start(priority=1)
            wgu_s[...] = wgu_f[ws].astype(_BF16)
            wd_s[...] = wd_f[ws].astype(_BF16)
            bgu_s[...] = bgu_ref[e]
            bd_s[...] = bd_ref[e]

        for s in range(3):
            pl.when(b % 3 == s)(functools.partial(run_block, b, s))
        return c

    lax.fori_loop(0, n_used + 1, body, 0)

    for d in (1, 2):
        for s in range(3):
            @pl.when((n_used + d) % 3 == s)
            def _():
                wait_rows(gather_copy, s)
                wait_rows(scatter_copy, s)


def _moe_call(blk_e, first, wslot, nxt_e, n_used, inv, x1t, w_gu, b_gu, w_down, b_down, *, n_slots):
    full = lambda shape: pl.BlockSpec(shape, lambda i, *_: (0,) * len(shape))
    grid_spec = pltpu.PrefetchScalarGridSpec(
        num_scalar_prefetch=6,
        grid=(1,),
        in_specs=[
            pl.BlockSpec(memory_space=pl.ANY),
            pl.BlockSpec(memory_space=pl.ANY),
            full(b_gu.shape),
            pl.BlockSpec(memory_space=pl.ANY),
            full(b_down.shape),
        ],
        out_specs=pl.BlockSpec(memory_space=pl.ANY),
        scratch_shapes=[pltpu.VMEM((MOE_ROWS * 8, LANES), _F32)] * 3
        + [pltpu.VMEM((MOE_ROWS * YROW, LANES), jnp.uint32)] * 3 + [
            pltpu.VMEM((2, D_MODEL, 2 * D_FF), _F32),
            pltpu.VMEM((2, D_FF, D_MODEL), _F32),
            pltpu.VMEM((D_MODEL, 2 * D_FF), _BF16),
            pltpu.VMEM((D_FF, D_MODEL), _BF16),
            pltpu.VMEM((1, 2 * D_FF), _F32),
            pltpu.VMEM((1, D_MODEL), _F32),
            pltpu.SemaphoreType.DMA((3,)),
            pltpu.SemaphoreType.DMA((3,)),
            pltpu.SemaphoreType.DMA((2,)),
        ],
    )
    return pl.pallas_call(
        functools.partial(_moe_kernel, n_slots=n_slots),
        grid_spec=grid_spec,
        out_shape=jax.ShapeDtypeStruct(((n_slots + 3 * MOE_ROWS) // 2, 8, LANES), jnp.uint32),
        compiler_params=pltpu.CompilerParams(
            dimension_semantics=("arbitrary",), vmem_limit_bytes=VMEM_LIMIT),
        name="moe",
    )(blk_e, first, wslot, nxt_e, n_used, inv, x1t, w_gu, b_gu, w_down, b_down)


def _combine_kernel(x1_ref, g_ref, y01_ref, y23_ref, g2_ref, be2_ref,
                    yp_hbm, ys_hbm, y_o, osem, *, alpha, n_p, bp, ts_p, nb_s, ss):
    i = pl.program_id(0)
    nt = pl.num_programs(0)
    rows = g_ref.shape[0]

    def prompt_copies(step, sl):
        return [pltpu.make_async_copy(y_o.at[sl, :, bb, :],
                                      yp_hbm.at[bb, pl.ds(step * ts_p, ts_p), :], osem.at[sl])
                for bb in range(bp)]

    def sample_copies(step, sl):
        flat = y_o.at[sl].reshape(rows, D_MODEL)
        return [pltpu.make_async_copy(flat.at[pl.ds(t * nb_s, nb_s), :],
                                      ys_hbm.at[pl.ds((step - n_p) * nb_s, nb_s), t, :], osem.at[sl])
                for t in range(ss)]

    def start_step(step, sl):
        @pl.when(step < n_p)
        def _():
            for cp in prompt_copies(step, sl):
                cp.start()

        @pl.when(step >= n_p)
        def _():
            for cp in sample_copies(step, sl):
                cp.start()

    def wait_step(sl):
        for cp in prompt_copies(0, sl):
            cp.wait()

    @pl.when(i >= 1)
    def _():
        start_step(i - 1, (i - 1) % 2)

    @pl.when(i >= 2)
    def _():
        wait_step(i % 2)

    g = g_ref[...]

    def token_major(ref):
        return jnp.concatenate([ref[pl.ds(s, rows, stride=8), :] for s in range(8)], axis=-1)

    def unpack(ref, half):
        u = jnp.concatenate([ref[pl.ds(half * YROW + q, rows, stride=8), :] for q in range(YROW)], axis=-1)
        hi = lax.bitcast_convert_type(u & jnp.uint32(0xFFFF0000), _F32)
        lo = lax.bitcast_convert_type(u << 16, _F32)
        return jnp.concatenate([hi, lo], axis=-1)

    pairs = [r.reshape(rows * 8, LANES) for r in (y01_ref, y23_ref)]
    ys = [unpack(pairs[k // 2], k % 2) for k in range(TOP_K)]
    moe = g[:, 0:1] * ys[0] + g[:, 1:2] * ys[1] + g[:, 2:3] * ys[2] + g[:, 3:4] * ys[3]
    y = _layernorm(alpha * token_major(x1_ref) + moe, g2_ref[...], be2_ref[...])
    y_o[i % 2] = y.reshape(ts_p, bp, D_MODEL)

    @pl.when(i == nt - 1)
    def _():
        start_step(i, i % 2)
        wait_step(i % 2)
        wait_step((i + 1) % 2)


def _combine_call(x1t, grow, ytok, g2, be2, *, prompt_shape, sample_shape, alpha):
    bp, sp, _ = prompt_shape
    bs, ss, _ = sample_shape
    ts_p = COMBINE_ROWS // bp
    nb_s = COMBINE_ROWS // ss
    n_p = sp // ts_p
    nt = n_p + bs // nb_s
    assert nt >= 2 and bp == ss
    pair = lambda h: pl.BlockSpec((COMBINE_ROWS, None, 8, LANES), lambda i, h=h: (i, h, 0, 0))
    vec = pl.BlockSpec((1, D_MODEL), lambda i: (0, 0))
    ytok4 = ytok.reshape(-1, TOP_K // 2, 8, LANES)
    return pl.pallas_call(
        functools.partial(_combine_kernel, alpha=alpha, n_p=n_p, bp=bp, ts_p=ts_p, nb_s=nb_s, ss=ss),
        grid=(nt,),
        in_specs=[pl.BlockSpec((COMBINE_ROWS * 8, LANES), lambda i: (i, 0)),
                  pl.BlockSpec((COMBINE_ROWS, LANES), lambda i: (i, 0)),
                  pair(0), pair(1), vec, vec],
        out_specs=(pl.BlockSpec(memory_space=pl.ANY), pl.BlockSpec(memory_space=pl.ANY)),
        out_shape=(jax.ShapeDtypeStruct(prompt_shape, _F32), jax.ShapeDtypeStruct(sample_shape, _F32)),
        scratch_shapes=[pltpu.VMEM((2, ts_p, bp, D_MODEL), _F32), pltpu.SemaphoreType.DMA((2,))],
        compiler_params=pltpu.CompilerParams(
            dimension_semantics=("arbitrary",), vmem_limit_bytes=VMEM_LIMIT),
        name="combine",
    )(x1t, grow, ytok4, ytok4, g2, be2)


def _pack_block_diag(w):
    w = w.reshape(N_GATE_TILES, HEADS_PER_TILE, LRU_BLOCK, LRU_BLOCK)
    eye = jnp.eye(HEADS_PER_TILE, dtype=w.dtype)
    t = jnp.einsum("qhij,hg->qhigj", w, eye)
    return t.reshape(N_GATE_TILES, GATE_TILE, GATE_TILE)


def _layer(xp, xs, h_s0, ca_s0, cb_s0, p, *, alpha):
    bp, sp, _ = xp.shape
    bs, ss, _ = xs.shape
    tp, tsm = bp * sp, bs * ss
    total = tp + tsm
    row2 = lambda v: v.reshape(1, -1)

    rwt = jnp.transpose(p["router_w"])
    rwt_hi = rwt.astype(_BF16)
    rwt_lo = (rwt - rwt_hi.astype(_F32)).astype(_BF16)
    ii = jnp.arange(MIXER_ROWS)
    tri = (ii[:, None] < ii[None, :]).astype(_BF16)
    col_scale = jnp.where(jnp.arange(N_GROUPS * D_MODEL) >= (N_GROUPS - 2) * D_MODEL, 0.5, 1.0)
    wts = (
        (p["w_in"] * col_scale).astype(_BF16), row2(p["b_in"] * col_scale), p["conv_a_w"], row2(p["conv_a_b"]),
        _pack_block_diag(0.5 * p["lru_wa"]).astype(_BF16), row2(0.5 * p["lru_ba"]),
        _pack_block_diag(0.5 * p["lru_wx"]).astype(_BF16), row2(0.5 * p["lru_bx"]), row2(p["lru_lambda"]),
        2.0 * p["conv_b_w"], (0.25 * p["w_out"]).astype(_BF16), row2(p["ln1_g"]), row2(p["ln1_b"]),
        rwt_hi, rwt_lo, jnp.broadcast_to(p["router_b"][:, None], (N_EXPERTS, LANES)), tri,
    )

    ts_p = MIXER_ROWS // bp
    zeros = lambda *s: jnp.zeros(s, _F32)
    n_assign = total * TOP_K
    n_spare = 3 * MOE_ROWS
    outs_p = _mixer_call(xp, zeros(1, bp, D_MODEL), zeros(1, CONV_A_WIDTH - 1, bp, D_MODEL),
                         zeros(1, CONV_B_WIDTH - 1, bp, D_MODEL), zeros(N_EXPERTS, LANES), zeros(8, LANES), wts,
                         ts=ts_p, blk_off=0, alpha=alpha, batch_major_x=True,
                         x1_rows=(total + n_spare // TOP_K) * 8)
    nb_s = MIXER_ROWS // ss
    nblk_s = bs // nb_s
    xs4 = jnp.transpose(xs.reshape(nblk_s, nb_s, ss, D_MODEL), (0, 2, 1, 3))
    h0 = h_s0.reshape(nblk_s, nb_s, D_MODEL)
    ca0 = jnp.transpose(ca_s0.reshape(nblk_s, nb_s, CONV_A_WIDTH - 1, D_MODEL), (0, 2, 1, 3))
    cb0 = jnp.transpose(cb_s0.reshape(nblk_s, nb_s, CONV_B_WIDTH - 1, D_MODEL), (0, 2, 1, 3))
    outs_s = _mixer_call(xs4, h0, ca0, cb0, outs_p[7], outs_p[0], wts,
                         ts=ss, blk_off=tp // MIXER_ROWS, alpha=alpha)
    x1t = outs_s[0]

    n_blocks = -(-n_assign // MOE_ROWS) + N_EXPERTS + 1
    n_rows = n_blocks * MOE_ROWS
    idx = jnp.concatenate([outs_p[1][:TOP_K], outs_s[1][:TOP_K]], axis=1)
    rank = jnp.concatenate([outs_p[2][:TOP_K], outs_s[2][:TOP_K]], axis=1)
    grow = jnp.concatenate([outs_p[3], outs_s[3]], axis=0)
    counts = outs_s[7][:, 0].astype(jnp.int32)

    padded = (counts + MOE_ROWS - 1) // MOE_ROWS * MOE_ROWS
    end_pad = jnp.cumsum(padded)
    start_pad = end_pad - padded
    experts = jnp.arange(N_EXPERTS, dtype=jnp.int32)
    start_of = jnp.sum(jnp.where(idx[:, :, None] == experts, start_pad, 0), axis=-1)
    dest = (start_of + rank).reshape(n_assign)
    blk_start = jnp.arange(n_blocks, dtype=jnp.int32) * MOE_ROWS
    blk_e = jnp.minimum(jnp.sum((blk_start[:, None] >= end_pad[None, :]).astype(jnp.int32), axis=1),
                        N_EXPERTS - 1)
    n_used = (end_pad[-1] // MOE_ROWS).astype(jnp.int32).reshape(1)

    blk_ids = jnp.arange(n_blocks, dtype=jnp.int32)
    changed = jnp.concatenate([jnp.ones((1,), bool), blk_e[1:] != blk_e[:-1]])
    first = changed & (blk_ids < n_used[0])
    wslot = (jnp.cumsum(first.astype(jnp.int32)) - 1) % 2
    later_first = first[None, :] & (blk_ids[None, :] > blk_ids[:, None])
    nxt_blk = jnp.min(jnp.where(later_first, blk_ids[None, :], n_blocks), axis=1)
    nxt_e = jnp.sum(jnp.where(nxt_blk[:, None] == blk_ids[None, :], blk_e[None, :], 0), axis=1)
    nxt_e = jnp.where(nxt_blk < n_blocks, nxt_e, -1)

    ext = jnp.arange(n_rows + MOE_ROWS, dtype=jnp.int32)
    row = ext - MOE_ROWS
    holds = jnp.any((row[:, None] >= start_pad[None, :]) & (row[:, None] < (start_pad + counts)[None, :]),
                    axis=1)
    spare = jnp.where(row < 0, n_assign + 2 * MOE_ROWS + ext, n_assign + (row & (2 * MOE_ROWS - 1)))
    scattered = _invmap_sc_call(dest + MOE_ROWS, total=total, n_out=n_rows + MOE_ROWS)
    inv = jnp.where(holds, scattered, spare)
    ytok = _moe_call(blk_e, first.astype(jnp.int32), wslot, nxt_e, n_used, inv, x1t.reshape(-1, 8, LANES),
                     p["w_gu"], p["b_gu"][:, None, :], p["w_down"], p["b_down"][:, None, :],
                     n_slots=n_assign)
    yp, ys = _combine_call(x1t, grow, ytok, row2(p["ln2_g"]), row2(p["ln2_b"]),
                           prompt_shape=xp.shape, sample_shape=xs.shape, alpha=alpha)

    def batch_major(v, nblk):
        return jnp.transpose(v, (0, 2, 1, 3)).reshape(nblk * v.shape[2], v.shape[1], D_MODEL)

    states_p = (outs_p[4].reshape(bp, D_MODEL), batch_major(outs_p[5], 1), batch_major(outs_p[6], 1))
    states_s = (outs_s[4].reshape(bs, D_MODEL), batch_major(outs_s[5], nblk_s), batch_major(outs_s[6], nblk_s))
    return yp, ys, states_p, states_s


def kernel(x_prompt, x_sample, state_rglru_h, state_rglru_conv, state_shortconv, w_in, b_in, conv_a_w, conv_a_b, lru_wa, lru_ba, lru_wx, lru_bx, lru_lambda, conv_b_w, w_out, ln1_g, ln1_b, router_w, router_b, w_gu, b_gu, w_down, b_down, ln2_g, ln2_b):
    depth = w_in.shape[0]
    alpha = (2.0 * depth) ** 0.25
    names = ("w_in", "b_in", "conv_a_w", "conv_a_b", "lru_wa", "lru_ba", "lru_wx", "lru_bx", "lru_lambda",
             "conv_b_w", "w_out", "ln1_g", "ln1_b", "router_w", "router_b", "w_gu", "b_gu", "w_down",
             "b_down", "ln2_g", "ln2_b")
    stacked = (w_in, b_in, conv_a_w, conv_a_b, lru_wa, lru_ba, lru_wx, lru_bx, lru_lambda, conv_b_w, w_out,
               ln1_g, ln1_b, router_w, router_b, w_gu, b_gu, w_down, b_down, ln2_g, ln2_b)
    xp, xs = x_prompt, x_sample
    hp_l, cp_l, sp_l, hs_l, cs_l, ss_l = [], [], [], [], [], []
    for l in range(depth):
        p = {n: v[l] for n, v in zip(names, stacked)}
        xp, xs, (hp, cp, sp), (hs, cs, ss) = _layer(
            xp, xs, state_rglru_h[l], state_rglru_conv[l], state_shortconv[l], p, alpha=alpha)
        hp_l.append(hp); cp_l.append(cp); sp_l.append(sp)
        hs_l.append(hs); cs_l.append(cs); ss_l.append(ss)
    return (xp, xs, jnp.stack(hp_l), jnp.stack(cp_l), jnp.stack(sp_l), jnp.stack(hs_l), jnp.stack(cs_l),
            jnp.stack(ss_l))
```

```python
import functools

import jax
import jax.numpy as jnp
from jax import lax
from jax.experimental import pallas as pl
from jax.experimental.pallas import tpu as pltpu
from jax.experimental.pallas import tpu_sc as plsc

D_MODEL = 1024
LRU_HEADS = 16
LRU_BLOCK = D_MODEL // LRU_HEADS
LRU_C = 8.0
CONV_A_WIDTH = 4
CONV_B_WIDTH = 3
N_GROUPS = 7
N_EXPERTS = 32
TOP_K = 4
D_FF = D_MODEL
SWIGLU_LIMIT = 7.0
SWIGLU_ALPHA = 1.702
LN_EPS = 1e-5

GATE_TILE = 256
HEADS_PER_TILE = GATE_TILE // LRU_BLOCK
N_GATE_TILES = D_MODEL // GATE_TILE
LANES = 128
MIXER_ROWS = 512
MOE_ROWS = 256
COMBINE_ROWS = 512
YROW = 4
VMEM_LIMIT = 58 * 1024 * 1024

_F32 = jnp.float32
_BF16 = jnp.bfloat16
_NT = (((1,), (1,)), ((), ()))


def _sigmoid(v):
    return 0.5 * jnp.tanh(0.5 * v) + 0.5


def _twice_gelu_tanh(v):
    c = 0.7978845608028654
    return v * (1.0 + jnp.tanh(v * (c + (c * 0.044715) * (v * v))))


def _layernorm(z, g, b):
    mu = jnp.mean(z, axis=-1, keepdims=True)
    zc = z - mu
    var = jnp.mean(zc * zc, axis=-1, keepdims=True)
    return zc * lax.rsqrt(var + LN_EPS) * g + b


def _mixer_kernel(x_ref, h0_ref, ca0_ref, cb0_ref, cnt0_ref, x1_buf_ref,
                  w_in_ref, b_in_ref, wca_ref, bca_ref, wa_ref, ba_ref, wx_ref, bx_ref, lam_ref,
                  wcb_ref, w_out_ref, g1_ref, be1_ref, rwt_hi_ref, rwt_lo_ref, rb_ref, tri_ref,
                  x1_ref, idx_ref, rank_ref, grow_ref, hl_ref, ca_ref, cb_ref, cnt_ref,
                  xa_s, u_s, a_s, b_s, h_s, hst_s, cnt_s, *maybe_xin, ts, nb, alpha, batch_major_x):
    i = pl.program_id(0)
    j = pl.program_id(1)
    rows = ts * nb
    ta = (CONV_A_WIDTH - 1) * nb
    tb = (CONV_B_WIDTH - 1) * nb

    @pl.when(j == 0)
    def _():
        hst_s[...] = h0_ref[...]
        xa_s[0:ta, :] = ca0_ref[...].reshape(ta, D_MODEL)
        u_s[0:tb, :] = cb0_ref[...].reshape(tb, D_MODEL)

    @pl.when((i == 0) & (j == 0))
    def _():
        cnt_s[...] = cnt0_ref[...]

    if batch_major_x:
        xin_s, xsem, x1_o, osem = maybe_xin
        nt = pl.num_programs(1)

        def chunk_copies(c, slot):
            return [pltpu.make_async_copy(x_ref.at[bb, pl.ds(c * ts, ts), :],
                                          xin_s.at[slot, :, bb, :], xsem.at[slot]) for bb in range(nb)]

        @pl.when(j == 0)
        def _():
            for cp in chunk_copies(0, 0):
                cp.start()

        @pl.when(j + 1 < nt)
        def _():
            for cp in chunk_copies(j + 1, (j + 1) % 2):
                cp.start()
        for cp in chunk_copies(j, j % 2):
            cp.wait()
        x = xin_s[j % 2].reshape(rows, D_MODEL)

        tile_rows = rows * 8

        def out_copy(c, sl):
            return pltpu.make_async_copy(
                x1_o.at[sl], x1_ref.at[pl.ds(pl.multiple_of(c * tile_rows, tile_rows), tile_rows), :],
                osem.at[sl])

        @pl.when(j == 0)
        def _():
            x1_o[1] = jnp.zeros((tile_rows, LANES), _F32)
            lo = nt * tile_rows
            tails = []
            while lo < x1_ref.shape[0]:
                n = min(tile_rows, x1_ref.shape[0] - lo)
                tails.append(pltpu.make_async_copy(
                    x1_o.at[1, pl.ds(0, n), :], x1_ref.at[pl.ds(lo, n), :], osem.at[1]))
                lo += n
            for cp in tails:
                cp.start()
            for cp in tails:
                cp.wait()

        @pl.when(j >= 1)
        def _():
            out_copy(j - 1, (j - 1) % 2).start()

        @pl.when(j >= 2)
        def _():
            out_copy(0, j % 2).wait()
    else:
        x = x_ref[...].reshape(rows, D_MODEL)
    xb = x.astype(_BF16)

    def proj(g):
        lo, hi = g * D_MODEL, (g + 1) * D_MODEL
        return jnp.dot(xb, w_in_ref[:, lo:hi], preferred_element_type=_F32) + b_in_ref[:, lo:hi]

    xa_s[ta:ta + rows, :] = proj(0)
    xc = bca_ref[...] + xa_s[0:rows, :] * wca_ref[0:1, :]
    for k in range(1, CONV_A_WIDTH):
        xc = xc + xa_s[k * nb:k * nb + rows, :] * wca_ref[k:k + 1, :]
    new_ta = xa_s[rows:rows + ta, :]
    xa_s[0:ta, :] = new_ta
    ca_ref[...] = new_ta.reshape(CONV_A_WIDTH - 1, nb, D_MODEL)

    xcb = xc.astype(_BF16)

    def block_diag(w_ref):
        return jnp.concatenate(
            [jnp.dot(xcb[:, q * GATE_TILE:(q + 1) * GATE_TILE], w_ref[q], preferred_element_type=_F32)
             for q in range(N_GATE_TILES)], axis=-1)

    t_r = jnp.tanh(block_diag(wa_ref) + ba_ref[...])
    ig = 0.5 * jnp.tanh(block_diag(wx_ref) + bx_ref[...]) + 0.5
    nlam = -lam_ref[...]
    softplus = jnp.maximum(nlam, 0.0) + jnp.log1p(jnp.exp(-jnp.abs(nlam)))
    half_c = (-0.5 * LRU_C) * softplus
    log_a = half_c * t_r + half_c
    a = jnp.exp(log_a)
    a_s[...] = a
    z = -jnp.tanh(log_a) * (a * a + 1.0)
    b_s[...] = jnp.where(z > 0.0, z * lax.rsqrt(z), 0.0) * (ig * xc)

    h = hst_s[...]
    for t in range(ts):
        sl = slice(t * nb, (t + 1) * nb)
        h = a_s[sl, :] * h + b_s[sl, :]
        h_s[sl, :] = h
    hst_s[...] = h
    hl_ref[...] = h

    a_s[...] = h_s[...] * _twice_gelu_tanh(proj(1))

    u_s[tb:tb + rows, :] = proj(3) * proj(4)
    uc = u_s[0:rows, :] * wcb_ref[0:1, :]
    for k in range(1, CONV_B_WIDTH):
        uc = uc + u_s[k * nb:k * nb + rows, :] * wcb_ref[k:k + 1, :]
    new_tb = u_s[rows:rows + tb, :]
    u_s[0:tb, :] = new_tb
    cb_ref[...] = new_tb.reshape(CONV_B_WIDTH - 1, nb, D_MODEL)
    y_b = proj(2) * uc

    merged = (jnp.tanh(proj(5)) + 1.0) * a_s[...] + (jnp.tanh(proj(6)) + 1.0) * y_b
    mixed = jnp.dot(merged.astype(_BF16), w_out_ref[...], preferred_element_type=_F32)
    x1 = _layernorm(alpha * x + mixed, g1_ref[...], be1_ref[...])
    x1_dst = x1_o.at[j % 2] if batch_major_x else x1_ref
    for s in range(8):
        x1_dst[pl.ds(s, rows, stride=8), :] = x1[:, s * LANES:(s + 1) * LANES]

    x1_hi = x1.astype(_BF16)
    x1_lo = (x1 - x1_hi.astype(_F32)).astype(_BF16)
    logits = (lax.dot_general(rwt_hi_ref[...], x1_hi, _NT, preferred_element_type=_F32)
              + lax.dot_general(rwt_hi_ref[...], x1_lo, _NT, preferred_element_type=_F32)
              + lax.dot_general(rwt_lo_ref[...], x1_hi, _NT, preferred_element_type=_F32)
              + rb_ref[:, 0:1])
    e_iota = lax.broadcasted_iota(jnp.int32, (N_EXPERTS, rows), 0)
    work = logits
    vals, sels, idxs = [], [], []
    for _ in range(TOP_K):
        m = jnp.max(work, axis=0, keepdims=True)
        ik = jnp.min(jnp.where(work == m, e_iota, N_EXPERTS), axis=0, keepdims=True)
        sel = e_iota == ik
        work = jnp.where(sel, -jnp.inf, work)
        vals.append(m)
        sels.append(sel)
        idxs.append(ik)
    exps = [jnp.exp(v - vals[0]) for v in vals]
    denom = exps[0] + exps[1] + exps[2] + exps[3]
    gates = [ex / denom for ex in exps]

    onehot = jnp.zeros((N_EXPERTS, rows), _F32)
    for sel in sels:
        onehot = onehot + sel.astype(_F32)
    prefix = jnp.dot(onehot.astype(_BF16), tri_ref[...], preferred_element_type=_F32)
    pos = prefix + cnt_s[:, 0:1]
    ranks = [jnp.sum(jnp.where(sel, pos, 0.0), axis=0, keepdims=True) for sel in sels]
    new_cnt = cnt_s[...] + jnp.sum(onehot, axis=1, keepdims=True)
    cnt_s[...] = new_cnt
    cnt_ref[...] = new_cnt

    row8 = lax.broadcasted_iota(jnp.int32, (8, rows), 0)
    idx8 = jnp.zeros((8, rows), jnp.int32)
    rank8 = jnp.zeros((8, rows), jnp.int32)
    for k in range(TOP_K):
        idx8 = jnp.where(row8 == k, idxs[k], idx8)
        rank8 = jnp.where(row8 == k, ranks[k].astype(jnp.int32), rank8)
    idx_ref[...] = idx8
    rank_ref[...] = rank8

    row_l = lax.broadcasted_iota(jnp.int32, (LANES, rows), 0)
    g_t = jnp.zeros((LANES, rows), _F32)
    for k in range(TOP_K):
        g_t = jnp.where(row_l == k, gates[k], g_t)
    grow_ref[...] = jnp.transpose(g_t)

    if batch_major_x:
        @pl.when(j == nt - 1)
        def _():
            out_copy(j, j % 2).start()
            out_copy(0, j % 2).wait()
            if nt > 1:
                out_copy(0, (j + 1) % 2).wait()


def _const_spec(shape):
    nd = len(shape)
    return pl.BlockSpec(shape, lambda i, j: (0,) * nd, pipeline_mode=pl.Buffered(1))


def _mixer_call(x4, h0, ca0, cb0, cnt0, x1_buf, wts, *, ts, blk_off, alpha, batch_major_x=False,
                x1_rows=None):
    if batch_major_x:
        (nb, seq, _), nblk = x4.shape, 1
    else:
        nblk, seq, nb, _ = x4.shape
    nt = seq // ts
    rows = ts * nb
    total = nblk * seq * nb
    f32 = lambda *s: jax.ShapeDtypeStruct(s, _F32)
    i32 = lambda *s: jax.ShapeDtypeStruct(s, jnp.int32)
    in_specs = [
        (pl.BlockSpec(memory_space=pl.ANY) if batch_major_x
         else pl.BlockSpec((None, ts, nb, D_MODEL), lambda i, j: (i, j, 0, 0))),
        pl.BlockSpec((None, nb, D_MODEL), lambda i, j: (i, 0, 0)),
        pl.BlockSpec((None, CONV_A_WIDTH - 1, nb, D_MODEL), lambda i, j: (i, 0, 0, 0)),
        pl.BlockSpec((None, CONV_B_WIDTH - 1, nb, D_MODEL), lambda i, j: (i, 0, 0, 0)),
        _const_spec(cnt0.shape),
        pl.BlockSpec(memory_space=pl.ANY),
    ] + [_const_spec(w.shape) for w in wts]
    out_shape = (
        f32(*((x1_rows, LANES) if batch_major_x else x1_buf.shape)),
        i32(8, total),
        i32(8, total),
        f32(total, LANES),
        f32(nblk, nb, D_MODEL),
        f32(nblk, CONV_A_WIDTH - 1, nb, D_MODEL),
        f32(nblk, CONV_B_WIDTH - 1, nb, D_MODEL),
        f32(N_EXPERTS, LANES),
    )
    out_specs = (
        (pl.BlockSpec(memory_space=pl.ANY) if batch_major_x
         else pl.BlockSpec((rows * 8, LANES), lambda i, j: (blk_off + i * nt + j, 0))),
        pl.BlockSpec((8, rows), lambda i, j: (0, i * nt + j)),
        pl.BlockSpec((8, rows), lambda i, j: (0, i * nt + j)),
        pl.BlockSpec((rows, LANES), lambda i, j: (i * nt + j, 0)),
        pl.BlockSpec((None, nb, D_MODEL), lambda i, j: (i, 0, 0)),
        pl.BlockSpec((None, CONV_A_WIDTH - 1, nb, D_MODEL), lambda i, j: (i, 0, 0, 0)),
        pl.BlockSpec((None, CONV_B_WIDTH - 1, nb, D_MODEL), lambda i, j: (i, 0, 0, 0)),
        pl.BlockSpec((N_EXPERTS, LANES), lambda i, j: (0, 0)),
    )
    scratch = [
        pltpu.VMEM((rows + (CONV_A_WIDTH - 1) * nb, D_MODEL), _F32),
        pltpu.VMEM((rows + (CONV_B_WIDTH - 1) * nb, D_MODEL), _F32),
        pltpu.VMEM((rows, D_MODEL), _F32),
        pltpu.VMEM((rows, D_MODEL), _F32),
        pltpu.VMEM((rows, D_MODEL), _F32),
        pltpu.VMEM((nb, D_MODEL), _F32),
        pltpu.VMEM((N_EXPERTS, LANES), _F32),
    ]
    if batch_major_x:
        scratch += [pltpu.VMEM((2, ts, nb, D_MODEL), _F32), pltpu.SemaphoreType.DMA((2,)),
                    pltpu.VMEM((2, rows * 8, LANES), _F32), pltpu.SemaphoreType.DMA((2,))]
    return pl.pallas_call(
        functools.partial(_mixer_kernel, ts=ts, nb=nb, alpha=alpha, batch_major_x=batch_major_x),
        grid=(nblk, nt),
        in_specs=in_specs,
        out_specs=out_specs,
        out_shape=out_shape,
        scratch_shapes=scratch,
        input_output_aliases={} if batch_major_x else {5: 0},
        compiler_params=pltpu.CompilerParams(
            dimension_semantics=("arbitrary", "arbitrary"), vmem_limit_bytes=VMEM_LIMIT),
        name="mixer",
    )(x4, h0, ca0, cb0, cnt0, x1_buf, *wts)


SC_WORKERS = 8
SC_WINDOW = 128
SC_LANES = 16


def _invmap_sc_call(dest, *, total, n_out):
    n = dest.shape[0]
    per_worker = n // SC_WORKERS
    n_windows = per_worker // SC_WINDOW
    assert per_worker * SC_WORKERS == n and n_windows * SC_WINDOW == per_worker
    assert total % per_worker == 0
    mesh = plsc.VectorSubcoreMesh(core_axis_name="c", subcore_axis_name="s")

    @functools.partial(
        pl.kernel, mesh=mesh,
        out_type=jax.ShapeDtypeStruct((SC_WORKERS, n_out), jnp.int32),
        scratch_types=[pltpu.VMEM((n_out,), jnp.int32),
                       pltpu.VMEM((per_worker,), jnp.int32),
                       pltpu.SemaphoreType.DMA],
        compiler_params=pltpu.CompilerParams(use_tc_tiling_on_sc=False, needs_layout_passes=False),
        name="invmap_sc")
    def scatter(dest_hbm, out_hbm, loc_v, idx_v, sem):
        worker = lax.axis_index("s") * 2 + lax.axis_index("c")

        @pl.when(worker < SC_WORKERS)
        def _():
            first = worker * per_worker
            k = first // total
            lane = lax.iota(jnp.int32, SC_LANES)
            load = pltpu.make_async_copy(dest_hbm.at[pl.ds(first, per_worker)], idx_v, sem)
            load.start()
            empty = jnp.full((SC_LANES,), -1, jnp.int32)
            unroll = 8

            @pl.loop(0, n_out // (SC_LANES * unroll))
            def _(i):
                for u in range(unroll):
                    loc_v[pl.ds((i * unroll + u) * SC_LANES, SC_LANES)] = empty
            load.wait()

            @pl.loop(0, per_worker // SC_LANES)
            def _(j):
                idx = idx_v[pl.ds(j * SC_LANES, SC_LANES)]
                tok = first - k * total + j * SC_LANES + lane
                plsc.store_scatter(loc_v, [idx], tok * TOP_K + k)
            pltpu.sync_copy(loc_v, out_hbm.at[worker])

    assert n_out % (SC_LANES * 8) == 0
    return jnp.max(scatter(dest), axis=0)


def _moe_kernel(blk_e_ref, first_ref, wslot_ref, nxt_e_ref, n_used_ref, inv_ref,
                x1_hbm, wgu_hbm, bgu_ref, wd_hbm, bd_ref,
                ytok_hbm,
                xb0, xb1, xb2, yb0, yb1, yb2, wgu_f, wd_f, wgu_s, wd_s, bgu_s, bd_s,
                gsem, ssem, wsem, *, n_slots):
    n_used = n_used_ref[0]
    n_blk = blk_e_ref.shape[0]
    xbufs = (xb0, xb1, xb2)
    ybufs = (yb0, yb1, yb2)

    def gather_copy(v, s, r):
        return pltpu.make_async_copy(
            x1_hbm.at[v >> 2], xbufs[s].at[pl.ds(r * 8, 8), :], gsem.at[s])

    def scatter_copy(v, s, r):
        return pltpu.make_async_copy(
            ybufs[s].at[pl.ds(r * YROW, YROW), :],
            ytok_hbm.at[v >> 1, pl.ds(pl.multiple_of((v & 1) * YROW, YROW), YROW), :], ssem.at[s])

    def weight_copies(e, ws):
        return (pltpu.make_async_copy(wgu_hbm.at[e], wgu_f.at[ws], wsem.at[ws]),
                pltpu.make_async_copy(wd_hbm.at[e], wd_f.at[ws], wsem.at[ws]))

    def wait_rows(copy_fn, s):
        for r in range(MOE_ROWS):
            copy_fn(0, s, r).wait()

    for cp in weight_copies(blk_e_ref[0], 0):
        cp.start(priority=1)
    for yb in ybufs:
        yb[...] = jnp.zeros(yb.shape, jnp.uint32)

    def prime(r, c):
        for s in range(2):
            spare = n_slots + s * MOE_ROWS + r
            pltpu.make_async_copy(
                ybufs[s].at[pl.ds(pl.multiple_of(r * YROW, YROW), YROW), :],
                ytok_hbm.at[spare >> 1, pl.ds(pl.multiple_of((spare & 1) * YROW, YROW), YROW), :],
                ssem.at[s]).start()
            pltpu.make_async_copy(
                x1_hbm.at[inv_ref[(s + 1) * MOE_ROWS + r] >> 2],
                xbufs[s].at[pl.ds(pl.multiple_of(r * 8, 8), 8), :], gsem.at[s]).start()
        return c
    lax.fori_loop(0, MOE_ROWS, prime, 0)

    def run_block(b, s):
        nxt = (s + 2) % 3
        wait_rows(gather_copy, s)
        wait_rows(scatter_copy, s)
        x = jnp.concatenate(
            [xbufs[s][pl.ds(q, MOE_ROWS, stride=8), :] for q in range(8)], axis=-1)
        gbase = (jnp.minimum(b + 2, n_blk - 1) + 1) * MOE_ROWS
        sbase = b * MOE_ROWS
        for r in range(MOE_ROWS):
            gather_copy(inv_ref[gbase + r], nxt, r).start()
        for r in range(MOE_ROWS):
            scatter_copy(inv_ref[sbase + r], nxt, r).start(priority=1)
        gu = jnp.dot(x.astype(_BF16), wgu_s[...], preferred_element_type=_F32) + bgu_s[...]
        gate = jnp.minimum(gu[:, :D_FF], SWIGLU_LIMIT)
        up = jnp.clip(gu[:, D_FF:], -SWIGLU_LIMIT, SWIGLU_LIMIT)
        hmid = (up + 1.0) * (gate * _sigmoid(SWIGLU_ALPHA * gate))
        y = jnp.dot(hmid.astype(_BF16), wd_s[...], preferred_element_type=_F32) + bd_s[...]
        bits = lax.bitcast_convert_type(y.astype(_BF16).astype(_F32), jnp.uint32)
        packed = (bits[:, :D_MODEL // 2] & jnp.uint32(0xFFFF0000)) | (bits[:, D_MODEL // 2:] >> 16)
        for q in range(YROW):
            ybufs[s][pl.ds(q, MOE_ROWS, stride=YROW), :] = packed[:, q * LANES:(q + 1) * LANES]

    def body(b, c):
        @pl.when(first_ref[b] == 1)
        def _():
            e = blk_e_ref[b]
            ws = wslot_ref[b]
            for cp in weight_copies(e, ws):
                cp.wait()
            nxt_e = nxt_e_ref[b]

            @pl.when(nxt_e >= 0)
            def _():
                for cp in weight_copies(nxt_e, 1 - ws):
                    cp.start(priority=1)
            wgu_s[...] = wgu_f[ws].astype(_BF16)
            wd_s[...] = wd_f[ws].astype(_BF16)
            bgu_s[...] = bgu_ref[e]
            bd_s[...] = bd_ref[e]

        for s in range(3):
            pl.when(b % 3 == s)(functools.partial(run_block, b, s))
        return c

    lax.fori_loop(0, n_used + 1, body, 0)

    for d in (1, 2):
        for s in range(3):
            @pl.when((n_used + d) % 3 == s)
            def _():
                wait_rows(gather_copy, s)
                wait_rows(scatter_copy, s)


def _moe_call(blk_e, first, wslot, nxt_e, n_used, inv, x1t, w_gu, b_gu, w_down, b_down, *, n_slots):
    full = lambda shape: pl.BlockSpec(shape, lambda i, *_: (0,) * len(shape))
    grid_spec = pltpu.PrefetchScalarGridSpec(
        num_scalar_prefetch=6,
        grid=(1,),
        in_specs=[
            pl.BlockSpec(memory_space=pl.ANY),
            pl.BlockSpec(memory_space=pl.ANY),
            full(b_gu.shape),
            pl.BlockSpec(memory_space=pl.ANY),
            full(b_down.shape),
        ],
        out_specs=pl.BlockSpec(memory_space=pl.ANY),
        scratch_shapes=[pltpu.VMEM((MOE_ROWS * 8, LANES), _F32)] * 3
        + [pltpu.VMEM((MOE_ROWS * YROW, LANES), jnp.uint32)] * 3 + [
            pltpu.VMEM((2, D_MODEL, 2 * D_FF), _F32),
            pltpu.VMEM((2, D_FF, D_MODEL), _F32),
            pltpu.VMEM((D_MODEL, 2 * D_FF), _BF16),
            pltpu.VMEM((D_FF, D_MODEL), _BF16),
            pltpu.VMEM((1, 2 * D_FF), _F32),
            pltpu.VMEM((1, D_MODEL), _F32),
            pltpu.SemaphoreType.DMA((3,)),
            pltpu.SemaphoreType.DMA((3,)),
            pltpu.SemaphoreType.DMA((2,)),
        ],
    )
    return pl.pallas_call(
        functools.partial(_moe_kernel, n_slots=n_slots),
        grid_spec=grid_spec,
        out_shape=jax.ShapeDtypeStruct(((n_slots + 3 * MOE_ROWS) // 2, 8, LANES), jnp.uint32),
        compiler_params=pltpu.CompilerParams(
            dimension_semantics=("arbitrary",), vmem_limit_bytes=VMEM_LIMIT),
        name="moe",
    )(blk_e, first, wslot, nxt_e, n_used, inv, x1t, w_gu, b_gu, w_down, b_down)


def _combine_kernel(x1_ref, g_ref, y01_ref, y23_ref, g2_ref, be2_ref,
                    yp_hbm, ys_hbm, y_o, osem, *, alpha, n_p, bp, ts_p, nb_s, ss):
    i = pl.program_id(0)
    nt = pl.num_programs(0)
    rows = g_ref.shape[0]

    def prompt_copies(step, sl):
        return [pltpu.make_async_copy(y_o.at[sl, :, bb, :],
                                      yp_hbm.at[bb, pl.ds(step * ts_p, ts_p), :], osem.at[sl])
                for bb in range(bp)]

    def sample_copies(step, sl):
        flat = y_o.at[sl].reshape(rows, D_MODEL)
        return [pltpu.make_async_copy(flat.at[pl.ds(t * nb_s, nb_s), :],
                                      ys_hbm.at[pl.ds((step - n_p) * nb_s, nb_s), t, :], osem.at[sl])
                for t in range(ss)]

    def start_step(step, sl):
        @pl.when(step < n_p)
        def _():
            for cp in prompt_copies(step, sl):
                cp.start()

        @pl.when(step >= n_p)
        def _():
            for cp in sample_copies(step, sl):
                cp.start()

    def wait_step(sl):
        for cp in prompt_copies(0, sl):
            cp.wait()

    @pl.when(i >= 1)
    def _():
        start_step(i - 1, (i - 1) % 2)

    @pl.when(i >= 2)
    def _():
        wait_step(i % 2)

    g = g_ref[...]

    def token_major(ref):
        return jnp.concatenate([ref[pl.ds(s, rows, stride=8), :] for s in range(8)], axis=-1)

    def unpack(ref, half):
        u = jnp.concatenate([ref[pl.ds(half * YROW + q, rows, stride=8), :] for q in range(YROW)], axis=-1)
        hi = lax.bitcast_convert_type(u & jnp.uint32(0xFFFF0000), _F32)
        lo = lax.bitcast_convert_type(u << 16, _F32)
        return jnp.concatenate([hi, lo], axis=-1)

    pairs = [r.reshape(rows * 8, LANES) for r in (y01_ref, y23_ref)]
    ys = [unpack(pairs[k // 2], k % 2) for k in range(TOP_K)]
    moe = g[:, 0:1] * ys[0] + g[:, 1:2] * ys[1] + g[:, 2:3] * ys[2] + g[:, 3:4] * ys[3]
    y = _layernorm(alpha * token_major(x1_ref) + moe, g2_ref[...], be2_ref[...])
    y_o[i % 2] = y.reshape(ts_p, bp, D_MODEL)

    @pl.when(i == nt - 1)
    def _():
        start_step(i, i % 2)
        wait_step(i % 2)
        wait_step((i + 1) % 2)


def _combine_call(x1t, grow, ytok, g2, be2, *, prompt_shape, sample_shape, alpha):
    bp, sp, _ = prompt_shape
    bs, ss, _ = sample_shape
    ts_p = COMBINE_ROWS // bp
    nb_s = COMBINE_ROWS // ss
    n_p = sp // ts_p
    nt = n_p + bs // nb_s
    assert nt >= 2 and bp == ss
    pair = lambda h: pl.BlockSpec((COMBINE_ROWS, None, 8, LANES), lambda i, h=h: (i, h, 0, 0))
    vec = pl.BlockSpec((1, D_MODEL), lambda i: (0, 0))
    ytok4 = ytok.reshape(-1, TOP_K // 2, 8, LANES)
    return pl.pallas_call(
        functools.partial(_combine_kernel, alpha=alpha, n_p=n_p, bp=bp, ts_p=ts_p, nb_s=nb_s, ss=ss),
        grid=(nt,),
        in_specs=[pl.BlockSpec((COMBINE_ROWS * 8, LANES), lambda i: (i, 0)),
                  pl.BlockSpec((COMBINE_ROWS, LANES), lambda i: (i, 0)),
                  pair(0), pair(1), vec, vec],
        out_specs=(pl.BlockSpec(memory_space=pl.ANY), pl.BlockSpec(memory_space=pl.ANY)),
        out_shape=(jax.ShapeDtypeStruct(prompt_shape, _F32), jax.ShapeDtypeStruct(sample_shape, _F32)),
        scratch_shapes=[pltpu.VMEM((2, ts_p, bp, D_MODEL), _F32), pltpu.SemaphoreType.DMA((2,))],
        compiler_params=pltpu.CompilerParams(
            dimension_semantics=("arbitrary",), vmem_limit_bytes=VMEM_LIMIT),
        name="combine",
    )(x1t, grow, ytok4, ytok4, g2, be2)


def _pack_block_diag(w):
    w = w.reshape(N_GATE_TILES, HEADS_PER_TILE, LRU_BLOCK, LRU_BLOCK)
    eye = jnp.eye(HEADS_PER_TILE, dtype=w.dtype)
    t = jnp.einsum("qhij,hg->qhigj", w, eye)
    return t.reshape(N_GATE_TILES, GATE_TILE, GATE_TILE)


def _layer(xp, xs, h_s0, ca_s0, cb_s0, p, *, alpha):
    bp, sp, _ = xp.shape
    bs, ss, _ = xs.shape
    tp, tsm = bp * sp, bs * ss
    total = tp + tsm
    row2 = lambda v: v.reshape(1, -1)

    rwt = jnp.transpose(p["router_w"])
    rwt_hi = rwt.astype(_BF16)
    rwt_lo = (rwt - rwt_hi.astype(_F32)).astype(_BF16)
    ii = jnp.arange(MIXER_ROWS)
    tri = (ii[:, None] < ii[None, :]).astype(_BF16)
    col_scale = jnp.where(jnp.arange(N_GROUPS * D_MODEL) >= (N_GROUPS - 2) * D_MODEL, 0.5, 1.0)
    wts = (
        (p["w_in"] * col_scale).astype(_BF16), row2(p["b_in"] * col_scale), p["conv_a_w"], row2(p["conv_a_b"]),
        _pack_block_diag(0.5 * p["lru_wa"]).astype(_BF16), row2(0.5 * p["lru_ba"]),
        _pack_block_diag(0.5 * p["lru_wx"]).astype(_BF16), row2(0.5 * p["lru_bx"]), row2(p["lru_lambda"]),
        2.0 * p["conv_b_w"], (0.25 * p["w_out"]).astype(_BF16), row2(p["ln1_g"]), row2(p["ln1_b"]),
        rwt_hi, rwt_lo, jnp.broadcast_to(p["router_b"][:, None], (N_EXPERTS, LANES)), tri,
    )

    ts_p = MIXER_ROWS // bp
    zeros = lambda *s: jnp.zeros(s, _F32)
    n_assign = total * TOP_K
    n_spare = 3 * MOE_ROWS
    outs_p = _mixer_call(xp, zeros(1, bp, D_MODEL), zeros(1, CONV_A_WIDTH - 1, bp, D_MODEL),
                         zeros(1, CONV_B_WIDTH - 1, bp, D_MODEL), zeros(N_EXPERTS, LANES), zeros(8, LANES), wts,
                         ts=ts_p, blk_off=0, alpha=alpha, batch_major_x=True,
                         x1_rows=(total + n_spare // TOP_K) * 8)
    nb_s = MIXER_ROWS // ss
    nblk_s = bs // nb_s
    xs4 = jnp.transpose(xs.reshape(nblk_s, nb_s, ss, D_MODEL), (0, 2, 1, 3))
    h0 = h_s0.reshape(nblk_s, nb_s, D_MODEL)
    ca0 = jnp.transpose(ca_s0.reshape(nblk_s, nb_s, CONV_A_WIDTH - 1, D_MODEL), (0, 2, 1, 3))
    cb0 = jnp.transpose(cb_s0.reshape(nblk_s, nb_s, CONV_B_WIDTH - 1, D_MODEL), (0, 2, 1, 3))
    outs_s = _mixer_call(xs4, h0, ca0, cb0, outs_p[7], outs_p[0], wts,
                         ts=ss, blk_off=tp // MIXER_ROWS, alpha=alpha)
    x1t = outs_s[0]

    n_blocks = -(-n_assign // MOE_ROWS) + N_EXPERTS + 1
    n_rows = n_blocks * MOE_ROWS
    idx = jnp.concatenate([outs_p[1][:TOP_K], outs_s[1][:TOP_K]], axis=1)
    rank = jnp.concatenate([outs_p[2][:TOP_K], outs_s[2][:TOP_K]], axis=1)
    grow = jnp.concatenate([outs_p[3], outs_s[3]], axis=0)
    counts = outs_s[7][:, 0].astype(jnp.int32)

    padded = (counts + MOE_ROWS - 1) // MOE_ROWS * MOE_ROWS
    end_pad = jnp.cumsum(padded)
    start_pad = end_pad - padded
    experts = jnp.arange(N_EXPERTS, dtype=jnp.int32)
    start_of = jnp.sum(jnp.where(idx[:, :, None] == experts, start_pad, 0), axis=-1)
    dest = (start_of + rank).reshape(n_assign)
    blk_start = jnp.arange(n_blocks, dtype=jnp.int32) * MOE_ROWS
    blk_e = jnp.minimum(jnp.sum((blk_start[:, None] >= end_pad[None, :]).astype(jnp.int32), axis=1),
                        N_EXPERTS - 1)
    n_used = (end_pad[-1] // MOE_ROWS).astype(jnp.int32).reshape(1)

    blk_ids = jnp.arange(n_blocks, dtype=jnp.int32)
    changed = jnp.concatenate([jnp.ones((1,), bool), blk_e[1:] != blk_e[:-1]])
    first = changed & (blk_ids < n_used[0])
    wslot = (jnp.cumsum(first.astype(jnp.int32)) - 1) % 2
    later_first = first[None, :] & (blk_ids[None, :] > blk_ids[:, None])
    nxt_blk = jnp.min(jnp.where(later_first, blk_ids[None, :], n_blocks), axis=1)
    nxt_e = jnp.sum(jnp.where(nxt_blk[:, None] == blk_ids[None, :], blk_e[None, :], 0), axis=1)
    nxt_e = jnp.where(nxt_blk < n_blocks, nxt_e, -1)

    ext = jnp.arange(n_rows + MOE_ROWS, dtype=jnp.int32)
    row = ext - MOE_ROWS
    holds = jnp.any((row[:, None] >= start_pad[None, :]) & (row[:, None] < (start_pad + counts)[None, :]),
                    axis=1)
    spare = jnp.where(row < 0, n_assign + 2 * MOE_ROWS + ext, n_assign + (row & (2 * MOE_ROWS - 1)))
    scattered = _invmap_sc_call(dest + MOE_ROWS, total=total, n_out=n_rows + MOE_ROWS)
    inv = jnp.where(holds, scattered, spare)
    ytok = _moe_call(blk_e, first.astype(jnp.int32), wslot, nxt_e, n_used, inv, x1t.reshape(-1, 8, LANES),
                     p["w_gu"], p["b_gu"][:, None, :], p["w_down"], p["b_down"][:, None, :],
                     n_slots=n_assign)
    yp, ys = _combine_call(x1t, grow, ytok, row2(p["ln2_g"]), row2(p["ln2_b"]),
                           prompt_shape=xp.shape, sample_shape=xs.shape, alpha=alpha)

    def batch_major(v, nblk):
        return jnp.transpose(v, (0, 2, 1, 3)).reshape(nblk * v.shape[2], v.shape[1], D_MODEL)

    states_p = (outs_p[4].reshape(bp, D_MODEL), batch_major(outs_p[5], 1), batch_major(outs_p[6], 1))
    states_s = (outs_s[4].reshape(bs, D_MODEL), batch_major(outs_s[5], nblk_s), batch_major(outs_s[6], nblk_s))
    return yp, ys, states_p, states_s


def kernel(x_prompt, x_sample, state_rglru_h, state_rglru_conv, state_shortconv, w_in, b_in, conv_a_w, conv_a_b, lru_wa, lru_ba, lru_wx, lru_bx, lru_lambda, conv_b_w, w_out, ln1_g, ln1_b, router_w, router_b, w_gu, b_gu, w_down, b_down, ln2_g, ln2_b):
    depth = w_in.shape[0]
    alpha = (2.0 * depth) ** 0.25
    names = ("w_in", "b_in", "conv_a_w", "conv_a_b", "lru_wa", "lru_ba", "lru_wx", "lru_bx", "lru_lambda",
             "conv_b_w", "w_out", "ln1_g", "ln1_b", "router_w", "router_b", "w_gu", "b_gu", "w_down",
             "b_down", "ln2_g", "ln2_b")
    stacked = (w_in, b_in, conv_a_w, conv_a_b, lru_wa, lru_ba, lru_wx, lru_bx, lru_lambda, conv_b_w, w_out,
               ln1_g, ln1_b, router_w, router_b, w_gu, b_gu, w_down, b_down, ln2_g, ln2_b)
    xp, xs = x_prompt, x_sample
    hp_l, cp_l, sp_l, hs_l, cs_l, ss_l = [], [], [], [], [], []
    for l in range(depth):
        p = {n: v[l] for n, v in zip(names, stacked)}
        xp, xs, (hp, cp, sp), (hs, cs, ss) = _layer(
            xp, xs, state_rglru_h[l], state_rglru_conv[l], state_shortconv[l], p, alpha=alpha)
        hp_l.append(hp); cp_l.append(cp); sp_l.append(sp)
        hs_l.append(hs); cs_l.append(cs); ss_l.append(ss)
    return (xp, xs, jnp.stack(hp_l), jnp.stack(cp_l), jnp.stack(sp_l), jnp.stack(hs_l), jnp.stack(cs_l),
            jnp.stack(ss_l))
```

```python
import functools

import jax
import jax.numpy as jnp
from jax import lax
from jax.experimental import pallas as pl
from jax.experimental.pallas import tpu as pltpu
from jax.experimental.pallas import tpu_sc as plsc

D_MODEL = 1024
LRU_HEADS = 16
LRU_BLOCK = D_MODEL // LRU_HEADS
LRU_C = 8.0
CONV_A_WIDTH = 4
CONV_B_WIDTH = 3
N_GROUPS = 7
N_EXPERTS = 32
TOP_K = 4
D_FF = D_MODEL
SWIGLU_LIMIT = 7.0
SWIGLU_ALPHA = 1.702
LN_EPS = 1e-5

GATE_TILE = 256
HEADS_PER_TILE = GATE_TILE // LRU_BLOCK
N_GATE_TILES = D_MODEL // GATE_TILE
LANES = 128
MIXER_ROWS = 512
MOE_ROWS = 256
COMBINE_ROWS = 512
YROW = 4
VMEM_LIMIT = 58 * 1024 * 1024

_F32 = jnp.float32
_BF16 = jnp.bfloat16
_NT = (((1,), (1,)), ((), ()))


def _sigmoid(v):
    return 0.5 * jnp.tanh(0.5 * v) + 0.5


def _twice_gelu_tanh(v):
    c = 0.7978845608028654
    return v * (1.0 + jnp.tanh(v * (c + (c * 0.044715) * (v * v))))


def _layernorm(z, g, b):
    mu = jnp.mean(z, axis=-1, keepdims=True)
    zc = z - mu
    var = jnp.mean(zc * zc, axis=-1, keepdims=True)
    return zc * lax.rsqrt(var + LN_EPS) * g + b


def _mixer_kernel(x_ref, h0_ref, ca0_ref, cb0_ref, cnt0_ref, x1_buf_ref,
                  w_in_ref, b_in_ref, wca_ref, bca_ref, wa_ref, ba_ref, wx_ref, bx_ref, lam_ref,
                  wcb_ref, w_out_ref, g1_ref, be1_ref, rwt_hi_ref, rwt_lo_ref, rb_ref, tri_ref,
                  x1_ref, idx_ref, rank_ref, grow_ref, hl_ref, ca_ref, cb_ref, cnt_ref,
                  xa_s, u_s, a_s, b_s, h_s, hst_s, cnt_s, *maybe_xin, ts, nb, alpha, batch_major_x):
    i = pl.program_id(0)
    j = pl.program_id(1)
    rows = ts * nb
    ta = (CONV_A_WIDTH - 1) * nb
    tb = (CONV_B_WIDTH - 1) * nb

    @pl.when(j == 0)
    def _():
        hst_s[...] = h0_ref[...]
        xa_s[0:ta, :] = ca0_ref[...].reshape(ta, D_MODEL)
        u_s[0:tb, :] = cb0_ref[...].reshape(tb, D_MODEL)

    @pl.when((i == 0) & (j == 0))
    def _():
        cnt_s[...] = cnt0_ref[...]

    if batch_major_x:
        xin_s, xsem, x1_o, osem = maybe_xin
        nt = pl.num_programs(1)

        def chunk_copies(c, slot):
            return [pltpu.make_async_copy(x_ref.at[bb, pl.ds(c * ts, ts), :],
                                          xin_s.at[slot, :, bb, :], xsem.at[slot]) for bb in range(nb)]

        @pl.when(j == 0)
        def _():
            for cp in chunk_copies(0, 0):
                cp.start()

        @pl.when(j + 1 < nt)
        def _():
            for cp in chunk_copies(j + 1, (j + 1) % 2):
                cp.start()
        for cp in chunk_copies(j, j % 2):
            cp.wait()
        x = xin_s[j % 2].reshape(rows, D_MODEL)

        tile_rows = rows * 8

        def out_copy(c, sl):
            return pltpu.make_async_copy(
                x1_o.at[sl], x1_ref.at[pl.ds(pl.multiple_of(c * tile_rows, tile_rows), tile_rows), :],
                osem.at[sl])

        @pl.when(j == 0)
        def _():
            x1_o[1] = jnp.zeros((tile_rows, LANES), _F32)
            lo = nt * tile_rows
            tails = []
            while lo < x1_ref.shape[0]:
                n = min(tile_rows, x1_ref.shape[0] - lo)
                tails.append(pltpu.make_async_copy(
                    x1_o.at[1, pl.ds(0, n), :], x1_ref.at[pl.ds(lo, n), :], osem.at[1]))
                lo += n
            for cp in tails:
                cp.start()
            for cp in tails:
                cp.wait()

        @pl.when(j >= 1)
        def _():
            out_copy(j - 1, (j - 1) % 2).start()

        @pl.when(j >= 2)
        def _():
            out_copy(0, j % 2).wait()
    else:
        x = x_ref[...].reshape(rows, D_MODEL)
    xb = x.astype(_BF16)

    def proj(g):
        lo, hi = g * D_MODEL, (g + 1) * D_MODEL
        return jnp.dot(xb, w_in_ref[:, lo:hi], preferred_element_type=_F32) + b_in_ref[:, lo:hi]

    xa_s[ta:ta + rows, :] = proj(0)
    xc = bca_ref[...] + xa_s[0:rows, :] * wca_ref[0:1, :]
    for k in range(1, CONV_A_WIDTH):
        xc = xc + xa_s[k * nb:k * nb + rows, :] * wca_ref[k:k + 1, :]
    new_ta = xa_s[rows:rows + ta, :]
    xa_s[0:ta, :] = new_ta
    ca_ref[...] = new_ta.reshape(CONV_A_WIDTH - 1, nb, D_MODEL)

    xcb = xc.astype(_BF16)

    def block_diag(w_ref):
        return jnp.concatenate(
            [jnp.dot(xcb[:, q * GATE_TILE:(q + 1) * GATE_TILE], w_ref[q], preferred_element_type=_F32)
             for q in range(N_GATE_TILES)], axis=-1)

    t_r = jnp.tanh(block_diag(wa_ref) + ba_ref[...])
    ig = 0.5 * jnp.tanh(block_diag(wx_ref) + bx_ref[...]) + 0.5
    nlam = -lam_ref[...]
    softplus = jnp.maximum(nlam, 0.0) + jnp.log1p(jnp.exp(-jnp.abs(nlam)))
    half_c = (-0.5 * LRU_C) * softplus
    log_a = half_c * t_r + half_c
    a = jnp.exp(log_a)
    a_s[...] = a
    z = -jnp.tanh(log_a) * (a * a + 1.0)
    b_s[...] = jnp.where(z > 0.0, z * lax.rsqrt(z), 0.0) * (ig * xc)

    h = hst_s[...]
    for t in range(ts):
        sl = slice(t * nb, (t + 1) * nb)
        h = a_s[sl, :] * h + b_s[sl, :]
        h_s[sl, :] = h
    hst_s[...] = h
    hl_ref[...] = h

    a_s[...] = h_s[...] * _twice_gelu_tanh(proj(1))

    u_s[tb:tb + rows, :] = proj(3) * proj(4)
    uc = u_s[0:rows, :] * wcb_ref[0:1, :]
    for k in range(1, CONV_B_WIDTH):
        uc = uc + u_s[k * nb:k * nb + rows, :] * wcb_ref[k:k + 1, :]
    new_tb = u_s[rows:rows + tb, :]
    u_s[0:tb, :] = new_tb
    cb_ref[...] = new_tb.reshape(CONV_B_WIDTH - 1, nb, D_MODEL)
    y_b = proj(2) * uc

    merged = (jnp.tanh(proj(5)) + 1.0) * a_s[...] + (jnp.tanh(proj(6)) + 1.0) * y_b
    mixed = jnp.dot(merged.astype(_BF16), w_out_ref[...], preferred_element_type=_F32)
    x1 = _layernorm(alpha * x + mixed, g1_ref[...], be1_ref[...])
    x1_dst = x1_o.at[j % 2] if batch_major_x else x1_ref
    for s in range(8):
        x1_dst[pl.ds(s, rows, stride=8), :] = x1[:, s * LANES:(s + 1) * LANES]

    x1_hi = x1.astype(_BF16)
    x1_lo = (x1 - x1_hi.astype(_F32)).astype(_BF16)
    logits = (lax.dot_general(rwt_hi_ref[...], x1_hi, _NT, preferred_element_type=_F32)
              + lax.dot_general(rwt_hi_ref[...], x1_lo, _NT, preferred_element_type=_F32)
              + lax.dot_general(rwt_lo_ref[...], x1_hi, _NT, preferred_element_type=_F32)
              + rb_ref[:, 0:1])
    e_iota = lax.broadcasted_iota(jnp.int32, (N_EXPERTS, rows), 0)
    work = logits
    vals, sels, idxs = [], [], []
    for _ in range(TOP_K):
        m = jnp.max(work, axis=0, keepdims=True)
        ik = jnp.min(jnp.where(work == m, e_iota, N_EXPERTS), axis=0, keepdims=True)
        sel = e_iota == ik
        work = jnp.where(sel, -jnp.inf, work)
        vals.append(m)
        sels.append(sel)
        idxs.append(ik)
    exps = [jnp.exp(v - vals[0]) for v in vals]
    denom = exps[0] + exps[1] + exps[2] + exps[3]
    gates = [ex / denom for ex in exps]

    onehot = jnp.zeros((N_EXPERTS, rows), _F32)
    for sel in sels:
        onehot = onehot + sel.astype(_F32)
    prefix = jnp.dot(onehot.astype(_BF16), tri_ref[...], preferred_element_type=_F32)
    pos = prefix + cnt_s[:, 0:1]
    ranks = [jnp.sum(jnp.where(sel, pos, 0.0), axis=0, keepdims=True) for sel in sels]
    new_cnt = cnt_s[...] + jnp.sum(onehot, axis=1, keepdims=True)
    cnt_s[...] = new_cnt
    cnt_ref[...] = new_cnt

    row8 = lax.broadcasted_iota(jnp.int32, (8, rows), 0)
    idx8 = jnp.zeros((8, rows), jnp.int32)
    rank8 = jnp.zeros((8, rows), jnp.int32)
    for k in range(TOP_K):
        idx8 = jnp.where(row8 == k, idxs[k], idx8)
        rank8 = jnp.where(row8 == k, ranks[k].astype(jnp.int32), rank8)
    idx_ref[...] = idx8
    rank_ref[...] = rank8

    row_l = lax.broadcasted_iota(jnp.int32, (LANES, rows), 0)
    g_t = jnp.zeros((LANES, rows), _F32)
    for k in range(TOP_K):
        g_t = jnp.where(row_l == k, gates[k], g_t)
    grow_ref[...] = jnp.transpose(g_t)

    if batch_major_x:
        @pl.when(j == nt - 1)
        def _():
            out_copy(j, j % 2).start()
            out_copy(0, j % 2).wait()
            if nt > 1:
                out_copy(0, (j + 1) % 2).wait()


def _const_spec(shape):
    nd = len(shape)
    return pl.BlockSpec(shape, lambda i, j: (0,) * nd, pipeline_mode=pl.Buffered(1))


def _mixer_call(x4, h0, ca0, cb0, cnt0, x1_buf, wts, *, ts, blk_off, alpha, batch_major_x=False,
                x1_rows=None):
    if batch_major_x:
        (nb, seq, _), nblk = x4.shape, 1
    else:
        nblk, seq, nb, _ = x4.shape
    nt = seq // ts
    rows = ts * nb
    total = nblk * seq * nb
    f32 = lambda *s: jax.ShapeDtypeStruct(s, _F32)
    i32 = lambda *s: jax.ShapeDtypeStruct(s, jnp.int32)
    in_specs = [
        (pl.BlockSpec(memory_space=pl.ANY) if batch_major_x
         else pl.BlockSpec((None, ts, nb, D_MODEL), lambda i, j: (i, j, 0, 0))),
        pl.BlockSpec((None, nb, D_MODEL), lambda i, j: (i, 0, 0)),
        pl.BlockSpec((None, CONV_A_WIDTH - 1, nb, D_MODEL), lambda i, j: (i, 0, 0, 0)),
        pl.BlockSpec((None, CONV_B_WIDTH - 1, nb, D_MODEL), lambda i, j: (i, 0, 0, 0)),
        _const_spec(cnt0.shape),
        pl.BlockSpec(memory_space=pl.ANY),
    ] + [_const_spec(w.shape) for w in wts]
    out_shape = (
        f32(*((x1_rows, LANES) if batch_major_x else x1_buf.shape)),
        i32(8, total),
        i32(8, total),
        f32(total, LANES),
        f32(nblk, nb, D_MODEL),
        f32(nblk, CONV_A_WIDTH - 1, nb, D_MODEL),
        f32(nblk, CONV_B_WIDTH - 1, nb, D_MODEL),
        f32(N_EXPERTS, LANES),
    )
    out_specs = (
        (pl.BlockSpec(memory_space=pl.ANY) if batch_major_x
         else pl.BlockSpec((rows * 8, LANES), lambda i, j: (blk_off + i * nt + j, 0))),
        pl.BlockSpec((8, rows), lambda i, j: (0, i * nt + j)),
        pl.BlockSpec((8, rows), lambda i, j: (0, i * nt + j)),
        pl.BlockSpec((rows, LANES), lambda i, j: (i * nt + j, 0)),
        pl.BlockSpec((None, nb, D_MODEL), lambda i, j: (i, 0, 0)),
        pl.BlockSpec((None, CONV_A_WIDTH - 1, nb, D_MODEL), lambda i, j: (i, 0, 0, 0)),
        pl.BlockSpec((None, CONV_B_WIDTH - 1, nb, D_MODEL), lambda i, j: (i, 0, 0, 0)),
        pl.BlockSpec((N_EXPERTS, LANES), lambda i, j: (0, 0)),
    )
    scratch = [
        pltpu.VMEM((rows + (CONV_A_WIDTH - 1) * nb, D_MODEL), _F32),
        pltpu.VMEM((rows + (CONV_B_WIDTH - 1) * nb, D_MODEL), _F32),
        pltpu.VMEM((rows, D_MODEL), _F32),
        pltpu.VMEM((rows, D_MODEL), _F32),
        pltpu.VMEM((rows, D_MODEL), _F32),
        pltpu.VMEM((nb, D_MODEL), _F32),
        pltpu.VMEM((N_EXPERTS, LANES), _F32),
    ]
    if batch_major_x:
        scratch += [pltpu.VMEM((2, ts, nb, D_MODEL), _F32), pltpu.SemaphoreType.DMA((2,)),
                    pltpu.VMEM((2, rows * 8, LANES), _F32), pltpu.SemaphoreType.DMA((2,))]
    return pl.pallas_call(
        functools.partial(_mixer_kernel, ts=ts, nb=nb, alpha=alpha, batch_major_x=batch_major_x),
        grid=(nblk, nt),
        in_specs=in_specs,
        out_specs=out_specs,
        out_shape=out_shape,
        scratch_shapes=scratch,
        input_output_aliases={} if batch_major_x else {5: 0},
        compiler_params=pltpu.CompilerParams(
            dimension_semantics=("arbitrary", "arbitrary"), vmem_limit_bytes=VMEM_LIMIT),
        name="mixer",
    )(x4, h0, ca0, cb0, cnt0, x1_buf, *wts)


SC_WORKERS = 8
SC_WINDOW = 128
SC_LANES = 16


def _invmap_sc_call(dest, *, total, n_out):
    n = dest.shape[0]
    per_worker = n // SC_WORKERS
    n_windows = per_worker // SC_WINDOW
    assert per_worker * SC_WORKERS == n and n_windows * SC_WINDOW == per_worker
    assert total % per_worker == 0
    mesh = plsc.VectorSubcoreMesh(core_axis_name="c", subcore_axis_name="s")

    @functools.partial(
        pl.kernel, mesh=mesh,
        out_type=jax.ShapeDtypeStruct((SC_WORKERS, n_out), jnp.int32),
        scratch_types=[pltpu.VMEM((n_out,), jnp.int32),
                       pltpu.VMEM((per_worker,), jnp.int32),
                       pltpu.SemaphoreType.DMA],
        compiler_params=pltpu.CompilerParams(use_tc_tiling_on_sc=False, needs_layout_passes=False),
        name="invmap_sc")
    def scatter(dest_hbm, out_hbm, loc_v, idx_v, sem):
        worker = lax.axis_index("s") * 2 + lax.axis_index("c")

        @pl.when(worker < SC_WORKERS)
        def _():
            first = worker * per_worker
            k = first // total
            lane = lax.iota(jnp.int32, SC_LANES)
            load = pltpu.make_async_copy(dest_hbm.at[pl.ds(first, per_worker)], idx_v, sem)
            load.start()
            empty = jnp.full((SC_LANES,), -1, jnp.int32)
            unroll = 8

            @pl.loop(0, n_out // (SC_LANES * unroll))
            def _(i):
                for u in range(unroll):
                    loc_v[pl.ds((i * unroll + u) * SC_LANES, SC_LANES)] = empty
            load.wait()

            @pl.loop(0, per_worker // SC_LANES)
            def _(j):
                idx = idx_v[pl.ds(j * SC_LANES, SC_LANES)]
                tok = first - k * total + j * SC_LANES + lane
                plsc.store_scatter(loc_v, [idx], tok * TOP_K + k)
            pltpu.sync_copy(loc_v, out_hbm.at[worker])

    assert n_out % (SC_LANES * 8) == 0
    return jnp.max(scatter(dest), axis=0)


def _moe_kernel(blk_e_ref, first_ref, wslot_ref, nxt_e_ref, n_used_ref, inv_ref,
                x1_hbm, wgu_hbm, bgu_ref, wd_hbm, bd_ref,
                ytok_hbm,
                xb0, xb1, xb2, yb0, yb1, yb2, wgu_f, wd_f, wgu_s, wd_s, bgu_s, bd_s,
                gsem, ssem, wsem, *, n_slots):
    n_used = n_used_ref[0]
    n_blk = blk_e_ref.shape[0]
    xbufs = (xb0, xb1, xb2)
    ybufs = (yb0, yb1, yb2)

    def gather_copy(v, s, r):
        return pltpu.make_async_copy(
            x1_hbm.at[v >> 2], xbufs[s].at[pl.ds(r * 8, 8), :], gsem.at[s])

    def scatter_copy(v, s, r):
        return pltpu.make_async_copy(
            ybufs[s].at[pl.ds(r * YROW, YROW), :],
            ytok_hbm.at[v >> 1, pl.ds(pl.multiple_of((v & 1) * YROW, YROW), YROW), :], ssem.at[s])

    def weight_copies(e, ws):
        return (pltpu.make_async_copy(wgu_hbm.at[e], wgu_f.at[ws], wsem.at[ws]),
                pltpu.make_async_copy(wd_hbm.at[e], wd_f.at[ws], wsem.at[ws]))

    def wait_rows(copy_fn, s):
        for r in range(MOE_ROWS):
            copy_fn(0, s, r).wait()

    for cp in weight_copies(blk_e_ref[0], 0):
        cp.start(priority=1)
    for yb in ybufs:
        yb[...] = jnp.zeros(yb.shape, jnp.uint32)

    def prime(r, c):
        for s in range(2):
            spare = n_slots + s * MOE_ROWS + r
            pltpu.make_async_copy(
                ybufs[s].at[pl.ds(pl.multiple_of(r * YROW, YROW), YROW), :],
                ytok_hbm.at[spare >> 1, pl.ds(pl.multiple_of((spare & 1) * YROW, YROW), YROW), :],
                ssem.at[s]).start()
            pltpu.make_async_copy(
                x1_hbm.at[inv_ref[(s + 1) * MOE_ROWS + r] >> 2],
                xbufs[s].at[pl.ds(pl.multiple_of(r * 8, 8), 8), :], gsem.at[s]).start()
        return c
    lax.fori_loop(0, MOE_ROWS, prime, 0)

    def run_block(b, s):
        nxt = (s + 2) % 3
        wait_rows(gather_copy, s)
        wait_rows(scatter_copy, s)
        x = jnp.concatenate(
            [xbufs[s][pl.ds(q, MOE_ROWS, stride=8), :] for q in range(8)], axis=-1)
        gbase = (jnp.minimum(b + 2, n_blk - 1) + 1) * MOE_ROWS
        sbase = b * MOE_ROWS
        for r in range(MOE_ROWS):
            gather_copy(inv_ref[gbase + r], nxt, r).start()
        for r in range(MOE_ROWS):
            scatter_copy(inv_ref[sbase + r], nxt, r).start(priority=1)
        gu = jnp.dot(x.astype(_BF16), wgu_s[...], preferred_element_type=_F32) + bgu_s[...]
        gate = jnp.minimum(gu[:, :D_FF], SWIGLU_LIMIT)
        up = jnp.clip(gu[:, D_FF:], -SWIGLU_LIMIT, SWIGLU_LIMIT)
        hmid = (up + 1.0) * (gate * _sigmoid(SWIGLU_ALPHA * gate))
        hb = hmid.astype(_BF16)
        half, gw = D_MODEL // 2, 2 * LANES

        def y_bits(c0):
            y = jnp.dot(hb, wd_s[:, c0:c0 + gw], preferred_element_type=_F32) + bd_s[:, c0:c0 + gw]
            return lax.bitcast_convert_type(y.astype(_BF16).astype(_F32), jnp.uint32)

        for c0 in range(0, half, gw):
            packed = (y_bits(c0) & jnp.uint32(0xFFFF0000)) | (y_bits(half + c0) >> 16)
            for q in range(gw // LANES):
                ybufs[s][pl.ds(c0 // LANES + q, MOE_ROWS, stride=YROW), :] = packed[:, q * LANES:(q + 1) * LANES]

    def body(b, c):
        @pl.when(first_ref[b] == 1)
        def _():
            e = blk_e_ref[b]
            ws = wslot_ref[b]
            for cp in weight_copies(e, ws):
                cp.wait()
            nxt_e = nxt_e_ref[b]

            @pl.when(nxt_e >= 0)
            def _():
                for cp in weight_copies(nxt_e, 1 - ws):
                    cp.start(priority=1)
            wgu_s[...] = wgu_f[ws].astype(_BF16)
            wd_s[...] = wd_f[ws].astype(_BF16)
            bgu_s[...] = bgu_ref[e]
            bd_s[...] = bd_ref[e]

        for s in range(3):
            pl.when(b % 3 == s)(functools.partial(run_block, b, s))
        return c

    lax.fori_loop(0, n_used + 1, body, 0)

    for d in (1, 2):
        for s in range(3):
            @pl.when((n_used + d) % 3 == s)
            def _():
                wait_rows(gather_copy, s)
                wait_rows(scatter_copy, s)


def _moe_call(blk_e, first, wslot, nxt_e, n_used, inv, x1t, w_gu, b_gu, w_down, b_down, *, n_slots):
    full = lambda shape: pl.BlockSpec(shape, lambda i, *_: (0,) * len(shape))
    grid_spec = pltpu.PrefetchScalarGridSpec(
        num_scalar_prefetch=6,
        grid=(1,),
        in_specs=[
            pl.BlockSpec(memory_space=pl.ANY),
            pl.BlockSpec(memory_space=pl.ANY),
            full(b_gu.shape),
            pl.BlockSpec(memory_space=pl.ANY),
            full(b_down.shape),
        ],
        out_specs=pl.BlockSpec(memory_space=pl.ANY),
        scratch_shapes=[pltpu.VMEM((MOE_ROWS * 8, LANES), _F32)] * 3
        + [pltpu.VMEM((MOE_ROWS * YROW, LANES), jnp.uint32)] * 3 + [
            pltpu.VMEM((2, D_MODEL, 2 * D_FF), _F32),
            pltpu.VMEM((2, D_FF, D_MODEL), _F32),
            pltpu.VMEM((D_MODEL, 2 * D_FF), _BF16),
            pltpu.VMEM((D_FF, D_MODEL), _BF16),
            pltpu.VMEM((1, 2 * D_FF), _F32),
            pltpu.VMEM((1, D_MODEL), _F32),
            pltpu.SemaphoreType.DMA((3,)),
            pltpu.SemaphoreType.DMA((3,)),
            pltpu.SemaphoreType.DMA((2,)),
        ],
    )
    return pl.pallas_call(
        functools.partial(_moe_kernel, n_slots=n_slots),
        grid_spec=grid_spec,
        out_shape=jax.ShapeDtypeStruct(((n_slots + 3 * MOE_ROWS) // 2, 8, LANES), jnp.uint32),
        compiler_params=pltpu.CompilerParams(
            dimension_semantics=("arbitrary",), vmem_limit_bytes=VMEM_LIMIT),
        name="moe",
    )(blk_e, first, wslot, nxt_e, n_used, inv, x1t, w_gu, b_gu, w_down, b_down)


def _combine_kernel(x1_ref, g_ref, y01_ref, y23_ref, g2_ref, be2_ref,
                    yp_hbm, ys_hbm, y_o, osem, *, alpha, n_p, bp, ts_p, nb_s, ss):
    i = pl.program_id(0)
    nt = pl.num_programs(0)
    rows = g_ref.shape[0]

    def prompt_copies(step, sl):
        return [pltpu.make_async_copy(y_o.at[sl, :, bb, :],
                                      yp_hbm.at[bb, pl.ds(step * ts_p, ts_p), :], osem.at[sl])
                for bb in range(bp)]

    def sample_copies(step, sl):
        flat = y_o.at[sl].reshape(rows, D_MODEL)
        return [pltpu.make_async_copy(flat.at[pl.ds(t * nb_s, nb_s), :],
                                      ys_hbm.at[pl.ds((step - n_p) * nb_s, nb_s), t, :], osem.at[sl])
                for t in range(ss)]

    def start_step(step, sl):
        @pl.when(step < n_p)
        def _():
            for cp in prompt_copies(step, sl):
                cp.start()

        @pl.when(step >= n_p)
        def _():
            for cp in sample_copies(step, sl):
                cp.start()

    def wait_step(sl):
        for cp in prompt_copies(0, sl):
            cp.wait()

    @pl.when(i >= 1)
    def _():
        start_step(i - 1, (i - 1) % 2)

    @pl.when(i >= 2)
    def _():
        wait_step(i % 2)

    g = g_ref[...]

    def token_major(ref):
        return jnp.concatenate([ref[pl.ds(s, rows, stride=8), :] for s in range(8)], axis=-1)

    def unpack(ref, half):
        u = jnp.concatenate([ref[pl.ds(half * YROW + q, rows, stride=8), :] for q in range(YROW)], axis=-1)
        hi = lax.bitcast_convert_type(u & jnp.uint32(0xFFFF0000), _F32)
        lo = lax.bitcast_convert_type(u << 16, _F32)
        return jnp.concatenate([hi, lo], axis=-1)

    pairs = [r.reshape(rows * 8, LANES) for r in (y01_ref, y23_ref)]
    ys = [unpack(pairs[k // 2], k % 2) for k in range(TOP_K)]
    moe = g[:, 0:1] * ys[0] + g[:, 1:2] * ys[1] + g[:, 2:3] * ys[2] + g[:, 3:4] * ys[3]
    y = _layernorm(alpha * token_major(x1_ref) + moe, g2_ref[...], be2_ref[...])
    y_o[i % 2] = y.reshape(ts_p, bp, D_MODEL)

    @pl.when(i == nt - 1)
    def _():
        start_step(i, i % 2)
        wait_step(i % 2)
        wait_step((i + 1) % 2)


def _combine_call(x1t, grow, ytok, g2, be2, *, prompt_shape, sample_shape, alpha):
    bp, sp, _ = prompt_shape
    bs, ss, _ = sample_shape
    ts_p = COMBINE_ROWS // bp
    nb_s = COMBINE_ROWS // ss
    n_p = sp // ts_p
    nt = n_p + bs // nb_s
    assert nt >= 2 and bp == ss
    pair = lambda h: pl.BlockSpec((COMBINE_ROWS, None, 8, LANES), lambda i, h=h: (i, h, 0, 0))
    vec = pl.BlockSpec((1, D_MODEL), lambda i: (0, 0))
    ytok4 = ytok.reshape(-1, TOP_K // 2, 8, LANES)
    return pl.pallas_call(
        functools.partial(_combine_kernel, alpha=alpha, n_p=n_p, bp=bp, ts_p=ts_p, nb_s=nb_s, ss=ss),
        grid=(nt,),
        in_specs=[pl.BlockSpec((COMBINE_ROWS * 8, LANES), lambda i: (i, 0)),
                  pl.BlockSpec((COMBINE_ROWS, LANES), lambda i: (i, 0)),
                  pair(0), pair(1), vec, vec],
        out_specs=(pl.BlockSpec(memory_space=pl.ANY), pl.BlockSpec(memory_space=pl.ANY)),
        out_shape=(jax.ShapeDtypeStruct(prompt_shape, _F32), jax.ShapeDtypeStruct(sample_shape, _F32)),
        scratch_shapes=[pltpu.VMEM((2, ts_p, bp, D_MODEL), _F32), pltpu.SemaphoreType.DMA((2,))],
        compiler_params=pltpu.CompilerParams(
            dimension_semantics=("arbitrary",), vmem_limit_bytes=VMEM_LIMIT),
        name="combine",
    )(x1t, grow, ytok4, ytok4, g2, be2)


def _pack_block_diag(w):
    w = w.reshape(N_GATE_TILES, HEADS_PER_TILE, LRU_BLOCK, LRU_BLOCK)
    eye = jnp.eye(HEADS_PER_TILE, dtype=w.dtype)
    t = jnp.einsum("qhij,hg->qhigj", w, eye)
    return t.reshape(N_GATE_TILES, GATE_TILE, GATE_TILE)


def _layer(xp, xs, h_s0, ca_s0, cb_s0, p, *, alpha):
    bp, sp, _ = xp.shape
    bs, ss, _ = xs.shape
    tp, tsm = bp * sp, bs * ss
    total = tp + tsm
    row2 = lambda v: v.reshape(1, -1)

    rwt = jnp.transpose(p["router_w"])
    rwt_hi = rwt.astype(_BF16)
    rwt_lo = (rwt - rwt_hi.astype(_F32)).astype(_BF16)
    ii = jnp.arange(MIXER_ROWS)
    tri = (ii[:, None] < ii[None, :]).astype(_BF16)
    col_scale = jnp.where(jnp.arange(N_GROUPS * D_MODEL) >= (N_GROUPS - 2) * D_MODEL, 0.5, 1.0)
    wts = (
        (p["w_in"] * col_scale).astype(_BF16), row2(p["b_in"] * col_scale), p["conv_a_w"], row2(p["conv_a_b"]),
        _pack_block_diag(0.5 * p["lru_wa"]).astype(_BF16), row2(0.5 * p["lru_ba"]),
        _pack_block_diag(0.5 * p["lru_wx"]).astype(_BF16), row2(0.5 * p["lru_bx"]), row2(p["lru_lambda"]),
        2.0 * p["conv_b_w"], (0.25 * p["w_out"]).astype(_BF16), row2(p["ln1_g"]), row2(p["ln1_b"]),
        rwt_hi, rwt_lo, jnp.broadcast_to(p["router_b"][:, None], (N_EXPERTS, LANES)), tri,
    )

    ts_p = MIXER_ROWS // bp
    zeros = lambda *s: jnp.zeros(s, _F32)
    n_assign = total * TOP_K
    n_spare = 3 * MOE_ROWS
    outs_p = _mixer_call(xp, zeros(1, bp, D_MODEL), zeros(1, CONV_A_WIDTH - 1, bp, D_MODEL),
                         zeros(1, CONV_B_WIDTH - 1, bp, D_MODEL), zeros(N_EXPERTS, LANES), zeros(8, LANES), wts,
                         ts=ts_p, blk_off=0, alpha=alpha, batch_major_x=True,
                         x1_rows=(total + n_spare // TOP_K) * 8)
    nb_s = MIXER_ROWS // ss
    nblk_s = bs // nb_s
    xs4 = jnp.transpose(xs.reshape(nblk_s, nb_s, ss, D_MODEL), (0, 2, 1, 3))
    h0 = h_s0.reshape(nblk_s, nb_s, D_MODEL)
    ca0 = jnp.transpose(ca_s0.reshape(nblk_s, nb_s, CONV_A_WIDTH - 1, D_MODEL), (0, 2, 1, 3))
    cb0 = jnp.transpose(cb_s0.reshape(nblk_s, nb_s, CONV_B_WIDTH - 1, D_MODEL), (0, 2, 1, 3))
    outs_s = _mixer_call(xs4, h0, ca0, cb0, outs_p[7], outs_p[0], wts,
                         ts=ss, blk_off=tp // MIXER_ROWS, alpha=alpha)
    x1t = outs_s[0]

    n_blocks = -(-n_assign // MOE_ROWS) + N_EXPERTS + 1
    n_rows = n_blocks * MOE_ROWS
    idx = jnp.concatenate([outs_p[1][:TOP_K], outs_s[1][:TOP_K]], axis=1)
    rank = jnp.concatenate([outs_p[2][:TOP_K], outs_s[2][:TOP_K]], axis=1)
    grow = jnp.concatenate([outs_p[3], outs_s[3]], axis=0)
    counts = outs_s[7][:, 0].astype(jnp.int32)

    padded = (counts + MOE_ROWS - 1) // MOE_ROWS * MOE_ROWS
    end_pad = jnp.cumsum(padded)
    start_pad = end_pad - padded
    experts = jnp.arange(N_EXPERTS, dtype=jnp.int32)
    start_of = jnp.sum(jnp.where(idx[:, :, None] == experts, start_pad, 0), axis=-1)
    dest = (start_of + rank).reshape(n_assign)
    blk_start = jnp.arange(n_blocks, dtype=jnp.int32) * MOE_ROWS
    blk_e = jnp.minimum(jnp.sum((blk_start[:, None] >= end_pad[None, :]).astype(jnp.int32), axis=1),
                        N_EXPERTS - 1)
    n_used = (end_pad[-1] // MOE_ROWS).astype(jnp.int32).reshape(1)

    blk_ids = jnp.arange(n_blocks, dtype=jnp.int32)
    changed = jnp.concatenate([jnp.ones((1,), bool), blk_e[1:] != blk_e[:-1]])
    first = changed & (blk_ids < n_used[0])
    wslot = (jnp.cumsum(first.astype(jnp.int32)) - 1) % 2
    later_first = first[None, :] & (blk_ids[None, :] > blk_ids[:, None])
    nxt_blk = jnp.min(jnp.where(later_first, blk_ids[None, :], n_blocks), axis=1)
    nxt_e = jnp.sum(jnp.where(nxt_blk[:, None] == blk_ids[None, :], blk_e[None, :], 0), axis=1)
    nxt_e = jnp.where(nxt_blk < n_blocks, nxt_e, -1)

    ext = jnp.arange(n_rows + MOE_ROWS, dtype=jnp.int32)
    row = ext - MOE_ROWS
    holds = jnp.any((row[:, None] >= start_pad[None, :]) & (row[:, None] < (start_pad + counts)[None, :]),
                    axis=1)
    spare = jnp.where(row < 0, n_assign + 2 * MOE_ROWS + ext, n_assign + (row & (2 * MOE_ROWS - 1)))
    scattered = _invmap_sc_call(dest + MOE_ROWS, total=total, n_out=n_rows + MOE_ROWS)
    inv = jnp.where(holds, scattered, spare)
    ytok = _moe_call(blk_e, first.astype(jnp.int32), wslot, nxt_e, n_used, inv, x1t.reshape(-1, 8, LANES),
                     p["w_gu"], p["b_gu"][:, None, :], p["w_down"], p["b_down"][:, None, :],
                     n_slots=n_assign)
    yp, ys = _combine_call(x1t, grow, ytok, row2(p["ln2_g"]), row2(p["ln2_b"]),
                           prompt_shape=xp.shape, sample_shape=xs.shape, alpha=alpha)

    def batch_major(v, nblk):
        return jnp.transpose(v, (0, 2, 1, 3)).reshape(nblk * v.shape[2], v.shape[1], D_MODEL)

    states_p = (outs_p[4].reshape(bp, D_MODEL), batch_major(outs_p[5], 1), batch_major(outs_p[6], 1))
    states_s = (outs_s[4].reshape(bs, D_MODEL), batch_major(outs_s[5], nblk_s), batch_major(outs_s[6], nblk_s))
    return yp, ys, states_p, states_s


def kernel(x_prompt, x_sample, state_rglru_h, state_rglru_conv, state_shortconv, w_in, b_in, conv_a_w, conv_a_b, lru_wa, lru_ba, lru_wx, lru_bx, lru_lambda, conv_b_w, w_out, ln1_g, ln1_b, router_w, router_b, w_gu, b_gu, w_down, b_down, ln2_g, ln2_b):
    depth = w_in.shape[0]
    alpha = (2.0 * depth) ** 0.25
    names = ("w_in", "b_in", "conv_a_w", "conv_a_b", "lru_wa", "lru_ba", "lru_wx", "lru_bx", "lru_lambda",
             "conv_b_w", "w_out", "ln1_g", "ln1_b", "router_w", "router_b", "w_gu", "b_gu", "w_down",
             "b_down", "ln2_g", "ln2_b")
    stacked = (w_in, b_in, conv_a_w, conv_a_b, lru_wa, lru_ba, lru_wx, lru_bx, lru_lambda, conv_b_w, w_out,
               ln1_g, ln1_b, router_w, router_b, w_gu, b_gu, w_down, b_down, ln2_g, ln2_b)
    xp, xs = x_prompt, x_sample
    hp_l, cp_l, sp_l, hs_l, cs_l, ss_l = [], [], [], [], [], []
    for l in range(depth):
        p = {n: v[l] for n, v in zip(names, stacked)}
        xp, xs, (hp, cp, sp), (hs, cs, ss) = _layer(
            xp, xs, state_rglru_h[l], state_rglru_conv[l], state_shortconv[l], p, alpha=alpha)
        hp_l.append(hp); cp_l.append(cp); sp_l.append(sp)
        hs_l.append(hs); cs_l.append(cs); ss_l.append(ss)
    return (xp, xs, jnp.stack(hp_l), jnp.stack(cp_l), jnp.stack(sp_l), jnp.stack(hs_l), jnp.stack(cs_l),
            jnp.stack(ss_l))
```

```python
import functools

import jax
import jax.numpy as jnp
from jax import lax
from jax.experimental import pallas as pl
from jax.experimental.pallas import tpu as pltpu
from jax.experimental.pallas import tpu_sc as plsc

D_MODEL = 1024
LRU_HEADS = 16
LRU_BLOCK = D_MODEL // LRU_HEADS
LRU_C = 8.0
CONV_A_WIDTH = 4
CONV_B_WIDTH = 3
N_GROUPS = 7
N_EXPERTS = 32
TOP_K = 4
D_FF = D_MODEL
SWIGLU_LIMIT = 7.0
SWIGLU_ALPHA = 1.702
LN_EPS = 1e-5

GATE_TILE = 256
HEADS_PER_TILE = GATE_TILE // LRU_BLOCK
N_GATE_TILES = D_MODEL // GATE_TILE
LANES = 128
MIXER_ROWS = 512
MOE_ROWS = 256
COMBINE_ROWS = 512
YROW = 4
VMEM_LIMIT = 58 * 1024 * 1024

_F32 = jnp.float32
_BF16 = jnp.bfloat16
_NT = (((1,), (1,)), ((), ()))


def _sigmoid(v):
    return 0.5 * jnp.tanh(0.5 * v) + 0.5


def _twice_gelu_tanh(v):
    c = 0.7978845608028654
    return v * (1.0 + jnp.tanh(v * (c + (c * 0.044715) * (v * v))))


def _layernorm(z, g, b):
    mu = jnp.mean(z, axis=-1, keepdims=True)
    zc = z - mu
    var = jnp.mean(zc * zc, axis=-1, keepdims=True)
    return zc * lax.rsqrt(var + LN_EPS) * g + b


def _mixer_kernel(x_ref, h0_ref, ca0_ref, cb0_ref, cnt0_ref, x1_buf_ref,
                  w_in_ref, b_in_ref, wca_ref, bca_ref, wa_ref, ba_ref, wx_ref, bx_ref, lam_ref,
                  wcb_ref, w_out_ref, g1_ref, be1_ref, rwt_hi_ref, rwt_lo_ref, rb_ref, tri_ref,
                  x1_ref, idx_ref, rank_ref, grow_ref, hl_ref, ca_ref, cb_ref, cnt_ref,
                  xa_s, u_s, a_s, b_s, h_s, hst_s, cnt_s, *maybe_xin, ts, nb, alpha, batch_major_x):
    i = pl.program_id(0)
    j = pl.program_id(1)
    rows = ts * nb
    ta = (CONV_A_WIDTH - 1) * nb
    tb = (CONV_B_WIDTH - 1) * nb

    @pl.when(j == 0)
    def _():
        hst_s[...] = h0_ref[...]
        xa_s[0:ta, :] = ca0_ref[...].reshape(ta, D_MODEL)
        u_s[0:tb, :] = cb0_ref[...].reshape(tb, D_MODEL)

    @pl.when((i == 0) & (j == 0))
    def _():
        cnt_s[...] = cnt0_ref[...]

    if batch_major_x:
        xin_s, xsem, x1_o, osem = maybe_xin
        nt = pl.num_programs(1)

        def chunk_copies(c, slot):
            return [pltpu.make_async_copy(x_ref.at[bb, pl.ds(c * ts, ts), :],
                                          xin_s.at[slot, :, bb, :], xsem.at[slot]) for bb in range(nb)]

        @pl.when(j == 0)
        def _():
            for cp in chunk_copies(0, 0):
                cp.start()

        @pl.when(j + 1 < nt)
        def _():
            for cp in chunk_copies(j + 1, (j + 1) % 2):
                cp.start()
        for cp in chunk_copies(j, j % 2):
            cp.wait()

        tile_rows = rows * 8

        def out_copy(c, sl):
            return pltpu.make_async_copy(
                x1_o.at[sl], x1_ref.at[pl.ds(pl.multiple_of(c * tile_rows, tile_rows), tile_rows), :],
                osem.at[sl])

        @pl.when(j == 0)
        def _():
            x1_o[1] = jnp.zeros((tile_rows, LANES), _F32)
            lo = nt * tile_rows
            tails = []
            while lo < x1_ref.shape[0]:
                n = min(tile_rows, x1_ref.shape[0] - lo)
                tails.append(pltpu.make_async_copy(
                    x1_o.at[1, pl.ds(0, n), :], x1_ref.at[pl.ds(lo, n), :], osem.at[1]))
                lo += n
            for cp in tails:
                cp.start()
            for cp in tails:
                cp.wait()

        @pl.when(j >= 1)
        def _():
            out_copy(j - 1, (j - 1) % 2).start()

        @pl.when(j >= 2)
        def _():
            out_copy(0, j % 2).wait()

        x = xin_s[j % 2].reshape(rows, D_MODEL)
    else:
        x = x_ref[...].reshape(rows, D_MODEL)
    xb = x.astype(_BF16)

    def proj(g):
        lo, hi = g * D_MODEL, (g + 1) * D_MODEL
        return jnp.dot(xb, w_in_ref[:, lo:hi], preferred_element_type=_F32) + b_in_ref[:, lo:hi]

    xa_s[ta:ta + rows, :] = proj(0)
    xc = bca_ref[...] + xa_s[0:rows, :] * wca_ref[0:1, :]
    for k in range(1, CONV_A_WIDTH):
        xc = xc + xa_s[k * nb:k * nb + rows, :] * wca_ref[k:k + 1, :]
    new_ta = xa_s[rows:rows + ta, :]
    xa_s[0:ta, :] = new_ta
    ca_ref[...] = new_ta.reshape(CONV_A_WIDTH - 1, nb, D_MODEL)

    xcb = xc.astype(_BF16)

    def block_diag(w_ref):
        return jnp.concatenate(
            [jnp.dot(xcb[:, q * GATE_TILE:(q + 1) * GATE_TILE], w_ref[q], preferred_element_type=_F32)
             for q in range(N_GATE_TILES)], axis=-1)

    t_r = jnp.tanh(block_diag(wa_ref) + ba_ref[...])
    ig = 0.5 * jnp.tanh(block_diag(wx_ref) + bx_ref[...]) + 0.5
    nlam = -lam_ref[...]
    softplus = jnp.maximum(nlam, 0.0) + jnp.log1p(jnp.exp(-jnp.abs(nlam)))
    half_c = (-0.5 * LRU_C) * softplus
    log_a = half_c * t_r + half_c
    a = jnp.exp(log_a)
    a_s[...] = a
    z = -jnp.tanh(log_a) * (a * a + 1.0)
    b_s[...] = jnp.where(z > 0.0, z * lax.rsqrt(z), 0.0) * (ig * xc)

    h = hst_s[...]
    for t in range(ts):
        sl = slice(t * nb, (t + 1) * nb)
        h = a_s[sl, :] * h + b_s[sl, :]
        h_s[sl, :] = h
    hst_s[...] = h
    hl_ref[...] = h

    a_s[...] = h_s[...] * _twice_gelu_tanh(proj(1))

    u_s[tb:tb + rows, :] = proj(3) * proj(4)
    uc = u_s[0:rows, :] * wcb_ref[0:1, :]
    for k in range(1, CONV_B_WIDTH):
        uc = uc + u_s[k * nb:k * nb + rows, :] * wcb_ref[k:k + 1, :]
    new_tb = u_s[rows:rows + tb, :]
    u_s[0:tb, :] = new_tb
    cb_ref[...] = new_tb.reshape(CONV_B_WIDTH - 1, nb, D_MODEL)
    y_b = proj(2) * uc

    merged = (jnp.tanh(proj(5)) + 1.0) * a_s[...] + (jnp.tanh(proj(6)) + 1.0) * y_b
    mixed = jnp.dot(merged.astype(_BF16), w_out_ref[...], preferred_element_type=_F32)
    x1 = _layernorm(alpha * x + mixed, g1_ref[...], be1_ref[...])
    x1_dst = x1_o.at[j % 2] if batch_major_x else x1_ref
    for s in range(8):
        x1_dst[pl.ds(s, rows, stride=8), :] = x1[:, s * LANES:(s + 1) * LANES]

    x1_hi = x1.astype(_BF16)
    x1_lo = (x1 - x1_hi.astype(_F32)).astype(_BF16)
    logits = (lax.dot_general(rwt_hi_ref[...], x1_hi, _NT, preferred_element_type=_F32)
              + lax.dot_general(rwt_hi_ref[...], x1_lo, _NT, preferred_element_type=_F32)
              + lax.dot_general(rwt_lo_ref[...], x1_hi, _NT, preferred_element_type=_F32)
              + rb_ref[:, 0:1])
    e_iota = lax.broadcasted_iota(jnp.int32, (N_EXPERTS, rows), 0)
    work = logits
    vals, sels, idxs = [], [], []
    for _ in range(TOP_K):
        m = jnp.max(work, axis=0, keepdims=True)
        ik = jnp.min(jnp.where(work == m, e_iota, N_EXPERTS), axis=0, keepdims=True)
        sel = e_iota == ik
        work = jnp.where(sel, -jnp.inf, work)
        vals.append(m)
        sels.append(sel)
        idxs.append(ik)
    exps = [jnp.exp(v - vals[0]) for v in vals]
    denom = exps[0] + exps[1] + exps[2] + exps[3]
    gates = [ex / denom for ex in exps]

    onehot = jnp.zeros((N_EXPERTS, rows), _F32)
    for sel in sels:
        onehot = onehot + sel.astype(_F32)
    prefix = jnp.dot(onehot.astype(_BF16), tri_ref[...], preferred_element_type=_F32)
    pos = prefix + cnt_s[:, 0:1]
    ranks = [jnp.sum(jnp.where(sel, pos, 0.0), axis=0, keepdims=True) for sel in sels]
    new_cnt = cnt_s[...] + jnp.sum(onehot, axis=1, keepdims=True)
    cnt_s[...] = new_cnt
    cnt_ref[...] = new_cnt

    row8 = lax.broadcasted_iota(jnp.int32, (8, rows), 0)
    idx8 = jnp.zeros((8, rows), jnp.int32)
    rank8 = jnp.zeros((8, rows), jnp.int32)
    for k in range(TOP_K):
        idx8 = jnp.where(row8 == k, idxs[k], idx8)
        rank8 = jnp.where(row8 == k, ranks[k].astype(jnp.int32), rank8)
    idx_ref[...] = idx8
    rank_ref[...] = rank8

    row_l = lax.broadcasted_iota(jnp.int32, (LANES, rows), 0)
    g_t = jnp.zeros((LANES, rows), _F32)
    for k in range(TOP_K):
        g_t = jnp.where(row_l == k, gates[k], g_t)
    grow_ref[...] = jnp.transpose(g_t)

    if batch_major_x:
        @pl.when(j == nt - 1)
        def _():
            out_copy(j, j % 2).start()
            out_copy(0, j % 2).wait()
            if nt > 1:
                out_copy(0, (j + 1) % 2).wait()


def _const_spec(shape):
    nd = len(shape)
    return pl.BlockSpec(shape, lambda i, j: (0,) * nd, pipeline_mode=pl.Buffered(1))


def _mixer_call(x4, h0, ca0, cb0, cnt0, x1_buf, wts, *, ts, blk_off, alpha, batch_major_x=False,
                x1_rows=None):
    if batch_major_x:
        (nb, seq, _), nblk = x4.shape, 1
    else:
        nblk, seq, nb, _ = x4.shape
    nt = seq // ts
    rows = ts * nb
    total = nblk * seq * nb
    f32 = lambda *s: jax.ShapeDtypeStruct(s, _F32)
    i32 = lambda *s: jax.ShapeDtypeStruct(s, jnp.int32)
    in_specs = [
        (pl.BlockSpec(memory_space=pl.ANY) if batch_major_x
         else pl.BlockSpec((None, ts, nb, D_MODEL), lambda i, j: (i, j, 0, 0))),
        pl.BlockSpec((None, nb, D_MODEL), lambda i, j: (i, 0, 0)),
        pl.BlockSpec((None, CONV_A_WIDTH - 1, nb, D_MODEL), lambda i, j: (i, 0, 0, 0)),
        pl.BlockSpec((None, CONV_B_WIDTH - 1, nb, D_MODEL), lambda i, j: (i, 0, 0, 0)),
        _const_spec(cnt0.shape),
        pl.BlockSpec(memory_space=pl.ANY),
    ] + [_const_spec(w.shape) for w in wts]
    out_shape = (
        f32(*((x1_rows, LANES) if batch_major_x else x1_buf.shape)),
        i32(8, total),
        i32(8, total),
        f32(total, LANES),
        f32(nblk, nb, D_MODEL),
        f32(nblk, CONV_A_WIDTH - 1, nb, D_MODEL),
        f32(nblk, CONV_B_WIDTH - 1, nb, D_MODEL),
        f32(N_EXPERTS, LANES),
    )
    out_specs = (
        (pl.BlockSpec(memory_space=pl.ANY) if batch_major_x
         else pl.BlockSpec((rows * 8, LANES), lambda i, j: (blk_off + i * nt + j, 0))),
        pl.BlockSpec((8, rows), lambda i, j: (0, i * nt + j)),
        pl.BlockSpec((8, rows), lambda i, j: (0, i * nt + j)),
        pl.BlockSpec((rows, LANES), lambda i, j: (i * nt + j, 0)),
        pl.BlockSpec((None, nb, D_MODEL), lambda i, j: (i, 0, 0)),
        pl.BlockSpec((None, CONV_A_WIDTH - 1, nb, D_MODEL), lambda i, j: (i, 0, 0, 0)),
        pl.BlockSpec((None, CONV_B_WIDTH - 1, nb, D_MODEL), lambda i, j: (i, 0, 0, 0)),
        pl.BlockSpec((N_EXPERTS, LANES), lambda i, j: (0, 0)),
    )
    scratch = [
        pltpu.VMEM((rows + (CONV_A_WIDTH - 1) * nb, D_MODEL), _F32),
        pltpu.VMEM((rows + (CONV_B_WIDTH - 1) * nb, D_MODEL), _F32),
        pltpu.VMEM((rows, D_MODEL), _F32),
        pltpu.VMEM((rows, D_MODEL), _F32),
        pltpu.VMEM((rows, D_MODEL), _F32),
        pltpu.VMEM((nb, D_MODEL), _F32),
        pltpu.VMEM((N_EXPERTS, LANES), _F32),
    ]
    if batch_major_x:
        scratch += [pltpu.VMEM((2, ts, nb, D_MODEL), _F32), pltpu.SemaphoreType.DMA((2,)),
                    pltpu.VMEM((2, rows * 8, LANES), _F32), pltpu.SemaphoreType.DMA((2,))]
    return pl.pallas_call(
        functools.partial(_mixer_kernel, ts=ts, nb=nb, alpha=alpha, batch_major_x=batch_major_x),
        grid=(nblk, nt),
        in_specs=in_specs,
        out_specs=out_specs,
        out_shape=out_shape,
        scratch_shapes=scratch,
        input_output_aliases={} if batch_major_x else {5: 0},
        compiler_params=pltpu.CompilerParams(
            dimension_semantics=("arbitrary", "arbitrary"), vmem_limit_bytes=VMEM_LIMIT),
        name="mixer",
    )(x4, h0, ca0, cb0, cnt0, x1_buf, *wts)


SC_WORKERS = 8
SC_WINDOW = 128
SC_LANES = 16


def _invmap_sc_call(dest, *, total, n_out):
    n = dest.shape[0]
    per_worker = n // SC_WORKERS
    n_windows = per_worker // SC_WINDOW
    assert per_worker * SC_WORKERS == n and n_windows * SC_WINDOW == per_worker
    assert total % per_worker == 0
    mesh = plsc.VectorSubcoreMesh(core_axis_name="c", subcore_axis_name="s")

    @functools.partial(
        pl.kernel, mesh=mesh,
        out_type=jax.ShapeDtypeStruct((SC_WORKERS, n_out), jnp.int32),
        scratch_types=[pltpu.VMEM((n_out,), jnp.int32),
                       pltpu.VMEM((per_worker,), jnp.int32),
                       pltpu.SemaphoreType.DMA],
        compiler_params=pltpu.CompilerParams(use_tc_tiling_on_sc=False, needs_layout_passes=False),
        name="invmap_sc")
    def scatter(dest_hbm, out_hbm, loc_v, idx_v, sem):
        worker = lax.axis_index("s") * 2 + lax.axis_index("c")

        @pl.when(worker < SC_WORKERS)
        def _():
            first = worker * per_worker
            k = first // total
            lane = lax.iota(jnp.int32, SC_LANES)
            load = pltpu.make_async_copy(dest_hbm.at[pl.ds(first, per_worker)], idx_v, sem)
            load.start()
            empty = jnp.full((SC_LANES,), -1, jnp.int32)
            unroll = 8

            @pl.loop(0, n_out // (SC_LANES * unroll))
            def _(i):
                for u in range(unroll):
                    loc_v[pl.ds((i * unroll + u) * SC_LANES, SC_LANES)] = empty
            load.wait()

            @pl.loop(0, per_worker // SC_LANES)
            def _(j):
                idx = idx_v[pl.ds(j * SC_LANES, SC_LANES)]
                tok = first - k * total + j * SC_LANES + lane
                plsc.store_scatter(loc_v, [idx], tok * TOP_K + k)
            pltpu.sync_copy(loc_v, out_hbm.at[worker])

    assert n_out % (SC_LANES * 8) == 0
    return jnp.max(scatter(dest), axis=0)


def _moe_kernel(blk_e_ref, first_ref, wslot_ref, nxt_e_ref, n_used_ref, inv_ref,
                x1_hbm, wgu_hbm, bgu_ref, wd_hbm, bd_ref,
                ytok_hbm,
                xb0, xb1, xb2, yb0, yb1, yb2, wgu_f, wd_f, wgu_s, wd_s, bgu_s, bd_s,
                gsem, ssem, wsem, *, n_slots):
    n_used = n_used_ref[0]
    n_blk = blk_e_ref.shape[0]
    xbufs = (xb0, xb1, xb2)
    ybufs = (yb0, yb1, yb2)

    def gather_copy(v, s, r):
        return pltpu.make_async_copy(
            x1_hbm.at[v >> 2], xbufs[s].at[pl.ds(r * 8, 8), :], gsem.at[s])

    def scatter_copy(v, s, r):
        return pltpu.make_async_copy(
            ybufs[s].at[pl.ds(r * YROW, YROW), :],
            ytok_hbm.at[v >> 1, pl.ds(pl.multiple_of((v & 1) * YROW, YROW), YROW), :], ssem.at[s])

    def weight_copies(e, ws):
        return (pltpu.make_async_copy(wgu_hbm.at[e], wgu_f.at[ws], wsem.at[ws]),
                pltpu.make_async_copy(wd_hbm.at[e], wd_f.at[ws], wsem.at[ws]))

    def wait_rows(copy_fn, s):
        for r in range(MOE_ROWS):
            copy_fn(0, s, r).wait()

    for cp in weight_copies(blk_e_ref[0], 0):
        cp.start(priority=1)
    for yb in ybufs:
        yb[...] = jnp.zeros(yb.shape, jnp.uint32)

    def prime(r, c):
        for s in range(2):
            spare = n_slots + s * MOE_ROWS + r
            pltpu.make_async_copy(
                ybufs[s].at[pl.ds(pl.multiple_of(r * YROW, YROW), YROW), :],
                ytok_hbm.at[spare >> 1, pl.ds(pl.multiple_of((spare & 1) * YROW, YROW), YROW), :],
                ssem.at[s]).start()
            pltpu.make_async_copy(
                x1_hbm.at[inv_ref[(s + 1) * MOE_ROWS + r] >> 2],
                xbufs[s].at[pl.ds(pl.multiple_of(r * 8, 8), 8), :], gsem.at[s]).start()
        return c
    lax.fori_loop(0, MOE_ROWS, prime, 0)

    def run_block(b, s):
        nxt = (s + 2) % 3
        wait_rows(gather_copy, s)
        wait_rows(scatter_copy, s)
        x = jnp.concatenate(
            [xbufs[s][pl.ds(q, MOE_ROWS, stride=8), :] for q in range(8)], axis=-1)
        gbase = (jnp.minimum(b + 2, n_blk - 1) + 1) * MOE_ROWS
        sbase = b * MOE_ROWS
        for r in range(MOE_ROWS):
            gather_copy(inv_ref[gbase + r], nxt, r).start()
        for r in range(MOE_ROWS):
            scatter_copy(inv_ref[sbase + r], nxt, r).start(priority=1)
        gu = jnp.dot(x.astype(_BF16), wgu_s[...], preferred_element_type=_F32) + bgu_s[...]
        gate = jnp.minimum(gu[:, :D_FF], SWIGLU_LIMIT)
        up = jnp.clip(gu[:, D_FF:], -SWIGLU_LIMIT, SWIGLU_LIMIT)
        hmid = (up + 1.0) * (gate * _sigmoid(SWIGLU_ALPHA * gate))
        hb = hmid.astype(_BF16)
        half, gw = D_MODEL // 2, 2 * LANES

        def y_bits(c0):
            y = jnp.dot(hb, wd_s[:, c0:c0 + gw], preferred_element_type=_F32) + bd_s[:, c0:c0 + gw]
            return lax.bitcast_convert_type(y.astype(_BF16).astype(_F32), jnp.uint32)

        for c0 in range(0, half, gw):
            packed = (y_bits(c0) & jnp.uint32(0xFFFF0000)) | (y_bits(half + c0) >> 16)
            for q in range(gw // LANES):
                ybufs[s][pl.ds(c0 // LANES + q, MOE_ROWS, stride=YROW), :] = packed[:, q * LANES:(q + 1) * LANES]

    def body(b, c):
        @pl.when(first_ref[b] == 1)
        def _():
            e = blk_e_ref[b]
            ws = wslot_ref[b]
            for cp in weight_copies(e, ws):
                cp.wait()
            nxt_e = nxt_e_ref[b]

            @pl.when(nxt_e >= 0)
            def _():
                for cp in weight_copies(nxt_e, 1 - ws):
                    cp.start(priority=1)
            wgu_s[...] = wgu_f[ws].astype(_BF16)
            wd_s[...] = wd_f[ws].astype(_BF16)
            bgu_s[...] = bgu_ref[e]
            bd_s[...] = bd_ref[e]

        for s in range(3):
            pl.when(b % 3 == s)(functools.partial(run_block, b, s))
        return c

    lax.fori_loop(0, n_used + 1, body, 0)

    for d in (1, 2):
        for s in range(3):
            @pl.when((n_used + d) % 3 == s)
            def _():
                wait_rows(gather_copy, s)
                wait_rows(scatter_copy, s)


def _moe_call(blk_e, first, wslot, nxt_e, n_used, inv, x1t, w_gu, b_gu, w_down, b_down, *, n_slots):
    full = lambda shape: pl.BlockSpec(shape, lambda i, *_: (0,) * len(shape))
    grid_spec = pltpu.PrefetchScalarGridSpec(
        num_scalar_prefetch=6,
        grid=(1,),
        in_specs=[
            pl.BlockSpec(memory_space=pl.ANY),
            pl.BlockSpec(memory_space=pl.ANY),
            full(b_gu.shape),
            pl.BlockSpec(memory_space=pl.ANY),
            full(b_down.shape),
        ],
        out_specs=pl.BlockSpec(memory_space=pl.ANY),
        scratch_shapes=[pltpu.VMEM((MOE_ROWS * 8, LANES), _F32)] * 3
        + [pltpu.VMEM((MOE_ROWS * YROW, LANES), jnp.uint32)] * 3 + [
            pltpu.VMEM((2, D_MODEL, 2 * D_FF), _F32),
            pltpu.VMEM((2, D_FF, D_MODEL), _F32),
            pltpu.VMEM((D_MODEL, 2 * D_FF), _BF16),
            pltpu.VMEM((D_FF, D_MODEL), _BF16),
            pltpu.VMEM((1, 2 * D_FF), _F32),
            pltpu.VMEM((1, D_MODEL), _F32),
            pltpu.SemaphoreType.DMA((3,)),
            pltpu.SemaphoreType.DMA((3,)),
            pltpu.SemaphoreType.DMA((2,)),
        ],
    )
    return pl.pallas_call(
        functools.partial(_moe_kernel, n_slots=n_slots),
        grid_spec=grid_spec,
        out_shape=jax.ShapeDtypeStruct(((n_slots + 3 * MOE_ROWS) // 2, 8, LANES), jnp.uint32),
        compiler_params=pltpu.CompilerParams(
            dimension_semantics=("arbitrary",), vmem_limit_bytes=VMEM_LIMIT),
        name="moe",
    )(blk_e, first, wslot, nxt_e, n_used, inv, x1t, w_gu, b_gu, w_down, b_down)


def _combine_kernel(x1_ref, g_ref, y01_ref, y23_ref, g2_ref, be2_ref,
                    yp_hbm, ys_hbm, y_o, osem, *, alpha, n_p, bp, ts_p, nb_s, ss):
    i = pl.program_id(0)
    nt = pl.num_programs(0)
    rows = g_ref.shape[0]

    def prompt_copies(step, sl):
        return [pltpu.make_async_copy(y_o.at[sl, :, bb, :],
                                      yp_hbm.at[bb, pl.ds(step * ts_p, ts_p), :], osem.at[sl])
                for bb in range(bp)]

    def sample_copies(step, sl):
        flat = y_o.at[sl].reshape(rows, D_MODEL)
        return [pltpu.make_async_copy(flat.at[pl.ds(t * nb_s, nb_s), :],
                                      ys_hbm.at[pl.ds((step - n_p) * nb_s, nb_s), t, :], osem.at[sl])
                for t in range(ss)]

    def start_step(step, sl):
        @pl.when(step < n_p)
        def _():
            for cp in prompt_copies(step, sl):
                cp.start()

        @pl.when(step >= n_p)
        def _():
            for cp in sample_copies(step, sl):
                cp.start()

    def wait_step(sl):
        for cp in prompt_copies(0, sl):
            cp.wait()

    @pl.when(i >= 1)
    def _():
        start_step(i - 1, (i - 1) % 2)

    @pl.when(i >= 2)
    def _():
        wait_step(i % 2)

    g = g_ref[...]

    def token_major(ref):
        return jnp.concatenate([ref[pl.ds(s, rows, stride=8), :] for s in range(8)], axis=-1)

    def unpack(ref, half):
        u = jnp.concatenate([ref[pl.ds(half * YROW + q, rows, stride=8), :] for q in range(YROW)], axis=-1)
        hi = lax.bitcast_convert_type(u & jnp.uint32(0xFFFF0000), _F32)
        lo = lax.bitcast_convert_type(u << 16, _F32)
        return jnp.concatenate([hi, lo], axis=-1)

    pairs = [r.reshape(rows * 8, LANES) for r in (y01_ref, y23_ref)]
    ys = [unpack(pairs[k // 2], k % 2) for k in range(TOP_K)]
    moe = g[:, 0:1] * ys[0] + g[:, 1:2] * ys[1] + g[:, 2:3] * ys[2] + g[:, 3:4] * ys[3]
    y = _layernorm(alpha * token_major(x1_ref) + moe, g2_ref[...], be2_ref[...])
    y_o[i % 2] = y.reshape(ts_p, bp, D_MODEL)

    @pl.when(i == nt - 1)
    def _():
        start_step(i, i % 2)
        wait_step(i % 2)
        wait_step((i + 1) % 2)


def _combine_call(x1t, grow, ytok, g2, be2, *, prompt_shape, sample_shape, alpha):
    bp, sp, _ = prompt_shape
    bs, ss, _ = sample_shape
    ts_p = COMBINE_ROWS // bp
    nb_s = COMBINE_ROWS // ss
    n_p = sp // ts_p
    nt = n_p + bs // nb_s
    assert nt >= 2 and bp == ss
    pair = lambda h: pl.BlockSpec((COMBINE_ROWS, None, 8, LANES), lambda i, h=h: (i, h, 0, 0))
    vec = pl.BlockSpec((1, D_MODEL), lambda i: (0, 0))
    ytok4 = ytok.reshape(-1, TOP_K // 2, 8, LANES)
    return pl.pallas_call(
        functools.partial(_combine_kernel, alpha=alpha, n_p=n_p, bp=bp, ts_p=ts_p, nb_s=nb_s, ss=ss),
        grid=(nt,),
        in_specs=[pl.BlockSpec((COMBINE_ROWS * 8, LANES), lambda i: (i, 0)),
                  pl.BlockSpec((COMBINE_ROWS, LANES), lambda i: (i, 0)),
                  pair(0), pair(1), vec, vec],
        out_specs=(pl.BlockSpec(memory_space=pl.ANY), pl.BlockSpec(memory_space=pl.ANY)),
        out_shape=(jax.ShapeDtypeStruct(prompt_shape, _F32), jax.ShapeDtypeStruct(sample_shape, _F32)),
        scratch_shapes=[pltpu.VMEM((2, ts_p, bp, D_MODEL), _F32), pltpu.SemaphoreType.DMA((2,))],
        compiler_params=pltpu.CompilerParams(
            dimension_semantics=("arbitrary",), vmem_limit_bytes=VMEM_LIMIT),
        name="combine",
    )(x1t, grow, ytok4, ytok4, g2, be2)


def _pack_block_diag(w):
    w = w.reshape(N_GATE_TILES, HEADS_PER_TILE, LRU_BLOCK, LRU_BLOCK)
    eye = jnp.eye(HEADS_PER_TILE, dtype=w.dtype)
    t = jnp.einsum("qhij,hg->qhigj", w, eye)
    return t.reshape(N_GATE_TILES, GATE_TILE, GATE_TILE)


def _layer(xp, xs, h_s0, ca_s0, cb_s0, p, *, alpha):
    bp, sp, _ = xp.shape
    bs, ss, _ = xs.shape
    tp, tsm = bp * sp, bs * ss
    total = tp + tsm
    row2 = lambda v: v.reshape(1, -1)

    rwt = jnp.transpose(p["router_w"])
    rwt_hi = rwt.astype(_BF16)
    rwt_lo = (rwt - rwt_hi.astype(_F32)).astype(_BF16)
    ii = jnp.arange(MIXER_ROWS)
    tri = (ii[:, None] < ii[None, :]).astype(_BF16)
    col_scale = jnp.where(jnp.arange(N_GROUPS * D_MODEL) >= (N_GROUPS - 2) * D_MODEL, 0.5, 1.0)
    wts = (
        (p["w_in"] * col_scale).astype(_BF16), row2(p["b_in"] * col_scale), p["conv_a_w"], row2(p["conv_a_b"]),
        _pack_block_diag(0.5 * p["lru_wa"]).astype(_BF16), row2(0.5 * p["lru_ba"]),
        _pack_block_diag(0.5 * p["lru_wx"]).astype(_BF16), row2(0.5 * p["lru_bx"]), row2(p["lru_lambda"]),
        2.0 * p["conv_b_w"], (0.25 * p["w_out"]).astype(_BF16), row2(p["ln1_g"]), row2(p["ln1_b"]),
        rwt_hi, rwt_lo, jnp.broadcast_to(p["router_b"][:, None], (N_EXPERTS, LANES)), tri,
    )

    ts_p = MIXER_ROWS // bp
    zeros = lambda *s: jnp.zeros(s, _F32)
    n_assign = total * TOP_K
    n_spare = 3 * MOE_ROWS
    outs_p = _mixer_call(xp, zeros(1, bp, D_MODEL), zeros(1, CONV_A_WIDTH - 1, bp, D_MODEL),
                         zeros(1, CONV_B_WIDTH - 1, bp, D_MODEL), zeros(N_EXPERTS, LANES), zeros(8, LANES), wts,
                         ts=ts_p, blk_off=0, alpha=alpha, batch_major_x=True,
                         x1_rows=(total + n_spare // TOP_K) * 8)
    nb_s = MIXER_ROWS // ss
    nblk_s = bs // nb_s
    xs4 = jnp.transpose(xs.reshape(nblk_s, nb_s, ss, D_MODEL), (0, 2, 1, 3))
    h0 = h_s0.reshape(nblk_s, nb_s, D_MODEL)
    ca0 = jnp.transpose(ca_s0.reshape(nblk_s, nb_s, CONV_A_WIDTH - 1, D_MODEL), (0, 2, 1, 3))
    cb0 = jnp.transpose(cb_s0.reshape(nblk_s, nb_s, CONV_B_WIDTH - 1, D_MODEL), (0, 2, 1, 3))
    outs_s = _mixer_call(xs4, h0, ca0, cb0, outs_p[7], outs_p[0], wts,
                         ts=ss, blk_off=tp // MIXER_ROWS, alpha=alpha)
    x1t = outs_s[0]

    n_blocks = -(-n_assign // MOE_ROWS) + N_EXPERTS + 1
    n_rows = n_blocks * MOE_ROWS
    idx = jnp.concatenate([outs_p[1][:TOP_K], outs_s[1][:TOP_K]], axis=1)
    rank = jnp.concatenate([outs_p[2][:TOP_K], outs_s[2][:TOP_K]], axis=1)
    grow = jnp.concatenate([outs_p[3], outs_s[3]], axis=0)
    counts = outs_s[7][:, 0].astype(jnp.int32)

    padded = (counts + MOE_ROWS - 1) // MOE_ROWS * MOE_ROWS
    end_pad = jnp.cumsum(padded)
    start_pad = end_pad - padded
    experts = jnp.arange(N_EXPERTS, dtype=jnp.int32)
    start_of = jnp.sum(jnp.where(idx[:, :, None] == experts, start_pad, 0), axis=-1)
    dest = (start_of + rank).reshape(n_assign)
    blk_start = jnp.arange(n_blocks, dtype=jnp.int32) * MOE_ROWS
    blk_e = jnp.minimum(jnp.sum((blk_start[:, None] >= end_pad[None, :]).astype(jnp.int32), axis=1),
                        N_EXPERTS - 1)
    n_used = (end_pad[-1] // MOE_ROWS).astype(jnp.int32).reshape(1)

    blk_ids = jnp.arange(n_blocks, dtype=jnp.int32)
    changed = jnp.concatenate([jnp.ones((1,), bool), blk_e[1:] != blk_e[:-1]])
    first = changed & (blk_ids < n_used[0])
    wslot = (jnp.cumsum(first.astype(jnp.int32)) - 1) % 2
    later_first = first[None, :] & (blk_ids[None, :] > blk_ids[:, None])
    nxt_blk = jnp.min(jnp.where(later_first, blk_ids[None, :], n_blocks), axis=1)
    nxt_e = jnp.sum(jnp.where(nxt_blk[:, None] == blk_ids[None, :], blk_e[None, :], 0), axis=1)
    nxt_e = jnp.where(nxt_blk < n_blocks, nxt_e, -1)

    ext = jnp.arange(n_rows + MOE_ROWS, dtype=jnp.int32)
    row = ext - MOE_ROWS
    holds = jnp.any((row[:, None] >= start_pad[None, :]) & (row[:, None] < (start_pad + counts)[None, :]),
                    axis=1)
    spare = jnp.where(row < 0, n_assign + 2 * MOE_ROWS + ext, n_assign + (row & (2 * MOE_ROWS - 1)))
    scattered = _invmap_sc_call(dest + MOE_ROWS, total=total, n_out=n_rows + MOE_ROWS)
    inv = jnp.where(holds, scattered, spare)
    ytok = _moe_call(blk_e, first.astype(jnp.int32), wslot, nxt_e, n_used, inv, x1t.reshape(-1, 8, LANES),
                     p["w_gu"], p["b_gu"][:, None, :], p["w_down"], p["b_down"][:, None, :],
                     n_slots=n_assign)
    yp, ys = _combine_call(x1t, grow, ytok, row2(p["ln2_g"]), row2(p["ln2_b"]),
                           prompt_shape=xp.shape, sample_shape=xs.shape, alpha=alpha)

    def batch_major(v, nblk):
        return jnp.transpose(v, (0, 2, 1, 3)).reshape(nblk * v.shape[2], v.shape[1], D_MODEL)

    states_p = (outs_p[4].reshape(bp, D_MODEL), batch_major(outs_p[5], 1), batch_major(outs_p[6], 1))
    states_s = (outs_s[4].reshape(bs, D_MODEL), batch_major(outs_s[5], nblk_s), batch_major(outs_s[6], nblk_s))
    return yp, ys, states_p, states_s


def kernel(x_prompt, x_sample, state_rglru_h, state_rglru_conv, state_shortconv, w_in, b_in, conv_a_w, conv_a_b, lru_wa, lru_ba, lru_wx, lru_bx, lru_lambda, conv_b_w, w_out, ln1_g, ln1_b, router_w, router_b, w_gu, b_gu, w_down, b_down, ln2_g, ln2_b):
    depth = w_in.shape[0]
    alpha = (2.0 * depth) ** 0.25
    names = ("w_in", "b_in", "conv_a_w", "conv_a_b", "lru_wa", "lru_ba", "lru_wx", "lru_bx", "lru_lambda",
             "conv_b_w", "w_out", "ln1_g", "ln1_b", "router_w", "router_b", "w_gu", "b_gu", "w_down",
             "b_down", "ln2_g", "ln2_b")
    stacked = (w_in, b_in, conv_a_w, conv_a_b, lru_wa, lru_ba, lru_wx, lru_bx, lru_lambda, conv_b_w, w_out,
               ln1_g, ln1_b, router_w, router_b, w_gu, b_gu, w_down, b_down, ln2_g, ln2_b)
    xp, xs = x_prompt, x_sample
    hp_l, cp_l, sp_l, hs_l, cs_l, ss_l = [], [], [], [], [], []
    for l in range(depth):
        p = {n: v[l] for n, v in zip(names, stacked)}
        xp, xs, (hp, cp, sp), (hs, cs, ss) = _layer(
            xp, xs, state_rglru_h[l], state_rglru_conv[l], state_shortconv[l], p, alpha=alpha)
        hp_l.append(hp); cp_l.append(cp); sp_l.append(sp)
        hs_l.append(hs); cs_l.append(cs); ss_l.append(ss)
    return (xp, xs, jnp.stack(hp_l), jnp.stack(cp_l), jnp.stack(sp_l), jnp.stack(hs_l), jnp.stack(cs_l),
            jnp.stack(ss_l))
```
